```python
import math
import jax, jax.numpy as jnp
from jax import lax
import numpy as np

D_MODEL = 1024
BATCH = 2
SEQ = 8192
DEPTH = 4

HEAD_DIM = 128
N_Q_HEADS = 8
N_KV_HEADS = 2
Q_PER_KV = N_Q_HEADS // N_KV_HEADS
ATTN_WIDTH = N_Q_HEADS * HEAD_DIM
KV_WIDTH = N_KV_HEADS * HEAD_DIM
WINDOW = 128
BLOCK = 128
ROPE_THETA = 10000.0

D_RNN = D_MODEL
N_RNN_BLOCKS = 8
RNN_BLOCK_W = D_RNN // N_RNN_BLOCKS
CONV_W = 4
LRU_C = 8.0

N_IN = ATTN_WIDTH + 2 * KV_WIDTH + 2 * D_RNN + 2 * D_MODEL
SPLITS = [ATTN_WIDTH,
          ATTN_WIDTH + KV_WIDTH,
          ATTN_WIDTH + 2 * KV_WIDTH,
          ATTN_WIDTH + 2 * KV_WIDTH + D_RNN,
          ATTN_WIDTH + 2 * KV_WIDTH + 2 * D_RNN,
          ATTN_WIDTH + 2 * KV_WIDTH + 2 * D_RNN + D_MODEL]

N_GROUPS = 4
EXPERTS_PER_GROUP = 8
N_EXPERTS = N_GROUPS * EXPERTS_PER_GROUP
TOP_K = 2
D_EXPERT = 512
MOE_BLOCK = 128

ALPHA = (2 * DEPTH) ** 0.25
BETA = (8 * DEPTH) ** -0.25
LN_EPS = 1e-5

kernel_name = "hybrid_gqa_rglru_hmoe_encoder"


def layer_norm(x, g, b):
    xf = x.astype(jnp.float32)
    mu = xf.mean(-1, keepdims=True)
    var = jnp.square(xf - mu).mean(-1, keepdims=True)
    y = (xf - mu) * lax.rsqrt(var + LN_EPS) * g.astype(jnp.float32) + b.astype(jnp.float32)
    return y.astype(x.dtype)


def rope(t):
    S = t.shape[1]
    inv = ROPE_THETA ** (-jnp.arange(0, HEAD_DIM, 2, dtype=jnp.float32) / HEAD_DIM)
    ang = jnp.arange(S, dtype=jnp.float32)[:, None] * inv[None, :]
    cos = jnp.cos(ang)[None, :, None, :]
    sin = jnp.sin(ang)[None, :, None, :]
    tf = t.astype(jnp.float32)
    t1, t2 = tf[..., : HEAD_DIM // 2], tf[..., HEAD_DIM // 2:]
    return jnp.concatenate([t1 * cos - t2 * sin, t2 * cos + t1 * sin], axis=-1).astype(t.dtype)


def window_attention(q, k, v, sink):
    B, S = q.shape[0], q.shape[1]
    nb = S // BLOCK
    qb = q.reshape(B, nb, BLOCK, N_KV_HEADS, Q_PER_KV, HEAD_DIM)

    def neighbours(t):
        tp = jnp.pad(t, ((0, 0), (BLOCK, BLOCK), (0, 0), (0, 0)))
        tp = tp.reshape(B, nb + 2, BLOCK, N_KV_HEADS, HEAD_DIM)
        return jnp.concatenate([tp[:, :-2], tp[:, 1:-1], tp[:, 2:]], axis=2)

    kw, vw = neighbours(k), neighbours(v)
    s = jnp.einsum('bnqhgd,bnkhd->bnhgqk', qb, kw,
                   preferred_element_type=jnp.float32) * (HEAD_DIM ** -0.5)
    qi = jnp.arange(BLOCK)[:, None]
    kj = jnp.arange(3 * BLOCK)[None, :]
    rel = kj - BLOCK - qi
    kpos = jnp.arange(nb)[:, None, None] * BLOCK + kj[None] - BLOCK
    mask = (jnp.abs(rel) <= WINDOW)[None] & (kpos >= 0) & (kpos < S)
    s = jnp.where(mask[None, :, None, None], s, -1e30)
    sk = sink.astype(jnp.float32).reshape(1, 1, N_KV_HEADS, Q_PER_KV, 1, 1)
    m = jnp.maximum(s.max(-1, keepdims=True), sk)
    p = jnp.exp(s - m)
    denom = p.sum(-1, keepdims=True) + jnp.exp(sk - m)
    p = (p / denom).astype(v.dtype)
    o = jnp.einsum('bnhgqk,bnkhd->bnqhgd', p, vw)
    return o.reshape(B, S, ATTN_WIDTH)


def centred_conv(x, w, b):
    S = x.shape[1]
    left = CONV_W // 2
    xp = jnp.pad(x, ((0, 0), (left, CONV_W - 1 - left), (0, 0)))
    y = xp[:, 0:S] * w[0]
    for tap in range(1, CONV_W):
        y = y + xp[:, tap:tap + S] * w[tap]
    return y + b


def block_diag(x, w, b):
    B, S = x.shape[0], x.shape[1]
    y = jnp.einsum('bsnc,ncd->bsnd', x.reshape(B, S, N_RNN_BLOCKS, RNN_BLOCK_W), w)
    return y.reshape(B, S, D_RNN) + b


def linear_scan(a, u):
    def comb(left, right):
        a1, b1 = left
        a2, b2 = right
        return a1 * a2, a2 * b1 + b2
    _, h = lax.associative_scan(comb, (a, u), axis=1)
    return h


def rg_lru(xc, w_r, b_r, w_i, b_i, lam, reverse):
    r = jax.nn.sigmoid(block_diag(xc, w_r, b_r).astype(jnp.float32))
    i = jax.nn.sigmoid(block_diag(xc, w_i, b_i).astype(jnp.float32))
    log_a = -LRU_C * r * jax.nn.softplus(-lam.astype(jnp.float32))
    a = jnp.exp(log_a)
    mult = jnp.sqrt(-jnp.expm1(2.0 * log_a))
    u = xc.astype(jnp.float32) * i * mult
    if reverse:
        h = jnp.flip(linear_scan(jnp.flip(a, 1), jnp.flip(u, 1)), 1)
    else:
        h = linear_scan(a, u)
    return h


def mixer(x, w_in, w_sink, w_conv, b_conv, w_rec_gate, b_rec_gate, w_in_gate, b_in_gate,
          lru_lambda, w_attn_o, w_rnn_o, w_out):
    B, S = x.shape[0], x.shape[1]
    z = x @ w_in
    q, k, v, xr, yr, ga, gr = jnp.split(z, SPLITS, axis=-1)
    q = rope(q.reshape(B, S, N_Q_HEADS, HEAD_DIM))
    k = rope(k.reshape(B, S, N_KV_HEADS, HEAD_DIM))
    v = v.reshape(B, S, N_KV_HEADS, HEAD_DIM)
    y_attn = window_attention(q, k, v, w_sink) @ w_attn_o

    xc = centred_conv(xr, w_conv, b_conv)
    h = (rg_lru(xc, w_rec_gate[0], b_rec_gate[0], w_in_gate[0], b_in_gate[0], lru_lambda[0], False)
         + rg_lru(xc, w_rec_gate[1], b_rec_gate[1], w_in_gate[1], b_in_gate[1], lru_lambda[1], True))
    y_rnn = (h.astype(x.dtype) * jax.nn.gelu(yr, approximate=True)) @ w_rnn_o

    merged = jax.nn.sigmoid(ga) * y_attn + jax.nn.sigmoid(gr) * y_rnn
    return merged @ w_out


def hier_moe(x, w_rg, b_rg, w_re, b_re, w_g, w_u, w_d):
    B, S, D = x.shape
    T = B * S
    xt = x.reshape(T, D)
    g_prob = jax.nn.softmax((xt @ w_rg + b_rg).astype(jnp.float32), axis=-1)
    g_val, g_idx = lax.top_k(g_prob, 1)
    e_logits = (xt @ w_re + b_re).astype(jnp.float32).reshape(T, N_GROUPS, EXPERTS_PER_GROUP)
    e_in_group = jnp.take_along_axis(e_logits, g_idx[:, :, None], axis=1)[:, 0]
    top_l, top_i = lax.top_k(e_in_group, TOP_K)
    gate = jax.nn.softmax(top_l, axis=-1) * g_val

    eid = (g_idx * EXPERTS_PER_GROUP + top_i).reshape(-1)
    tok = jnp.repeat(jnp.arange(T, dtype=jnp.int32), TOP_K)
    wt = gate.reshape(-1).astype(x.dtype)
    n_assign = T * TOP_K
    order = jnp.argsort(eid)
    se = eid[order]
    counts = jnp.bincount(eid, length=N_EXPERTS)
    starts = jnp.cumsum(counts) - counts
    pcounts = (counts + MOE_BLOCK - 1) // MOE_BLOCK * MOE_BLOCK
    pends = jnp.cumsum(pcounts)
    pstarts = pends - pcounts
    dest = pstarts[se] + jnp.arange(n_assign, dtype=jnp.int32) - starts[se]
    n_blocks = -(-n_assign // MOE_BLOCK) + N_EXPERTS
    n_rows = n_blocks * MOE_BLOCK
    buf_tok = jnp.full((n_rows,), T, jnp.int32).at[dest].set(tok[order])
    buf_w = jnp.zeros((n_rows,), x.dtype).at[dest].set(wt[order])
    block_e = jnp.minimum(jnp.searchsorted(pends, jnp.arange(n_blocks) * MOE_BLOCK, side='right'),
                          N_EXPERTS - 1)
    x_pad = jnp.concatenate([xt, jnp.zeros((1, D), xt.dtype)], axis=0)
    xb = x_pad[buf_tok].reshape(n_blocks, MOE_BLOCK, D)

    def expert_block(args):
        xblk, e = args
        hid = jax.nn.silu(xblk @ w_g[e]) * (xblk @ w_u[e])
        return hid @ w_d[e]

    yb = lax.map(expert_block, (xb, block_e)).reshape(n_rows, D) * buf_w[:, None]
    out = jnp.zeros((T + 1, D), x.dtype).at[buf_tok].add(yb)[:T]
    return out.reshape(B, S, D)


def setup_inputs(seed: int = 0) -> dict:
    key = jax.random.key(seed)
    ks = jax.random.split(key, 24)
    f32 = jnp.float32

    def nrm(k, shape, scale):
        return jax.random.normal(k, shape, f32) * scale

    x = nrm(ks[0], (BATCH, SEQ, D_MODEL), 1.0)
    col_scale = jnp.concatenate([jnp.ones((ATTN_WIDTH + KV_WIDTH,), f32),
                                 jnp.full((KV_WIDTH,), BETA, f32),
                                 jnp.ones((2 * D_RNN + 2 * D_MODEL,), f32)])
    w_in = nrm(ks[1], (DEPTH, D_MODEL, N_IN), D_MODEL ** -0.5) * col_scale
    w_sink = nrm(ks[2], (DEPTH, N_Q_HEADS), 0.5)
    w_conv = nrm(ks[3], (DEPTH, CONV_W, D_RNN), CONV_W ** -0.5)
    b_conv = nrm(ks[4], (DEPTH, D_RNN), 0.02)
    w_rec_gate = nrm(ks[5], (DEPTH, 2, N_RNN_BLOCKS, RNN_BLOCK_W, RNN_BLOCK_W), RNN_BLOCK_W ** -0.5)
    b_rec_gate = nrm(ks[6], (DEPTH, 2, D_RNN), 0.02)
    w_in_gate = nrm(ks[7], (DEPTH, 2, N_RNN_BLOCKS, RNN_BLOCK_W, RNN_BLOCK_W), RNN_BLOCK_W ** -0.5)
    b_in_gate = nrm(ks[8], (DEPTH, 2, D_RNN), 0.02)
    u = jax.random.uniform(ks[9], (DEPTH, 2, D_RNN), f32, 0.9, 0.999)
    s = u ** (1.0 / LRU_C)
    lru_lambda = jnp.log(s) - jnp.log1p(-s)
    w_attn_o = nrm(ks[10], (DEPTH, ATTN_WIDTH, D_MODEL), ATTN_WIDTH ** -0.5 * BETA)
    w_rnn_o = nrm(ks[11], (DEPTH, D_RNN, D_MODEL), D_RNN ** -0.5 * BETA)
    w_out = nrm(ks[12], (DEPTH, D_MODEL, D_MODEL), D_MODEL ** -0.5 * BETA)
    ln_g = 1.0 + nrm(ks[13], (DEPTH, 2, D_MODEL), 0.02)
    ln_b = nrm(ks[14], (DEPTH, 2, D_MODEL), 0.02)
    w_router_group = nrm(ks[15], (DEPTH, D_MODEL, N_GROUPS), D_MODEL ** -0.5)
    b_router_group = nrm(ks[16], (DEPTH, N_GROUPS), 0.01)
    w_router_expert = nrm(ks[17], (DEPTH, D_MODEL, N_EXPERTS), D_MODEL ** -0.5)
    b_router_expert = nrm(ks[18], (DEPTH, N_EXPERTS), 0.01)
    w_exp_gate = nrm(ks[19], (DEPTH, N_EXPERTS, D_MODEL, D_EXPERT), D_MODEL ** -0.5)
    w_exp_up = nrm(ks[20], (DEPTH, N_EXPERTS, D_MODEL, D_EXPERT), D_MODEL ** -0.5 * BETA)
    w_exp_down = nrm(ks[21], (DEPTH, N_EXPERTS, D_EXPERT, D_MODEL), D_EXPERT ** -0.5 * BETA)
    return {"x": x, "w_in": w_in, "w_sink": w_sink, "w_conv": w_conv, "b_conv": b_conv,
            "w_rec_gate": w_rec_gate, "b_rec_gate": b_rec_gate,
            "w_in_gate": w_in_gate, "b_in_gate": b_in_gate, "lru_lambda": lru_lambda,
            "w_attn_o": w_attn_o, "w_rnn_o": w_rnn_o, "w_out": w_out,
            "ln_g": ln_g, "ln_b": ln_b,
            "w_router_group": w_router_group, "b_router_group": b_router_group,
            "w_router_expert": w_router_expert, "b_router_expert": b_router_expert,
            "w_exp_gate": w_exp_gate, "w_exp_up": w_exp_up, "w_exp_down": w_exp_down}


def reference(x, w_in, w_sink, w_conv, b_conv, w_rec_gate, b_rec_gate, w_in_gate, b_in_gate,
              lru_lambda, w_attn_o, w_rnn_o, w_out, ln_g, ln_b,
              w_router_group, b_router_group, w_router_expert, b_router_expert,
              w_exp_gate, w_exp_up, w_exp_down):
    for l in range(DEPTH):
        mix = mixer(x, w_in[l], w_sink[l], w_conv[l], b_conv[l], w_rec_gate[l], b_rec_gate[l],
                    w_in_gate[l], b_in_gate[l], lru_lambda[l], w_attn_o[l], w_rnn_o[l], w_out[l])
        x = layer_norm(ALPHA * x + mix, ln_g[l, 0], ln_b[l, 0])
        ffn = hier_moe(x, w_router_group[l], b_router_group[l], w_router_expert[l], b_router_expert[l],
                       w_exp_gate[l], w_exp_up[l], w_exp_down[l])
        x = layer_norm(ALPHA * x + ffn, ln_g[l, 1], ln_b[l, 1])
    return x
```

```python
import functools
import math

import jax
import jax.numpy as jnp
from jax import lax
from jax.experimental import pallas as pl
from jax.experimental.pallas import tpu as pltpu

F32 = jnp.float32
BF16 = jnp.bfloat16

HEAD_DIM = 128
N_Q_HEADS = 8
N_KV_HEADS = 2
Q_PER_KV = N_Q_HEADS // N_KV_HEADS
WINDOW = 128
ROPE_THETA = 10000.0
CONV_W = 4
LRU_C = 8.0
N_GROUPS = 4
EXPERTS_PER_GROUP = 8
N_EXPERTS = N_GROUPS * EXPERTS_PER_GROUP
LN_EPS = 1e-5
NEG = -1e30

SUBLANES = 8
LANES = 128

TM_PROJ = 256
TQ_ATTN = 512
SEG = 256
SEG_PITCH = SEG + 8
TD = 256
CHUNK = SUBLANES
STAGE_ROWS = 2 * TD + CHUNK * N_EXPERTS
MOE_BLOCK = 128
CHUNKS_PER_BLOCK = MOE_BLOCK // CHUNK
VMEM_LIMIT = 56 * 1024 * 1024


def _cparams(n_axes):
    return pltpu.CompilerParams(dimension_semantics=("arbitrary",) * n_axes,
                                vmem_limit_bytes=VMEM_LIMIT)


def _softplus(z):
    e = jnp.exp(-jnp.abs(z))
    w = 1.0 + e
    tiny = w == 1.0
    log1p = jnp.where(tiny, e, jnp.log(w) * (e / jnp.where(tiny, 1.0, w - 1.0)))
    return jnp.maximum(z, 0.0) + log1p


def _layer_norm(y, g, b):
    mu = jnp.mean(y, axis=-1, keepdims=True)
    d = y - mu
    var = jnp.mean(d * d, axis=-1, keepdims=True)
    return d * lax.rsqrt(var + LN_EPS) * g + b


def _inproj_kernel(x_ref, w_ref, cos_ref, sin_ref,
                   q_ref, k_ref, v_ref, xr_ref, yr_ref, ga_ref, gr_ref, *, d_model):
    xb = x_ref[...].astype(BF16)
    cos = cos_ref[...]
    sin = sin_ref[...]
    aw = N_Q_HEADS * HEAD_DIM
    kw = N_KV_HEADS * HEAD_DIM

    def proj(c0, n):
        return jnp.dot(xb, w_ref[:, c0:c0 + n], preferred_element_type=F32)

    def rope(t):
        return t * cos + pltpu.roll(t, HEAD_DIM // 2, 1) * sin

    zq = proj(0, aw)
    scale = HEAD_DIM ** -0.5
    for h in range(N_Q_HEADS):
        sl = slice(h * HEAD_DIM, (h + 1) * HEAD_DIM)
        q_ref[:, sl] = (rope(zq[:, sl]) * scale).astype(BF16)
    zk = proj(aw, kw)
    for h in range(N_KV_HEADS):
        sl = slice(h * HEAD_DIM, (h + 1) * HEAD_DIM)
        k_ref[:, sl] = rope(zk[:, sl]).astype(BF16)
    v_ref[...] = proj(aw + kw, kw).astype(BF16)
    c0 = aw + 2 * kw
    xr_ref[...] = proj(c0, d_model)
    yr_ref[...] = proj(c0 + d_model, d_model)
    ga_ref[...] = proj(c0 + 2 * d_model, d_model)
    gr_ref[...] = proj(c0 + 3 * d_model, d_model)


def _inproj(x2, w_in_b, cosf, sinf, seq):
    t, d = x2.shape
    n_in = w_in_b.shape[1]
    tm = TM_PROJ
    aw = N_Q_HEADS * HEAD_DIM
    kw = N_KV_HEADS * HEAD_DIM
    tiles_per_seq = seq // tm
    row = lambda i: (i, 0)
    pos = lambda i: (i % tiles_per_seq, 0)
    outs = [jax.ShapeDtypeStruct((t, aw), BF16), jax.ShapeDtypeStruct((t, kw), BF16),
            jax.ShapeDtypeStruct((t, kw), BF16)] + [jax.ShapeDtypeStruct((t, d), F32)] * 4
    return pl.pallas_call(
        functools.partial(_inproj_kernel, d_model=d),
        grid=(t // tm,),
        in_specs=[pl.BlockSpec((tm, d), row),
                  pl.BlockSpec((d, n_in), lambda i: (0, 0)),
                  pl.BlockSpec((tm, HEAD_DIM), pos),
                  pl.BlockSpec((tm, HEAD_DIM), pos)],
        out_specs=[pl.BlockSpec((tm, aw), row), pl.BlockSpec((tm, kw), row),
                   pl.BlockSpec((tm, kw), row)] + [pl.BlockSpec((tm, d), row)] * 4,
        out_shape=outs,
        compiler_params=_cparams(1),
        name="inproj",
    )(x2, w_in_b, cosf, sinf)


def _attn_kernel(sink_ref, q_ref, kp_ref, kc_ref, kn_ref, vp_ref, vc_ref, vn_ref, o_ref,
                 *, seq):
    i = pl.program_id(1)
    tq = q_ref.shape[1]
    blk = WINDOW
    t0 = i * tq
    kext = jnp.concatenate([kp_ref[0], kc_ref[0], kn_ref[0]], axis=0)
    vext = jnp.concatenate([vp_ref[0], vc_ref[0], vn_ref[0]], axis=0)
    nrow = Q_PER_KV * blk
    blk_shift = blk.bit_length() - 1
    qi = lax.broadcasted_iota(jnp.int32, (nrow, 3 * blk), 0) & (blk - 1)
    kj = lax.broadcasted_iota(jnp.int32, (nrow, 3 * blk), 1)
    rel = kj - blk - qi
    band = jnp.abs(rel) <= WINDOW
    rowg = lax.broadcasted_iota(jnp.int32, (nrow, 1), 0) >> blk_shift
    for j in range(tq // blk):
        kpos = t0 + j * blk - blk + kj
        valid = band & (kpos >= 0) & (kpos < seq)
        for h in range(N_KV_HEADS):
            hs = slice(h * HEAD_DIM, (h + 1) * HEAD_DIM)
            kblk = kext[j * blk:j * blk + 3 * blk, hs]
            vblk = vext[j * blk:j * blk + 3 * blk, hs]
            qs = [q_ref[0, j * blk:(j + 1) * blk,
                        (h * Q_PER_KV + g) * HEAD_DIM:(h * Q_PER_KV + g + 1) * HEAD_DIM]
                  for g in range(Q_PER_KV)]
            qblk = jnp.concatenate(qs, axis=0)
            s = lax.dot_general(qblk, kblk, (((1,), (1,)), ((), ())),
                                preferred_element_type=F32)
            s = jnp.where(valid, s, NEG)
            sk = jnp.full((nrow, 1), sink_ref[h * Q_PER_KV], F32)
            for g in range(1, Q_PER_KV):
                sk = jnp.where(rowg == g, sink_ref[h * Q_PER_KV + g], sk)
            m = jnp.maximum(jnp.max(s, axis=-1, keepdims=True), sk)
            p = jnp.exp(s - m)
            denom = jnp.sum(p, axis=-1, keepdims=True) + jnp.exp(sk - m)
            o = jnp.dot(p.astype(BF16), vblk, preferred_element_type=F32) / denom
            for g in range(Q_PER_KV):
                c = (h * Q_PER_KV + g) * HEAD_DIM
                o_ref[0, j * blk:(j + 1) * blk, c:c + HEAD_DIM] = (
                    o[g * blk:(g + 1) * blk].astype(BF16))


def _attention(q3, k3, v3, sink):
    b, s, aw = q3.shape
    kw = k3.shape[2]
    tq = TQ_ATTN
    blk = WINDOW
    r = tq // blk
    nblk = s // blk
    cur = lambda bi, i: (bi, i, 0)
    prev = lambda bi, i: (bi, jnp.maximum(i * r - 1, 0), 0)
    nxt = lambda bi, i: (bi, jnp.minimum((i + 1) * r, nblk - 1), 0)
    kv_specs = [pl.BlockSpec((1, blk, kw), prev), pl.BlockSpec((1, tq, kw), cur),
                pl.BlockSpec((1, blk, kw), nxt)]
    return pl.pallas_call(
        functools.partial(_attn_kernel, seq=s),
        grid=(b, s // tq),
        in_specs=[pl.BlockSpec(memory_space=pltpu.SMEM),
                  pl.BlockSpec((1, tq, aw), cur)] + kv_specs + kv_specs,
        out_specs=pl.BlockSpec((1, tq, aw), cur),
        out_shape=jax.ShapeDtypeStruct((b, s, aw), BF16),
        compiler_params=_cparams(2),
        name="attention",
    )(sink, q3, k3, k3, k3, v3, v3, v3)


def _rnn_kernel(xr_ref, yr_ref, wc_ref, bc_ref, wg_ref, bg_ref, lam_ref, out_ref,
                af_ref, uf_ref, ab_ref, ub_ref, cf_ref, cb_ref, *, seq):
    nseg = seq // SEG
    ngrp = nseg // SUBLANES
    wc = wc_ref[...]
    bc = bc_ref[0]
    wg = wg_ref[0]
    bg = bg_ref[0]
    lam = lam_ref[0]
    cdec = -LRU_C * _softplus(-lam)
    a_refs = (af_ref, ab_ref)
    u_refs = (uf_ref, ub_ref)

    def gates(c, carry):
        t0 = pl.multiple_of(c * SEG, SEG)
        cur = xr_ref[0, pl.ds(t0, SEG), :]
        prev = xr_ref[0, pl.ds(jnp.maximum(t0 - SUBLANES, 0), SUBLANES), :]
        nxt = xr_ref[0, pl.ds(jnp.minimum(t0 + SEG, seq - SUBLANES), SUBLANES), :]
        prev = jnp.where(c > 0, prev, 0.0)
        nxt = jnp.where(c < nseg - 1, nxt, 0.0)
        ext = jnp.concatenate([prev, cur, nxt], axis=0)
        left = CONV_W // 2
        xc = bc
        for tap in range(CONV_W):
            o = SUBLANES - left + tap
            xc = xc + ext[o:o + SEG] * wc[tap:tap + 1]
        g = jnp.dot(xc.astype(BF16), wg, preferred_element_type=F32) + bg
        dst = pl.multiple_of(c * SEG_PITCH, SUBLANES)
        for d in range(2):
            r = jax.nn.sigmoid(g[:, (2 * d) * LANES:(2 * d + 1) * LANES])
            ig = jax.nn.sigmoid(g[:, (2 * d + 1) * LANES:(2 * d + 2) * LANES])
            log_a = cdec[:, d * LANES:(d + 1) * LANES] * r
            a = jnp.exp(log_a)
            mult = jnp.sqrt(-jnp.tanh(log_a) * (a * a + 1.0))
            a_refs[d][pl.ds(dst, SEG), :] = a
            u_refs[d][pl.ds(dst, SEG), :] = xc * ig * mult
        return carry

    lax.fori_loop(0, nseg, gates, 0)

    def seg_rows(g, j):
        return pl.ds(g * SUBLANES * SEG_PITCH + j, SUBLANES, stride=SEG_PITCH)

    def scan_step(j, carry):
        hf, pf, hb, pb = carry
        jb = SEG - 1 - j
        nhf, npf, nhb, npb = [], [], [], []
        for g in range(ngrp):
            rows = seg_rows(g, j)
            a = af_ref[rows, :]
            h = a * hf[g] + uf_ref[rows, :]
            p = a * pf[g]
            uf_ref[rows, :] = h
            af_ref[rows, :] = p
            nhf.append(h)
            npf.append(p)
            rows = seg_rows(g, jb)
            a = ab_ref[rows, :]
            h = a * hb[g] + ub_ref[rows, :]
            p = a * pb[g]
            ub_ref[rows, :] = h
            ab_ref[rows, :] = p
            nhb.append(h)
            npb.append(p)
        return tuple(nhf), tuple(npf), tuple(nhb), tuple(npb)

    zero = tuple(jnp.zeros((SUBLANES, LANES), F32) for _ in range(ngrp))
    one = tuple(jnp.ones((SUBLANES, LANES), F32) for _ in range(ngrp))
    hf, pf, hb, pb = lax.fori_loop(0, SEG, scan_step, (zero, one, zero, one))

    c = jnp.zeros((1, LANES), F32)
    for s in range(nseg):
        g, r = divmod(s, SUBLANES)
        cf_ref[s:s + 1, :] = c
        c = pf[g][r:r + 1] * c + hf[g][r:r + 1]
    c = jnp.zeros((1, LANES), F32)
    for s in range(nseg - 1, -1, -1):
        g, r = divmod(s, SUBLANES)
        cb_ref[s:s + 1, :] = c
        c = pb[g][r:r + 1] * c + hb[g][r:r + 1]

    def finish(c, carry):
        src = pl.multiple_of(c * SEG_PITCH, SUBLANES)
        t0 = pl.multiple_of(c * SEG, SEG)
        h = (uf_ref[pl.ds(src, SEG), :] + af_ref[pl.ds(src, SEG), :] * cf_ref[pl.ds(c, 1), :]
             + ub_ref[pl.ds(src, SEG), :] + ab_ref[pl.ds(src, SEG), :] * cb_ref[pl.ds(c, 1), :])
        y = yr_ref[0, pl.ds(t0, SEG), :]
        out_ref[0, pl.ds(t0, SEG), :] = (h * jax.nn.gelu(y, approximate=True)).astype(BF16)
        return carry

    lax.fori_loop(0, nseg, finish, 0)


def _rnn(xr3, yr3, w_conv, b_conv, wg_cat, bg_cat, lam_cat):
    b, s, d = xr3.shape
    nblk = d // LANES
    nseg = s // SEG
    slab = lambda bi, n: (bi, 0, n)
    per_blk = lambda bi, n: (n, 0, 0)
    scratch = [pltpu.VMEM((nseg * SEG_PITCH, LANES), F32)] * 4 + [pltpu.VMEM((nseg, LANES), F32)] * 2
    return pl.pallas_call(
        functools.partial(_rnn_kernel, seq=s),
        grid=(b, nblk),
        in_specs=[pl.BlockSpec((1, s, LANES), slab),
                  pl.BlockSpec((1, s, LANES), slab),
                  pl.BlockSpec((CONV_W, LANES), lambda bi, n: (0, n)),
                  pl.BlockSpec((1, 1, LANES), per_blk),
                  pl.BlockSpec((1, LANES, 4 * LANES), per_blk),
                  pl.BlockSpec((1, 1, 4 * LANES), per_blk),
                  pl.BlockSpec((1, 1, 2 * LANES), per_blk)],
        out_specs=pl.BlockSpec((1, s, LANES), slab),
        out_shape=jax.ShapeDtypeStruct((b, s, d), BF16),
        scratch_shapes=scratch,
        compiler_params=_cparams(2),
        name="rglru",
    )(xr3, yr3, w_conv, b_conv, wg_cat, bg_cat, lam_cat)


def _route(logits):
    lane = lax.broadcasted_iota(jnp.int32, logits.shape, 1)
    lanef = lane.astype(F32)
    big = float(4 * LANES)
    gmask = (lane >= N_EXPERTS) & (lane < N_EXPERTS + N_GROUPS)
    gl = jnp.where(gmask, logits, NEG)
    gmax = jnp.max(gl, axis=-1, keepdims=True)
    ge = jnp.exp(gl - gmax)
    gprob = ge / jnp.sum(ge, axis=-1, keepdims=True)
    gval = jnp.max(gprob, axis=-1, keepdims=True)
    gidx = jnp.min(jnp.where((gprob == gval) & gmask, lanef, big), axis=-1, keepdims=True)
    gidx = gidx.astype(jnp.int32) - N_EXPERTS
    group_shift = EXPERTS_PER_GROUP.bit_length() - 1
    emask = (lane < N_EXPERTS) & ((lane >> group_shift) == gidx)
    el = jnp.where(emask, logits, NEG)
    m1 = jnp.max(el, axis=-1, keepdims=True)
    i1 = jnp.min(jnp.where((el == m1) & emask, lanef, big), axis=-1, keepdims=True)
    emask2 = emask & (lanef != i1)
    el2 = jnp.where(emask2, logits, NEG)
    m2 = jnp.max(el2, axis=-1, keepdims=True)
    i2 = jnp.min(jnp.where((el2 == m2) & emask2, lanef, big), axis=-1, keepdims=True)
    e2 = jnp.exp(m2 - m1)
    den = 1.0 + e2
    g1 = (1.0 / den) * gval
    g2 = (e2 / den) * gval
    route = jnp.where(lane == 0, i1, jnp.where(lane == 1, i2,
                      jnp.where(lane == 2, g1, jnp.where(lane == 3, g2, 0.0))))
    onehot = (lanef == i1).astype(F32) + (lanef == i2).astype(F32)
    return route, jnp.sum(onehot, axis=0, keepdims=True)


def _mixout_kernel(x_ref, o_ref, hy_ref, ga_ref, gr_ref, wao_ref, wro_ref, wout_ref,
                   lng_ref, lnb_ref, wr_ref, br_ref, x1_ref, route_ref, cnt_ref, *, alpha):
    ya = jnp.dot(o_ref[...], wao_ref[...], preferred_element_type=F32)
    yr = jnp.dot(hy_ref[...], wro_ref[...], preferred_element_type=F32)
    merged = jax.nn.sigmoid(ga_ref[...]) * ya + jax.nn.sigmoid(gr_ref[...]) * yr
    mix = jnp.dot(merged.astype(BF16), wout_ref[...], preferred_element_type=F32)
    x1 = _layer_norm(alpha * x_ref[...] + mix, lng_ref[...], lnb_ref[...])
    x1_ref[...] = x1
    logits = jnp.dot(x1.astype(BF16), wr_ref[...], preferred_element_type=F32) + br_ref[...]
    route, cnt = _route(logits)
    route_ref[...] = route
    cnt_ref[0] = cnt


def _mixout(x2, o2, hy2, ga, gr, wao, wro, wout, lng, lnb, wr, br, alpha):
    t, d = x2.shape
    tm = TD
    nt = t // tm
    row = lambda i: (i, 0)
    full = lambda i: (0, 0)
    return pl.pallas_call(
        functools.partial(_mixout_kernel, alpha=alpha),
        grid=(nt,),
        in_specs=[pl.BlockSpec((tm, d), row)] * 5
                 + [pl.BlockSpec((d, d), full)] * 3
                 + [pl.BlockSpec((1, d), full)] * 2
                 + [pl.BlockSpec((d, LANES), full), pl.BlockSpec((1, LANES), full)],
        out_specs=[pl.BlockSpec((tm, d), row), pl.BlockSpec((tm, LANES), row),
                   pl.BlockSpec((1, 1, LANES), lambda i: (i, 0, 0))],
        out_shape=[jax.ShapeDtypeStruct((t, d), F32), jax.ShapeDtypeStruct((t, LANES), F32),
                   jax.ShapeDtypeStruct((nt, 1, LANES), F32)],
        compiler_params=_cparams(1),
        name="mixout",
    )(x2, o2, hy2, ga, gr, wao, wro, wout, lng, lnb, wr, br)


def _perm_matrix(route, soff_row, w1, w2):
    td = route.shape[0]
    lane = lax.broadcasted_iota(jnp.int32, (td, LANES), 1).astype(F32)
    e1 = lane == route[:, 0:1]
    e2 = lane == route[:, 1:2]
    cnt = (e1.astype(F32) + e2.astype(F32)).astype(BF16)
    ti = lax.broadcasted_iota(jnp.int32, (td, td), 0)
    tj = lax.broadcasted_iota(jnp.int32, (td, td), 1)
    lower = (tj < ti).astype(BF16)
    pos = jnp.dot(lower, cnt, preferred_element_type=F32) + soff_row
    r1 = jnp.sum(jnp.where(e1, pos, 0.0), axis=-1, keepdims=True).astype(jnp.int32)
    r2 = jnp.sum(jnp.where(e2, pos, 0.0), axis=-1, keepdims=True).astype(jnp.int32)
    col = lax.broadcasted_iota(jnp.int32, (td, STAGE_ROWS), 1)
    return jnp.where(col == r1, w1, 0.0) + jnp.where(col == r2, w2, 0.0)


def _chunk_rows(c):
    return pl.ds(pl.multiple_of(c * CHUNK, CHUNK), CHUNK)


def _dispatch_kernel(cch_ref, soff_ref, dch_ref, tot_ref, padst_ref, padn_ref, nb_ref,
                     x_ref, route_ref, soffrow_ref, xb_ref, stage_ref, zero_ref, sem):
    i = pl.program_id(0)
    nt = pl.num_programs(0)
    n_blocks = xb_ref.shape[0] // MOE_BLOCK
    pt = _perm_matrix(route_ref[...], soffrow_ref[0], 1.0, 1.0).astype(BF16)
    stage_ref[...] = lax.dot_general(pt, x_ref[...].astype(BF16), (((0,), (0,)), ((), ())),
                                     preferred_element_type=F32)

    def copy(src_chunk, dst_chunk):
        return pltpu.make_async_copy(stage_ref.at[_chunk_rows(src_chunk)],
                                     xb_ref.at[_chunk_rows(dst_chunk)], sem)

    def per_expert(e, carry):
        k = i * N_EXPERTS + e
        so = soff_ref[k]
        do = dch_ref[k]

        def per_chunk(c, carry2):
            copy(so + c, do + c).start()
            return carry2

        return lax.fori_loop(0, cch_ref[k], per_chunk, carry)

    lax.fori_loop(0, N_EXPERTS, per_expert, 0)

    def wait_one(c, carry):
        copy(0, 0).wait()
        return carry

    lax.fori_loop(0, tot_ref[i], wait_one, 0)

    @pl.when(i == nt - 1)
    def _():
        zero_ref[...] = jnp.zeros_like(zero_ref)

        def zcopy(dst_chunk):
            return pltpu.make_async_copy(zero_ref.at[pl.ds(0, CHUNK)],
                                         xb_ref.at[_chunk_rows(dst_chunk)], sem)

        def pad_expert(e, n):
            def pad_chunk(c, carry2):
                zcopy(padst_ref[e] + c).start()
                return carry2

            lax.fori_loop(0, padn_ref[e], pad_chunk, 0)
            return n + padn_ref[e]

        npad = lax.fori_loop(0, N_EXPERTS, pad_expert, 0)

        def wait_pad(c, carry):
            zcopy(0).wait()
            return carry

        lax.fori_loop(0, npad, wait_pad, 0)

        def zblock(b):
            rows = pl.ds(pl.multiple_of(b * MOE_BLOCK, MOE_BLOCK), MOE_BLOCK)
            return pltpu.make_async_copy(zero_ref, xb_ref.at[rows], sem)

        def start_block(b, carry):
            zblock(b).start()
            return carry

        def wait_block(b, carry):
            zblock(b).wait()
            return carry

        lax.fori_loop(nb_ref[0], n_blocks, start_block, 0)
        lax.fori_loop(nb_ref[0], n_blocks, wait_block, 0)


def _dispatch(x1, route, soffrow, tables, n_rows):
    t, d = x1.shape
    nt = t // TD
    grid_spec = pltpu.PrefetchScalarGridSpec(
        num_scalar_prefetch=7,
        grid=(nt,),
        in_specs=[pl.BlockSpec((TD, d), lambda i, *_: (i, 0)),
                  pl.BlockSpec((TD, LANES), lambda i, *_: (i, 0)),
                  pl.BlockSpec((1, 1, LANES), lambda i, *_: (i, 0, 0))],
        out_specs=pl.BlockSpec(memory_space=pl.ANY),
        scratch_shapes=[pltpu.VMEM((STAGE_ROWS, d), F32), pltpu.VMEM((MOE_BLOCK, d), F32),
                        pltpu.SemaphoreType.DMA(())],
    )
    return pl.pallas_call(
        _dispatch_kernel,
        grid_spec=grid_spec,
        out_shape=jax.ShapeDtypeStruct((n_rows, d), F32),
        compiler_params=_cparams(1),
        name="dispatch",
    )(*tables, x1, route, soffrow)


def _moe_kernel(be_ref, nb_ref, xb_ref, wg_ref, wu_ref, wd_ref, y_ref, wgb_ref, wub_ref, wdb_ref):
    b = pl.program_id(0)

    @pl.when(b < nb_ref[0])
    def _():
        new_expert = jnp.logical_or(b == 0, be_ref[b] != be_ref[jnp.maximum(b - 1, 0)])

        @pl.when(new_expert)
        def _():
            wgb_ref[...] = wg_ref[0].astype(BF16)
            wub_ref[...] = wu_ref[0].astype(BF16)
            wdb_ref[...] = wd_ref[0].astype(BF16)

        x = xb_ref[...].astype(BF16)
        hg = jnp.dot(x, wgb_ref[...], preferred_element_type=F32)
        hu = jnp.dot(x, wub_ref[...], preferred_element_type=F32)
        hid = (hg * jax.nn.sigmoid(hg)) * hu
        y_ref[...] = jnp.dot(hid.astype(BF16), wdb_ref[...], preferred_element_type=F32)

    @pl.when(b >= nb_ref[0])
    def _():
        y_ref[...] = jnp.zeros_like(y_ref)


def _moe(xb, block_e, nb_used, w_g, w_u, w_d, n_blocks):
    d = xb.shape[1]
    de = w_g.shape[2]

    def blk(b, be, nb):
        return (jnp.minimum(b, nb[0] - 1), 0)

    def oblk(b, be, nb):
        return (b, 0)

    def wsel(b, be, nb):
        return (be[b], 0, 0)

    grid_spec = pltpu.PrefetchScalarGridSpec(
        num_scalar_prefetch=2,
        grid=(n_blocks,),
        in_specs=[pl.BlockSpec((MOE_BLOCK, d), blk),
                  pl.BlockSpec((1, d, de), wsel),
                  pl.BlockSpec((1, d, de), wsel),
                  pl.BlockSpec((1, de, d), wsel)],
        out_specs=pl.BlockSpec((MOE_BLOCK, d), oblk),
        scratch_shapes=[pltpu.VMEM((d, de), BF16), pltpu.VMEM((d, de), BF16),
                        pltpu.VMEM((de, d), BF16)],
    )
    return pl.pallas_call(
        _moe_kernel,
        grid_spec=grid_spec,
        out_shape=jax.ShapeDtypeStruct((n_blocks * MOE_BLOCK, d), F32),
        compiler_params=_cparams(1),
        name="experts",
    )(block_e, nb_used, xb, w_g, w_u, w_d)


def _combine_kernel(cch_ref, soff_ref, dch_ref, tot_ref,
                    x_ref, route_ref, soffrow_ref, yb_ref, lng_ref, lnb_ref, out_ref,
                    stage_ref, sem, *, alpha):
    i = pl.program_id(0)

    @pl.when(i == 0)
    def _():
        stage_ref[...] = jnp.zeros_like(stage_ref)

    def copy(src_chunk, dst_chunk):
        return pltpu.make_async_copy(yb_ref.at[_chunk_rows(src_chunk)],
                                     stage_ref.at[_chunk_rows(dst_chunk)], sem)

    def per_expert(e, carry):
        k = i * N_EXPERTS + e
        so = soff_ref[k]
        do = dch_ref[k]

        def per_chunk(c, carry2):
            copy(do + c, so + c).start()
            return carry2

        return lax.fori_loop(0, cch_ref[k], per_chunk, carry)

    lax.fori_loop(0, N_EXPERTS, per_expert, 0)

    def wait_one(c, carry):
        copy(0, 0).wait()
        return carry

    lax.fori_loop(0, tot_ref[i], wait_one, 0)

    route = route_ref[...]
    pt = _perm_matrix(route, soffrow_ref[0], route[:, 2:3], route[:, 3:4]).astype(BF16)
    ffn = jnp.dot(pt, stage_ref[...].astype(BF16), preferred_element_type=F32)
    out_ref[...] = _layer_norm(alpha * x_ref[...] + ffn, lng_ref[...], lnb_ref[...])


def _combine(x1, route, soffrow, tables, yb, lng, lnb, alpha):
    t, d = x1.shape
    nt = t // TD
    grid_spec = pltpu.PrefetchScalarGridSpec(
        num_scalar_prefetch=4,
        grid=(nt,),
        in_specs=[pl.BlockSpec((TD, d), lambda i, *_: (i, 0)),
                  pl.BlockSpec((TD, LANES), lambda i, *_: (i, 0)),
                  pl.BlockSpec((1, 1, LANES), lambda i, *_: (i, 0, 0)),
                  pl.BlockSpec(memory_space=pl.ANY),
                  pl.BlockSpec((1, d), lambda i, *_: (0, 0)),
                  pl.BlockSpec((1, d), lambda i, *_: (0, 0))],
        out_specs=pl.BlockSpec((TD, d), lambda i, *_: (i, 0)),
        scratch_shapes=[pltpu.VMEM((STAGE_ROWS, d), F32), pltpu.SemaphoreType.DMA(())],
    )
    return pl.pallas_call(
        functools.partial(_combine_kernel, alpha=alpha),
        grid_spec=grid_spec,
        out_shape=jax.ShapeDtypeStruct((t, d), F32),
        compiler_params=_cparams(1),
        name="combine",
    )(*tables, x1, route, soffrow, yb, lng, lnb)


def _max_blocks(t):
    nt = t // TD
    rows = 2 * t + (CHUNK - 1) * nt * N_EXPERTS + (MOE_BLOCK - CHUNK) * N_EXPERTS
    return -(-rows // MOE_BLOCK)


def _dispatch_tables(cnt, n_blocks):
    n = cnt[:, 0, :N_EXPERTS].astype(jnp.int32)
    cch = (n + CHUNK - 1) // CHUNK
    soff = jnp.cumsum(cch, axis=1) - cch
    tot_tile = jnp.sum(cch, axis=1)
    tot_e = jnp.sum(cch, axis=0)
    reg = (tot_e + CHUNKS_PER_BLOCK - 1) // CHUNKS_PER_BLOCK * CHUNKS_PER_BLOCK
    pend = jnp.cumsum(reg)
    pstart = pend - reg
    dch = pstart[None, :] + jnp.cumsum(cch, axis=0) - cch
    nb_used = pend[-1] // CHUNKS_PER_BLOCK
    blocks = jnp.arange(n_blocks, dtype=jnp.int32)
    be = jnp.searchsorted(pend, jnp.minimum(blocks, nb_used - 1) * CHUNKS_PER_BLOCK, side="right")
    block_e = jnp.minimum(be, N_EXPERTS - 1).astype(jnp.int32)
    soffrow = jnp.zeros((n.shape[0], 1, LANES), F32).at[:, 0, :N_EXPERTS].set(
        (soff * CHUNK).astype(F32))
    i32 = lambda a: a.reshape(-1).astype(jnp.int32)
    return dict(cch=i32(cch), soff=i32(soff), dch=i32(dch), tot=i32(tot_tile),
                padst=i32(pstart + tot_e), padn=i32(reg - tot_e),
                block_e=block_e, nb_used=i32(nb_used), soffrow=soffrow)


def _rope_tables(seq):
    inv = ROPE_THETA ** (-jnp.arange(0, HEAD_DIM, 2, dtype=F32) / HEAD_DIM)
    ang = jnp.arange(seq, dtype=F32)[:, None] * inv[None, :]
    cos, sin = jnp.cos(ang), jnp.sin(ang)
    return jnp.concatenate([cos, cos], axis=1), jnp.concatenate([-sin, sin], axis=1)


@jax.jit
def kernel(x, w_in, w_sink, w_conv, b_conv, w_rec_gate, b_rec_gate, w_in_gate, b_in_gate,
           lru_lambda, w_attn_o, w_rnn_o, w_out, ln_g, ln_b, w_router_group, b_router_group,
           w_router_expert, b_router_expert, w_exp_gate, w_exp_up, w_exp_down):
    bsz, seq, d = x.shape
    depth = w_in.shape[0]
    t = bsz * seq
    nblk = d // LANES
    alpha = (2 * depth) ** 0.25
    cosf, sinf = _rope_tables(seq)
    n_blocks = _max_blocks(t)
    x2 = x.reshape(t, d)
    for l in range(depth):
        q, k, v, xr, yr, ga, gr = _inproj(x2, w_in[l].astype(BF16), cosf, sinf, seq)
        o = _attention(q.reshape(bsz, seq, -1), k.reshape(bsz, seq, -1), v.reshape(bsz, seq, -1),
                       w_sink[l])
        wg_cat = jnp.concatenate([w_rec_gate[l, 0], w_in_gate[l, 0], w_rec_gate[l, 1],
                                  w_in_gate[l, 1]], axis=-1).astype(BF16)
        bg_cat = jnp.concatenate([b_rec_gate[l, 0].reshape(nblk, 1, LANES),
                                  b_in_gate[l, 0].reshape(nblk, 1, LANES),
                                  b_rec_gate[l, 1].reshape(nblk, 1, LANES),
                                  b_in_gate[l, 1].reshape(nblk, 1, LANES)], axis=-1)
        lam_cat = jnp.concatenate([lru_lambda[l, 0].reshape(nblk, 1, LANES),
                                   lru_lambda[l, 1].reshape(nblk, 1, LANES)], axis=-1)
        hy = _rnn(xr.reshape(bsz, seq, d), yr.reshape(bsz, seq, d), w_conv[l],
                  b_conv[l].reshape(nblk, 1, LANES), wg_cat, bg_cat, lam_cat)
        wr = jnp.zeros((d, LANES), F32)
        wr = wr.at[:, :N_EXPERTS].set(w_router_expert[l])
        wr = wr.at[:, N_EXPERTS:N_EXPERTS + N_GROUPS].set(w_router_group[l]).astype(BF16)
        br = jnp.zeros((1, LANES), F32)
        br = br.at[0, :N_EXPERTS].set(b_router_expert[l])
        br = br.at[0, N_EXPERTS:N_EXPERTS + N_GROUPS].set(b_router_group[l])
        x1, route, cnt = _mixout(x2, o.reshape(t, -1), hy.reshape(t, d), ga, gr,
                                 w_attn_o[l].astype(BF16), w_rnn_o[l].astype(BF16),
                                 w_out[l].astype(BF16), ln_g[l, 0].reshape(1, d),
                                 ln_b[l, 0].reshape(1, d), wr, br, alpha)
        tb = _dispatch_tables(cnt, n_blocks)
        xb = _dispatch(x1, route, tb["soffrow"],
                       (tb["cch"], tb["soff"], tb["dch"], tb["tot"], tb["padst"], tb["padn"],
                        tb["nb_used"]), n_blocks * MOE_BLOCK)
        yb = _moe(xb, tb["block_e"], tb["nb_used"], w_exp_gate[l], w_exp_up[l], w_exp_down[l],
                  n_blocks)
        x2 = _combine(x1, route, tb["soffrow"], (tb["cch"], tb["soff"], tb["dch"], tb["tot"]),
                      yb, ln_g[l, 1].reshape(1, d), ln_b[l, 1].reshape(1, d), alpha)
    return x2.reshape(bsz, seq, d)
```

```python
import functools
import math

import jax
import jax.numpy as jnp
from jax import lax
from jax.experimental import pallas as pl
from jax.experimental.pallas import tpu as pltpu

F32 = jnp.float32
BF16 = jnp.bfloat16

HEAD_DIM = 128
N_Q_HEADS = 8
N_KV_HEADS = 2
Q_PER_KV = N_Q_HEADS // N_KV_HEADS
WINDOW = 128
ROPE_THETA = 10000.0
CONV_W = 4
LRU_C = 8.0
N_GROUPS = 4
EXPERTS_PER_GROUP = 8
N_EXPERTS = N_GROUPS * EXPERTS_PER_GROUP
LN_EPS = 1e-5
NEG = -1e30

SUBLANES = 8
LANES = 128

TM_PROJ = 256
TQ_ATTN = 512
SEG = 256
SCAN_LEN = SEG + 4
SCAN_UNROLL = 10
TD = 256
CHUNK = SUBLANES
STAGE_ROWS = 2 * TD + CHUNK * N_EXPERTS
MOE_BLOCK = 128
CHUNKS_PER_BLOCK = MOE_BLOCK // CHUNK
EXP_TILE = 2 * MOE_BLOCK
VMEM_LIMIT = 56 * 1024 * 1024


def _cparams(n_axes):
    return pltpu.CompilerParams(dimension_semantics=("arbitrary",) * n_axes,
                                vmem_limit_bytes=VMEM_LIMIT)


def _softplus(z):
    e = jnp.exp(-jnp.abs(z))
    w = 1.0 + e
    tiny = w == 1.0
    log1p = jnp.where(tiny, e, jnp.log(w) * (e / jnp.where(tiny, 1.0, w - 1.0)))
    return jnp.maximum(z, 0.0) + log1p


def _sigmoid(x):
    return 0.5 * jnp.tanh(0.5 * x) + 0.5


def _gelu_tanh(y):
    c1 = math.sqrt(2.0 / math.pi)
    half = 0.5 * y
    return half + half * jnp.tanh(y * (c1 + (c1 * 0.044715) * (y * y)))


def _layer_norm(y, g, b):
    mu = jnp.mean(y, axis=-1, keepdims=True)
    d = y - mu
    var = jnp.mean(d * d, axis=-1, keepdims=True)
    return d * lax.rsqrt(var + LN_EPS) * g + b


def _inproj_kernel(x_ref, w_ref, cos_ref, sin_ref,
                   q_ref, k_ref, v_ref, xr_ref, yr_ref, ga_ref, gr_ref, *, d_model):
    xb = x_ref[...].astype(BF16)
    cos = cos_ref[...]
    sin = sin_ref[...]
    aw = N_Q_HEADS * HEAD_DIM
    kw = N_KV_HEADS * HEAD_DIM

    def proj(c0, n):
        return jnp.dot(xb, w_ref[:, c0:c0 + n], preferred_element_type=F32)

    def rope(t):
        return t * cos + pltpu.roll(t, HEAD_DIM // 2, 1) * sin

    zq = proj(0, aw)
    scale = HEAD_DIM ** -0.5
    for h in range(N_Q_HEADS):
        sl = slice(h * HEAD_DIM, (h + 1) * HEAD_DIM)
        q_ref[:, sl] = (rope(zq[:, sl]) * scale).astype(BF16)
    zk = proj(aw, kw)
    for h in range(N_KV_HEADS):
        sl = slice(h * HEAD_DIM, (h + 1) * HEAD_DIM)
        k_ref[:, sl] = rope(zk[:, sl]).astype(BF16)
    v_ref[...] = proj(aw + kw, kw).astype(BF16)
    c0 = aw + 2 * kw
    xr_ref[...] = proj(c0, d_model)
    yr_ref[...] = proj(c0 + d_model, d_model)
    ga_ref[...] = proj(c0 + 2 * d_model, d_model)
    gr_ref[...] = proj(c0 + 3 * d_model, d_model)


def _inproj(x2, w_in_b, cosf, sinf, seq):
    t, d = x2.shape
    n_in = w_in_b.shape[1]
    tm = TM_PROJ
    aw = N_Q_HEADS * HEAD_DIM
    kw = N_KV_HEADS * HEAD_DIM
    tiles_per_seq = seq // tm
    row = lambda i: (i, 0)
    pos = lambda i: (i % tiles_per_seq, 0)
    outs = [jax.ShapeDtypeStruct((t, aw), BF16), jax.ShapeDtypeStruct((t, kw), BF16),
            jax.ShapeDtypeStruct((t, kw), BF16)] + [jax.ShapeDtypeStruct((t, d), F32)] * 4
    return pl.pallas_call(
        functools.partial(_inproj_kernel, d_model=d),
        grid=(t // tm,),
        in_specs=[pl.BlockSpec((tm, d), row),
                  pl.BlockSpec((d, n_in), lambda i: (0, 0)),
                  pl.BlockSpec((tm, HEAD_DIM), pos),
                  pl.BlockSpec((tm, HEAD_DIM), pos)],
        out_specs=[pl.BlockSpec((tm, aw), row), pl.BlockSpec((tm, kw), row),
                   pl.BlockSpec((tm, kw), row)] + [pl.BlockSpec((tm, d), row)] * 4,
        out_shape=outs,
        compiler_params=_cparams(1),
        name="inproj",
    )(x2, w_in_b, cosf, sinf)


def _attn_kernel(sink_ref, q_ref, kp_ref, kc_ref, kn_ref, vp_ref, vc_ref, vn_ref, o_ref,
                 *, seq):
    i = pl.program_id(1)
    tq = q_ref.shape[1]
    blk = WINDOW
    t0 = i * tq
    kext = jnp.concatenate([kp_ref[0], kc_ref[0], kn_ref[0]], axis=0)
    vext = jnp.concatenate([vp_ref[0], vc_ref[0], vn_ref[0]], axis=0)
    nrow = Q_PER_KV * blk
    blk_shift = blk.bit_length() - 1
    qi = lax.broadcasted_iota(jnp.int32, (nrow, 3 * blk), 0) & (blk - 1)
    kj = lax.broadcasted_iota(jnp.int32, (nrow, 3 * blk), 1)
    rel = kj - blk - qi
    band = jnp.abs(rel) <= WINDOW
    rowg = lax.broadcasted_iota(jnp.int32, (nrow, 1), 0) >> blk_shift
    for j in range(tq // blk):
        kpos = t0 + j * blk - blk + kj
        valid = band & (kpos >= 0) & (kpos < seq)
        for h in range(N_KV_HEADS):
            hs = slice(h * HEAD_DIM, (h + 1) * HEAD_DIM)
            kblk = kext[j * blk:j * blk + 3 * blk, hs]
            vblk = vext[j * blk:j * blk + 3 * blk, hs]
            qs = [q_ref[0, j * blk:(j + 1) * blk,
                        (h * Q_PER_KV + g) * HEAD_DIM:(h * Q_PER_KV + g + 1) * HEAD_DIM]
                  for g in range(Q_PER_KV)]
            qblk = jnp.concatenate(qs, axis=0)
            s = lax.dot_general(qblk, kblk, (((1,), (1,)), ((), ())),
                                preferred_element_type=F32)
            s = jnp.where(valid, s, NEG)
            sk = jnp.full((nrow, 1), sink_ref[h * Q_PER_KV], F32)
            for g in range(1, Q_PER_KV):
                sk = jnp.where(rowg == g, sink_ref[h * Q_PER_KV + g], sk)
            m = jnp.maximum(jnp.max(s, axis=-1, keepdims=True), sk)
            p = jnp.exp(s - m)
            denom = jnp.sum(p, axis=-1, keepdims=True) + jnp.exp(sk - m)
            o = jnp.dot(p.astype(BF16), vblk, preferred_element_type=F32) / denom
            for g in range(Q_PER_KV):
                c = (h * Q_PER_KV + g) * HEAD_DIM
                o_ref[0, j * blk:(j + 1) * blk, c:c + HEAD_DIM] = (
                    o[g * blk:(g + 1) * blk].astype(BF16))


def _attention(q3, k3, v3, sink):
    b, s, aw = q3.shape
    kw = k3.shape[2]
    tq = TQ_ATTN
    blk = WINDOW
    r = tq // blk
    nblk = s // blk
    cur = lambda bi, i: (bi, i, 0)
    prev = lambda bi, i: (bi, jnp.maximum(i * r - 1, 0), 0)
    nxt = lambda bi, i: (bi, jnp.minimum((i + 1) * r, nblk - 1), 0)
    kv_specs = [pl.BlockSpec((1, blk, kw), prev), pl.BlockSpec((1, tq, kw), cur),
                pl.BlockSpec((1, blk, kw), nxt)]
    return pl.pallas_call(
        functools.partial(_attn_kernel, seq=s),
        grid=(b, s // tq),
        in_specs=[pl.BlockSpec(memory_space=pltpu.SMEM),
                  pl.BlockSpec((1, tq, aw), cur)] + kv_specs + kv_specs,
        out_specs=pl.BlockSpec((1, tq, aw), cur),
        out_shape=jax.ShapeDtypeStruct((b, s, aw), BF16),
        compiler_params=_cparams(2),
        name="attention",
    )(sink, q3, k3, k3, k3, v3, v3, v3)


def _rnn_kernel(xr_ref, yr_ref, wc_ref, bc_ref, wg_ref, bg_ref, lam_ref, out_ref,
                xpad_ref, af_ref, uf_ref, ab_ref, ub_ref, hb_ref, cf_ref, cb_ref, *, seq):
    nseg = seq // SEG
    nlane = nseg
    ngrp = nlane // SUBLANES
    hf_ref = xpad_ref
    wc = wc_ref[...]
    bc = bc_ref[0]
    wg = wg_ref[0]
    bg = 0.5 * bg_ref[0]
    lam = lam_ref[0]
    rate = (0.5 * LRU_C) * _softplus(-lam)
    a_refs = (af_ref, ab_ref)
    u_refs = (uf_ref, ub_ref)
    left = CONV_W // 2

    halo = jnp.zeros((SUBLANES, LANES), F32)
    xpad_ref[pl.ds(0, SUBLANES), :] = halo
    xpad_ref[pl.ds(seq + SUBLANES, SUBLANES), :] = halo

    def pad_copy(c, carry):
        t0 = pl.multiple_of(c * SEG, SEG)
        xpad_ref[pl.ds(t0 + SUBLANES, SEG), :] = xr_ref[0, pl.ds(t0, SEG), :]
        return carry

    lax.fori_loop(0, nseg, pad_copy, 0)

    tail = nlane * SCAN_LEN - seq
    for a_ref, u_ref in zip(a_refs, u_refs):
        a_ref[pl.ds(seq, tail), :] = jnp.ones((tail, LANES), F32)
        u_ref[pl.ds(seq, tail), :] = jnp.zeros((tail, LANES), F32)

    def gates(c, carry):
        t0 = pl.multiple_of(c * SEG, SEG)
        xc = bc
        for tap in range(CONV_W):
            xc = xc + xpad_ref[pl.ds(t0 + SUBLANES - left + tap, SEG), :] * wc[tap:tap + 1]
        gh = jnp.dot(xc.astype(BF16), wg, preferred_element_type=F32) + bg
        xch = 0.5 * xc
        for d in range(2):
            rt = rate[:, d * LANES:(d + 1) * LANES]
            nlog_a = rt * jnp.tanh(gh[:, (2 * d) * LANES:(2 * d + 1) * LANES]) + rt
            a = jnp.exp2(nlog_a * (-1.0 / math.log(2.0)))
            z = jnp.tanh(nlog_a) * (a * a + 1.0)
            mult = jnp.where(z > 0.0, z * lax.rsqrt(z), 0.0)
            in_gate2 = jnp.tanh(gh[:, (2 * d + 1) * LANES:(2 * d + 2) * LANES]) + 1.0
            a_refs[d][pl.ds(t0, SEG), :] = a
            u_refs[d][pl.ds(t0, SEG), :] = (xch * mult) * in_gate2
        return carry

    lax.fori_loop(0, nseg, gates, 0, unroll=2)

    def lane_rows(g, j):
        return pl.ds(g * SUBLANES * SCAN_LEN + j, SUBLANES, stride=SCAN_LEN)

    def totals_step(j, carry):
        hf, pf, hb, pb = carry
        jb = SCAN_LEN - 1 - j
        nhf, npf, nhb, npb = [], [], [], []
        for g in range(ngrp):
            a = af_ref[lane_rows(g, j), :]
            nhf.append(a * hf[g] + uf_ref[lane_rows(g, j), :])
            npf.append(a * pf[g])
            a = ab_ref[lane_rows(g, jb), :]
            nhb.append(a * hb[g] + ub_ref[lane_rows(g, jb), :])
            npb.append(a * pb[g])
        return tuple(nhf), tuple(npf), tuple(nhb), tuple(npb)

    zero = tuple(jnp.zeros((SUBLANES, LANES), F32) for _ in range(ngrp))
    one = tuple(jnp.ones((SUBLANES, LANES), F32) for _ in range(ngrp))
    hf, pf, hb, pb = lax.fori_loop(0, SCAN_LEN, totals_step, (zero, one, zero, one),
                                   unroll=SCAN_UNROLL)

    c = jnp.zeros((1, LANES), F32)
    for s in range(nlane):
        g, r = divmod(s, SUBLANES)
        cf_ref[s:s + 1, :] = c
        c = pf[g][r:r + 1] * c + hf[g][r:r + 1]
    c = jnp.zeros((1, LANES), F32)
    for s in range(nlane - 1, -1, -1):
        g, r = divmod(s, SUBLANES)
        cb_ref[s:s + 1, :] = c
        c = pb[g][r:r + 1] * c + hb[g][r:r + 1]

    def scan_step(j, carry):
        hf, hb = carry
        jb = SCAN_LEN - 1 - j
        nhf, nhb = [], []
        for g in range(ngrp):
            h = af_ref[lane_rows(g, j), :] * hf[g] + uf_ref[lane_rows(g, j), :]
            hf_ref[lane_rows(g, j), :] = h
            nhf.append(h)
            h = ab_ref[lane_rows(g, jb), :] * hb[g] + ub_ref[lane_rows(g, jb), :]
            hb_ref[lane_rows(g, jb), :] = h
            nhb.append(h)
        return tuple(nhf), tuple(nhb)

    hf0 = tuple(cf_ref[g * SUBLANES:(g + 1) * SUBLANES, :] for g in range(ngrp))
    hb0 = tuple(cb_ref[g * SUBLANES:(g + 1) * SUBLANES, :] for g in range(ngrp))
    lax.fori_loop(0, SCAN_LEN, scan_step, (hf0, hb0), unroll=SCAN_UNROLL)

    def finish(c, carry):
        t0 = pl.multiple_of(c * SEG, SEG)
        h = hf_ref[pl.ds(t0, SEG), :] + hb_ref[pl.ds(t0, SEG), :]
        out_ref[0, pl.ds(t0, SEG), :] = (h * _gelu_tanh(yr_ref[0, pl.ds(t0, SEG), :])).astype(BF16)
        return carry

    lax.fori_loop(0, nseg, finish, 0)


def _rnn(xr3, yr3, w_conv, b_conv, wg_cat, bg_cat, lam_cat):
    b, s, d = xr3.shape
    nblk = d // LANES
    nseg = s // SEG
    slab = lambda bi, n: (bi, 0, n)
    per_blk = lambda bi, n: (n, 0, 0)
    assert nseg % SUBLANES == 0 and nseg * SCAN_LEN >= s + 2 * SUBLANES
    scratch = ([pltpu.VMEM((nseg * SCAN_LEN, LANES), F32)] * 6
               + [pltpu.VMEM((nseg, LANES), F32)] * 2)
    return pl.pallas_call(
        functools.partial(_rnn_kernel, seq=s),
        grid=(b, nblk),
        in_specs=[pl.BlockSpec((1, s, LANES), slab),
                  pl.BlockSpec((1, s, LANES), slab),
                  pl.BlockSpec((CONV_W, LANES), lambda bi, n: (0, n)),
                  pl.BlockSpec((1, 1, LANES), per_blk),
                  pl.BlockSpec((1, LANES, 4 * LANES), per_blk),
                  pl.BlockSpec((1, 1, 4 * LANES), per_blk),
                  pl.BlockSpec((1, 1, 2 * LANES), per_blk)],
        out_specs=pl.BlockSpec((1, s, LANES), slab),
        out_shape=jax.ShapeDtypeStruct((b, s, d), BF16),
        scratch_shapes=scratch,
        compiler_params=_cparams(2),
        name="rglru",
    )(xr3, yr3, w_conv, b_conv, wg_cat, bg_cat, lam_cat)


def _route(logits):
    lane = lax.broadcasted_iota(jnp.int32, logits.shape, 1)
    lanef = lane.astype(F32)
    big = float(4 * LANES)
    gmask = (lane >= N_EXPERTS) & (lane < N_EXPERTS + N_GROUPS)
    gl = jnp.where(gmask, logits, NEG)
    gmax = jnp.max(gl, axis=-1, keepdims=True)
    ge = jnp.exp(gl - gmax)
    gprob = ge / jnp.sum(ge, axis=-1, keepdims=True)
    gval = jnp.max(gprob, axis=-1, keepdims=True)
    gidx = jnp.min(jnp.where((gprob == gval) & gmask, lanef, big), axis=-1, keepdims=True)
    gidx = gidx.astype(jnp.int32) - N_EXPERTS
    group_shift = EXPERTS_PER_GROUP.bit_length() - 1
    emask = (lane < N_EXPERTS) & ((lane >> group_shift) == gidx)
    el = jnp.where(emask, logits, NEG)
    m1 = jnp.max(el, axis=-1, keepdims=True)
    i1 = jnp.min(jnp.where((el == m1) & emask, lanef, big), axis=-1, keepdims=True)
    emask2 = emask & (lanef != i1)
    el2 = jnp.where(emask2, logits, NEG)
    m2 = jnp.max(el2, axis=-1, keepdims=True)
    i2 = jnp.min(jnp.where((el2 == m2) & emask2, lanef, big), axis=-1, keepdims=True)
    e2 = jnp.exp(m2 - m1)
    den = 1.0 + e2
    g1 = (1.0 / den) * gval
    g2 = (e2 / den) * gval
    route = jnp.where(lane == 0, i1, jnp.where(lane == 1, i2,
                      jnp.where(lane == 2, g1, jnp.where(lane == 3, g2, 0.0))))
    onehot = (lanef == i1).astype(F32) + (lanef == i2).astype(F32)
    return route, jnp.sum(onehot, axis=0, keepdims=True)


def _mixout_kernel(x_ref, o_ref, hy_ref, ga_ref, gr_ref, wao_ref, wro_ref, wout_ref,
                   lng_ref, lnb_ref, wr_ref, br_ref, x1_ref, route_ref, cnt_ref, *, alpha):
    ya = jnp.dot(o_ref[...], wao_ref[...], preferred_element_type=F32)
    yr = jnp.dot(hy_ref[...], wro_ref[...], preferred_element_type=F32)
    merged = _sigmoid(ga_ref[...]) * ya + _sigmoid(gr_ref[...]) * yr
    mix = jnp.dot(merged.astype(BF16), wout_ref[...], preferred_element_type=F32)
    x1 = _layer_norm(alpha * x_ref[...] + mix, lng_ref[...], lnb_ref[...])
    x1_ref[...] = x1
    logits = jnp.dot(x1.astype(BF16), wr_ref[...], preferred_element_type=F32) + br_ref[...]
    route, cnt = _route(logits)
    route_ref[...] = route
    cnt_ref[0] = cnt


def _mixout(x2, o2, hy2, ga, gr, wao, wro, wout, lng, lnb, wr, br, alpha):
    t, d = x2.shape
    tm = TD
    nt = t // tm
    row = lambda i: (i, 0)
    full = lambda i: (0, 0)
    return pl.pallas_call(
        functools.partial(_mixout_kernel, alpha=alpha),
        grid=(nt,),
        in_specs=[pl.BlockSpec((tm, d), row)] * 5
                 + [pl.BlockSpec((d, d), full)] * 3
                 + [pl.BlockSpec((1, d), full)] * 2
                 + [pl.BlockSpec((d, LANES), full), pl.BlockSpec((1, LANES), full)],
        out_specs=[pl.BlockSpec((tm, d), row), pl.BlockSpec((tm, LANES), row),
                   pl.BlockSpec((1, 1, LANES), lambda i: (i, 0, 0))],
        out_shape=[jax.ShapeDtypeStruct((t, d), F32), jax.ShapeDtypeStruct((t, LANES), F32),
                   jax.ShapeDtypeStruct((nt, 1, LANES), F32)],
        compiler_params=_cparams(1),
        name="mixout",
    )(x2, o2, hy2, ga, gr, wao, wro, wout, lng, lnb, wr, br)


def _perm_matrix(route, soff_row, w1, w2):
    td = route.shape[0]
    lane = lax.broadcasted_iota(jnp.int32, (td, LANES), 1).astype(F32)
    e1 = lane == route[:, 0:1]
    e2 = lane == route[:, 1:2]
    cnt = (e1.astype(F32) + e2.astype(F32)).astype(BF16)
    ti = lax.broadcasted_iota(jnp.int32, (td, td), 0)
    tj = lax.broadcasted_iota(jnp.int32, (td, td), 1)
    lower = (tj < ti).astype(BF16)
    pos = jnp.dot(lower, cnt, preferred_element_type=F32) + soff_row
    r1 = jnp.sum(jnp.where(e1, pos, 0.0), axis=-1, keepdims=True).astype(jnp.int32)
    r2 = jnp.sum(jnp.where(e2, pos, 0.0), axis=-1, keepdims=True).astype(jnp.int32)
    col = lax.broadcasted_iota(jnp.int32, (td, STAGE_ROWS), 1)
    return jnp.where(col == r1, w1, 0.0) + jnp.where(col == r2, w2, 0.0)


def _chunk_rows(c):
    return pl.ds(pl.multiple_of(c * CHUNK, CHUNK), CHUNK)


def _block_rows(b):
    return pl.ds(pl.multiple_of(b * MOE_BLOCK, MOE_BLOCK), MOE_BLOCK)


def _for_each_chunk(i, cch_ref, soff_ref, dch_ref, fn):
    def per_expert(e, carry):
        k = i * N_EXPERTS + e
        so = soff_ref[k]
        do = dch_ref[k]

        def per_chunk(c, carry2):
            fn(so + c, do + c)
            return carry2

        return lax.fori_loop(0, cch_ref[k], per_chunk, carry)

    lax.fori_loop(0, N_EXPERTS, per_expert, 0)


def _repeat(n, fn):
    def body(c, carry):
        fn()
        return carry

    lax.fori_loop(0, n, body, 0)


def _dispatch_kernel(cch_ref, soff_ref, dch_ref, tot_ref, padst_ref, padn_ref, nb_ref,
                     x_ref, route_ref, soffrow_ref, xb_ref, stage_ref, zero_ref, sems, zsem):
    i = pl.program_id(0)
    nt = pl.num_programs(0)
    n_blocks = xb_ref.shape[0] // MOE_BLOCK
    slot = i % 2

    def copy(s, src_chunk, dst_chunk):
        return pltpu.make_async_copy(stage_ref.at[s, _chunk_rows(src_chunk)],
                                     xb_ref.at[_chunk_rows(dst_chunk)], sems.at[s])

    @pl.when(i >= 2)
    def _():
        _repeat(tot_ref[jnp.maximum(i - 2, 0)], lambda: copy(slot, 0, 0).wait())

    pt = _perm_matrix(route_ref[...], soffrow_ref[0], 1.0, 1.0).astype(BF16)
    stage_ref[slot] = lax.dot_general(pt, x_ref[...].astype(BF16), (((0,), (0,)), ((), ())),
                                      preferred_element_type=F32)
    _for_each_chunk(i, cch_ref, soff_ref, dch_ref, lambda s, d: copy(slot, s, d).start())

    @pl.when(i == nt - 1)
    def _():
        @pl.when(i >= 1)
        def _():
            _repeat(tot_ref[jnp.maximum(i - 1, 0)], lambda: copy(1 - slot, 0, 0).wait())

        _repeat(tot_ref[i], lambda: copy(slot, 0, 0).wait())

        zero_ref[...] = jnp.zeros_like(zero_ref)

        def zcopy(dst_chunk):
            return pltpu.make_async_copy(zero_ref.at[pl.ds(0, CHUNK)],
                                         xb_ref.at[_chunk_rows(dst_chunk)], zsem)

        def pad_start(e, n):
            def pad_chunk(c, carry2):
                zcopy(padst_ref[e] + c).start()
                return carry2

            lax.fori_loop(0, padn_ref[e], pad_chunk, 0)
            return n + padn_ref[e]

        npad = lax.fori_loop(0, N_EXPERTS, pad_start, 0)
        _repeat(npad, lambda: zcopy(0).wait())

        def zblock(b):
            return pltpu.make_async_copy(zero_ref, xb_ref.at[_block_rows(b)], zsem)

        def start_block(b, carry):
            zblock(b).start()
            return carry

        def wait_block(b, carry):
            zblock(b).wait()
            return carry

        lax.fori_loop(nb_ref[0], n_blocks, start_block, 0)
        lax.fori_loop(nb_ref[0], n_blocks, wait_block, 0)


def _dispatch(x1, route, soffrow, tables, n_rows):
    t, d = x1.shape
    nt = t // TD
    grid_spec = pltpu.PrefetchScalarGridSpec(
        num_scalar_prefetch=7,
        grid=(nt,),
        in_specs=[pl.BlockSpec((TD, d), lambda i, *_: (i, 0)),
                  pl.BlockSpec((TD, LANES), lambda i, *_: (i, 0)),
                  pl.BlockSpec((1, 1, LANES), lambda i, *_: (i, 0, 0))],
        out_specs=pl.BlockSpec(memory_space=pl.ANY),
        scratch_shapes=[pltpu.VMEM((2, STAGE_ROWS, d), F32), pltpu.VMEM((MOE_BLOCK, d), F32),
                        pltpu.SemaphoreType.DMA((2,)), pltpu.SemaphoreType.DMA(())],
    )
    return pl.pallas_call(
        _dispatch_kernel,
        grid_spec=grid_spec,
        out_shape=jax.ShapeDtypeStruct((n_rows, d), F32),
        compiler_params=_cparams(1),
        name="dispatch",
    )(*tables, x1, route, soffrow)


def _experts_kernel(pst_ref, ntile_ref, nb_ref, xb_ref, wg_ref, wu_ref, wd_ref, yb_ref,
                    wgb_ref, wub_ref, wdb_ref, xbuf_ref, ybuf_ref, in_sems, out_sems, zsem):
    e = pl.program_id(0)
    n = ntile_ref[e]
    row0 = pst_ref[e]
    n_blocks = yb_ref.shape[0] // MOE_BLOCK

    def tile_rows(t):
        return pl.ds(pl.multiple_of(row0 + t * EXP_TILE, MOE_BLOCK), EXP_TILE)

    def in_copy(t, s):
        return pltpu.make_async_copy(xb_ref.at[tile_rows(t)], xbuf_ref.at[s], in_sems.at[s])

    def out_copy(t, s):
        return pltpu.make_async_copy(ybuf_ref.at[s], yb_ref.at[tile_rows(t)], out_sems.at[s])

    @pl.when(n > 0)
    def _():
        in_copy(0, 0).start()
        wgb_ref[...] = wg_ref[0, 0].astype(BF16)
        wub_ref[...] = wu_ref[0, 0].astype(BF16)
        wdb_ref[...] = wd_ref[0, 0].astype(BF16)

    def tile(t, carry):
        s = t % 2
        in_copy(t, s).wait()

        @pl.when(t + 1 < n)
        def _():
            in_copy(t + 1, 1 - s).start()

        @pl.when(t >= 2)
        def _():
            out_copy(t - 2, s).wait()

        x = xbuf_ref[s].astype(BF16)
        hg = jnp.dot(x, wgb_ref[...], preferred_element_type=F32)
        hu = jnp.dot(x, wub_ref[...], preferred_element_type=F32)
        hid = (hg * _sigmoid(hg)) * hu
        ybuf_ref[s] = jnp.dot(hid.astype(BF16), wdb_ref[...], preferred_element_type=F32)
        out_copy(t, s).start()
        return carry

    lax.fori_loop(0, n, tile, 0)

    @pl.when(n >= 2)
    def _():
        out_copy(n - 2, n % 2).wait()

    @pl.when(n >= 1)
    def _():
        out_copy(n - 1, (n - 1) % 2).wait()

    @pl.when(e == pl.num_programs(0) - 1)
    def _():
        ybuf_ref[0] = jnp.zeros(ybuf_ref.shape[1:], F32)

        def zblock(b):
            return pltpu.make_async_copy(ybuf_ref.at[0, pl.ds(0, MOE_BLOCK)],
                                         yb_ref.at[_block_rows(b)], zsem)

        def start_block(b, carry):
            zblock(b).start()
            return carry

        def wait_block(b, carry):
            zblock(b).wait()
            return carry

        lax.fori_loop(nb_ref[0], n_blocks, start_block, 0)
        lax.fori_loop(nb_ref[0], n_blocks, wait_block, 0)


def _experts(xb, pst, ntile, nb_used, w_g, w_u, w_d, layer):
    n_rows, d = xb.shape
    n_exp, de = w_g.shape[1], w_g.shape[3]
    wsel = lambda e, *_: (layer, e, 0, 0)
    grid_spec = pltpu.PrefetchScalarGridSpec(
        num_scalar_prefetch=3,
        grid=(n_exp,),
        in_specs=[pl.BlockSpec(memory_space=pl.ANY),
                  pl.BlockSpec((1, 1, d, de), wsel),
                  pl.BlockSpec((1, 1, d, de), wsel),
                  pl.BlockSpec((1, 1, de, d), wsel)],
        out_specs=pl.BlockSpec(memory_space=pl.ANY),
        scratch_shapes=[pltpu.VMEM((d, de), BF16), pltpu.VMEM((d, de), BF16),
                        pltpu.VMEM((de, d), BF16),
                        pltpu.VMEM((2, EXP_TILE, d), F32), pltpu.VMEM((2, EXP_TILE, d), F32),
                        pltpu.SemaphoreType.DMA((2,)), pltpu.SemaphoreType.DMA((2,)),
                        pltpu.SemaphoreType.DMA(())],
    )
    return pl.pallas_call(
        _experts_kernel,
        grid_spec=grid_spec,
        out_shape=jax.ShapeDtypeStruct((n_rows, d), F32),
        compiler_params=_cparams(1),
        name="experts",
    )(pst, ntile, nb_used, xb, w_g, w_u, w_d)


def _combine_kernel(cch_ref, soff_ref, dch_ref, tot_ref,
                    x_ref, route_ref, soffrow_ref, yb_ref, lng_ref, lnb_ref, out_ref,
                    stage_ref, sems, *, alpha):
    i = pl.program_id(0)
    nt = pl.num_programs(0)
    slot = i % 2

    def copy(s, src_chunk, dst_chunk):
        return pltpu.make_async_copy(yb_ref.at[_chunk_rows(src_chunk)],
                                     stage_ref.at[s, _chunk_rows(dst_chunk)], sems.at[s])

    def fetch(tile_idx, s):
        _for_each_chunk(tile_idx, cch_ref, soff_ref, dch_ref, lambda so, do: copy(s, do, so).start())

    @pl.when(i == 0)
    def _():
        stage_ref[...] = jnp.zeros_like(stage_ref)
        fetch(i, slot)

    @pl.when(i + 1 < nt)
    def _():
        fetch(i + 1, 1 - slot)

    _repeat(tot_ref[i], lambda: copy(slot, 0, 0).wait())

    route = route_ref[...]
    pt = _perm_matrix(route, soffrow_ref[0], route[:, 2:3], route[:, 3:4]).astype(BF16)
    ffn = jnp.dot(pt, stage_ref[slot].astype(BF16), preferred_element_type=F32)
    out_ref[...] = _layer_norm(alpha * x_ref[...] + ffn, lng_ref[...], lnb_ref[...])


def _combine(x1, route, soffrow, tables, yb, lng, lnb, alpha):
    t, d = x1.shape
    nt = t // TD
    grid_spec = pltpu.PrefetchScalarGridSpec(
        num_scalar_prefetch=4,
        grid=(nt,),
        in_specs=[pl.BlockSpec((TD, d), lambda i, *_: (i, 0)),
                  pl.BlockSpec((TD, LANES), lambda i, *_: (i, 0)),
                  pl.BlockSpec((1, 1, LANES), lambda i, *_: (i, 0, 0)),
                  pl.BlockSpec(memory_space=pl.ANY),
                  pl.BlockSpec((1, d), lambda i, *_: (0, 0)),
                  pl.BlockSpec((1, d), lambda i, *_: (0, 0))],
        out_specs=pl.BlockSpec((TD, d), lambda i, *_: (i, 0)),
        scratch_shapes=[pltpu.VMEM((2, STAGE_ROWS, d), F32), pltpu.SemaphoreType.DMA((2,))],
    )
    return pl.pallas_call(
        functools.partial(_combine_kernel, alpha=alpha),
        grid_spec=grid_spec,
        out_shape=jax.ShapeDtypeStruct((t, d), F32),
        compiler_params=_cparams(1),
        name="combine",
    )(*tables, x1, route, soffrow, yb, lng, lnb)


def _max_blocks(t):
    nt = t // TD
    rows = 2 * t + (CHUNK - 1) * nt * N_EXPERTS + (MOE_BLOCK - CHUNK) * N_EXPERTS
    return -(-rows // MOE_BLOCK)


def _dispatch_tables(cnt):
    n = cnt[:, 0, :N_EXPERTS].astype(jnp.int32)
    cch = (n + CHUNK - 1) // CHUNK
    soff = jnp.cumsum(cch, axis=1) - cch
    tot_tile = jnp.sum(cch, axis=1)
    tot_e = jnp.sum(cch, axis=0)
    reg = (tot_e + CHUNKS_PER_BLOCK - 1) // CHUNKS_PER_BLOCK * CHUNKS_PER_BLOCK
    pend = jnp.cumsum(reg)
    pstart = pend - reg
    dch = pstart[None, :] + jnp.cumsum(cch, axis=0) - cch
    nb_used = pend[-1] // CHUNKS_PER_BLOCK
    soffrow = jnp.zeros((n.shape[0], 1, LANES), F32).at[:, 0, :N_EXPERTS].set(
        (soff * CHUNK).astype(F32))
    i32 = lambda a: a.reshape(-1).astype(jnp.int32)
    return dict(cch=i32(cch), soff=i32(soff), dch=i32(dch), tot=i32(tot_tile),
                padst=i32(pstart + tot_e), padn=i32(reg - tot_e), nb_used=i32(nb_used),
                pst=i32(pstart * CHUNK), ntile=i32((reg * CHUNK + EXP_TILE - 1) // EXP_TILE),
                soffrow=soffrow)


def _rope_tables(seq):
    inv = ROPE_THETA ** (-jnp.arange(0, HEAD_DIM, 2, dtype=F32) / HEAD_DIM)
    ang = jnp.arange(seq, dtype=F32)[:, None] * inv[None, :]
    cos, sin = jnp.cos(ang), jnp.sin(ang)
    return jnp.concatenate([cos, cos], axis=1), jnp.concatenate([-sin, sin], axis=1)


@jax.jit
def kernel(x, w_in, w_sink, w_conv, b_conv, w_rec_gate, b_rec_gate, w_in_gate, b_in_gate,
           lru_lambda, w_attn_o, w_rnn_o, w_out, ln_g, ln_b, w_router_group, b_router_group,
           w_router_expert, b_router_expert, w_exp_gate, w_exp_up, w_exp_down):
    bsz, seq, d = x.shape
    depth = w_in.shape[0]
    t = bsz * seq
    nblk = d // LANES
    alpha = (2 * depth) ** 0.25
    cosf, sinf = _rope_tables(seq)
    n_rows = _max_blocks(t) * MOE_BLOCK + (EXP_TILE - MOE_BLOCK)
    x2 = x.reshape(t, d)
    for l in range(depth):
        q, k, v, xr, yr, ga, gr = _inproj(x2, w_in[l].astype(BF16), cosf, sinf, seq)
        o = _attention(q.reshape(bsz, seq, -1), k.reshape(bsz, seq, -1), v.reshape(bsz, seq, -1),
                       w_sink[l])
        wg_cat = (0.5 * jnp.concatenate([w_rec_gate[l, 0], w_in_gate[l, 0], w_rec_gate[l, 1],
                                         w_in_gate[l, 1]], axis=-1)).astype(BF16)
        bg_cat = jnp.concatenate([b_rec_gate[l, 0].reshape(nblk, 1, LANES),
                                  b_in_gate[l, 0].reshape(nblk, 1, LANES),
                                  b_rec_gate[l, 1].reshape(nblk, 1, LANES),
                                  b_in_gate[l, 1].reshape(nblk, 1, LANES)], axis=-1)
        lam_cat = jnp.concatenate([lru_lambda[l, 0].reshape(nblk, 1, LANES),
                                   lru_lambda[l, 1].reshape(nblk, 1, LANES)], axis=-1)
        hy = _rnn(xr.reshape(bsz, seq, d), yr.reshape(bsz, seq, d), w_conv[l],
                  b_conv[l].reshape(nblk, 1, LANES), wg_cat, bg_cat, lam_cat)
        wr = jnp.zeros((d, LANES), F32)
        wr = wr.at[:, :N_EXPERTS].set(w_router_expert[l])
        wr = wr.at[:, N_EXPERTS:N_EXPERTS + N_GROUPS].set(w_router_group[l]).astype(BF16)
        br = jnp.zeros((1, LANES), F32)
        br = br.at[0, :N_EXPERTS].set(b_router_expert[l])
        br = br.at[0, N_EXPERTS:N_EXPERTS + N_GROUPS].set(b_router_group[l])
        x1, route, cnt = _mixout(x2, o.reshape(t, -1), hy.reshape(t, d), ga, gr,
                                 w_attn_o[l].astype(BF16), w_rnn_o[l].astype(BF16),
                                 w_out[l].astype(BF16), ln_g[l, 0].reshape(1, d),
                                 ln_b[l, 0].reshape(1, d), wr, br, alpha)
        tb = _dispatch_tables(cnt)
        xb = _dispatch(x1, route, tb["soffrow"],
                       (tb["cch"], tb["soff"], tb["dch"], tb["tot"], tb["padst"], tb["padn"],
                        tb["nb_used"]), n_rows)
        yb = _experts(xb, tb["pst"], tb["ntile"], tb["nb_used"], w_exp_gate, w_exp_up,
                      w_exp_down, l)
        x2 = _combine(x1, route, tb["soffrow"], (tb["cch"], tb["soff"], tb["dch"], tb["tot"]),
                      yb, ln_g[l, 1].reshape(1, d), ln_b[l, 1].reshape(1, d), alpha)
    return x2.reshape(bsz, seq, d)
```

```python
import functools
import math

import jax
import jax.numpy as jnp
from jax import lax
from jax.experimental import pallas as pl
from jax.experimental.pallas import tpu as pltpu

F32 = jnp.float32
BF16 = jnp.bfloat16

HEAD_DIM = 128
N_Q_HEADS = 8
N_KV_HEADS = 2
Q_PER_KV = N_Q_HEADS // N_KV_HEADS
WINDOW = 128
ROPE_THETA = 10000.0
CONV_W = 4
LRU_C = 8.0
N_GROUPS = 4
EXPERTS_PER_GROUP = 8
N_EXPERTS = N_GROUPS * EXPERTS_PER_GROUP
LN_EPS = 1e-5
NEG = -1e30

SUBLANES = 8
LANES = 128

TM_PROJ = 256
TQ_ATTN = 512
SEG = 256
SCAN_LEN = SEG + 4
SCAN_UNROLL = 10
BF16_ROWS = 2 * SUBLANES
TD = 512
CHUNK = BF16_ROWS
STAGE_ROWS = 2 * TD + CHUNK * N_EXPERTS
MOE_BLOCK = 128
CHUNKS_PER_BLOCK = MOE_BLOCK // CHUNK
EXP_TILE = 2 * MOE_BLOCK
VMEM_LIMIT = 56 * 1024 * 1024


def _cparams(n_axes):
    return pltpu.CompilerParams(dimension_semantics=("arbitrary",) * n_axes,
                                vmem_limit_bytes=VMEM_LIMIT)


def _softplus(z):
    e = jnp.exp(-jnp.abs(z))
    w = 1.0 + e
    tiny = w == 1.0
    log1p = jnp.where(tiny, e, jnp.log(w) * (e / jnp.where(tiny, 1.0, w - 1.0)))
    return jnp.maximum(z, 0.0) + log1p


def _sigmoid(x):
    return 0.5 * jnp.tanh(0.5 * x) + 0.5


def _gelu_tanh(y):
    c1 = math.sqrt(2.0 / math.pi)
    half = 0.5 * y
    return half + half * jnp.tanh(y * (c1 + (c1 * 0.044715) * (y * y)))


def _layer_norm(y, g, b):
    mu = jnp.mean(y, axis=-1, keepdims=True)
    d = y - mu
    var = jnp.mean(d * d, axis=-1, keepdims=True)
    return d * lax.rsqrt(var + LN_EPS) * g + b


def _inproj_kernel(x_ref, w_ref, cos_ref, sin_ref,
                   q_ref, k_ref, v_ref, xr_ref, yr_ref, *, d_model):
    xb = x_ref[...].astype(BF16)
    cos = cos_ref[...]
    sin = sin_ref[...]
    aw = N_Q_HEADS * HEAD_DIM
    kw = N_KV_HEADS * HEAD_DIM

    def proj(c0, n):
        return jnp.dot(xb, w_ref[:, c0:c0 + n], preferred_element_type=F32)

    def rope(t):
        return t * cos + pltpu.roll(t, HEAD_DIM // 2, 1) * sin

    zq = proj(0, aw)
    scale = HEAD_DIM ** -0.5
    for h in range(N_Q_HEADS):
        sl = slice(h * HEAD_DIM, (h + 1) * HEAD_DIM)
        q_ref[:, sl] = (rope(zq[:, sl]) * scale).astype(BF16)
    zk = proj(aw, kw)
    for h in range(N_KV_HEADS):
        sl = slice(h * HEAD_DIM, (h + 1) * HEAD_DIM)
        k_ref[:, sl] = rope(zk[:, sl]).astype(BF16)
    v_ref[...] = proj(aw + kw, kw).astype(BF16)
    c0 = aw + 2 * kw
    xr_ref[...] = proj(c0, d_model)
    yr_ref[...] = proj(c0 + d_model, d_model)


def _inproj(x2, w_in_b, cosf, sinf, seq):
    t, d = x2.shape
    n_in = w_in_b.shape[1]
    tm = TM_PROJ
    aw = N_Q_HEADS * HEAD_DIM
    kw = N_KV_HEADS * HEAD_DIM
    tiles_per_seq = seq // tm
    row = lambda i: (i, 0)
    pos = lambda i: (i % tiles_per_seq, 0)
    outs = [jax.ShapeDtypeStruct((t, aw), BF16), jax.ShapeDtypeStruct((t, kw), BF16),
            jax.ShapeDtypeStruct((t, kw), BF16)] + [jax.ShapeDtypeStruct((t, d), F32)] * 2
    return pl.pallas_call(
        functools.partial(_inproj_kernel, d_model=d),
        grid=(t // tm,),
        in_specs=[pl.BlockSpec((tm, d), row),
                  pl.BlockSpec((d, n_in), lambda i: (0, 0)),
                  pl.BlockSpec((tm, HEAD_DIM), pos),
                  pl.BlockSpec((tm, HEAD_DIM), pos)],
        out_specs=[pl.BlockSpec((tm, aw), row), pl.BlockSpec((tm, kw), row),
                   pl.BlockSpec((tm, kw), row)] + [pl.BlockSpec((tm, d), row)] * 2,
        out_shape=outs,
        compiler_params=_cparams(1),
        name="inproj",
    )(x2, w_in_b, cosf, sinf)


def _attn_kernel(sink_ref, q_ref, kp_ref, kc_ref, kn_ref, vp_ref, vc_ref, vn_ref, o_ref,
                 *, seq):
    i = pl.program_id(1)
    tq = q_ref.shape[1]
    blk = WINDOW
    t0 = i * tq
    kext = jnp.concatenate([kp_ref[0], kc_ref[0], kn_ref[0]], axis=0)
    vext = jnp.concatenate([vp_ref[0], vc_ref[0], vn_ref[0]], axis=0)
    nrow = Q_PER_KV * blk
    blk_shift = blk.bit_length() - 1
    qi = lax.broadcasted_iota(jnp.int32, (nrow, 3 * blk), 0) & (blk - 1)
    kj = lax.broadcasted_iota(jnp.int32, (nrow, 3 * blk), 1)
    band = jnp.where(jnp.abs(kj - blk - qi) <= WINDOW, 0.0, NEG)
    kj_row = lax.broadcasted_iota(jnp.int32, (1, 3 * blk), 1)
    rowg = lax.broadcasted_iota(jnp.int32, (nrow, 1), 0) >> blk_shift
    for j in range(tq // blk):
        kpos = t0 + j * blk - blk + kj_row
        bias = band + jnp.where((kpos >= 0) & (kpos < seq), 0.0, NEG)
        for h in range(N_KV_HEADS):
            hs = slice(h * HEAD_DIM, (h + 1) * HEAD_DIM)
            kblk = kext[j * blk:j * blk + 3 * blk, hs]
            vblk = vext[j * blk:j * blk + 3 * blk, hs]
            qs = [q_ref[0, j * blk:(j + 1) * blk,
                        (h * Q_PER_KV + g) * HEAD_DIM:(h * Q_PER_KV + g + 1) * HEAD_DIM]
                  for g in range(Q_PER_KV)]
            qblk = jnp.concatenate(qs, axis=0)
            s = lax.dot_general(qblk, kblk, (((1,), (1,)), ((), ())),
                                preferred_element_type=F32) + bias
            sk = jnp.full((nrow, 1), sink_ref[h * Q_PER_KV], F32)
            for g in range(1, Q_PER_KV):
                sk = jnp.where(rowg == g, sink_ref[h * Q_PER_KV + g], sk)
            m = jnp.maximum(jnp.max(s, axis=-1, keepdims=True), sk)
            p = jnp.exp(s - m)
            denom = jnp.sum(p, axis=-1, keepdims=True) + jnp.exp(sk - m)
            o = jnp.dot(p.astype(BF16), vblk, preferred_element_type=F32) / denom
            for g in range(Q_PER_KV):
                c = (h * Q_PER_KV + g) * HEAD_DIM
                o_ref[0, j * blk:(j + 1) * blk, c:c + HEAD_DIM] = (
                    o[g * blk:(g + 1) * blk].astype(BF16))


def _attention(q3, k3, v3, sink):
    b, s, aw = q3.shape
    kw = k3.shape[2]
    tq = TQ_ATTN
    blk = WINDOW
    r = tq // blk
    nblk = s // blk
    cur = lambda bi, i: (bi, i, 0)
    prev = lambda bi, i: (bi, jnp.maximum(i * r - 1, 0), 0)
    nxt = lambda bi, i: (bi, jnp.minimum((i + 1) * r, nblk - 1), 0)
    kv_specs = [pl.BlockSpec((1, blk, kw), prev), pl.BlockSpec((1, tq, kw), cur),
                pl.BlockSpec((1, blk, kw), nxt)]
    return pl.pallas_call(
        functools.partial(_attn_kernel, seq=s),
        grid=(b, s // tq),
        in_specs=[pl.BlockSpec(memory_space=pltpu.SMEM),
                  pl.BlockSpec((1, tq, aw), cur)] + kv_specs + kv_specs,
        out_specs=pl.BlockSpec((1, tq, aw), cur),
        out_shape=jax.ShapeDtypeStruct((b, s, aw), BF16),
        compiler_params=_cparams(2),
        name="attention",
    )(sink, q3, k3, k3, k3, v3, v3, v3)


def _rnn_kernel(xr_ref, yr_ref, wc_ref, bc_ref, wg_ref, bg_ref, lam_ref, out_ref,
                xpad_ref, af_ref, uf_ref, ab_ref, ub_ref, hb_ref, cf_ref, cb_ref, *, seq):
    nseg = seq // SEG
    nlane = nseg
    ngrp = nlane // SUBLANES
    hf_ref = xpad_ref
    wc = wc_ref[...]
    bc = bc_ref[0]
    wg = wg_ref[0]
    bg = 0.5 * bg_ref[0]
    lam = lam_ref[0]
    rate = (0.5 * LRU_C) * _softplus(-lam)
    a_refs = (af_ref, ab_ref)
    u_refs = (uf_ref, ub_ref)
    left = CONV_W // 2

    halo = jnp.zeros((SUBLANES, LANES), F32)
    xpad_ref[pl.ds(0, SUBLANES), :] = halo
    xpad_ref[pl.ds(seq + SUBLANES, SUBLANES), :] = halo

    def pad_copy(c, carry):
        t0 = pl.multiple_of(c * SEG, SEG)
        xpad_ref[pl.ds(t0 + SUBLANES, SEG), :] = xr_ref[0, pl.ds(t0, SEG), :]
        return carry

    lax.fori_loop(0, nseg, pad_copy, 0)

    tail = nlane * SCAN_LEN - seq
    for a_ref, u_ref in zip(a_refs, u_refs):
        a_ref[pl.ds(seq, tail), :] = jnp.ones((tail, LANES), F32)
        u_ref[pl.ds(seq, tail), :] = jnp.zeros((tail, LANES), F32)

    def gates(c, carry):
        t0 = pl.multiple_of(c * SEG, SEG)
        xc = bc
        for tap in range(CONV_W):
            xc = xc + xpad_ref[pl.ds(t0 + SUBLANES - left + tap, SEG), :] * wc[tap:tap + 1]
        gh = jnp.dot(xc.astype(BF16), wg, preferred_element_type=F32) + bg
        xch = 0.5 * xc
        for d in range(2):
            rt = rate[:, d * LANES:(d + 1) * LANES]
            nlog_a = rt * jnp.tanh(gh[:, (2 * d) * LANES:(2 * d + 1) * LANES]) + rt
            a = jnp.exp2(nlog_a * (-1.0 / math.log(2.0)))
            z = jnp.tanh(nlog_a) * (a * a + 1.0)
            mult = jnp.where(z > 0.0, z * lax.rsqrt(z), 0.0)
            in_gate2 = jnp.tanh(gh[:, (2 * d + 1) * LANES:(2 * d + 2) * LANES]) + 1.0
            a_refs[d][pl.ds(t0, SEG), :] = a
            u_refs[d][pl.ds(t0, SEG), :] = (xch * mult) * in_gate2
        return carry

    lax.fori_loop(0, nseg, gates, 0, unroll=2)

    def lane_rows(g, j):
        return pl.ds(g * SUBLANES * SCAN_LEN + j, SUBLANES, stride=SCAN_LEN)

    def totals_step(j, carry):
        hf, pf, hb, pb = carry
        jb = SCAN_LEN - 1 - j
        nhf, npf, nhb, npb = [], [], [], []
        for g in range(ngrp):
            a = af_ref[lane_rows(g, j), :]
            nhf.append(a * hf[g] + uf_ref[lane_rows(g, j), :])
            npf.append(a * pf[g])
            a = ab_ref[lane_rows(g, jb), :]
            nhb.append(a * hb[g] + ub_ref[lane_rows(g, jb), :])
            npb.append(a * pb[g])
        return tuple(nhf), tuple(npf), tuple(nhb), tuple(npb)

    zero = tuple(jnp.zeros((SUBLANES, LANES), F32) for _ in range(ngrp))
    one = tuple(jnp.ones((SUBLANES, LANES), F32) for _ in range(ngrp))
    hf, pf, hb, pb = lax.fori_loop(0, SCAN_LEN, totals_step, (zero, one, zero, one),
                                   unroll=SCAN_UNROLL)

    c = jnp.zeros((1, LANES), F32)
    for s in range(nlane):
        g, r = divmod(s, SUBLANES)
        cf_ref[s:s + 1, :] = c
        c = pf[g][r:r + 1] * c + hf[g][r:r + 1]
    c = jnp.zeros((1, LANES), F32)
    for s in range(nlane - 1, -1, -1):
        g, r = divmod(s, SUBLANES)
        cb_ref[s:s + 1, :] = c
        c = pb[g][r:r + 1] * c + hb[g][r:r + 1]

    def scan_step(j, carry):
        hf, hb = carry
        jb = SCAN_LEN - 1 - j
        nhf, nhb = [], []
        for g in range(ngrp):
            h = af_ref[lane_rows(g, j), :] * hf[g] + uf_ref[lane_rows(g, j), :]
            hf_ref[lane_rows(g, j), :] = h
            nhf.append(h)
            h = ab_ref[lane_rows(g, jb), :] * hb[g] + ub_ref[lane_rows(g, jb), :]
            hb_ref[lane_rows(g, jb), :] = h
            nhb.append(h)
        return tuple(nhf), tuple(nhb)

    hf0 = tuple(cf_ref[g * SUBLANES:(g + 1) * SUBLANES, :] for g in range(ngrp))
    hb0 = tuple(cb_ref[g * SUBLANES:(g + 1) * SUBLANES, :] for g in range(ngrp))
    lax.fori_loop(0, SCAN_LEN, scan_step, (hf0, hb0), unroll=SCAN_UNROLL)

    def finish(c, carry):
        t0 = pl.multiple_of(c * SEG, SEG)
        h = hf_ref[pl.ds(t0, SEG), :] + hb_ref[pl.ds(t0, SEG), :]
        out_ref[0, pl.ds(t0, SEG), :] = (h * _gelu_tanh(yr_ref[0, pl.ds(t0, SEG), :])).astype(BF16)
        return carry

    lax.fori_loop(0, nseg, finish, 0)


def _rnn(xr3, yr3, w_conv, b_conv, wg_cat, bg_cat, lam_cat):
    b, s, d = xr3.shape
    nblk = d // LANES
    nseg = s // SEG
    slab = lambda bi, n: (bi, 0, n)
    per_blk = lambda bi, n: (n, 0, 0)
    assert nseg % SUBLANES == 0 and nseg * SCAN_LEN >= s + 2 * SUBLANES
    scratch = ([pltpu.VMEM((nseg * SCAN_LEN, LANES), F32)] * 6
               + [pltpu.VMEM((nseg, LANES), F32)] * 2)
    return pl.pallas_call(
        functools.partial(_rnn_kernel, seq=s),
        grid=(b, nblk),
        in_specs=[pl.BlockSpec((1, s, LANES), slab),
                  pl.BlockSpec((1, s, LANES), slab),
                  pl.BlockSpec((CONV_W, LANES), lambda bi, n: (0, n)),
                  pl.BlockSpec((1, 1, LANES), per_blk),
                  pl.BlockSpec((1, LANES, 4 * LANES), per_blk),
                  pl.BlockSpec((1, 1, 4 * LANES), per_blk),
                  pl.BlockSpec((1, 1, 2 * LANES), per_blk)],
        out_specs=pl.BlockSpec((1, s, LANES), slab),
        out_shape=jax.ShapeDtypeStruct((b, s, d), BF16),
        scratch_shapes=scratch,
        compiler_params=_cparams(2),
        name="rglru",
    )(xr3, yr3, w_conv, b_conv, wg_cat, bg_cat, lam_cat)


def _route(logits):
    lane = lax.broadcasted_iota(jnp.int32, logits.shape, 1)
    lanef = lane.astype(F32)
    big = float(4 * LANES)
    gmask = (lane >= N_EXPERTS) & (lane < N_EXPERTS + N_GROUPS)
    gl = jnp.where(gmask, logits, NEG)
    gmax = jnp.max(gl, axis=-1, keepdims=True)
    ge = jnp.exp(gl - gmax)
    gprob = ge / jnp.sum(ge, axis=-1, keepdims=True)
    gval = jnp.max(gprob, axis=-1, keepdims=True)
    gidx = jnp.min(jnp.where((gprob == gval) & gmask, lanef, big), axis=-1, keepdims=True)
    gidx = gidx.astype(jnp.int32) - N_EXPERTS
    group_shift = EXPERTS_PER_GROUP.bit_length() - 1
    emask = (lane < N_EXPERTS) & ((lane >> group_shift) == gidx)
    el = jnp.where(emask, logits, NEG)
    m1 = jnp.max(el, axis=-1, keepdims=True)
    i1 = jnp.min(jnp.where((el == m1) & emask, lanef, big), axis=-1, keepdims=True)
    emask2 = emask & (lanef != i1)
    el2 = jnp.where(emask2, logits, NEG)
    m2 = jnp.max(el2, axis=-1, keepdims=True)
    i2 = jnp.min(jnp.where((el2 == m2) & emask2, lanef, big), axis=-1, keepdims=True)
    e2 = jnp.exp(m2 - m1)
    den = 1.0 + e2
    g1 = (1.0 / den) * gval
    g2 = (e2 / den) * gval
    route = jnp.where(lane == 0, i1, jnp.where(lane == 1, i2,
                      jnp.where(lane == 2, g1, jnp.where(lane == 3, g2, 0.0))))
    onehot = (lanef == i1).astype(F32) + (lanef == i2).astype(F32)
    return route, jnp.sum(onehot, axis=0, keepdims=True)


def _mixout_kernel(x_ref, o_ref, hy_ref, wmg_ref, wao_ref, wro_ref, wout_ref,
                   lng_ref, lnb_ref, wr_ref, br_ref, x1_ref, route_ref, cnt_ref, *, alpha):
    d = x_ref.shape[1]
    x = x_ref[...]
    xb = x.astype(BF16)
    ga = jnp.dot(xb, wmg_ref[:, :d], preferred_element_type=F32)
    gr = jnp.dot(xb, wmg_ref[:, d:], preferred_element_type=F32)
    ya = jnp.dot(o_ref[...], wao_ref[...], preferred_element_type=F32)
    yr = jnp.dot(hy_ref[...], wro_ref[...], preferred_element_type=F32)
    merged = _sigmoid(ga) * ya + _sigmoid(gr) * yr
    mix = jnp.dot(merged.astype(BF16), wout_ref[...], preferred_element_type=F32)
    x1 = _layer_norm(alpha * x + mix, lng_ref[...], lnb_ref[...])
    x1_ref[...] = x1
    logits = jnp.dot(x1.astype(BF16), wr_ref[...], preferred_element_type=F32) + br_ref[...]
    route, cnt = _route(logits)
    route_ref[...] = route
    cnt_ref[0] = cnt


def _mixout(x2, o2, hy2, wmg, wao, wro, wout, lng, lnb, wr, br, alpha):
    t, d = x2.shape
    tm = TD
    nt = t // tm
    row = lambda i: (i, 0)
    full = lambda i: (0, 0)
    return pl.pallas_call(
        functools.partial(_mixout_kernel, alpha=alpha),
        grid=(nt,),
        in_specs=[pl.BlockSpec((tm, d), row)] * 3
                 + [pl.BlockSpec((d, 2 * d), full)]
                 + [pl.BlockSpec((d, d), full)] * 3
                 + [pl.BlockSpec((1, d), full)] * 2
                 + [pl.BlockSpec((d, LANES), full), pl.BlockSpec((1, LANES), full)],
        out_specs=[pl.BlockSpec((tm, d), row), pl.BlockSpec((tm, LANES), row),
                   pl.BlockSpec((1, 1, LANES), lambda i: (i, 0, 0))],
        out_shape=[jax.ShapeDtypeStruct((t, d), F32), jax.ShapeDtypeStruct((t, LANES), F32),
                   jax.ShapeDtypeStruct((nt, 1, LANES), F32)],
        compiler_params=_cparams(1),
        name="mixout",
    )(x2, o2, hy2, wmg, wao, wro, wout, lng, lnb, wr, br)


def _perm_matrix(route, soff_row, w1, w2):
    td = route.shape[0]
    lane = lax.broadcasted_iota(jnp.int32, (td, LANES), 1).astype(F32)
    e1 = lane == route[:, 0:1]
    e2 = lane == route[:, 1:2]
    cnt = (e1.astype(F32) + e2.astype(F32)).astype(BF16)
    ti = lax.broadcasted_iota(jnp.int32, (td, td), 0)
    tj = lax.broadcasted_iota(jnp.int32, (td, td), 1)
    lower = (tj < ti).astype(BF16)
    pos = jnp.dot(lower, cnt, preferred_element_type=F32) + soff_row
    r1 = jnp.sum(jnp.where(e1, pos, 0.0), axis=-1, keepdims=True).astype(jnp.int32)
    r2 = jnp.sum(jnp.where(e2, pos, 0.0), axis=-1, keepdims=True).astype(jnp.int32)
    col = lax.broadcasted_iota(jnp.int32, (td, STAGE_ROWS), 1)
    return jnp.where(col == r1, w1, 0.0) + jnp.where(col == r2, w2, 0.0)


def _chunk_rows(c):
    return pl.ds(pl.multiple_of(c * CHUNK, CHUNK), CHUNK)


def _block_rows(b):
    return pl.ds(pl.multiple_of(b * MOE_BLOCK, MOE_BLOCK), MOE_BLOCK)


def _for_each_chunk(i, cch_ref, soff_ref, dch_ref, fn):
    def per_expert(e, carry):
        k = i * N_EXPERTS + e
        so = soff_ref[k]
        do = dch_ref[k]

        def per_chunk(c, carry2):
            fn(so + c, do + c)
            return carry2

        return lax.fori_loop(0, cch_ref[k], per_chunk, carry)

    lax.fori_loop(0, N_EXPERTS, per_expert, 0)


def _repeat(n, fn):
    def body(c, carry):
        fn()
        return carry

    lax.fori_loop(0, n, body, 0)


def _dispatch_kernel(cch_ref, soff_ref, dch_ref, tot_ref, padst_ref, padn_ref, nb_ref,
                     x_ref, route_ref, soffrow_ref, xb_ref, stage_ref, zero_ref, sems, zsem):
    i = pl.program_id(0)
    nt = pl.num_programs(0)
    n_blocks = xb_ref.shape[0] // MOE_BLOCK
    slot = i % 2

    def copy(s, src_chunk, dst_chunk):
        return pltpu.make_async_copy(stage_ref.at[s, _chunk_rows(src_chunk)],
                                     xb_ref.at[_chunk_rows(dst_chunk)], sems.at[s])

    @pl.when(i >= 2)
    def _():
        _repeat(tot_ref[jnp.maximum(i - 2, 0)], lambda: copy(slot, 0, 0).wait())

    pt = _perm_matrix(route_ref[...], soffrow_ref[0], 1.0, 1.0).astype(BF16)
    stage_ref[slot] = lax.dot_general(pt, x_ref[...].astype(BF16), (((0,), (0,)), ((), ())),
                                      preferred_element_type=F32).astype(BF16)
    _for_each_chunk(i, cch_ref, soff_ref, dch_ref, lambda s, d: copy(slot, s, d).start())

    @pl.when(i == nt - 1)
    def _():
        @pl.when(i >= 1)
        def _():
            _repeat(tot_ref[jnp.maximum(i - 1, 0)], lambda: copy(1 - slot, 0, 0).wait())

        _repeat(tot_ref[i], lambda: copy(slot, 0, 0).wait())

        zero_ref[...] = jnp.zeros_like(zero_ref)

        def zcopy(dst_chunk):
            return pltpu.make_async_copy(zero_ref.at[pl.ds(0, CHUNK)],
                                         xb_ref.at[_chunk_rows(dst_chunk)], zsem)

        def pad_start(e, n):
            def pad_chunk(c, carry2):
                zcopy(padst_ref[e] + c).start()
                return carry2

            lax.fori_loop(0, padn_ref[e], pad_chunk, 0)
            return n + padn_ref[e]

        npad = lax.fori_loop(0, N_EXPERTS, pad_start, 0)
        _repeat(npad, lambda: zcopy(0).wait())

        def zblock(b):
            return pltpu.make_async_copy(zero_ref, xb_ref.at[_block_rows(b)], zsem)

        def start_block(b, carry):
            zblock(b).start()
            return carry

        def wait_block(b, carry):
            zblock(b).wait()
            return carry

        lax.fori_loop(nb_ref[0], n_blocks, start_block, 0)
        lax.fori_loop(nb_ref[0], n_blocks, wait_block, 0)


def _dispatch(x1, route, soffrow, tables, n_rows):
    t, d = x1.shape
    nt = t // TD
    grid_spec = pltpu.PrefetchScalarGridSpec(
        num_scalar_prefetch=7,
        grid=(nt,),
        in_specs=[pl.BlockSpec((TD, d), lambda i, *_: (i, 0)),
                  pl.BlockSpec((TD, LANES), lambda i, *_: (i, 0)),
                  pl.BlockSpec((1, 1, LANES), lambda i, *_: (i, 0, 0))],
        out_specs=pl.BlockSpec(memory_space=pl.ANY),
        scratch_shapes=[pltpu.VMEM((2, STAGE_ROWS, d), BF16), pltpu.VMEM((MOE_BLOCK, d), BF16),
                        pltpu.SemaphoreType.DMA((2,)), pltpu.SemaphoreType.DMA(())],
    )
    return pl.pallas_call(
        _dispatch_kernel,
        grid_spec=grid_spec,
        out_shape=jax.ShapeDtypeStruct((n_rows, d), BF16),
        compiler_params=_cparams(1),
        name="dispatch",
    )(*tables, x1, route, soffrow)


def _experts_kernel(pst_ref, ntile_ref, nb_ref, xb_ref, wg_ref, wu_ref, wd_ref, yb_ref,
                    wgb_ref, wub_ref, wdb_ref, xbuf_ref, ybuf_ref, in_sems, out_sems, zsem):
    e = pl.program_id(0)
    n = ntile_ref[e]
    row0 = pst_ref[e]
    n_blocks = yb_ref.shape[0] // MOE_BLOCK

    def tile_rows(t):
        return pl.ds(pl.multiple_of(row0 + t * EXP_TILE, MOE_BLOCK), EXP_TILE)

    def in_copy(t, s):
        return pltpu.make_async_copy(xb_ref.at[tile_rows(t)], xbuf_ref.at[s], in_sems.at[s])

    def out_copy(t, s):
        return pltpu.make_async_copy(ybuf_ref.at[s], yb_ref.at[tile_rows(t)], out_sems.at[s])

    @pl.when(n > 0)
    def _():
        in_copy(0, 0).start()
        wgb_ref[...] = wg_ref[0, 0].astype(BF16)
        wub_ref[...] = wu_ref[0, 0].astype(BF16)
        wdb_ref[...] = wd_ref[0, 0].astype(BF16)

    def tile(t, carry):
        s = t % 2
        in_copy(t, s).wait()

        @pl.when(t + 1 < n)
        def _():
            in_copy(t + 1, 1 - s).start()

        @pl.when(t >= 2)
        def _():
            out_copy(t - 2, s).wait()

        x = xbuf_ref[s]
        hg = jnp.dot(x, wgb_ref[...], preferred_element_type=F32)
        hu = jnp.dot(x, wub_ref[...], preferred_element_type=F32)
        hid = (hg * _sigmoid(hg)) * hu
        ybuf_ref[s] = jnp.dot(hid.astype(BF16), wdb_ref[...],
                              preferred_element_type=F32).astype(BF16)
        out_copy(t, s).start()
        return carry

    lax.fori_loop(0, n, tile, 0)

    @pl.when(n >= 2)
    def _():
        out_copy(n - 2, n % 2).wait()

    @pl.when(n >= 1)
    def _():
        out_copy(n - 1, (n - 1) % 2).wait()

    @pl.when(e == pl.num_programs(0) - 1)
    def _():
        ybuf_ref[0] = jnp.zeros(ybuf_ref.shape[1:], BF16)

        def zblock(b):
            return pltpu.make_async_copy(ybuf_ref.at[0, pl.ds(0, MOE_BLOCK)],
                                         yb_ref.at[_block_rows(b)], zsem)

        def start_block(b, carry):
            zblock(b).start()
            return carry

        def wait_block(b, carry):
            zblock(b).wait()
            return carry

        lax.fori_loop(nb_ref[0], n_blocks, start_block, 0)
        lax.fori_loop(nb_ref[0], n_blocks, wait_block, 0)


def _experts(xb, pst, ntile, nb_used, w_g, w_u, w_d, layer):
    n_rows, d = xb.shape
    n_exp, de = w_g.shape[1], w_g.shape[3]
    wsel = lambda e, *_: (layer, e, 0, 0)
    grid_spec = pltpu.PrefetchScalarGridSpec(
        num_scalar_prefetch=3,
        grid=(n_exp,),
        in_specs=[pl.BlockSpec(memory_space=pl.ANY),
                  pl.BlockSpec((1, 1, d, de), wsel),
                  pl.BlockSpec((1, 1, d, de), wsel),
                  pl.BlockSpec((1, 1, de, d), wsel)],
        out_specs=pl.BlockSpec(memory_space=pl.ANY),
        scratch_shapes=[pltpu.VMEM((d, de), BF16), pltpu.VMEM((d, de), BF16),
                        pltpu.VMEM((de, d), BF16),
                        pltpu.VMEM((2, EXP_TILE, d), BF16), pltpu.VMEM((2, EXP_TILE, d), BF16),
                        pltpu.SemaphoreType.DMA((2,)), pltpu.SemaphoreType.DMA((2,)),
                        pltpu.SemaphoreType.DMA(())],
    )
    return pl.pallas_call(
        _experts_kernel,
        grid_spec=grid_spec,
        out_shape=jax.ShapeDtypeStruct((n_rows, d), BF16),
        compiler_params=_cparams(1),
        name="experts",
    )(pst, ntile, nb_used, xb, w_g, w_u, w_d)


def _combine_kernel(cch_ref, soff_ref, dch_ref, tot_ref,
                    x_ref, route_ref, soffrow_ref, yb_ref, lng_ref, lnb_ref, out_ref,
                    stage_ref, sems, *, alpha):
    i = pl.program_id(0)
    nt = pl.num_programs(0)
    slot = i % 2

    def copy(s, src_chunk, dst_chunk):
        return pltpu.make_async_copy(yb_ref.at[_chunk_rows(src_chunk)],
                                     stage_ref.at[s, _chunk_rows(dst_chunk)], sems.at[s])

    def fetch(tile_idx, s):
        _for_each_chunk(tile_idx, cch_ref, soff_ref, dch_ref, lambda so, do: copy(s, do, so).start())

    @pl.when(i == 0)
    def _():
        stage_ref[...] = jnp.zeros_like(stage_ref)
        fetch(i, slot)

    @pl.when(i + 1 < nt)
    def _():
        fetch(i + 1, 1 - slot)

    _repeat(tot_ref[i], lambda: copy(slot, 0, 0).wait())

    route = route_ref[...]
    pt = _perm_matrix(route, soffrow_ref[0], route[:, 2:3], route[:, 3:4]).astype(BF16)
    ffn = jnp.dot(pt, stage_ref[slot], preferred_element_type=F32)
    out_ref[...] = _layer_norm(alpha * x_ref[...] + ffn, lng_ref[...], lnb_ref[...])


def _combine(x1, route, soffrow, tables, yb, lng, lnb, alpha):
    t, d = x1.shape
    nt = t // TD
    grid_spec = pltpu.PrefetchScalarGridSpec(
        num_scalar_prefetch=4,
        grid=(nt,),
        in_specs=[pl.BlockSpec((TD, d), lambda i, *_: (i, 0)),
                  pl.BlockSpec((TD, LANES), lambda i, *_: (i, 0)),
                  pl.BlockSpec((1, 1, LANES), lambda i, *_: (i, 0, 0)),
                  pl.BlockSpec(memory_space=pl.ANY),
                  pl.BlockSpec((1, d), lambda i, *_: (0, 0)),
                  pl.BlockSpec((1, d), lambda i, *_: (0, 0))],
        out_specs=pl.BlockSpec((TD, d), lambda i, *_: (i, 0)),
        scratch_shapes=[pltpu.VMEM((2, STAGE_ROWS, d), BF16), pltpu.SemaphoreType.DMA((2,))],
    )
    return pl.pallas_call(
        functools.partial(_combine_kernel, alpha=alpha),
        grid_spec=grid_spec,
        out_shape=jax.ShapeDtypeStruct((t, d), F32),
        compiler_params=_cparams(1),
        name="combine",
    )(*tables, x1, route, soffrow, yb, lng, lnb)


def _max_blocks(t):
    nt = t // TD
    rows = 2 * t + (CHUNK - 1) * nt * N_EXPERTS + (MOE_BLOCK - CHUNK) * N_EXPERTS
    return -(-rows // MOE_BLOCK)


def _dispatch_tables(cnt):
    n = cnt[:, 0, :N_EXPERTS].astype(jnp.int32)
    cch = (n + CHUNK - 1) // CHUNK
    soff = jnp.cumsum(cch, axis=1) - cch
    tot_tile = jnp.sum(cch, axis=1)
    tot_e = jnp.sum(cch, axis=0)
    reg = (tot_e + CHUNKS_PER_BLOCK - 1) // CHUNKS_PER_BLOCK * CHUNKS_PER_BLOCK
    pend = jnp.cumsum(reg)
    pstart = pend - reg
    dch = pstart[None, :] + jnp.cumsum(cch, axis=0) - cch
    nb_used = pend[-1] // CHUNKS_PER_BLOCK
    soffrow = jnp.zeros((n.shape[0], 1, LANES), F32).at[:, 0, :N_EXPERTS].set(
        (soff * CHUNK).astype(F32))
    i32 = lambda a: a.reshape(-1).astype(jnp.int32)
    return dict(cch=i32(cch), soff=i32(soff), dch=i32(dch), tot=i32(tot_tile),
                padst=i32(pstart + tot_e), padn=i32(reg - tot_e), nb_used=i32(nb_used),
                pst=i32(pstart * CHUNK), ntile=i32((reg * CHUNK + EXP_TILE - 1) // EXP_TILE),
                soffrow=soffrow)


def _rope_tables(seq):
    inv = ROPE_THETA ** (-jnp.arange(0, HEAD_DIM, 2, dtype=F32) / HEAD_DIM)
    ang = jnp.arange(seq, dtype=F32)[:, None] * inv[None, :]
    cos, sin = jnp.cos(ang), jnp.sin(ang)
    return jnp.concatenate([cos, cos], axis=1), jnp.concatenate([-sin, sin], axis=1)


@jax.jit
def kernel(x, w_in, w_sink, w_conv, b_conv, w_rec_gate, b_rec_gate, w_in_gate, b_in_gate,
           lru_lambda, w_attn_o, w_rnn_o, w_out, ln_g, ln_b, w_router_group, b_router_group,
           w_router_expert, b_router_expert, w_exp_gate, w_exp_up, w_exp_down):
    bsz, seq, d = x.shape
    depth = w_in.shape[0]
    t = bsz * seq
    nblk = d // LANES
    alpha = (2 * depth) ** 0.25
    cosf, sinf = _rope_tables(seq)
    n_rows = _max_blocks(t) * MOE_BLOCK + (EXP_TILE - MOE_BLOCK)
    x2 = x.reshape(t, d)
    for l in range(depth):
        n_branch = w_in.shape[2] - 2 * d
        q, k, v, xr, yr = _inproj(x2, w_in[l, :, :n_branch].astype(BF16), cosf, sinf, seq)
        o = _attention(q.reshape(bsz, seq, -1), k.reshape(bsz, seq, -1), v.reshape(bsz, seq, -1),
                       w_sink[l])
        wg_cat = (0.5 * jnp.concatenate([w_rec_gate[l, 0], w_in_gate[l, 0], w_rec_gate[l, 1],
                                         w_in_gate[l, 1]], axis=-1)).astype(BF16)
        bg_cat = jnp.concatenate([b_rec_gate[l, 0].reshape(nblk, 1, LANES),
                                  b_in_gate[l, 0].reshape(nblk, 1, LANES),
                                  b_rec_gate[l, 1].reshape(nblk, 1, LANES),
                                  b_in_gate[l, 1].reshape(nblk, 1, LANES)], axis=-1)
        lam_cat = jnp.concatenate([lru_lambda[l, 0].reshape(nblk, 1, LANES),
                                   lru_lambda[l, 1].reshape(nblk, 1, LANES)], axis=-1)
        hy = _rnn(xr.reshape(bsz, seq, d), yr.reshape(bsz, seq, d), w_conv[l],
                  b_conv[l].reshape(nblk, 1, LANES), wg_cat, bg_cat, lam_cat)
        wr = jnp.zeros((d, LANES), F32)
        wr = wr.at[:, :N_EXPERTS].set(w_router_expert[l])
        wr = wr.at[:, N_EXPERTS:N_EXPERTS + N_GROUPS].set(w_router_group[l]).astype(BF16)
        br = jnp.zeros((1, LANES), F32)
        br = br.at[0, :N_EXPERTS].set(b_router_expert[l])
        br = br.at[0, N_EXPERTS:N_EXPERTS + N_GROUPS].set(b_router_group[l])
        x1, route, cnt = _mixout(x2, o.reshape(t, -1), hy.reshape(t, d),
                                 w_in[l, :, n_branch:].astype(BF16),
                                 w_attn_o[l].astype(BF16), w_rnn_o[l].astype(BF16),
                                 w_out[l].astype(BF16), ln_g[l, 0].reshape(1, d),
                                 ln_b[l, 0].reshape(1, d), wr, br, alpha)
        tb = _dispatch_tables(cnt)
        xb = _dispatch(x1, route, tb["soffrow"],
                       (tb["cch"], tb["soff"], tb["dch"], tb["tot"], tb["padst"], tb["padn"],
                        tb["nb_used"]), n_rows)
        yb = _experts(xb, tb["pst"], tb["ntile"], tb["nb_used"], w_exp_gate, w_exp_up,
                      w_exp_down, l)
        x2 = _combine(x1, route, tb["soffrow"], (tb["cch"], tb["soff"], tb["dch"], tb["tot"]),
                      yb, ln_g[l, 1].reshape(1, d), ln_b[l, 1].reshape(1, d), alpha)
    return x2.reshape(bsz, seq, d)
```

```python
import functools
import math

import jax
import jax.numpy as jnp
from jax import lax
from jax.experimental import pallas as pl
from jax.experimental.pallas import tpu as pltpu

F32 = jnp.float32
BF16 = jnp.bfloat16

HEAD_DIM = 128
N_Q_HEADS = 8
N_KV_HEADS = 2
Q_PER_KV = N_Q_HEADS // N_KV_HEADS
WINDOW = 128
ROPE_THETA = 10000.0
CONV_W = 4
LRU_C = 8.0
N_GROUPS = 4
EXPERTS_PER_GROUP = 8
N_EXPERTS = N_GROUPS * EXPERTS_PER_GROUP
LN_EPS = 1e-5
NEG = -1e30

SUBLANES = 8
LANES = 128

TM_PROJ = 256
TQ_ATTN = 512
SEG = 256
SCAN_LEN = SEG + 4
SCAN_UNROLL = 10
BF16_ROWS = 2 * SUBLANES
TD = 512
CHUNK = BF16_ROWS
STAGE_ROWS = 2 * TD + CHUNK * N_EXPERTS
MOE_BLOCK = 128
CHUNKS_PER_BLOCK = MOE_BLOCK // CHUNK
EXP_TILE = 2 * MOE_BLOCK
VMEM_LIMIT = 56 * 1024 * 1024
TILE_DMA_PRIORITY = 1


def _cparams(n_axes):
    return pltpu.CompilerParams(dimension_semantics=("arbitrary",) * n_axes,
                                vmem_limit_bytes=VMEM_LIMIT)


def _softplus(z):
    e = jnp.exp(-jnp.abs(z))
    w = 1.0 + e
    tiny = w == 1.0
    log1p = jnp.where(tiny, e, jnp.log(w) * (e / jnp.where(tiny, 1.0, w - 1.0)))
    return jnp.maximum(z, 0.0) + log1p


def _sigmoid(x):
    return 0.5 * jnp.tanh(0.5 * x) + 0.5


def _gelu_tanh(y):
    c1 = math.sqrt(2.0 / math.pi)
    half = 0.5 * y
    return half + half * jnp.tanh(y * (c1 + (c1 * 0.044715) * (y * y)))


def _layer_norm(y, g, b):
    mu = jnp.mean(y, axis=-1, keepdims=True)
    d = y - mu
    var = jnp.mean(d * d, axis=-1, keepdims=True)
    return d * lax.rsqrt(var + LN_EPS) * g + b


def _inproj_kernel(x_ref, w_ref, cos_ref, sin_ref,
                   q_ref, k_ref, v_ref, xr_ref, yr_ref, *, d_model):
    xb = x_ref[...].astype(BF16)
    cos = cos_ref[...]
    sin = sin_ref[...]
    aw = N_Q_HEADS * HEAD_DIM
    kw = N_KV_HEADS * HEAD_DIM

    def proj(c0, n):
        return jnp.dot(xb, w_ref[:, c0:c0 + n], preferred_element_type=F32)

    def rope(t):
        return t * cos + pltpu.roll(t, HEAD_DIM // 2, 1) * sin

    zq = proj(0, aw)
    scale = HEAD_DIM ** -0.5
    for h in range(N_Q_HEADS):
        sl = slice(h * HEAD_DIM, (h + 1) * HEAD_DIM)
        q_ref[:, sl] = (rope(zq[:, sl]) * scale).astype(BF16)
    zk = proj(aw, kw)
    for h in range(N_KV_HEADS):
        sl = slice(h * HEAD_DIM, (h + 1) * HEAD_DIM)
        k_ref[:, sl] = rope(zk[:, sl]).astype(BF16)
    v_ref[...] = proj(aw + kw, kw).astype(BF16)
    c0 = aw + 2 * kw
    xr_ref[...] = proj(c0, d_model)
    yr_ref[...] = proj(c0 + d_model, d_model)


def _inproj(x2, w_in_b, cosf, sinf, seq):
    t, d = x2.shape
    n_in = w_in_b.shape[1]
    tm = TM_PROJ
    aw = N_Q_HEADS * HEAD_DIM
    kw = N_KV_HEADS * HEAD_DIM
    tiles_per_seq = seq // tm
    row = lambda i: (i, 0)
    pos = lambda i: (i % tiles_per_seq, 0)
    outs = [jax.ShapeDtypeStruct((t, aw), BF16), jax.ShapeDtypeStruct((t, kw), BF16),
            jax.ShapeDtypeStruct((t, kw), BF16)] + [jax.ShapeDtypeStruct((t, d), F32)] * 2
    return pl.pallas_call(
        functools.partial(_inproj_kernel, d_model=d),
        grid=(t // tm,),
        in_specs=[pl.BlockSpec((tm, d), row),
                  pl.BlockSpec((d, n_in), lambda i: (0, 0)),
                  pl.BlockSpec((tm, HEAD_DIM), pos),
                  pl.BlockSpec((tm, HEAD_DIM), pos)],
        out_specs=[pl.BlockSpec((tm, aw), row), pl.BlockSpec((tm, kw), row),
                   pl.BlockSpec((tm, kw), row)] + [pl.BlockSpec((tm, d), row)] * 2,
        out_shape=outs,
        compiler_params=_cparams(1),
        name="inproj",
    )(x2, w_in_b, cosf, sinf)


def _attn_kernel(sink_ref, q_ref, kp_ref, kc_ref, kn_ref, vp_ref, vc_ref, vn_ref, o_ref,
                 *, seq):
    i = pl.program_id(1)
    tq = q_ref.shape[1]
    blk = WINDOW
    t0 = i * tq
    kext = jnp.concatenate([kp_ref[0], kc_ref[0], kn_ref[0]], axis=0)
    vext = jnp.concatenate([vp_ref[0], vc_ref[0], vn_ref[0]], axis=0)
    nrow = Q_PER_KV * blk
    blk_shift = blk.bit_length() - 1
    qi = lax.broadcasted_iota(jnp.int32, (nrow, 3 * blk), 0) & (blk - 1)
    kj = lax.broadcasted_iota(jnp.int32, (nrow, 3 * blk), 1)
    band = jnp.where(jnp.abs(kj - blk - qi) <= WINDOW, 0.0, NEG)
    kj_row = lax.broadcasted_iota(jnp.int32, (1, 3 * blk), 1)
    rowg = lax.broadcasted_iota(jnp.int32, (nrow, 1), 0) >> blk_shift
    for j in range(tq // blk):
        kpos = t0 + j * blk - blk + kj_row
        bias = band + jnp.where((kpos >= 0) & (kpos < seq), 0.0, NEG)
        for h in range(N_KV_HEADS):
            hs = slice(h * HEAD_DIM, (h + 1) * HEAD_DIM)
            kblk = kext[j * blk:j * blk + 3 * blk, hs]
            vblk = vext[j * blk:j * blk + 3 * blk, hs]
            qs = [q_ref[0, j * blk:(j + 1) * blk,
                        (h * Q_PER_KV + g) * HEAD_DIM:(h * Q_PER_KV + g + 1) * HEAD_DIM]
                  for g in range(Q_PER_KV)]
            qblk = jnp.concatenate(qs, axis=0)
            s = lax.dot_general(qblk, kblk, (((1,), (1,)), ((), ())),
                                preferred_element_type=F32) + bias
            sk = jnp.full((nrow, 1), sink_ref[h * Q_PER_KV], F32)
            for g in range(1, Q_PER_KV):
                sk = jnp.where(rowg == g, sink_ref[h * Q_PER_KV + g], sk)
            m = jnp.maximum(jnp.max(s, axis=-1, keepdims=True), sk)
            p = jnp.exp(s - m)
            denom = jnp.sum(p, axis=-1, keepdims=True) + jnp.exp(sk - m)
            o = jnp.dot(p.astype(BF16), vblk, preferred_element_type=F32) / denom
            for g in range(Q_PER_KV):
                c = (h * Q_PER_KV + g) * HEAD_DIM
                o_ref[0, j * blk:(j + 1) * blk, c:c + HEAD_DIM] = (
                    o[g * blk:(g + 1) * blk].astype(BF16))


def _attention(q3, k3, v3, sink):
    b, s, aw = q3.shape
    kw = k3.shape[2]
    tq = TQ_ATTN
    blk = WINDOW
    r = tq // blk
    nblk = s // blk
    cur = lambda bi, i: (bi, i, 0)
    prev = lambda bi, i: (bi, jnp.maximum(i * r - 1, 0), 0)
    nxt = lambda bi, i: (bi, jnp.minimum((i + 1) * r, nblk - 1), 0)
    kv_specs = [pl.BlockSpec((1, blk, kw), prev), pl.BlockSpec((1, tq, kw), cur),
                pl.BlockSpec((1, blk, kw), nxt)]
    return pl.pallas_call(
        functools.partial(_attn_kernel, seq=s),
        grid=(b, s // tq),
        in_specs=[pl.BlockSpec(memory_space=pltpu.SMEM),
                  pl.BlockSpec((1, tq, aw), cur)] + kv_specs + kv_specs,
        out_specs=pl.BlockSpec((1, tq, aw), cur),
        out_shape=jax.ShapeDtypeStruct((b, s, aw), BF16),
        compiler_params=_cparams(2),
        name="attention",
    )(sink, q3, k3, k3, k3, v3, v3, v3)


def _rnn_kernel(xr_ref, yr_ref, wc_ref, bc_ref, wg_ref, bg_ref, lam_ref, out_ref,
                xpad_ref, af_ref, uf_ref, ab_ref, ub_ref, hb_ref, cf_ref, cb_ref, *, seq):
    nseg = seq // SEG
    nlane = nseg
    ngrp = nlane // SUBLANES
    hf_ref = xpad_ref
    wc = wc_ref[...]
    bc = bc_ref[0]
    wg = wg_ref[0]
    bg = 0.5 * bg_ref[0]
    lam = lam_ref[0]
    rate = (0.5 * LRU_C) * _softplus(-lam)
    a_refs = (af_ref, ab_ref)
    u_refs = (uf_ref, ub_ref)
    left = CONV_W // 2

    halo = jnp.zeros((SUBLANES, LANES), F32)
    xpad_ref[pl.ds(0, SUBLANES), :] = halo
    xpad_ref[pl.ds(seq + SUBLANES, SUBLANES), :] = halo

    def pad_copy(c, carry):
        t0 = pl.multiple_of(c * SEG, SEG)
        xpad_ref[pl.ds(t0 + SUBLANES, SEG), :] = xr_ref[0, pl.ds(t0, SEG), :]
        return carry

    lax.fori_loop(0, nseg, pad_copy, 0)

    tail = nlane * SCAN_LEN - seq
    for a_ref, u_ref in zip(a_refs, u_refs):
        a_ref[pl.ds(seq, tail), :] = jnp.ones((tail, LANES), F32)
        u_ref[pl.ds(seq, tail), :] = jnp.zeros((tail, LANES), F32)

    def gates(c, carry):
        t0 = pl.multiple_of(c * SEG, SEG)
        xc = bc
        for tap in range(CONV_W):
            xc = xc + xpad_ref[pl.ds(t0 + SUBLANES - left + tap, SEG), :] * wc[tap:tap + 1]
        gh = jnp.dot(xc.astype(BF16), wg, preferred_element_type=F32) + bg
        xch = 0.5 * xc
        for d in range(2):
            rt = rate[:, d * LANES:(d + 1) * LANES]
            nlog_a = rt * jnp.tanh(gh[:, (2 * d) * LANES:(2 * d + 1) * LANES]) + rt
            a = jnp.exp2(nlog_a * (-1.0 / math.log(2.0)))
            z = jnp.tanh(nlog_a) * (a * a + 1.0)
            mult = jnp.where(z > 0.0, z * lax.rsqrt(z), 0.0)
            in_gate2 = jnp.tanh(gh[:, (2 * d + 1) * LANES:(2 * d + 2) * LANES]) + 1.0
            a_refs[d][pl.ds(t0, SEG), :] = a
            u_refs[d][pl.ds(t0, SEG), :] = (xch * mult) * in_gate2
        return carry

    lax.fori_loop(0, nseg, gates, 0, unroll=2)

    def lane_rows(g, j):
        return pl.ds(g * SUBLANES * SCAN_LEN + j, SUBLANES, stride=SCAN_LEN)

    def totals_step(j, carry):
        hf, pf, hb, pb = carry
        jb = SCAN_LEN - 1 - j
        nhf, npf, nhb, npb = [], [], [], []
        for g in range(ngrp):
            a = af_ref[lane_rows(g, j), :]
            nhf.append(a * hf[g] + uf_ref[lane_rows(g, j), :])
            npf.append(a * pf[g])
            a = ab_ref[lane_rows(g, jb), :]
            nhb.append(a * hb[g] + ub_ref[lane_rows(g, jb), :])
            npb.append(a * pb[g])
        return tuple(nhf), tuple(npf), tuple(nhb), tuple(npb)

    zero = tuple(jnp.zeros((SUBLANES, LANES), F32) for _ in range(ngrp))
    one = tuple(jnp.ones((SUBLANES, LANES), F32) for _ in range(ngrp))
    hf, pf, hb, pb = lax.fori_loop(0, SCAN_LEN, totals_step, (zero, one, zero, one),
                                   unroll=SCAN_UNROLL)

    c = jnp.zeros((1, LANES), F32)
    for s in range(nlane):
        g, r = divmod(s, SUBLANES)
        cf_ref[s:s + 1, :] = c
        c = pf[g][r:r + 1] * c + hf[g][r:r + 1]
    c = jnp.zeros((1, LANES), F32)
    for s in range(nlane - 1, -1, -1):
        g, r = divmod(s, SUBLANES)
        cb_ref[s:s + 1, :] = c
        c = pb[g][r:r + 1] * c + hb[g][r:r + 1]

    def scan_step(j, carry):
        hf, hb = carry
        jb = SCAN_LEN - 1 - j
        nhf, nhb = [], []
        for g in range(ngrp):
            h = af_ref[lane_rows(g, j), :] * hf[g] + uf_ref[lane_rows(g, j), :]
            hf_ref[lane_rows(g, j), :] = h
            nhf.append(h)
            h = ab_ref[lane_rows(g, jb), :] * hb[g] + ub_ref[lane_rows(g, jb), :]
            hb_ref[lane_rows(g, jb), :] = h
            nhb.append(h)
        return tuple(nhf), tuple(nhb)

    hf0 = tuple(cf_ref[g * SUBLANES:(g + 1) * SUBLANES, :] for g in range(ngrp))
    hb0 = tuple(cb_ref[g * SUBLANES:(g + 1) * SUBLANES, :] for g in range(ngrp))
    lax.fori_loop(0, SCAN_LEN, scan_step, (hf0, hb0), unroll=SCAN_UNROLL)

    def finish(c, carry):
        t0 = pl.multiple_of(c * SEG, SEG)
        h = hf_ref[pl.ds(t0, SEG), :] + hb_ref[pl.ds(t0, SEG), :]
        out_ref[0, pl.ds(t0, SEG), :] = (h * _gelu_tanh(yr_ref[0, pl.ds(t0, SEG), :])).astype(BF16)
        return carry

    lax.fori_loop(0, nseg, finish, 0)


def _rnn(xr3, yr3, w_conv, b_conv, wg_cat, bg_cat, lam_cat):
    b, s, d = xr3.shape
    nblk = d // LANES
    nseg = s // SEG
    slab = lambda bi, n: (bi, 0, n)
    per_blk = lambda bi, n: (n, 0, 0)
    assert nseg % SUBLANES == 0 and nseg * SCAN_LEN >= s + 2 * SUBLANES
    scratch = ([pltpu.VMEM((nseg * SCAN_LEN, LANES), F32)] * 6
               + [pltpu.VMEM((nseg, LANES), F32)] * 2)
    return pl.pallas_call(
        functools.partial(_rnn_kernel, seq=s),
        grid=(b, nblk),
        in_specs=[pl.BlockSpec((1, s, LANES), slab),
                  pl.BlockSpec((1, s, LANES), slab),
                  pl.BlockSpec((CONV_W, LANES), lambda bi, n: (0, n)),
                  pl.BlockSpec((1, 1, LANES), per_blk),
                  pl.BlockSpec((1, LANES, 4 * LANES), per_blk),
                  pl.BlockSpec((1, 1, 4 * LANES), per_blk),
                  pl.BlockSpec((1, 1, 2 * LANES), per_blk)],
        out_specs=pl.BlockSpec((1, s, LANES), slab),
        out_shape=jax.ShapeDtypeStruct((b, s, d), BF16),
        scratch_shapes=scratch,
        compiler_params=_cparams(2),
        name="rglru",
    )(xr3, yr3, w_conv, b_conv, wg_cat, bg_cat, lam_cat)


def _route(logits):
    lane = lax.broadcasted_iota(jnp.int32, logits.shape, 1)
    lanef = lane.astype(F32)
    big = float(4 * LANES)
    gmask = (lane >= N_EXPERTS) & (lane < N_EXPERTS + N_GROUPS)
    gl = jnp.where(gmask, logits, NEG)
    gmax = jnp.max(gl, axis=-1, keepdims=True)
    ge = jnp.exp(gl - gmax)
    gprob = ge / jnp.sum(ge, axis=-1, keepdims=True)
    gval = jnp.max(gprob, axis=-1, keepdims=True)
    gidx = jnp.min(jnp.where((gprob == gval) & gmask, lanef, big), axis=-1, keepdims=True)
    gidx = gidx.astype(jnp.int32) - N_EXPERTS
    group_shift = EXPERTS_PER_GROUP.bit_length() - 1
    emask = (lane < N_EXPERTS) & ((lane >> group_shift) == gidx)
    el = jnp.where(emask, logits, NEG)
    m1 = jnp.max(el, axis=-1, keepdims=True)
    i1 = jnp.min(jnp.where((el == m1) & emask, lanef, big), axis=-1, keepdims=True)
    emask2 = emask & (lanef != i1)
    el2 = jnp.where(emask2, logits, NEG)
    m2 = jnp.max(el2, axis=-1, keepdims=True)
    i2 = jnp.min(jnp.where((el2 == m2) & emask2, lanef, big), axis=-1, keepdims=True)
    e2 = jnp.exp(m2 - m1)
    den = 1.0 + e2
    g1 = (1.0 / den) * gval
    g2 = (e2 / den) * gval
    route = jnp.where(lane == 0, i1, jnp.where(lane == 1, i2,
                      jnp.where(lane == 2, g1, jnp.where(lane == 3, g2, 0.0))))
    onehot = (lanef == i1).astype(F32) + (lanef == i2).astype(F32)
    return route, jnp.sum(onehot, axis=0, keepdims=True)


def _mixout_kernel(x_ref, o_ref, hy_ref, wmg_ref, wao_ref, wro_ref, wout_ref,
                   lng_ref, lnb_ref, wr_ref, br_ref, x1_ref, x1b_ref, route_ref, cnt_ref,
                   logits_ref, *, alpha):
    @pl.when(pl.program_id(0) == 0)
    def _():
        logits_ref[...] = jnp.zeros_like(logits_ref)

    route, cnt = _route(logits_ref[...])
    route_ref[...] = route
    cnt_ref[0] = cnt

    d = x_ref.shape[1]
    x = x_ref[...]
    xb = x.astype(BF16)
    ga = jnp.dot(xb, wmg_ref[:, :d], preferred_element_type=F32)
    gr = jnp.dot(xb, wmg_ref[:, d:], preferred_element_type=F32)
    ya = jnp.dot(o_ref[...], wao_ref[...], preferred_element_type=F32)
    yr = jnp.dot(hy_ref[...], wro_ref[...], preferred_element_type=F32)
    merged = _sigmoid(ga) * ya + _sigmoid(gr) * yr
    mix = jnp.dot(merged.astype(BF16), wout_ref[...], preferred_element_type=F32)
    x1 = _layer_norm(alpha * x + mix, lng_ref[...], lnb_ref[...])
    x1_ref[...] = x1
    x1b = x1.astype(BF16)
    x1b_ref[...] = x1b
    logits_ref[...] = jnp.dot(x1b, wr_ref[...], preferred_element_type=F32) + br_ref[...]


def _mixout(x2, o2, hy2, wmg, wao, wro, wout, lng, lnb, wr, br, alpha):
    t, d = x2.shape
    tm = TD
    nt = t // tm
    row = lambda i: (jnp.minimum(i, nt - 1), 0)
    prev_row = lambda i: (jnp.maximum(i - 1, 0), 0)
    full = lambda i: (0, 0)
    return pl.pallas_call(
        functools.partial(_mixout_kernel, alpha=alpha),
        grid=(nt + 1,),
        in_specs=[pl.BlockSpec((tm, d), row)] * 3
                 + [pl.BlockSpec((d, 2 * d), full)]
                 + [pl.BlockSpec((d, d), full)] * 3
                 + [pl.BlockSpec((1, d), full)] * 2
                 + [pl.BlockSpec((d, LANES), full), pl.BlockSpec((1, LANES), full)],
        out_specs=[pl.BlockSpec((tm, d), row), pl.BlockSpec((tm, d), row),
                   pl.BlockSpec((tm, LANES), prev_row),
                   pl.BlockSpec((1, 1, LANES), lambda i: (jnp.maximum(i - 1, 0), 0, 0))],
        out_shape=[jax.ShapeDtypeStruct((t, d), F32), jax.ShapeDtypeStruct((t, d), BF16),
                   jax.ShapeDtypeStruct((t, LANES), F32),
                   jax.ShapeDtypeStruct((nt, 1, LANES), F32)],
        scratch_shapes=[pltpu.VMEM((tm, LANES), F32)],
        compiler_params=_cparams(1),
        name="mixout",
    )(x2, o2, hy2, wmg, wao, wro, wout, lng, lnb, wr, br)


def _perm_matrix(route, soff_row, w1, w2):
    td = route.shape[0]
    lane = lax.broadcasted_iota(jnp.int32, (td, LANES), 1).astype(F32)
    e1 = lane == route[:, 0:1]
    e2 = lane == route[:, 1:2]
    cnt = (e1.astype(F32) + e2.astype(F32)).astype(BF16)
    ti = lax.broadcasted_iota(jnp.int32, (td, td), 0)
    tj = lax.broadcasted_iota(jnp.int32, (td, td), 1)
    lower = (tj < ti).astype(BF16)
    pos = jnp.dot(lower, cnt, preferred_element_type=F32) + soff_row
    r1 = jnp.sum(jnp.where(e1, pos, 0.0), axis=-1, keepdims=True).astype(jnp.int32)
    r2 = jnp.sum(jnp.where(e2, pos, 0.0), axis=-1, keepdims=True).astype(jnp.int32)
    col = lax.broadcasted_iota(jnp.int32, (td, STAGE_ROWS), 1)
    return jnp.where(col == r1, w1, 0.0) + jnp.where(col == r2, w2, 0.0)


def _chunk_rows(c):
    return pl.ds(pl.multiple_of(c * CHUNK, CHUNK), CHUNK)


def _block_rows(b):
    return pl.ds(pl.multiple_of(b * MOE_BLOCK, MOE_BLOCK), MOE_BLOCK)


def _for_each_chunk(i, cch_ref, soff_ref, dch_ref, fn):
    def per_expert(e, carry):
        k = i * N_EXPERTS + e
        so = soff_ref[k]
        do = dch_ref[k]

        def per_chunk(c, carry2):
            fn(so + c, do + c)
            return carry2

        return lax.fori_loop(0, cch_ref[k], per_chunk, carry)

    lax.fori_loop(0, N_EXPERTS, per_expert, 0)


def _repeat(n, fn):
    def body(c, carry):
        fn()
        return carry

    lax.fori_loop(0, n, body, 0)


def _dispatch_kernel(cch_ref, soff_ref, dch_ref, tot_ref, padst_ref, padn_ref, nb_ref,
                     x_ref, route_ref, soffrow_ref, xb_ref, stage_ref, zero_ref, sems, zsem):
    i = pl.program_id(0)
    nt = pl.num_programs(0)
    n_blocks = xb_ref.shape[0] // MOE_BLOCK
    slot = i % 2

    def copy(s, src_chunk, dst_chunk):
        return pltpu.make_async_copy(stage_ref.at[s, _chunk_rows(src_chunk)],
                                     xb_ref.at[_chunk_rows(dst_chunk)], sems.at[s])

    @pl.when(i >= 2)
    def _():
        _repeat(tot_ref[jnp.maximum(i - 2, 0)], lambda: copy(slot, 0, 0).wait())

    pt = _perm_matrix(route_ref[...], soffrow_ref[0], 1.0, 1.0).astype(BF16)
    stage_ref[slot] = lax.dot_general(pt, x_ref[...], (((0,), (0,)), ((), ())),
                                      preferred_element_type=F32).astype(BF16)
    _for_each_chunk(i, cch_ref, soff_ref, dch_ref,
                    lambda s, d: copy(slot, s, d).start(priority=TILE_DMA_PRIORITY))

    @pl.when(i == nt - 1)
    def _():
        @pl.when(i >= 1)
        def _():
            _repeat(tot_ref[jnp.maximum(i - 1, 0)], lambda: copy(1 - slot, 0, 0).wait())

        _repeat(tot_ref[i], lambda: copy(slot, 0, 0).wait())

        zero_ref[...] = jnp.zeros_like(zero_ref)

        def zcopy(dst_chunk):
            return pltpu.make_async_copy(zero_ref.at[pl.ds(0, CHUNK)],
                                         xb_ref.at[_chunk_rows(dst_chunk)], zsem)

        def pad_start(e, n):
            def pad_chunk(c, carry2):
                zcopy(padst_ref[e] + c).start()
                return carry2

            lax.fori_loop(0, padn_ref[e], pad_chunk, 0)
            return n + padn_ref[e]

        npad = lax.fori_loop(0, N_EXPERTS, pad_start, 0)
        _repeat(npad, lambda: zcopy(0).wait())

        def zblock(b):
            return pltpu.make_async_copy(zero_ref, xb_ref.at[_block_rows(b)], zsem)

        def start_block(b, carry):
            zblock(b).start()
            return carry

        def wait_block(b, carry):
            zblock(b).wait()
            return carry

        lax.fori_loop(nb_ref[0], n_blocks, start_block, 0)
        lax.fori_loop(nb_ref[0], n_blocks, wait_block, 0)


def _dispatch(x1, route, soffrow, tables, n_rows):
    t, d = x1.shape
    nt = t // TD
    grid_spec = pltpu.PrefetchScalarGridSpec(
        num_scalar_prefetch=7,
        grid=(nt,),
        in_specs=[pl.BlockSpec((TD, d), lambda i, *_: (i, 0)),
                  pl.BlockSpec((TD, LANES), lambda i, *_: (i, 0)),
                  pl.BlockSpec((1, 1, LANES), lambda i, *_: (i, 0, 0))],
        out_specs=pl.BlockSpec(memory_space=pl.ANY),
        scratch_shapes=[pltpu.VMEM((2, STAGE_ROWS, d), BF16), pltpu.VMEM((MOE_BLOCK, d), BF16),
                        pltpu.SemaphoreType.DMA((2,)), pltpu.SemaphoreType.DMA(())],
    )
    return pl.pallas_call(
        _dispatch_kernel,
        grid_spec=grid_spec,
        out_shape=jax.ShapeDtypeStruct((n_rows, d), BF16),
        compiler_params=_cparams(1),
        name="dispatch",
    )(*tables, x1, route, soffrow)


def _experts_kernel(pst_ref, ntile_ref, nb_ref, xb_ref, wg_ref, wu_ref, wd_ref, yb_ref,
                    wgb_ref, wub_ref, wdb_ref, xbuf_ref, ybuf_ref, in_sems, out_sems, zsem):
    e = pl.program_id(0)
    n = ntile_ref[e]
    row0 = pst_ref[e]
    n_blocks = yb_ref.shape[0] // MOE_BLOCK

    def tile_rows(t):
        return pl.ds(pl.multiple_of(row0 + t * EXP_TILE, MOE_BLOCK), EXP_TILE)

    def in_copy(t, s):
        return pltpu.make_async_copy(xb_ref.at[tile_rows(t)], xbuf_ref.at[s], in_sems.at[s])

    def out_copy(t, s):
        return pltpu.make_async_copy(ybuf_ref.at[s], yb_ref.at[tile_rows(t)], out_sems.at[s])

    @pl.when(n > 0)
    def _():
        in_copy(0, 0).start(priority=TILE_DMA_PRIORITY)
        wgb_ref[...] = wg_ref[0, 0].astype(BF16)
        wub_ref[...] = wu_ref[0, 0].astype(BF16)
        wdb_ref[...] = wd_ref[0, 0].astype(BF16)

    def tile(t, carry):
        s = t % 2
        in_copy(t, s).wait()

        @pl.when(t + 1 < n)
        def _():
            in_copy(t + 1, 1 - s).start(priority=TILE_DMA_PRIORITY)

        @pl.when(t >= 2)
        def _():
            out_copy(t - 2, s).wait()

        x = xbuf_ref[s]
        hg = jnp.dot(x, wgb_ref[...], preferred_element_type=F32)
        hu = jnp.dot(x, wub_ref[...], preferred_element_type=F32)
        hid = (hg * _sigmoid(hg)) * hu
        ybuf_ref[s] = jnp.dot(hid.astype(BF16), wdb_ref[...],
                              preferred_element_type=F32).astype(BF16)
        out_copy(t, s).start(priority=TILE_DMA_PRIORITY)
        return carry

    lax.fori_loop(0, n, tile, 0)

    @pl.when(n >= 2)
    def _():
        out_copy(n - 2, n % 2).wait()

    @pl.when(n >= 1)
    def _():
        out_copy(n - 1, (n - 1) % 2).wait()

    @pl.when(e == pl.num_programs(0) - 1)
    def _():
        ybuf_ref[0] = jnp.zeros(ybuf_ref.shape[1:], BF16)

        def zblock(b):
            return pltpu.make_async_copy(ybuf_ref.at[0, pl.ds(0, MOE_BLOCK)],
                                         yb_ref.at[_block_rows(b)], zsem)

        def start_block(b, carry):
            zblock(b).start()
            return carry

        def wait_block(b, carry):
            zblock(b).wait()
            return carry

        lax.fori_loop(nb_ref[0], n_blocks, start_block, 0)
        lax.fori_loop(nb_ref[0], n_blocks, wait_block, 0)


def _experts(xb, pst, ntile, nb_used, w_g, w_u, w_d, layer):
    n_rows, d = xb.shape
    n_exp, de = w_g.shape[1], w_g.shape[3]
    wsel = lambda e, *_: (layer, e, 0, 0)
    grid_spec = pltpu.PrefetchScalarGridSpec(
        num_scalar_prefetch=3,
        grid=(n_exp,),
        in_specs=[pl.BlockSpec(memory_space=pl.ANY),
                  pl.BlockSpec((1, 1, d, de), wsel),
                  pl.BlockSpec((1, 1, d, de), wsel),
                  pl.BlockSpec((1, 1, de, d), wsel)],
        out_specs=pl.BlockSpec(memory_space=pl.ANY),
        scratch_shapes=[pltpu.VMEM((d, de), BF16), pltpu.VMEM((d, de), BF16),
                        pltpu.VMEM((de, d), BF16),
                        pltpu.VMEM((2, EXP_TILE, d), BF16), pltpu.VMEM((2, EXP_TILE, d), BF16),
                        pltpu.SemaphoreType.DMA((2,)), pltpu.SemaphoreType.DMA((2,)),
                        pltpu.SemaphoreType.DMA(())],
    )
    return pl.pallas_call(
        _experts_kernel,
        grid_spec=grid_spec,
        out_shape=jax.ShapeDtypeStruct((n_rows, d), BF16),
        compiler_params=_cparams(1),
        name="experts",
    )(pst, ntile, nb_used, xb, w_g, w_u, w_d)


def _combine_kernel(cch_ref, soff_ref, dch_ref, tot_ref,
                    x_ref, route_ref, soffrow_ref, yb_ref, lng_ref, lnb_ref, out_ref,
                    stage_ref, sems, *, alpha):
    i = pl.program_id(0)
    nt = pl.num_programs(0)
    slot = i % 2

    def copy(s, src_chunk, dst_chunk):
        return pltpu.make_async_copy(yb_ref.at[_chunk_rows(src_chunk)],
                                     stage_ref.at[s, _chunk_rows(dst_chunk)], sems.at[s])

    def fetch(tile_idx, s):
        _for_each_chunk(tile_idx, cch_ref, soff_ref, dch_ref,
                        lambda so, do: copy(s, do, so).start(priority=TILE_DMA_PRIORITY))

    @pl.when(i == 0)
    def _():
        stage_ref[...] = jnp.zeros_like(stage_ref)
        fetch(i, slot)

    @pl.when(i + 1 < nt)
    def _():
        fetch(i + 1, 1 - slot)

    _repeat(tot_ref[i], lambda: copy(slot, 0, 0).wait())

    route = route_ref[...]
    pt = _perm_matrix(route, soffrow_ref[0], route[:, 2:3], route[:, 3:4]).astype(BF16)
    ffn = jnp.dot(pt, stage_ref[slot], preferred_element_type=F32)
    out_ref[...] = _layer_norm(alpha * x_ref[...] + ffn, lng_ref[...], lnb_ref[...])


def _combine(x1, route, soffrow, tables, yb, lng, lnb, alpha):
    t, d = x1.shape
    nt = t // TD
    grid_spec = pltpu.PrefetchScalarGridSpec(
        num_scalar_prefetch=4,
        grid=(nt,),
        in_specs=[pl.BlockSpec((TD, d), lambda i, *_: (i, 0)),
                  pl.BlockSpec((TD, LANES), lambda i, *_: (i, 0)),
                  pl.BlockSpec((1, 1, LANES), lambda i, *_: (i, 0, 0)),
                  pl.BlockSpec(memory_space=pl.ANY),
                  pl.BlockSpec((1, d), lambda i, *_: (0, 0)),
                  pl.BlockSpec((1, d), lambda i, *_: (0, 0))],
        out_specs=pl.BlockSpec((TD, d), lambda i, *_: (i, 0)),
        scratch_shapes=[pltpu.VMEM((2, STAGE_ROWS, d), BF16), pltpu.SemaphoreType.DMA((2,))],
    )
    return pl.pallas_call(
        functools.partial(_combine_kernel, alpha=alpha),
        grid_spec=grid_spec,
        out_shape=jax.ShapeDtypeStruct((t, d), F32),
        compiler_params=_cparams(1),
        name="combine",
    )(*tables, x1, route, soffrow, yb, lng, lnb)


def _max_blocks(t):
    nt = t // TD
    rows = 2 * t + (CHUNK - 1) * nt * N_EXPERTS + (MOE_BLOCK - CHUNK) * N_EXPERTS
    return -(-rows // MOE_BLOCK)


def _dispatch_tables(cnt):
    n = cnt[:, 0, :N_EXPERTS].astype(jnp.int32)
    cch = (n + CHUNK - 1) // CHUNK
    soff = jnp.cumsum(cch, axis=1) - cch
    tot_tile = jnp.sum(cch, axis=1)
    tot_e = jnp.sum(cch, axis=0)
    reg = (tot_e + CHUNKS_PER_BLOCK - 1) // CHUNKS_PER_BLOCK * CHUNKS_PER_BLOCK
    pend = jnp.cumsum(reg)
    pstart = pend - reg
    dch = pstart[None, :] + jnp.cumsum(cch, axis=0) - cch
    nb_used = pend[-1] // CHUNKS_PER_BLOCK
    soffrow = jnp.zeros((n.shape[0], 1, LANES), F32).at[:, 0, :N_EXPERTS].set(
        (soff * CHUNK).astype(F32))
    i32 = lambda a: a.reshape(-1).astype(jnp.int32)
    return dict(cch=i32(cch), soff=i32(soff), dch=i32(dch), tot=i32(tot_tile),
                padst=i32(pstart + tot_e), padn=i32(reg - tot_e), nb_used=i32(nb_used),
                pst=i32(pstart * CHUNK), ntile=i32((reg * CHUNK + EXP_TILE - 1) // EXP_TILE),
                soffrow=soffrow)


def _rope_tables(seq):
    inv = ROPE_THETA ** (-jnp.arange(0, HEAD_DIM, 2, dtype=F32) / HEAD_DIM)
    ang = jnp.arange(seq, dtype=F32)[:, None] * inv[None, :]
    cos, sin = jnp.cos(ang), jnp.sin(ang)
    return jnp.concatenate([cos, cos], axis=1), jnp.concatenate([-sin, sin], axis=1)


@jax.jit
def kernel(x, w_in, w_sink, w_conv, b_conv, w_rec_gate, b_rec_gate, w_in_gate, b_in_gate,
           lru_lambda, w_attn_o, w_rnn_o, w_out, ln_g, ln_b, w_router_group, b_router_group,
           w_router_expert, b_router_expert, w_exp_gate, w_exp_up, w_exp_down):
    bsz, seq, d = x.shape
    depth = w_in.shape[0]
    t = bsz * seq
    nblk = d // LANES
    alpha = (2 * depth) ** 0.25
    cosf, sinf = _rope_tables(seq)
    n_rows = _max_blocks(t) * MOE_BLOCK + (EXP_TILE - MOE_BLOCK)
    x2 = x.reshape(t, d)
    for l in range(depth):
        n_branch = w_in.shape[2] - 2 * d
        q, k, v, xr, yr = _inproj(x2, w_in[l, :, :n_branch].astype(BF16), cosf, sinf, seq)
        o = _attention(q.reshape(bsz, seq, -1), k.reshape(bsz, seq, -1), v.reshape(bsz, seq, -1),
                       w_sink[l])
        wg_cat = (0.5 * jnp.concatenate([w_rec_gate[l, 0], w_in_gate[l, 0], w_rec_gate[l, 1],
                                         w_in_gate[l, 1]], axis=-1)).astype(BF16)
        bg_cat = jnp.concatenate([b_rec_gate[l, 0].reshape(nblk, 1, LANES),
                                  b_in_gate[l, 0].reshape(nblk, 1, LANES),
                                  b_rec_gate[l, 1].reshape(nblk, 1, LANES),
                                  b_in_gate[l, 1].reshape(nblk, 1, LANES)], axis=-1)
        lam_cat = jnp.concatenate([lru_lambda[l, 0].reshape(nblk, 1, LANES),
                                   lru_lambda[l, 1].reshape(nblk, 1, LANES)], axis=-1)
        hy = _rnn(xr.reshape(bsz, seq, d), yr.reshape(bsz, seq, d), w_conv[l],
                  b_conv[l].reshape(nblk, 1, LANES), wg_cat, bg_cat, lam_cat)
        wr = jnp.zeros((d, LANES), F32)
        wr = wr.at[:, :N_EXPERTS].set(w_router_expert[l])
        wr = wr.at[:, N_EXPERTS:N_EXPERTS + N_GROUPS].set(w_router_group[l]).astype(BF16)
        br = jnp.zeros((1, LANES), F32)
        br = br.at[0, :N_EXPERTS].set(b_router_expert[l])
        br = br.at[0, N_EXPERTS:N_EXPERTS + N_GROUPS].set(b_router_group[l])
        x1, x1b, route, cnt = _mixout(x2, o.reshape(t, -1), hy.reshape(t, d),
                                 w_in[l, :, n_branch:].astype(BF16),
                                 w_attn_o[l].astype(BF16), w_rnn_o[l].astype(BF16),
                                 w_out[l].astype(BF16), ln_g[l, 0].reshape(1, d),
                                 ln_b[l, 0].reshape(1, d), wr, br, alpha)
        tb = _dispatch_tables(cnt)
        xb = _dispatch(x1b, route, tb["soffrow"],
                       (tb["cch"], tb["soff"], tb["dch"], tb["tot"], tb["padst"], tb["padn"],
                        tb["nb_used"]), n_rows)
        yb = _experts(xb, tb["pst"], tb["ntile"], tb["nb_used"], w_exp_gate, w_exp_up,
                      w_exp_down, l)
        x2 = _combine(x1, route, tb["soffrow"], (tb["cch"], tb["soff"], tb["dch"], tb["tot"]),
                      yb, ln_g[l, 1].reshape(1, d), ln_b[l, 1].reshape(1, d), alpha)
    return x2.reshape(bsz, seq, d)
```

```python
import functools
import math

import jax
import jax.numpy as jnp
from jax import lax
from jax.experimental import pallas as pl
from jax.experimental.pallas import tpu as pltpu

F32 = jnp.float32
BF16 = jnp.bfloat16

HEAD_DIM = 128
N_Q_HEADS = 8
N_KV_HEADS = 2
Q_PER_KV = N_Q_HEADS // N_KV_HEADS
WINDOW = 128
ROPE_THETA = 10000.0
CONV_W = 4
LRU_C = 8.0
N_GROUPS = 4
EXPERTS_PER_GROUP = 8
N_EXPERTS = N_GROUPS * EXPERTS_PER_GROUP
LN_EPS = 1e-5
NEG = -1e30

SUBLANES = 8
LANES = 128

TM_PROJ = 256
TQ_ATTN = 512
SEG = 256
SCAN_LEN = SEG + 4
SCAN_UNROLL = 10
BF16_ROWS = 2 * SUBLANES
TD = 512
CHUNK = BF16_ROWS
STAGE_ROWS = 2 * TD + CHUNK * N_EXPERTS
MOE_BLOCK = 128
CHUNKS_PER_BLOCK = MOE_BLOCK // CHUNK
EXP_TILE = 2 * MOE_BLOCK
EXP_BUFS = 6
VMEM_LIMIT = 56 * 1024 * 1024
TILE_DMA_PRIORITY = 1


def _cparams(n_axes):
    return pltpu.CompilerParams(dimension_semantics=("arbitrary",) * n_axes,
                                vmem_limit_bytes=VMEM_LIMIT)


def _softplus(z):
    e = jnp.exp(-jnp.abs(z))
    w = 1.0 + e
    tiny = w == 1.0
    log1p = jnp.where(tiny, e, jnp.log(w) * (e / jnp.where(tiny, 1.0, w - 1.0)))
    return jnp.maximum(z, 0.0) + log1p


def _sigmoid(x):
    return 0.5 * jnp.tanh(0.5 * x) + 0.5


def _gelu_tanh(y):
    c1 = math.sqrt(2.0 / math.pi)
    half = 0.5 * y
    return half + half * jnp.tanh(y * (c1 + (c1 * 0.044715) * (y * y)))


def _layer_norm(y, g, b):
    mu = jnp.mean(y, axis=-1, keepdims=True)
    d = y - mu
    var = jnp.mean(d * d, axis=-1, keepdims=True)
    return d * lax.rsqrt(var + LN_EPS) * g + b


def _inproj_kernel(x_ref, w_ref, cos_ref, sin_ref,
                   q_ref, k_ref, v_ref, xr_ref, yr_ref, *, d_model):
    xb = x_ref[...].astype(BF16)
    cos = cos_ref[...]
    sin = sin_ref[...]
    aw = N_Q_HEADS * HEAD_DIM
    kw = N_KV_HEADS * HEAD_DIM

    def proj(c0, n):
        return jnp.dot(xb, w_ref[:, c0:c0 + n], preferred_element_type=F32)

    def rope(t):
        return t * cos + pltpu.roll(t, HEAD_DIM // 2, 1) * sin

    zq = proj(0, aw)
    scale = HEAD_DIM ** -0.5
    for h in range(N_Q_HEADS):
        sl = slice(h * HEAD_DIM, (h + 1) * HEAD_DIM)
        q_ref[:, sl] = (rope(zq[:, sl]) * scale).astype(BF16)
    zk = proj(aw, kw)
    for h in range(N_KV_HEADS):
        sl = slice(h * HEAD_DIM, (h + 1) * HEAD_DIM)
        k_ref[:, sl] = rope(zk[:, sl]).astype(BF16)
    v_ref[...] = proj(aw + kw, kw).astype(BF16)
    c0 = aw + 2 * kw
    xr_ref[...] = proj(c0, d_model)
    yr_ref[...] = proj(c0 + d_model, d_model)


def _inproj(x2, w_in_b, cosf, sinf, seq):
    t, d = x2.shape
    n_in = w_in_b.shape[1]
    tm = TM_PROJ
    aw = N_Q_HEADS * HEAD_DIM
    kw = N_KV_HEADS * HEAD_DIM
    tiles_per_seq = seq // tm
    row = lambda i: (i, 0)
    pos = lambda i: (i % tiles_per_seq, 0)
    outs = [jax.ShapeDtypeStruct((t, aw), BF16), jax.ShapeDtypeStruct((t, kw), BF16),
            jax.ShapeDtypeStruct((t, kw), BF16)] + [jax.ShapeDtypeStruct((t, d), F32)] * 2
    return pl.pallas_call(
        functools.partial(_inproj_kernel, d_model=d),
        grid=(t // tm,),
        in_specs=[pl.BlockSpec((tm, d), row),
                  pl.BlockSpec((d, n_in), lambda i: (0, 0)),
                  pl.BlockSpec((tm, HEAD_DIM), pos),
                  pl.BlockSpec((tm, HEAD_DIM), pos)],
        out_specs=[pl.BlockSpec((tm, aw), row), pl.BlockSpec((tm, kw), row),
                   pl.BlockSpec((tm, kw), row)] + [pl.BlockSpec((tm, d), row)] * 2,
        out_shape=outs,
        compiler_params=_cparams(1),
        name="inproj",
    )(x2, w_in_b, cosf, sinf)


def _attn_kernel(sink_ref, q_ref, kp_ref, kc_ref, kn_ref, vp_ref, vc_ref, vn_ref, o_ref,
                 *, seq):
    i = pl.program_id(1)
    tq = q_ref.shape[1]
    blk = WINDOW
    t0 = i * tq
    kext = jnp.concatenate([kp_ref[0], kc_ref[0], kn_ref[0]], axis=0)
    vext = jnp.concatenate([vp_ref[0], vc_ref[0], vn_ref[0]], axis=0)
    nrow = Q_PER_KV * blk
    blk_shift = blk.bit_length() - 1
    qi = lax.broadcasted_iota(jnp.int32, (nrow, 3 * blk), 0) & (blk - 1)
    kj = lax.broadcasted_iota(jnp.int32, (nrow, 3 * blk), 1)
    band = jnp.where(jnp.abs(kj - blk - qi) <= WINDOW, 0.0, NEG)
    kj_row = lax.broadcasted_iota(jnp.int32, (1, 3 * blk), 1)
    rowg = lax.broadcasted_iota(jnp.int32, (nrow, 1), 0) >> blk_shift
    sinks = []
    for h in range(N_KV_HEADS):
        sk = jnp.full((nrow, 1), sink_ref[h * Q_PER_KV], F32)
        for g in range(1, Q_PER_KV):
            sk = jnp.where(rowg == g, sink_ref[h * Q_PER_KV + g], sk)
        sinks.append(sk)
    for j in range(tq // blk):
        kpos = t0 + j * blk - blk + kj_row
        bias = band + jnp.where((kpos >= 0) & (kpos < seq), 0.0, NEG)
        for h in range(N_KV_HEADS):
            hs = slice(h * HEAD_DIM, (h + 1) * HEAD_DIM)
            kblk = kext[j * blk:j * blk + 3 * blk, hs]
            vblk = vext[j * blk:j * blk + 3 * blk, hs]
            qs = [q_ref[0, j * blk:(j + 1) * blk,
                        (h * Q_PER_KV + g) * HEAD_DIM:(h * Q_PER_KV + g + 1) * HEAD_DIM]
                  for g in range(Q_PER_KV)]
            qblk = jnp.concatenate(qs, axis=0)
            s = lax.dot_general(qblk, kblk, (((1,), (1,)), ((), ())),
                                preferred_element_type=F32) + bias
            sk = sinks[h]
            m = jnp.maximum(jnp.max(s, axis=-1, keepdims=True), sk)
            p = jnp.exp(s - m)
            denom = jnp.sum(p, axis=-1, keepdims=True) + jnp.exp(sk - m)
            o = jnp.dot(p.astype(BF16), vblk, preferred_element_type=F32) / denom
            for g in range(Q_PER_KV):
                c = (h * Q_PER_KV + g) * HEAD_DIM
                o_ref[0, j * blk:(j + 1) * blk, c:c + HEAD_DIM] = (
                    o[g * blk:(g + 1) * blk].astype(BF16))


def _attention(q3, k3, v3, sink):
    b, s, aw = q3.shape
    kw = k3.shape[2]
    tq = TQ_ATTN
    blk = WINDOW
    r = tq // blk
    nblk = s // blk
    cur = lambda bi, i: (bi, i, 0)
    prev = lambda bi, i: (bi, jnp.maximum(i * r - 1, 0), 0)
    nxt = lambda bi, i: (bi, jnp.minimum((i + 1) * r, nblk - 1), 0)
    kv_specs = [pl.BlockSpec((1, blk, kw), prev), pl.BlockSpec((1, tq, kw), cur),
                pl.BlockSpec((1, blk, kw), nxt)]
    return pl.pallas_call(
        functools.partial(_attn_kernel, seq=s),
        grid=(b, s // tq),
        in_specs=[pl.BlockSpec(memory_space=pltpu.SMEM),
                  pl.BlockSpec((1, tq, aw), cur)] + kv_specs + kv_specs,
        out_specs=pl.BlockSpec((1, tq, aw), cur),
        out_shape=jax.ShapeDtypeStruct((b, s, aw), BF16),
        compiler_params=_cparams(2),
        name="attention",
    )(sink, q3, k3, k3, k3, v3, v3, v3)


def _rnn_kernel(xr_ref, yr_ref, wc_ref, bc_ref, wg_ref, bg_ref, lam_ref, out_ref,
                xpad_ref, af_ref, uf_ref, ab_ref, ub_ref, hb_ref, cf_ref, cb_ref, *, seq):
    nseg = seq // SEG
    nlane = nseg
    ngrp = nlane // SUBLANES
    hf_ref = xpad_ref
    wc = wc_ref[...]
    bc = bc_ref[0]
    wg = wg_ref[0]
    bg = 0.5 * bg_ref[0]
    lam = lam_ref[0]
    rate = (0.5 * LRU_C) * _softplus(-lam)
    a_refs = (af_ref, ab_ref)
    u_refs = (uf_ref, ub_ref)
    left = CONV_W // 2

    halo = jnp.zeros((SUBLANES, LANES), F32)
    xpad_ref[pl.ds(0, SUBLANES), :] = halo
    xpad_ref[pl.ds(seq + SUBLANES, SUBLANES), :] = halo

    def pad_copy(c, carry):
        t0 = pl.multiple_of(c * SEG, SEG)
        xpad_ref[pl.ds(t0 + SUBLANES, SEG), :] = xr_ref[0, pl.ds(t0, SEG), :]
        return carry

    lax.fori_loop(0, nseg, pad_copy, 0)

    tail = nlane * SCAN_LEN - seq
    for a_ref, u_ref in zip(a_refs, u_refs):
        a_ref[pl.ds(seq, tail), :] = jnp.ones((tail, LANES), F32)
        u_ref[pl.ds(seq, tail), :] = jnp.zeros((tail, LANES), F32)

    def gates(c, carry):
        t0 = pl.multiple_of(c * SEG, SEG)
        xc = bc
        for tap in range(CONV_W):
            xc = xc + xpad_ref[pl.ds(t0 + SUBLANES - left + tap, SEG), :] * wc[tap:tap + 1]
        gh = jnp.dot(xc.astype(BF16), wg, preferred_element_type=F32) + bg
        xch = 0.5 * xc
        for d in range(2):
            rt = rate[:, d * LANES:(d + 1) * LANES]
            nlog_a = rt * jnp.tanh(gh[:, (2 * d) * LANES:(2 * d + 1) * LANES]) + rt
            a = jnp.exp2(nlog_a * (-1.0 / math.log(2.0)))
            z = jnp.tanh(nlog_a) * (a * a + 1.0)
            mult = jnp.where(z > 0.0, z * lax.rsqrt(z), 0.0)
            in_gate2 = jnp.tanh(gh[:, (2 * d + 1) * LANES:(2 * d + 2) * LANES]) + 1.0
            a_refs[d][pl.ds(t0, SEG), :] = a
            u_refs[d][pl.ds(t0, SEG), :] = (xch * mult) * in_gate2
        return carry

    lax.fori_loop(0, nseg, gates, 0, unroll=2)

    def lane_rows(g, j):
        return pl.ds(g * SUBLANES * SCAN_LEN + j, SUBLANES, stride=SCAN_LEN)

    def totals_step(j, carry):
        hf, pf, hb, pb = carry
        jb = SCAN_LEN - 1 - j
        nhf, npf, nhb, npb = [], [], [], []
        for g in range(ngrp):
            a = af_ref[lane_rows(g, j), :]
            nhf.append(a * hf[g] + uf_ref[lane_rows(g, j), :])
            npf.append(a * pf[g])
            a = ab_ref[lane_rows(g, jb), :]
            nhb.append(a * hb[g] + ub_ref[lane_rows(g, jb), :])
            npb.append(a * pb[g])
        return tuple(nhf), tuple(npf), tuple(nhb), tuple(npb)

    zero = tuple(jnp.zeros((SUBLANES, LANES), F32) for _ in range(ngrp))
    one = tuple(jnp.ones((SUBLANES, LANES), F32) for _ in range(ngrp))
    hf, pf, hb, pb = lax.fori_loop(0, SCAN_LEN, totals_step, (zero, one, zero, one),
                                   unroll=SCAN_UNROLL)

    c = jnp.zeros((1, LANES), F32)
    for s in range(nlane):
        g, r = divmod(s, SUBLANES)
        cf_ref[s:s + 1, :] = c
        c = pf[g][r:r + 1] * c + hf[g][r:r + 1]
    c = jnp.zeros((1, LANES), F32)
    for s in range(nlane - 1, -1, -1):
        g, r = divmod(s, SUBLANES)
        cb_ref[s:s + 1, :] = c
        c = pb[g][r:r + 1] * c + hb[g][r:r + 1]

    def scan_step(j, carry):
        hf, hb = carry
        jb = SCAN_LEN - 1 - j
        nhf, nhb = [], []
        for g in range(ngrp):
            h = af_ref[lane_rows(g, j), :] * hf[g] + uf_ref[lane_rows(g, j), :]
            hf_ref[lane_rows(g, j), :] = h
            nhf.append(h)
            h = ab_ref[lane_rows(g, jb), :] * hb[g] + ub_ref[lane_rows(g, jb), :]
            hb_ref[lane_rows(g, jb), :] = h
            nhb.append(h)
        return tuple(nhf), tuple(nhb)

    hf0 = tuple(cf_ref[g * SUBLANES:(g + 1) * SUBLANES, :] for g in range(ngrp))
    hb0 = tuple(cb_ref[g * SUBLANES:(g + 1) * SUBLANES, :] for g in range(ngrp))
    lax.fori_loop(0, SCAN_LEN, scan_step, (hf0, hb0), unroll=SCAN_UNROLL)

    def finish(c, carry):
        t0 = pl.multiple_of(c * SEG, SEG)
        h = hf_ref[pl.ds(t0, SEG), :] + hb_ref[pl.ds(t0, SEG), :]
        out_ref[0, pl.ds(t0, SEG), :] = (h * _gelu_tanh(yr_ref[0, pl.ds(t0, SEG), :])).astype(BF16)
        return carry

    lax.fori_loop(0, nseg, finish, 0)


def _rnn(xr3, yr3, w_conv, b_conv, wg_cat, bg_cat, lam_cat):
    b, s, d = xr3.shape
    nblk = d // LANES
    nseg = s // SEG
    slab = lambda bi, n: (bi, 0, n)
    per_blk = lambda bi, n: (n, 0, 0)
    assert nseg % SUBLANES == 0 and nseg * SCAN_LEN >= s + 2 * SUBLANES
    scratch = ([pltpu.VMEM((nseg * SCAN_LEN, LANES), F32)] * 6
               + [pltpu.VMEM((nseg, LANES), F32)] * 2)
    return pl.pallas_call(
        functools.partial(_rnn_kernel, seq=s),
        grid=(b, nblk),
        in_specs=[pl.BlockSpec((1, s, LANES), slab),
                  pl.BlockSpec((1, s, LANES), slab),
                  pl.BlockSpec((CONV_W, LANES), lambda bi, n: (0, n)),
                  pl.BlockSpec((1, 1, LANES), per_blk),
                  pl.BlockSpec((1, LANES, 4 * LANES), per_blk),
                  pl.BlockSpec((1, 1, 4 * LANES), per_blk),
                  pl.BlockSpec((1, 1, 2 * LANES), per_blk)],
        out_specs=pl.BlockSpec((1, s, LANES), slab),
        out_shape=jax.ShapeDtypeStruct((b, s, d), BF16),
        scratch_shapes=scratch,
        compiler_params=_cparams(2),
        name="rglru",
    )(xr3, yr3, w_conv, b_conv, wg_cat, bg_cat, lam_cat)


def _route(logits):
    lane = lax.broadcasted_iota(jnp.int32, logits.shape, 1)
    lanef = lane.astype(F32)
    big = float(4 * LANES)
    gmask = (lane >= N_EXPERTS) & (lane < N_EXPERTS + N_GROUPS)
    gl = jnp.where(gmask, logits, NEG)
    gmax = jnp.max(gl, axis=-1, keepdims=True)
    ge = jnp.exp(gl - gmax)
    gprob = ge / jnp.sum(ge, axis=-1, keepdims=True)
    gval = jnp.max(gprob, axis=-1, keepdims=True)
    gidx = jnp.min(jnp.where((gprob == gval) & gmask, lanef, big), axis=-1, keepdims=True)
    gidx = gidx.astype(jnp.int32) - N_EXPERTS
    group_shift = EXPERTS_PER_GROUP.bit_length() - 1
    emask = (lane < N_EXPERTS) & ((lane >> group_shift) == gidx)
    el = jnp.where(emask, logits, NEG)
    m1 = jnp.max(el, axis=-1, keepdims=True)
    i1 = jnp.min(jnp.where((el == m1) & emask, lanef, big), axis=-1, keepdims=True)
    emask2 = emask & (lanef != i1)
    el2 = jnp.where(emask2, logits, NEG)
    m2 = jnp.max(el2, axis=-1, keepdims=True)
    i2 = jnp.min(jnp.where((el2 == m2) & emask2, lanef, big), axis=-1, keepdims=True)
    e2 = jnp.exp(m2 - m1)
    den = 1.0 + e2
    g1 = (1.0 / den) * gval
    g2 = (e2 / den) * gval
    route = jnp.where(lane == 0, i1, jnp.where(lane == 1, i2,
                      jnp.where(lane == 2, g1, jnp.where(lane == 3, g2, 0.0))))
    onehot = (lanef == i1).astype(F32) + (lanef == i2).astype(F32)
    return route, jnp.sum(onehot, axis=0, keepdims=True)


def _mixout_kernel(x_ref, o_ref, hy_ref, wmg_ref, wao_ref, wro_ref, wout_ref,
                   lng_ref, lnb_ref, wr_ref, br_ref, x1_ref, x1b_ref, route_ref, cnt_ref,
                   logits_ref, *, alpha):
    @pl.when(pl.program_id(0) == 0)
    def _():
        logits_ref[...] = jnp.zeros_like(logits_ref)

    route, cnt = _route(logits_ref[...])
    route_ref[...] = route
    cnt_ref[0] = cnt

    d = x_ref.shape[1]
    x = x_ref[...]
    xb = x.astype(BF16)
    ga = jnp.dot(xb, wmg_ref[:, :d], preferred_element_type=F32)
    gr = jnp.dot(xb, wmg_ref[:, d:], preferred_element_type=F32)
    ya = jnp.dot(o_ref[...], wao_ref[...], preferred_element_type=F32)
    yr = jnp.dot(hy_ref[...], wro_ref[...], preferred_element_type=F32)
    merged = _sigmoid(ga) * ya + _sigmoid(gr) * yr
    mix = jnp.dot(merged.astype(BF16), wout_ref[...], preferred_element_type=F32)
    x1 = _layer_norm(alpha * x + mix, lng_ref[...], lnb_ref[...])
    x1_ref[...] = x1
    x1b = x1.astype(BF16)
    x1b_ref[...] = x1b
    logits_ref[...] = jnp.dot(x1b, wr_ref[...], preferred_element_type=F32) + br_ref[...]


def _mixout(x2, o2, hy2, wmg, wao, wro, wout, lng, lnb, wr, br, alpha):
    t, d = x2.shape
    tm = TD
    nt = t // tm
    row = lambda i: (jnp.minimum(i, nt - 1), 0)
    prev_row = lambda i: (jnp.maximum(i - 1, 0), 0)
    full = lambda i: (0, 0)
    return pl.pallas_call(
        functools.partial(_mixout_kernel, alpha=alpha),
        grid=(nt + 1,),
        in_specs=[pl.BlockSpec((tm, d), row)] * 3
                 + [pl.BlockSpec((d, 2 * d), full)]
                 + [pl.BlockSpec((d, d), full)] * 3
                 + [pl.BlockSpec((1, d), full)] * 2
                 + [pl.BlockSpec((d, LANES), full), pl.BlockSpec((1, LANES), full)],
        out_specs=[pl.BlockSpec((tm, d), row), pl.BlockSpec((tm, d), row),
                   pl.BlockSpec((tm, LANES), prev_row),
                   pl.BlockSpec((1, 1, LANES), lambda i: (jnp.maximum(i - 1, 0), 0, 0))],
        out_shape=[jax.ShapeDtypeStruct((t, d), F32), jax.ShapeDtypeStruct((t, d), BF16),
                   jax.ShapeDtypeStruct((t, LANES), F32),
                   jax.ShapeDtypeStruct((nt, 1, LANES), F32)],
        scratch_shapes=[pltpu.VMEM((tm, LANES), F32)],
        compiler_params=_cparams(1),
        name="mixout",
    )(x2, o2, hy2, wmg, wao, wro, wout, lng, lnb, wr, br)


def _perm_matrix(route, soff_row, w1, w2):
    td = route.shape[0]
    lane = lax.broadcasted_iota(jnp.int32, (td, LANES), 1).astype(F32)
    e1 = lane == route[:, 0:1]
    e2 = lane == route[:, 1:2]
    cnt = (e1.astype(F32) + e2.astype(F32)).astype(BF16)
    ti = lax.broadcasted_iota(jnp.int32, (td, td), 0)
    tj = lax.broadcasted_iota(jnp.int32, (td, td), 1)
    lower = (tj < ti).astype(BF16)
    pos = jnp.dot(lower, cnt, preferred_element_type=F32) + soff_row
    r1 = jnp.sum(jnp.where(e1, pos, 0.0), axis=-1, keepdims=True).astype(jnp.int32)
    r2 = jnp.sum(jnp.where(e2, pos, 0.0), axis=-1, keepdims=True).astype(jnp.int32)
    col = lax.broadcasted_iota(jnp.int32, (td, STAGE_ROWS), 1)
    return jnp.where(col == r1, w1, 0.0) + jnp.where(col == r2, w2, 0.0)


def _chunk_rows(c):
    return pl.ds(pl.multiple_of(c * CHUNK, CHUNK), CHUNK)


def _block_rows(b):
    return pl.ds(pl.multiple_of(b * MOE_BLOCK, MOE_BLOCK), MOE_BLOCK)


def _for_each_chunk(i, cch_ref, soff_ref, dch_ref, fn):
    def per_expert(e, carry):
        k = i * N_EXPERTS + e
        so = soff_ref[k]
        do = dch_ref[k]

        def per_chunk(c, carry2):
            fn(so + c, do + c)
            return carry2

        return lax.fori_loop(0, cch_ref[k], per_chunk, carry)

    lax.fori_loop(0, N_EXPERTS, per_expert, 0)


def _repeat(n, fn):
    def body(c, carry):
        fn()
        return carry

    lax.fori_loop(0, n, body, 0)


def _dispatch_kernel(cch_ref, soff_ref, dch_ref, tot_ref, padst_ref, padn_ref, nb_ref,
                     x_ref, route_ref, soffrow_ref, xb_ref, stage_ref, zero_ref, sems, zsem):
    i = pl.program_id(0)
    nt = pl.num_programs(0)
    n_blocks = xb_ref.shape[0] // MOE_BLOCK
    slot = i % 2

    def copy(s, src_chunk, dst_chunk):
        return pltpu.make_async_copy(stage_ref.at[s, _chunk_rows(src_chunk)],
                                     xb_ref.at[_chunk_rows(dst_chunk)], sems.at[s])

    @pl.when(i >= 2)
    def _():
        _repeat(tot_ref[jnp.maximum(i - 2, 0)], lambda: copy(slot, 0, 0).wait())

    pt = _perm_matrix(route_ref[...], soffrow_ref[0], 1.0, 1.0).astype(BF16)
    stage_ref[slot] = lax.dot_general(pt, x_ref[...], (((0,), (0,)), ((), ())),
                                      preferred_element_type=F32).astype(BF16)
    _for_each_chunk(i, cch_ref, soff_ref, dch_ref,
                    lambda s, d: copy(slot, s, d).start(priority=TILE_DMA_PRIORITY))

    @pl.when(i == nt - 1)
    def _():
        @pl.when(i >= 1)
        def _():
            _repeat(tot_ref[jnp.maximum(i - 1, 0)], lambda: copy(1 - slot, 0, 0).wait())

        _repeat(tot_ref[i], lambda: copy(slot, 0, 0).wait())

        zero_ref[...] = jnp.zeros_like(zero_ref)

        def zcopy(dst_chunk):
            return pltpu.make_async_copy(zero_ref.at[pl.ds(0, CHUNK)],
                                         xb_ref.at[_chunk_rows(dst_chunk)], zsem)

        def pad_start(e, n):
            def pad_chunk(c, carry2):
                zcopy(padst_ref[e] + c).start()
                return carry2

            lax.fori_loop(0, padn_ref[e], pad_chunk, 0)
            return n + padn_ref[e]

        npad = lax.fori_loop(0, N_EXPERTS, pad_start, 0)
        _repeat(npad, lambda: zcopy(0).wait())

        def zblock(b):
            return pltpu.make_async_copy(zero_ref, xb_ref.at[_block_rows(b)], zsem)

        def start_block(b, carry):
            zblock(b).start()
            return carry

        def wait_block(b, carry):
            zblock(b).wait()
            return carry

        lax.fori_loop(nb_ref[0], n_blocks, start_block, 0)
        lax.fori_loop(nb_ref[0], n_blocks, wait_block, 0)


def _dispatch(x1, route, soffrow, tables, n_rows):
    t, d = x1.shape
    nt = t // TD
    grid_spec = pltpu.PrefetchScalarGridSpec(
        num_scalar_prefetch=7,
        grid=(nt,),
        in_specs=[pl.BlockSpec((TD, d), lambda i, *_: (i, 0)),
                  pl.BlockSpec((TD, LANES), lambda i, *_: (i, 0)),
                  pl.BlockSpec((1, 1, LANES), lambda i, *_: (i, 0, 0))],
        out_specs=pl.BlockSpec(memory_space=pl.ANY),
        scratch_shapes=[pltpu.VMEM((2, STAGE_ROWS, d), BF16), pltpu.VMEM((MOE_BLOCK, d), BF16),
                        pltpu.SemaphoreType.DMA((2,)), pltpu.SemaphoreType.DMA(())],
    )
    return pl.pallas_call(
        _dispatch_kernel,
        grid_spec=grid_spec,
        out_shape=jax.ShapeDtypeStruct((n_rows, d), BF16),
        compiler_params=_cparams(1),
        name="dispatch",
    )(*tables, x1, route, soffrow)


def _experts_kernel(pst_ref, ntile_ref, nb_ref, xb_ref, wg_ref, wu_ref, wd_ref, yb_ref,
                    wgb_ref, wub_ref, wdb_ref, xbuf_ref, ybuf_ref, in_sems, out_sems, zsem):
    e = pl.program_id(0)
    ne = pl.num_programs(0)
    n = ntile_ref[e]
    row0 = pst_ref[e]
    n_blocks = yb_ref.shape[0] // MOE_BLOCK
    ahead = EXP_BUFS - 1

    def tile_rows(r0, t):
        return pl.ds(pl.multiple_of(r0 + t * EXP_TILE, MOE_BLOCK), EXP_TILE)

    def in_copy(r0, t, s):
        return pltpu.make_async_copy(xb_ref.at[tile_rows(r0, t)], xbuf_ref.at[s], in_sems.at[s])

    def out_copy(t, s):
        return pltpu.make_async_copy(ybuf_ref.at[s], yb_ref.at[tile_rows(row0, t)],
                                     out_sems.at[s])

    def start_head(r0, count):
        for k in range(ahead):
            @pl.when(k < count)
            def _():
                in_copy(r0, k, k).start(priority=TILE_DMA_PRIORITY)

    @pl.when(e == 0)
    def _():
        start_head(row0, n)

    @pl.when(n > 0)
    def _():
        wgb_ref[...] = wg_ref[0, 0].astype(BF16)
        wub_ref[...] = wu_ref[0, 0].astype(BF16)
        wdb_ref[...] = wd_ref[0, 0].astype(BF16)

    def tile(t, carry):
        s = t % EXP_BUFS

        @pl.when(t + ahead < n)
        def _():
            in_copy(row0, t + ahead, (t + ahead) % EXP_BUFS).start(priority=TILE_DMA_PRIORITY)

        in_copy(row0, t, s).wait()

        @pl.when(t >= EXP_BUFS)
        def _():
            out_copy(t - EXP_BUFS, s).wait()

        x = xbuf_ref[s]
        hg = jnp.dot(x, wgb_ref[...], preferred_element_type=F32)
        hu = jnp.dot(x, wub_ref[...], preferred_element_type=F32)
        hid = (hg * _sigmoid(hg)) * hu
        ybuf_ref[s] = jnp.dot(hid.astype(BF16), wdb_ref[...],
                              preferred_element_type=F32).astype(BF16)
        out_copy(t, s).start(priority=TILE_DMA_PRIORITY)
        return carry

    lax.fori_loop(0, n, tile, 0)

    @pl.when(e + 1 < ne)
    def _():
        nxt = jnp.minimum(e + 1, ne - 1)
        start_head(pst_ref[nxt], ntile_ref[nxt])

    for k in range(EXP_BUFS):
        @pl.when(n > k)
        def _():
            out_copy(n - 1 - k, (n - 1 - k) % EXP_BUFS).wait()

    @pl.when(e == ne - 1)
    def _():
        ybuf_ref[0] = jnp.zeros(ybuf_ref.shape[1:], BF16)

        def zblock(b):
            return pltpu.make_async_copy(ybuf_ref.at[0, pl.ds(0, MOE_BLOCK)],
                                         yb_ref.at[_block_rows(b)], zsem)

        def start_block(b, carry):
            zblock(b).start()
            return carry

        def wait_block(b, carry):
            zblock(b).wait()
            return carry

        lax.fori_loop(nb_ref[0], n_blocks, start_block, 0)
        lax.fori_loop(nb_ref[0], n_blocks, wait_block, 0)


def _experts(xb, pst, ntile, nb_used, w_g, w_u, w_d, layer):
    n_rows, d = xb.shape
    n_exp, de = w_g.shape[1], w_g.shape[3]
    wsel = lambda e, *_: (layer, e, 0, 0)
    grid_spec = pltpu.PrefetchScalarGridSpec(
        num_scalar_prefetch=3,
        grid=(n_exp,),
        in_specs=[pl.BlockSpec(memory_space=pl.ANY),
                  pl.BlockSpec((1, 1, d, de), wsel),
                  pl.BlockSpec((1, 1, d, de), wsel),
                  pl.BlockSpec((1, 1, de, d), wsel)],
        out_specs=pl.BlockSpec(memory_space=pl.ANY),
        scratch_shapes=[pltpu.VMEM((d, de), BF16), pltpu.VMEM((d, de), BF16),
                        pltpu.VMEM((de, d), BF16),
                        pltpu.VMEM((EXP_BUFS, EXP_TILE, d), BF16),
                        pltpu.VMEM((EXP_BUFS, EXP_TILE, d), BF16),
                        pltpu.SemaphoreType.DMA((EXP_BUFS,)), pltpu.SemaphoreType.DMA((EXP_BUFS,)),
                        pltpu.SemaphoreType.DMA(())],
    )
    return pl.pallas_call(
        _experts_kernel,
        grid_spec=grid_spec,
        out_shape=jax.ShapeDtypeStruct((n_rows, d), BF16),
        compiler_params=_cparams(1),
        name="experts",
    )(pst, ntile, nb_used, xb, w_g, w_u, w_d)


def _combine_kernel(cch_ref, soff_ref, dch_ref, tot_ref,
                    x_ref, route_ref, soffrow_ref, yb_ref, lng_ref, lnb_ref, out_ref,
                    stage_ref, sems, *, alpha):
    i = pl.program_id(0)
    nt = pl.num_programs(0)
    slot = i % 2

    def copy(s, src_chunk, dst_chunk):
        return pltpu.make_async_copy(yb_ref.at[_chunk_rows(src_chunk)],
                                     stage_ref.at[s, _chunk_rows(dst_chunk)], sems.at[s])

    def fetch(tile_idx, s):
        _for_each_chunk(tile_idx, cch_ref, soff_ref, dch_ref,
                        lambda so, do: copy(s, do, so).start(priority=TILE_DMA_PRIORITY))

    @pl.when(i == 0)
    def _():
        stage_ref[...] = jnp.zeros_like(stage_ref)
        fetch(i, slot)

    @pl.when(i + 1 < nt)
    def _():
        fetch(i + 1, 1 - slot)

    _repeat(tot_ref[i], lambda: copy(slot, 0, 0).wait())

    route = route_ref[...]
    pt = _perm_matrix(route, soffrow_ref[0], route[:, 2:3], route[:, 3:4]).astype(BF16)
    ffn = jnp.dot(pt, stage_ref[slot], preferred_element_type=F32)
    out_ref[...] = _layer_norm(alpha * x_ref[...] + ffn, lng_ref[...], lnb_ref[...])


def _combine(x1, route, soffrow, tables, yb, lng, lnb, alpha):
    t, d = x1.shape
    nt = t // TD
    grid_spec = pltpu.PrefetchScalarGridSpec(
        num_scalar_prefetch=4,
        grid=(nt,),
        in_specs=[pl.BlockSpec((TD, d), lambda i, *_: (i, 0)),
                  pl.BlockSpec((TD, LANES), lambda i, *_: (i, 0)),
                  pl.BlockSpec((1, 1, LANES), lambda i, *_: (i, 0, 0)),
                  pl.BlockSpec(memory_space=pl.ANY),
                  pl.BlockSpec((1, d), lambda i, *_: (0, 0)),
                  pl.BlockSpec((1, d), lambda i, *_: (0, 0))],
        out_specs=pl.BlockSpec((TD, d), lambda i, *_: (i, 0)),
        scratch_shapes=[pltpu.VMEM((2, STAGE_ROWS, d), BF16), pltpu.SemaphoreType.DMA((2,))],
    )
    return pl.pallas_call(
        functools.partial(_combine_kernel, alpha=alpha),
        grid_spec=grid_spec,
        out_shape=jax.ShapeDtypeStruct((t, d), F32),
        compiler_params=_cparams(1),
        name="combine",
    )(*tables, x1, route, soffrow, yb, lng, lnb)


def _max_blocks(t):
    nt = t // TD
    rows = 2 * t + (CHUNK - 1) * nt * N_EXPERTS + (MOE_BLOCK - CHUNK) * N_EXPERTS
    return -(-rows // MOE_BLOCK)


def _dispatch_tables(cnt):
    n = cnt[:, 0, :N_EXPERTS].astype(jnp.int32)
    cch = (n + CHUNK - 1) // CHUNK
    soff = jnp.cumsum(cch, axis=1) - cch
    tot_tile = jnp.sum(cch, axis=1)
    tot_e = jnp.sum(cch, axis=0)
    reg = (tot_e + CHUNKS_PER_BLOCK - 1) // CHUNKS_PER_BLOCK * CHUNKS_PER_BLOCK
    pend = jnp.cumsum(reg)
    pstart = pend - reg
    dch = pstart[None, :] + jnp.cumsum(cch, axis=0) - cch
    nb_used = pend[-1] // CHUNKS_PER_BLOCK
    soffrow = jnp.zeros((n.shape[0], 1, LANES), F32).at[:, 0, :N_EXPERTS].set(
        (soff * CHUNK).astype(F32))
    i32 = lambda a: a.reshape(-1).astype(jnp.int32)
    return dict(cch=i32(cch), soff=i32(soff), dch=i32(dch), tot=i32(tot_tile),
                padst=i32(pstart + tot_e), padn=i32(reg - tot_e), nb_used=i32(nb_used),
                pst=i32(pstart * CHUNK), ntile=i32((reg * CHUNK + EXP_TILE - 1) // EXP_TILE),
                soffrow=soffrow)


def _rope_tables(seq):
    inv = ROPE_THETA ** (-jnp.arange(0, HEAD_DIM, 2, dtype=F32) / HEAD_DIM)
    ang = jnp.arange(seq, dtype=F32)[:, None] * inv[None, :]
    cos, sin = jnp.cos(ang), jnp.sin(ang)
    return jnp.concatenate([cos, cos], axis=1), jnp.concatenate([-sin, sin], axis=1)


@jax.jit
def kernel(x, w_in, w_sink, w_conv, b_conv, w_rec_gate, b_rec_gate, w_in_gate, b_in_gate,
           lru_lambda, w_attn_o, w_rnn_o, w_out, ln_g, ln_b, w_router_group, b_router_group,
           w_router_expert, b_router_expert, w_exp_gate, w_exp_up, w_exp_down):
    bsz, seq, d = x.shape
    depth = w_in.shape[0]
    t = bsz * seq
    nblk = d // LANES
    alpha = (2 * depth) ** 0.25
    cosf, sinf = _rope_tables(seq)
    n_rows = _max_blocks(t) * MOE_BLOCK + (EXP_TILE - MOE_BLOCK)
    x2 = x.reshape(t, d)
    for l in range(depth):
        n_branch = w_in.shape[2] - 2 * d
        q, k, v, xr, yr = _inproj(x2, w_in[l, :, :n_branch].astype(BF16), cosf, sinf, seq)
        o = _attention(q.reshape(bsz, seq, -1), k.reshape(bsz, seq, -1), v.reshape(bsz, seq, -1),
                       w_sink[l])
        wg_cat = (0.5 * jnp.concatenate([w_rec_gate[l, 0], w_in_gate[l, 0], w_rec_gate[l, 1],
                                         w_in_gate[l, 1]], axis=-1)).astype(BF16)
        bg_cat = jnp.concatenate([b_rec_gate[l, 0].reshape(nblk, 1, LANES),
                                  b_in_gate[l, 0].reshape(nblk, 1, LANES),
                                  b_rec_gate[l, 1].reshape(nblk, 1, LANES),
                                  b_in_gate[l, 1].reshape(nblk, 1, LANES)], axis=-1)
        lam_cat = jnp.concatenate([lru_lambda[l, 0].reshape(nblk, 1, LANES),
                                   lru_lambda[l, 1].reshape(nblk, 1, LANES)], axis=-1)
        hy = _rnn(xr.reshape(bsz, seq, d), yr.reshape(bsz, seq, d), w_conv[l],
                  b_conv[l].reshape(nblk, 1, LANES), wg_cat, bg_cat, lam_cat)
        wr = jnp.zeros((d, LANES), F32)
        wr = wr.at[:, :N_EXPERTS].set(w_router_expert[l])
        wr = wr.at[:, N_EXPERTS:N_EXPERTS + N_GROUPS].set(w_router_group[l]).astype(BF16)
        br = jnp.zeros((1, LANES), F32)
        br = br.at[0, :N_EXPERTS].set(b_router_expert[l])
        br = br.at[0, N_EXPERTS:N_EXPERTS + N_GROUPS].set(b_router_group[l])
        x1, x1b, route, cnt = _mixout(x2, o.reshape(t, -1), hy.reshape(t, d),
                                 w_in[l, :, n_branch:].astype(BF16),
                                 w_attn_o[l].astype(BF16), w_rnn_o[l].astype(BF16),
                                 w_out[l].astype(BF16), ln_g[l, 0].reshape(1, d),
                                 ln_b[l, 0].reshape(1, d), wr, br, alpha)
        tb = _dispatch_tables(cnt)
        xb = _dispatch(x1b, route, tb["soffrow"],
                       (tb["cch"], tb["soff"], tb["dch"], tb["tot"], tb["padst"], tb["padn"],
                        tb["nb_used"]), n_rows)
        yb = _experts(xb, tb["pst"], tb["ntile"], tb["nb_used"], w_exp_gate, w_exp_up,
                      w_exp_down, l)
        x2 = _combine(x1, route, tb["soffrow"], (tb["cch"], tb["soff"], tb["dch"], tb["tot"]),
                      yb, ln_g[l, 1].reshape(1, d), ln_b[l, 1].reshape(1, d), alpha)
    return x2.reshape(bsz, seq, d)
```

```python
import functools
import math

import jax
import jax.numpy as jnp
from jax import lax
from jax.experimental import pallas as pl
from jax.experimental.pallas import tpu as pltpu

F32 = jnp.float32
BF16 = jnp.bfloat16

HEAD_DIM = 128
N_Q_HEADS = 8
N_KV_HEADS = 2
Q_PER_KV = N_Q_HEADS // N_KV_HEADS
WINDOW = 128
ROPE_THETA = 10000.0
CONV_W = 4
LRU_C = 8.0
N_GROUPS = 4
EXPERTS_PER_GROUP = 8
N_EXPERTS = N_GROUPS * EXPERTS_PER_GROUP
LN_EPS = 1e-5
NEG = -1e30

SUBLANES = 8
LANES = 128

TM_PROJ = 512
TQ_ATTN = 512
SEG = 256
SCAN_LEN = SEG + 4
SCAN_UNROLL = 10
BF16_ROWS = 2 * SUBLANES
TD = 512
MIX_ROWS = 256
CHUNK = BF16_ROWS
STAGE_ROWS = 2 * TD + CHUNK * N_EXPERTS
MOE_BLOCK = 128
CHUNKS_PER_BLOCK = MOE_BLOCK // CHUNK
EXP_TILE = 2 * MOE_BLOCK
EXP_BUFS = 6
VMEM_LIMIT = 56 * 1024 * 1024
TILE_DMA_PRIORITY = 1


def _cparams(n_axes):
    return pltpu.CompilerParams(dimension_semantics=("arbitrary",) * n_axes,
                                vmem_limit_bytes=VMEM_LIMIT)


def _softplus(z):
    e = jnp.exp(-jnp.abs(z))
    w = 1.0 + e
    tiny = w == 1.0
    log1p = jnp.where(tiny, e, jnp.log(w) * (e / jnp.where(tiny, 1.0, w - 1.0)))
    return jnp.maximum(z, 0.0) + log1p


def _sigmoid(x):
    return 0.5 * jnp.tanh(0.5 * x) + 0.5


def _gelu_tanh(y):
    c1 = math.sqrt(2.0 / math.pi)
    half = 0.5 * y
    return half + half * jnp.tanh(y * (c1 + (c1 * 0.044715) * (y * y)))


def _layer_norm(y, g, b):
    mu = jnp.mean(y, axis=-1, keepdims=True)
    d = y - mu
    var = jnp.mean(d * d, axis=-1, keepdims=True)
    return d * lax.rsqrt(var + LN_EPS) * g + b


def _inproj_kernel(x_ref, w_ref, cos_ref, sin_ref,
                   q_ref, k_ref, v_ref, xr_ref, yr_ref, *, d_model):
    xb = x_ref[...].astype(BF16)
    cos = cos_ref[...]
    sin = sin_ref[...]
    aw = N_Q_HEADS * HEAD_DIM
    kw = N_KV_HEADS * HEAD_DIM

    def proj(c0, n):
        return jnp.dot(xb, w_ref[:, c0:c0 + n], preferred_element_type=F32)

    def rope(t):
        return t * cos + pltpu.roll(t, HEAD_DIM // 2, 1) * sin

    zq = proj(0, aw)
    scale = HEAD_DIM ** -0.5
    for h in range(N_Q_HEADS):
        sl = slice(h * HEAD_DIM, (h + 1) * HEAD_DIM)
        q_ref[:, sl] = (rope(zq[:, sl]) * scale).astype(BF16)
    zk = proj(aw, kw)
    for h in range(N_KV_HEADS):
        sl = slice(h * HEAD_DIM, (h + 1) * HEAD_DIM)
        k_ref[:, sl] = rope(zk[:, sl]).astype(BF16)
    v_ref[...] = proj(aw + kw, kw).astype(BF16)
    c0 = aw + 2 * kw
    xr_ref[...] = proj(c0, d_model)
    yr_ref[...] = proj(c0 + d_model, d_model)


def _inproj(x2, w_in_b, cosf, sinf, seq):
    t, d = x2.shape
    n_in = w_in_b.shape[1]
    tm = TM_PROJ
    aw = N_Q_HEADS * HEAD_DIM
    kw = N_KV_HEADS * HEAD_DIM
    tiles_per_seq = seq // tm
    row = lambda i: (i, 0)
    pos = lambda i: (i % tiles_per_seq, 0)
    outs = [jax.ShapeDtypeStruct((t, aw), BF16), jax.ShapeDtypeStruct((t, kw), BF16),
            jax.ShapeDtypeStruct((t, kw), BF16)] + [jax.ShapeDtypeStruct((t, d), F32)] * 2
    return pl.pallas_call(
        functools.partial(_inproj_kernel, d_model=d),
        grid=(t // tm,),
        in_specs=[pl.BlockSpec((tm, d), row),
                  pl.BlockSpec((d, n_in), lambda i: (0, 0)),
                  pl.BlockSpec((tm, HEAD_DIM), pos),
                  pl.BlockSpec((tm, HEAD_DIM), pos)],
        out_specs=[pl.BlockSpec((tm, aw), row), pl.BlockSpec((tm, kw), row),
                   pl.BlockSpec((tm, kw), row)] + [pl.BlockSpec((tm, d), row)] * 2,
        out_shape=outs,
        compiler_params=_cparams(1),
        name="inproj",
    )(x2, w_in_b, cosf, sinf)


def _attn_kernel(sink_ref, q_ref, kp_ref, kc_ref, kn_ref, vp_ref, vc_ref, vn_ref, o_ref,
                 *, seq):
    i = pl.program_id(1)
    tq = q_ref.shape[1]
    blk = WINDOW
    t0 = i * tq
    kext = jnp.concatenate([kp_ref[0], kc_ref[0], kn_ref[0]], axis=0)
    vext = jnp.concatenate([vp_ref[0], vc_ref[0], vn_ref[0]], axis=0)
    nrow = Q_PER_KV * blk
    blk_shift = blk.bit_length() - 1
    qi = lax.broadcasted_iota(jnp.int32, (nrow, 3 * blk), 0) & (blk - 1)
    kj = lax.broadcasted_iota(jnp.int32, (nrow, 3 * blk), 1)
    band = jnp.where(jnp.abs(kj - blk - qi) <= WINDOW, 0.0, NEG)
    kj_row = lax.broadcasted_iota(jnp.int32, (1, 3 * blk), 1)
    rowg = lax.broadcasted_iota(jnp.int32, (nrow, 1), 0) >> blk_shift
    sinks = []
    for h in range(N_KV_HEADS):
        sk = jnp.full((nrow, 1), sink_ref[h * Q_PER_KV], F32)
        for g in range(1, Q_PER_KV):
            sk = jnp.where(rowg == g, sink_ref[h * Q_PER_KV + g], sk)
        sinks.append(sk)
    for j in range(tq // blk):
        kpos = t0 + j * blk - blk + kj_row
        bias = band + jnp.where((kpos >= 0) & (kpos < seq), 0.0, NEG)
        for h in range(N_KV_HEADS):
            hs = slice(h * HEAD_DIM, (h + 1) * HEAD_DIM)
            kblk = kext[j * blk:j * blk + 3 * blk, hs]
            vblk = vext[j * blk:j * blk + 3 * blk, hs]
            qs = [q_ref[0, j * blk:(j + 1) * blk,
                        (h * Q_PER_KV + g) * HEAD_DIM:(h * Q_PER_KV + g + 1) * HEAD_DIM]
                  for g in range(Q_PER_KV)]
            qblk = jnp.concatenate(qs, axis=0)
            s = lax.dot_general(qblk, kblk, (((1,), (1,)), ((), ())),
                                preferred_element_type=F32) + bias
            sk = sinks[h]
            m = jnp.maximum(jnp.max(s, axis=-1, keepdims=True), sk)
            p = jnp.exp(s - m)
            denom = jnp.sum(p, axis=-1, keepdims=True) + jnp.exp(sk - m)
            o = jnp.dot(p.astype(BF16), vblk, preferred_element_type=F32) / denom
            for g in range(Q_PER_KV):
                c = (h * Q_PER_KV + g) * HEAD_DIM
                o_ref[0, j * blk:(j + 1) * blk, c:c + HEAD_DIM] = (
                    o[g * blk:(g + 1) * blk].astype(BF16))


def _attention(q3, k3, v3, sink):
    b, s, aw = q3.shape
    kw = k3.shape[2]
    tq = TQ_ATTN
    blk = WINDOW
    r = tq // blk
    nblk = s // blk
    cur = lambda bi, i: (bi, i, 0)
    prev = lambda bi, i: (bi, jnp.maximum(i * r - 1, 0), 0)
    nxt = lambda bi, i: (bi, jnp.minimum((i + 1) * r, nblk - 1), 0)
    kv_specs = [pl.BlockSpec((1, blk, kw), prev), pl.BlockSpec((1, tq, kw), cur),
                pl.BlockSpec((1, blk, kw), nxt)]
    return pl.pallas_call(
        functools.partial(_attn_kernel, seq=s),
        grid=(b, s // tq),
        in_specs=[pl.BlockSpec(memory_space=pltpu.SMEM),
                  pl.BlockSpec((1, tq, aw), cur)] + kv_specs + kv_specs,
        out_specs=pl.BlockSpec((1, tq, aw), cur),
        out_shape=jax.ShapeDtypeStruct((b, s, aw), BF16),
        compiler_params=_cparams(2),
        name="attention",
    )(sink, q3, k3, k3, k3, v3, v3, v3)


def _rnn_kernel(xr_ref, yr_ref, wc_ref, bc_ref, wg_ref, bg_ref, lam_ref, out_ref,
                xpad_ref, af_ref, uf_ref, ab_ref, ub_ref, hb_ref, cf_ref, cb_ref, *, seq):
    nseg = seq // SEG
    nlane = nseg
    ngrp = nlane // SUBLANES
    hf_ref = xpad_ref
    wc = wc_ref[...]
    bc = bc_ref[0]
    wg = wg_ref[0]
    bg = 0.5 * bg_ref[0]
    lam = lam_ref[0]
    rate = (0.5 * LRU_C) * _softplus(-lam)
    a_refs = (af_ref, ab_ref)
    u_refs = (uf_ref, ub_ref)
    left = CONV_W // 2

    halo = jnp.zeros((SUBLANES, LANES), F32)
    xpad_ref[pl.ds(0, SUBLANES), :] = halo
    xpad_ref[pl.ds(seq + SUBLANES, SUBLANES), :] = halo

    def pad_copy(c, carry):
        t0 = pl.multiple_of(c * SEG, SEG)
        xpad_ref[pl.ds(t0 + SUBLANES, SEG), :] = xr_ref[0, pl.ds(t0, SEG), :]
        return carry

    lax.fori_loop(0, nseg, pad_copy, 0)

    tail = nlane * SCAN_LEN - seq
    for a_ref, u_ref in zip(a_refs, u_refs):
        a_ref[pl.ds(seq, tail), :] = jnp.ones((tail, LANES), F32)
        u_ref[pl.ds(seq, tail), :] = jnp.zeros((tail, LANES), F32)

    def gates(c, carry):
        t0 = pl.multiple_of(c * SEG, SEG)
        xc = bc
        for tap in range(CONV_W):
            xc = xc + xpad_ref[pl.ds(t0 + SUBLANES - left + tap, SEG), :] * wc[tap:tap + 1]
        gh = jnp.dot(xc.astype(BF16), wg, preferred_element_type=F32) + bg
        xch = 0.5 * xc
        for d in range(2):
            rt = rate[:, d * LANES:(d + 1) * LANES]
            nlog_a = rt * jnp.tanh(gh[:, (2 * d) * LANES:(2 * d + 1) * LANES]) + rt
            a = jnp.exp2(nlog_a * (-1.0 / math.log(2.0)))
            z = jnp.tanh(nlog_a) * (a * a + 1.0)
            mult = jnp.where(z > 0.0, z * lax.rsqrt(z), 0.0)
            in_gate2 = jnp.tanh(gh[:, (2 * d + 1) * LANES:(2 * d + 2) * LANES]) + 1.0
            a_refs[d][pl.ds(t0, SEG), :] = a
            u_refs[d][pl.ds(t0, SEG), :] = (xch * mult) * in_gate2
        return carry

    lax.fori_loop(0, nseg, gates, 0, unroll=2)

    def lane_rows(g, j):
        return pl.ds(g * SUBLANES * SCAN_LEN + j, SUBLANES, stride=SCAN_LEN)

    def totals_step(j, carry):
        hf, pf, hb, pb = carry
        jb = SCAN_LEN - 1 - j
        nhf, npf, nhb, npb = [], [], [], []
        for g in range(ngrp):
            a = af_ref[lane_rows(g, j), :]
            nhf.append(a * hf[g] + uf_ref[lane_rows(g, j), :])
            npf.append(a * pf[g])
            a = ab_ref[lane_rows(g, jb), :]
            nhb.append(a * hb[g] + ub_ref[lane_rows(g, jb), :])
            npb.append(a * pb[g])
        return tuple(nhf), tuple(npf), tuple(nhb), tuple(npb)

    zero = tuple(jnp.zeros((SUBLANES, LANES), F32) for _ in range(ngrp))
    one = tuple(jnp.ones((SUBLANES, LANES), F32) for _ in range(ngrp))
    hf, pf, hb, pb = lax.fori_loop(0, SCAN_LEN, totals_step, (zero, one, zero, one),
                                   unroll=SCAN_UNROLL)

    c = jnp.zeros((1, LANES), F32)
    for s in range(nlane):
        g, r = divmod(s, SUBLANES)
        cf_ref[s:s + 1, :] = c
        c = pf[g][r:r + 1] * c + hf[g][r:r + 1]
    c = jnp.zeros((1, LANES), F32)
    for s in range(nlane - 1, -1, -1):
        g, r = divmod(s, SUBLANES)
        cb_ref[s:s + 1, :] = c
        c = pb[g][r:r + 1] * c + hb[g][r:r + 1]

    def scan_step(j, carry):
        hf, hb = carry
        jb = SCAN_LEN - 1 - j
        nhf, nhb = [], []
        for g in range(ngrp):
            h = af_ref[lane_rows(g, j), :] * hf[g] + uf_ref[lane_rows(g, j), :]
            hf_ref[lane_rows(g, j), :] = h
            nhf.append(h)
            h = ab_ref[lane_rows(g, jb), :] * hb[g] + ub_ref[lane_rows(g, jb), :]
            hb_ref[lane_rows(g, jb), :] = h
            nhb.append(h)
        return tuple(nhf), tuple(nhb)

    hf0 = tuple(cf_ref[g * SUBLANES:(g + 1) * SUBLANES, :] for g in range(ngrp))
    hb0 = tuple(cb_ref[g * SUBLANES:(g + 1) * SUBLANES, :] for g in range(ngrp))
    lax.fori_loop(0, SCAN_LEN, scan_step, (hf0, hb0), unroll=SCAN_UNROLL)

    def finish(c, carry):
        t0 = pl.multiple_of(c * SEG, SEG)
        h = hf_ref[pl.ds(t0, SEG), :] + hb_ref[pl.ds(t0, SEG), :]
        out_ref[0, pl.ds(t0, SEG), :] = (h * _gelu_tanh(yr_ref[0, pl.ds(t0, SEG), :])).astype(BF16)
        return carry

    lax.fori_loop(0, nseg, finish, 0)


def _rnn(xr3, yr3, w_conv, b_conv, wg_cat, bg_cat, lam_cat):
    b, s, d = xr3.shape
    nblk = d // LANES
    nseg = s // SEG
    slab = lambda bi, n: (bi, 0, n)
    per_blk = lambda bi, n: (n, 0, 0)
    assert nseg % SUBLANES == 0 and nseg * SCAN_LEN >= s + 2 * SUBLANES
    scratch = ([pltpu.VMEM((nseg * SCAN_LEN, LANES), F32)] * 6
               + [pltpu.VMEM((nseg, LANES), F32)] * 2)
    return pl.pallas_call(
        functools.partial(_rnn_kernel, seq=s),
        grid=(b, nblk),
        in_specs=[pl.BlockSpec((1, s, LANES), slab),
                  pl.BlockSpec((1, s, LANES), slab),
                  pl.BlockSpec((CONV_W, LANES), lambda bi, n: (0, n)),
                  pl.BlockSpec((1, 1, LANES), per_blk),
                  pl.BlockSpec((1, LANES, 4 * LANES), per_blk),
                  pl.BlockSpec((1, 1, 4 * LANES), per_blk),
                  pl.BlockSpec((1, 1, 2 * LANES), per_blk)],
        out_specs=pl.BlockSpec((1, s, LANES), slab),
        out_shape=jax.ShapeDtypeStruct((b, s, d), BF16),
        scratch_shapes=scratch,
        compiler_params=_cparams(2),
        name="rglru",
    )(xr3, yr3, w_conv, b_conv, wg_cat, bg_cat, lam_cat)


def _route(logits):
    lane = lax.broadcasted_iota(jnp.int32, logits.shape, 1)
    lanef = lane.astype(F32)
    big = float(4 * LANES)
    gmask = (lane >= N_EXPERTS) & (lane < N_EXPERTS + N_GROUPS)
    gl = jnp.where(gmask, logits, NEG)
    gmax = jnp.max(gl, axis=-1, keepdims=True)
    ge = jnp.exp(gl - gmax)
    gprob = ge / jnp.sum(ge, axis=-1, keepdims=True)
    gval = jnp.max(gprob, axis=-1, keepdims=True)
    gidx = jnp.min(jnp.where((gprob == gval) & gmask, lanef, big), axis=-1, keepdims=True)
    gidx = gidx.astype(jnp.int32) - N_EXPERTS
    group_shift = EXPERTS_PER_GROUP.bit_length() - 1
    emask = (lane < N_EXPERTS) & ((lane >> group_shift) == gidx)
    el = jnp.where(emask, logits, NEG)
    m1 = jnp.max(el, axis=-1, keepdims=True)
    i1 = jnp.min(jnp.where((el == m1) & emask, lanef, big), axis=-1, keepdims=True)
    emask2 = emask & (lanef != i1)
    el2 = jnp.where(emask2, logits, NEG)
    m2 = jnp.max(el2, axis=-1, keepdims=True)
    i2 = jnp.min(jnp.where((el2 == m2) & emask2, lanef, big), axis=-1, keepdims=True)
    e2 = jnp.exp(m2 - m1)
    den = 1.0 + e2
    g1 = (1.0 / den) * gval
    g2 = (e2 / den) * gval
    route = jnp.where(lane == 0, i1, jnp.where(lane == 1, i2,
                      jnp.where(lane == 2, g1, jnp.where(lane == 3, g2, 0.0))))
    onehot = (lanef == i1).astype(F32) + (lanef == i2).astype(F32)
    return route, jnp.sum(onehot, axis=0, keepdims=True)


def _mixout_kernel(x_ref, o_ref, hy_ref, wmg_ref, wao_ref, wro_ref, wout_ref,
                   lng_ref, lnb_ref, wr_ref, br_ref, x1_ref, x1b_ref, route_ref, cnt_ref,
                   logits_ref, *, alpha):
    @pl.when(pl.program_id(0) == 0)
    def _():
        logits_ref[...] = jnp.zeros_like(logits_ref)

    route, cnt = _route(logits_ref[...])
    route_ref[...] = route
    cnt_ref[0] = cnt

    d = x_ref.shape[1]
    for r0 in range(0, x_ref.shape[0], MIX_ROWS):
        rows = pl.ds(r0, MIX_ROWS)
        x = x_ref[rows, :]
        xb = x.astype(BF16)
        ga = jnp.dot(xb, wmg_ref[:, :d], preferred_element_type=F32)
        gr = jnp.dot(xb, wmg_ref[:, d:], preferred_element_type=F32)
        ya = jnp.dot(o_ref[rows, :], wao_ref[...], preferred_element_type=F32)
        yr = jnp.dot(hy_ref[rows, :], wro_ref[...], preferred_element_type=F32)
        merged = _sigmoid(ga) * ya + _sigmoid(gr) * yr
        mix = jnp.dot(merged.astype(BF16), wout_ref[...], preferred_element_type=F32)
        x1 = _layer_norm(alpha * x + mix, lng_ref[...], lnb_ref[...])
        x1_ref[rows, :] = x1
        x1b = x1.astype(BF16)
        x1b_ref[rows, :] = x1b
        logits_ref[rows, :] = (jnp.dot(x1b, wr_ref[...], preferred_element_type=F32)
                               + br_ref[...])


def _mixout(x2, o2, hy2, wmg, wao, wro, wout, lng, lnb, wr, br, alpha):
    t, d = x2.shape
    tm = TD
    nt = t // tm
    row = lambda i: (jnp.minimum(i, nt - 1), 0)
    prev_row = lambda i: (jnp.maximum(i - 1, 0), 0)
    full = lambda i: (0, 0)
    return pl.pallas_call(
        functools.partial(_mixout_kernel, alpha=alpha),
        grid=(nt + 1,),
        in_specs=[pl.BlockSpec((tm, d), row)] * 3
                 + [pl.BlockSpec((d, 2 * d), full)]
                 + [pl.BlockSpec((d, d), full)] * 3
                 + [pl.BlockSpec((1, d), full)] * 2
                 + [pl.BlockSpec((d, LANES), full), pl.BlockSpec((1, LANES), full)],
        out_specs=[pl.BlockSpec((tm, d), row), pl.BlockSpec((tm, d), row),
                   pl.BlockSpec((tm, LANES), prev_row),
                   pl.BlockSpec((1, 1, LANES), lambda i: (jnp.maximum(i - 1, 0), 0, 0))],
        out_shape=[jax.ShapeDtypeStruct((t, d), F32), jax.ShapeDtypeStruct((t, d), BF16),
                   jax.ShapeDtypeStruct((t, LANES), F32),
                   jax.ShapeDtypeStruct((nt, 1, LANES), F32)],
        scratch_shapes=[pltpu.VMEM((tm, LANES), F32)],
        compiler_params=_cparams(1),
        name="mixout",
    )(x2, o2, hy2, wmg, wao, wro, wout, lng, lnb, wr, br)


def _perm_matrix(route, soff_row, w1, w2):
    td = route.shape[0]
    lane = lax.broadcasted_iota(jnp.int32, (td, LANES), 1).astype(F32)
    e1 = lane == route[:, 0:1]
    e2 = lane == route[:, 1:2]
    cnt = (e1.astype(F32) + e2.astype(F32)).astype(BF16)
    ti = lax.broadcasted_iota(jnp.int32, (td, td), 0)
    tj = lax.broadcasted_iota(jnp.int32, (td, td), 1)
    lower = (tj < ti).astype(BF16)
    pos = jnp.dot(lower, cnt, preferred_element_type=F32) + soff_row
    r1 = jnp.sum(jnp.where(e1, pos, 0.0), axis=-1, keepdims=True).astype(jnp.int32)
    r2 = jnp.sum(jnp.where(e2, pos, 0.0), axis=-1, keepdims=True).astype(jnp.int32)
    col = lax.broadcasted_iota(jnp.int32, (td, STAGE_ROWS), 1)
    return jnp.where(col == r1, w1, 0.0) + jnp.where(col == r2, w2, 0.0)


def _chunk_rows(c):
    return pl.ds(pl.multiple_of(c * CHUNK, CHUNK), CHUNK)


def _block_rows(b):
    return pl.ds(pl.multiple_of(b * MOE_BLOCK, MOE_BLOCK), MOE_BLOCK)


def _for_each_chunk(i, cch_ref, soff_ref, dch_ref, fn):
    def per_expert(e, carry):
        k = i * N_EXPERTS + e
        so = soff_ref[k]
        do = dch_ref[k]

        def per_chunk(c, carry2):
            fn(so + c, do + c)
            return carry2

        return lax.fori_loop(0, cch_ref[k], per_chunk, carry)

    lax.fori_loop(0, N_EXPERTS, per_expert, 0)


def _repeat(n, fn):
    def body(c, carry):
        fn()
        return carry

    lax.fori_loop(0, n, body, 0)


def _dispatch_kernel(cch_ref, soff_ref, dch_ref, tot_ref, padst_ref, padn_ref, nb_ref,
                     x_ref, route_ref, soffrow_ref, xb_ref, stage_ref, zero_ref, sems, zsem):
    i = pl.program_id(0)
    nt = pl.num_programs(0)
    n_blocks = xb_ref.shape[0] // MOE_BLOCK
    slot = i % 2

    def copy(s, src_chunk, dst_chunk):
        return pltpu.make_async_copy(stage_ref.at[s, _chunk_rows(src_chunk)],
                                     xb_ref.at[_chunk_rows(dst_chunk)], sems.at[s])

    @pl.when(i >= 2)
    def _():
        _repeat(tot_ref[jnp.maximum(i - 2, 0)], lambda: copy(slot, 0, 0).wait())

    pt = _perm_matrix(route_ref[...], soffrow_ref[0], 1.0, 1.0).astype(BF16)
    stage_ref[slot] = lax.dot_general(pt, x_ref[...], (((0,), (0,)), ((), ())),
                                      preferred_element_type=F32).astype(BF16)
    _for_each_chunk(i, cch_ref, soff_ref, dch_ref,
                    lambda s, d: copy(slot, s, d).start(priority=TILE_DMA_PRIORITY))

    @pl.when(i == nt - 1)
    def _():
        @pl.when(i >= 1)
        def _():
            _repeat(tot_ref[jnp.maximum(i - 1, 0)], lambda: copy(1 - slot, 0, 0).wait())

        _repeat(tot_ref[i], lambda: copy(slot, 0, 0).wait())

        zero_ref[...] = jnp.zeros_like(zero_ref)

        def zcopy(dst_chunk):
            return pltpu.make_async_copy(zero_ref.at[pl.ds(0, CHUNK)],
                                         xb_ref.at[_chunk_rows(dst_chunk)], zsem)

        def pad_start(e, n):
            def pad_chunk(c, carry2):
                zcopy(padst_ref[e] + c).start()
                return carry2

            lax.fori_loop(0, padn_ref[e], pad_chunk, 0)
            return n + padn_ref[e]

        npad = lax.fori_loop(0, N_EXPERTS, pad_start, 0)
        _repeat(npad, lambda: zcopy(0).wait())

        def zblock(b):
            return pltpu.make_async_copy(zero_ref, xb_ref.at[_block_rows(b)], zsem)

        def start_block(b, carry):
            zblock(b).start()
            return carry

        def wait_block(b, carry):
            zblock(b).wait()
            return carry

        lax.fori_loop(nb_ref[0], n_blocks, start_block, 0)
        lax.fori_loop(nb_ref[0], n_blocks, wait_block, 0)


def _dispatch(x1, route, soffrow, tables, n_rows):
    t, d = x1.shape
    nt = t // TD
    grid_spec = pltpu.PrefetchScalarGridSpec(
        num_scalar_prefetch=7,
        grid=(nt,),
        in_specs=[pl.BlockSpec((TD, d), lambda i, *_: (i, 0)),
                  pl.BlockSpec((TD, LANES), lambda i, *_: (i, 0)),
                  pl.BlockSpec((1, 1, LANES), lambda i, *_: (i, 0, 0))],
        out_specs=pl.BlockSpec(memory_space=pl.ANY),
        scratch_shapes=[pltpu.VMEM((2, STAGE_ROWS, d), BF16), pltpu.VMEM((MOE_BLOCK, d), BF16),
                        pltpu.SemaphoreType.DMA((2,)), pltpu.SemaphoreType.DMA(())],
    )
    return pl.pallas_call(
        _dispatch_kernel,
        grid_spec=grid_spec,
        out_shape=jax.ShapeDtypeStruct((n_rows, d), BF16),
        compiler_params=_cparams(1),
        name="dispatch",
    )(*tables, x1, route, soffrow)


def _experts_kernel(pst_ref, ntile_ref, nb_ref, xb_ref, wg_ref, wu_ref, wd_ref, yb_ref,
                    wgb_ref, wub_ref, wdb_ref, xbuf_ref, ybuf_ref, in_sems, out_sems, zsem):
    e = pl.program_id(0)
    ne = pl.num_programs(0)
    n = ntile_ref[e]
    row0 = pst_ref[e]
    n_blocks = yb_ref.shape[0] // MOE_BLOCK
    ahead = EXP_BUFS - 1

    def tile_rows(r0, t):
        return pl.ds(pl.multiple_of(r0 + t * EXP_TILE, MOE_BLOCK), EXP_TILE)

    def in_copy(r0, t, s):
        return pltpu.make_async_copy(xb_ref.at[tile_rows(r0, t)], xbuf_ref.at[s], in_sems.at[s])

    def out_copy(t, s):
        return pltpu.make_async_copy(ybuf_ref.at[s], yb_ref.at[tile_rows(row0, t)],
                                     out_sems.at[s])

    def start_head(r0, count):
        for k in range(ahead):
            @pl.when(k < count)
            def _():
                in_copy(r0, k, k).start(priority=TILE_DMA_PRIORITY)

    @pl.when(e == 0)
    def _():
        start_head(row0, n)

    @pl.when(n > 0)
    def _():
        wgb_ref[...] = wg_ref[0, 0].astype(BF16)
        wub_ref[...] = wu_ref[0, 0].astype(BF16)
        wdb_ref[...] = wd_ref[0, 0].astype(BF16)

    def tile(t, carry):
        s = t % EXP_BUFS

        @pl.when(t + ahead < n)
        def _():
            in_copy(row0, t + ahead, (t + ahead) % EXP_BUFS).start(priority=TILE_DMA_PRIORITY)

        in_copy(row0, t, s).wait()

        @pl.when(t >= EXP_BUFS)
        def _():
            out_copy(t - EXP_BUFS, s).wait()

        x = xbuf_ref[s]
        hg = jnp.dot(x, wgb_ref[...], preferred_element_type=F32)
        hu = jnp.dot(x, wub_ref[...], preferred_element_type=F32)
        hid = (hg * _sigmoid(hg)) * hu
        ybuf_ref[s] = jnp.dot(hid.astype(BF16), wdb_ref[...],
                              preferred_element_type=F32).astype(BF16)
        out_copy(t, s).start(priority=TILE_DMA_PRIORITY)
        return carry

    lax.fori_loop(0, n, tile, 0)

    @pl.when(e + 1 < ne)
    def _():
        nxt = jnp.minimum(e + 1, ne - 1)
        start_head(pst_ref[nxt], ntile_ref[nxt])

    for k in range(EXP_BUFS):
        @pl.when(n > k)
        def _():
            out_copy(n - 1 - k, (n - 1 - k) % EXP_BUFS).wait()

    @pl.when(e == ne - 1)
    def _():
        ybuf_ref[0] = jnp.zeros(ybuf_ref.shape[1:], BF16)

        def zblock(b):
            return pltpu.make_async_copy(ybuf_ref.at[0, pl.ds(0, MOE_BLOCK)],
                                         yb_ref.at[_block_rows(b)], zsem)

        def start_block(b, carry):
            zblock(b).start()
            return carry

        def wait_block(b, carry):
            zblock(b).wait()
            return carry

        lax.fori_loop(nb_ref[0], n_blocks, start_block, 0)
        lax.fori_loop(nb_ref[0], n_blocks, wait_block, 0)


def _experts(xb, pst, ntile, nb_used, w_g, w_u, w_d, layer):
    n_rows, d = xb.shape
    n_exp, de = w_g.shape[1], w_g.shape[3]
    wsel = lambda e, *_: (layer, e, 0, 0)
    grid_spec = pltpu.PrefetchScalarGridSpec(
        num_scalar_prefetch=3,
        grid=(n_exp,),
        in_specs=[pl.BlockSpec(memory_space=pl.ANY),
                  pl.BlockSpec((1, 1, d, de), wsel),
                  pl.BlockSpec((1, 1, d, de), wsel),
                  pl.BlockSpec((1, 1, de, d), wsel)],
        out_specs=pl.BlockSpec(memory_space=pl.ANY),
        scratch_shapes=[pltpu.VMEM((d, de), BF16), pltpu.VMEM((d, de), BF16),
                        pltpu.VMEM((de, d), BF16),
                        pltpu.VMEM((EXP_BUFS, EXP_TILE, d), BF16),
                        pltpu.VMEM((EXP_BUFS, EXP_TILE, d), BF16),
                        pltpu.SemaphoreType.DMA((EXP_BUFS,)), pltpu.SemaphoreType.DMA((EXP_BUFS,)),
                        pltpu.SemaphoreType.DMA(())],
    )
    return pl.pallas_call(
        _experts_kernel,
        grid_spec=grid_spec,
        out_shape=jax.ShapeDtypeStruct((n_rows, d), BF16),
        compiler_params=_cparams(1),
        name="experts",
    )(pst, ntile, nb_used, xb, w_g, w_u, w_d)


def _combine_kernel(cch_ref, soff_ref, dch_ref, tot_ref,
                    x_ref, route_ref, soffrow_ref, yb_ref, lng_ref, lnb_ref, out_ref,
                    stage_ref, sems, *, alpha):
    i = pl.program_id(0)
    nt = pl.num_programs(0)
    slot = i % 2

    def copy(s, src_chunk, dst_chunk):
        return pltpu.make_async_copy(yb_ref.at[_chunk_rows(src_chunk)],
                                     stage_ref.at[s, _chunk_rows(dst_chunk)], sems.at[s])

    def fetch(tile_idx, s):
        _for_each_chunk(tile_idx, cch_ref, soff_ref, dch_ref,
                        lambda so, do: copy(s, do, so).start(priority=TILE_DMA_PRIORITY))

    @pl.when(i == 0)
    def _():
        stage_ref[...] = jnp.zeros_like(stage_ref)
        fetch(i, slot)

    @pl.when(i + 1 < nt)
    def _():
        fetch(i + 1, 1 - slot)

    _repeat(tot_ref[i], lambda: copy(slot, 0, 0).wait())

    route = route_ref[...]
    pt = _perm_matrix(route, soffrow_ref[0], route[:, 2:3], route[:, 3:4]).astype(BF16)
    ffn = jnp.dot(pt, stage_ref[slot], preferred_element_type=F32)
    out_ref[...] = _layer_norm(alpha * x_ref[...] + ffn, lng_ref[...], lnb_ref[...])


def _combine(x1, route, soffrow, tables, yb, lng, lnb, alpha):
    t, d = x1.shape
    nt = t // TD
    grid_spec = pltpu.PrefetchScalarGridSpec(
        num_scalar_prefetch=4,
        grid=(nt,),
        in_specs=[pl.BlockSpec((TD, d), lambda i, *_: (i, 0)),
                  pl.BlockSpec((TD, LANES), lambda i, *_: (i, 0)),
                  pl.BlockSpec((1, 1, LANES), lambda i, *_: (i, 0, 0)),
                  pl.BlockSpec(memory_space=pl.ANY),
                  pl.BlockSpec((1, d), lambda i, *_: (0, 0)),
                  pl.BlockSpec((1, d), lambda i, *_: (0, 0))],
        out_specs=pl.BlockSpec((TD, d), lambda i, *_: (i, 0)),
        scratch_shapes=[pltpu.VMEM((2, STAGE_ROWS, d), BF16), pltpu.SemaphoreType.DMA((2,))],
    )
    return pl.pallas_call(
        functools.partial(_combine_kernel, alpha=alpha),
        grid_spec=grid_spec,
        out_shape=jax.ShapeDtypeStruct((t, d), F32),
        compiler_params=_cparams(1),
        name="combine",
    )(*tables, x1, route, soffrow, yb, lng, lnb)


def _max_blocks(t):
    nt = t // TD
    rows = 2 * t + (CHUNK - 1) * nt * N_EXPERTS + (MOE_BLOCK - CHUNK) * N_EXPERTS
    return -(-rows // MOE_BLOCK)


def _dispatch_tables(cnt):
    n = cnt[:, 0, :N_EXPERTS].astype(jnp.int32)
    cch = (n + CHUNK - 1) // CHUNK
    soff = jnp.cumsum(cch, axis=1) - cch
    tot_tile = jnp.sum(cch, axis=1)
    tot_e = jnp.sum(cch, axis=0)
    reg = (tot_e + CHUNKS_PER_BLOCK - 1) // CHUNKS_PER_BLOCK * CHUNKS_PER_BLOCK
    pend = jnp.cumsum(reg)
    pstart = pend - reg
    dch = pstart[None, :] + jnp.cumsum(cch, axis=0) - cch
    nb_used = pend[-1] // CHUNKS_PER_BLOCK
    soffrow = jnp.zeros((n.shape[0], 1, LANES), F32).at[:, 0, :N_EXPERTS].set(
        (soff * CHUNK).astype(F32))
    i32 = lambda a: a.reshape(-1).astype(jnp.int32)
    return dict(cch=i32(cch), soff=i32(soff), dch=i32(dch), tot=i32(tot_tile),
                padst=i32(pstart + tot_e), padn=i32(reg - tot_e), nb_used=i32(nb_used),
                pst=i32(pstart * CHUNK), ntile=i32((reg * CHUNK + EXP_TILE - 1) // EXP_TILE),
                soffrow=soffrow)


def _rope_tables(seq):
    inv = ROPE_THETA ** (-jnp.arange(0, HEAD_DIM, 2, dtype=F32) / HEAD_DIM)
    ang = jnp.arange(seq, dtype=F32)[:, None] * inv[None, :]
    cos, sin = jnp.cos(ang), jnp.sin(ang)
    return jnp.concatenate([cos, cos], axis=1), jnp.concatenate([-sin, sin], axis=1)


@jax.jit
def kernel(x, w_in, w_sink, w_conv, b_conv, w_rec_gate, b_rec_gate, w_in_gate, b_in_gate,
           lru_lambda, w_attn_o, w_rnn_o, w_out, ln_g, ln_b, w_router_group, b_router_group,
           w_router_expert, b_router_expert, w_exp_gate, w_exp_up, w_exp_down):
    bsz, seq, d = x.shape
    depth = w_in.shape[0]
    t = bsz * seq
    nblk = d // LANES
    alpha = (2 * depth) ** 0.25
    cosf, sinf = _rope_tables(seq)
    n_rows = _max_blocks(t) * MOE_BLOCK + (EXP_TILE - MOE_BLOCK)
    x2 = x.reshape(t, d)
    for l in range(depth):
        n_branch = w_in.shape[2] - 2 * d
        q, k, v, xr, yr = _inproj(x2, w_in[l, :, :n_branch].astype(BF16), cosf, sinf, seq)
        o = _attention(q.reshape(bsz, seq, -1), k.reshape(bsz, seq, -1), v.reshape(bsz, seq, -1),
                       w_sink[l])
        wg_cat = (0.5 * jnp.concatenate([w_rec_gate[l, 0], w_in_gate[l, 0], w_rec_gate[l, 1],
                                         w_in_gate[l, 1]], axis=-1)).astype(BF16)
        bg_cat = jnp.concatenate([b_rec_gate[l, 0].reshape(nblk, 1, LANES),
                                  b_in_gate[l, 0].reshape(nblk, 1, LANES),
                                  b_rec_gate[l, 1].reshape(nblk, 1, LANES),
                                  b_in_gate[l, 1].reshape(nblk, 1, LANES)], axis=-1)
        lam_cat = jnp.concatenate([lru_lambda[l, 0].reshape(nblk, 1, LANES),
                                   lru_lambda[l, 1].reshape(nblk, 1, LANES)], axis=-1)
        hy = _rnn(xr.reshape(bsz, seq, d), yr.reshape(bsz, seq, d), w_conv[l],
                  b_conv[l].reshape(nblk, 1, LANES), wg_cat, bg_cat, lam_cat)
        wr = jnp.zeros((d, LANES), F32)
        wr = wr.at[:, :N_EXPERTS].set(w_router_expert[l])
        wr = wr.at[:, N_EXPERTS:N_EXPERTS + N_GROUPS].set(w_router_group[l]).astype(BF16)
        br = jnp.zeros((1, LANES), F32)
        br = br.at[0, :N_EXPERTS].set(b_router_expert[l])
        br = br.at[0, N_EXPERTS:N_EXPERTS + N_GROUPS].set(b_router_group[l])
        x1, x1b, route, cnt = _mixout(x2, o.reshape(t, -1), hy.reshape(t, d),
                                 w_in[l, :, n_branch:].astype(BF16),
                                 w_attn_o[l].astype(BF16), w_rnn_o[l].astype(BF16),
                                 w_out[l].astype(BF16), ln_g[l, 0].reshape(1, d),
                                 ln_b[l, 0].reshape(1, d), wr, br, alpha)
        tb = _dispatch_tables(cnt)
        xb = _dispatch(x1b, route, tb["soffrow"],
                       (tb["cch"], tb["soff"], tb["dch"], tb["tot"], tb["padst"], tb["padn"],
                        tb["nb_used"]), n_rows)
        yb = _experts(xb, tb["pst"], tb["ntile"], tb["nb_used"], w_exp_gate, w_exp_up,
                      w_exp_down, l)
        x2 = _combine(x1, route, tb["soffrow"], (tb["cch"], tb["soff"], tb["dch"], tb["tot"]),
                      yb, ln_g[l, 1].reshape(1, d), ln_b[l, 1].reshape(1, d), alpha)
    return x2.reshape(bsz, seq, d)
```

```python
import functools
import math

import jax
import jax.numpy as jnp
from jax import lax
from jax.experimental import pallas as pl
from jax.experimental.pallas import tpu as pltpu

F32 = jnp.float32
BF16 = jnp.bfloat16

HEAD_DIM = 128
N_Q_HEADS = 8
N_KV_HEADS = 2
Q_PER_KV = N_Q_HEADS // N_KV_HEADS
WINDOW = 128
ROPE_THETA = 10000.0
CONV_W = 4
LRU_C = 8.0
N_GROUPS = 4
EXPERTS_PER_GROUP = 8
N_EXPERTS = N_GROUPS * EXPERTS_PER_GROUP
LN_EPS = 1e-5
NEG = -1e30

SUBLANES = 8
LANES = 128

TM_PROJ = 512
TQ_ATTN = 512
ATTN_STACK = Q_PER_KV
SEG = 256
SCAN_LEN = SEG + 4
SCAN_UNROLL = 10
BF16_ROWS = 2 * SUBLANES
TD = 512
MIX_ROWS = 256
CHUNK = BF16_ROWS
STAGE_ROWS = 2 * TD + CHUNK * N_EXPERTS
MOE_BLOCK = 128
CHUNKS_PER_BLOCK = MOE_BLOCK // CHUNK
EXP_TILE = 2 * MOE_BLOCK
EXP_BUFS = 6
VMEM_LIMIT = 56 * 1024 * 1024
TILE_DMA_PRIORITY = 1


def _cparams(n_axes):
    return pltpu.CompilerParams(dimension_semantics=("arbitrary",) * n_axes,
                                vmem_limit_bytes=VMEM_LIMIT)


def _softplus(z):
    e = jnp.exp(-jnp.abs(z))
    w = 1.0 + e
    tiny = w == 1.0
    log1p = jnp.where(tiny, e, jnp.log(w) * (e / jnp.where(tiny, 1.0, w - 1.0)))
    return jnp.maximum(z, 0.0) + log1p


def _sigmoid(x):
    return 0.5 * jnp.tanh(0.5 * x) + 0.5


def _gelu_tanh(y):
    c1 = math.sqrt(2.0 / math.pi)
    half = 0.5 * y
    return half + half * jnp.tanh(y * (c1 + (c1 * 0.044715) * (y * y)))


def _layer_norm(y, g, b):
    mu = jnp.mean(y, axis=-1, keepdims=True)
    d = y - mu
    var = jnp.mean(d * d, axis=-1, keepdims=True)
    return d * lax.rsqrt(var + LN_EPS) * g + b


def _inproj_kernel(x_ref, w_ref, cos_ref, sin_ref,
                   q_ref, k_ref, v_ref, xr_ref, yr_ref, *, d_model):
    xb = x_ref[...].astype(BF16)
    cos = cos_ref[...]
    sin = sin_ref[...]
    aw = N_Q_HEADS * HEAD_DIM
    kw = N_KV_HEADS * HEAD_DIM

    def proj(c0, n):
        return jnp.dot(xb, w_ref[:, c0:c0 + n], preferred_element_type=F32)

    def rope(t):
        return t * cos + pltpu.roll(t, HEAD_DIM // 2, 1) * sin

    zq = proj(0, aw)
    scale = HEAD_DIM ** -0.5
    for h in range(N_Q_HEADS):
        sl = slice(h * HEAD_DIM, (h + 1) * HEAD_DIM)
        q_ref[:, sl] = (rope(zq[:, sl]) * scale).astype(BF16)
    zk = proj(aw, kw)
    for h in range(N_KV_HEADS):
        sl = slice(h * HEAD_DIM, (h + 1) * HEAD_DIM)
        k_ref[:, sl] = rope(zk[:, sl]).astype(BF16)
    v_ref[...] = proj(aw + kw, kw).astype(BF16)
    c0 = aw + 2 * kw
    xr_ref[...] = proj(c0, d_model)
    yr_ref[...] = proj(c0 + d_model, d_model)


def _inproj(x2, w_in_b, cosf, sinf, seq):
    t, d = x2.shape
    n_in = w_in_b.shape[1]
    tm = TM_PROJ
    aw = N_Q_HEADS * HEAD_DIM
    kw = N_KV_HEADS * HEAD_DIM
    tiles_per_seq = seq // tm
    row = lambda i: (i, 0)
    pos = lambda i: (i % tiles_per_seq, 0)
    outs = [jax.ShapeDtypeStruct((t, aw), BF16), jax.ShapeDtypeStruct((t, kw), BF16),
            jax.ShapeDtypeStruct((t, kw), BF16)] + [jax.ShapeDtypeStruct((t, d), F32)] * 2
    return pl.pallas_call(
        functools.partial(_inproj_kernel, d_model=d),
        grid=(t // tm,),
        in_specs=[pl.BlockSpec((tm, d), row),
                  pl.BlockSpec((d, n_in), lambda i: (0, 0)),
                  pl.BlockSpec((tm, HEAD_DIM), pos),
                  pl.BlockSpec((tm, HEAD_DIM), pos)],
        out_specs=[pl.BlockSpec((tm, aw), row), pl.BlockSpec((tm, kw), row),
                   pl.BlockSpec((tm, kw), row)] + [pl.BlockSpec((tm, d), row)] * 2,
        out_shape=outs,
        compiler_params=_cparams(1),
        name="inproj",
    )(x2, w_in_b, cosf, sinf)


def _attn_kernel(sink_ref, q_ref, kp_ref, kc_ref, kn_ref, vp_ref, vc_ref, vn_ref, o_ref,
                 *, seq):
    i = pl.program_id(1)
    tq = q_ref.shape[1]
    blk = WINDOW
    t0 = i * tq
    kext = jnp.concatenate([kp_ref[0], kc_ref[0], kn_ref[0]], axis=0)
    vext = jnp.concatenate([vp_ref[0], vc_ref[0], vn_ref[0]], axis=0)
    nrow = ATTN_STACK * blk
    blk_shift = blk.bit_length() - 1
    qi = lax.broadcasted_iota(jnp.int32, (nrow, 3 * blk), 0) & (blk - 1)
    kj = lax.broadcasted_iota(jnp.int32, (nrow, 3 * blk), 1)
    band = jnp.where(jnp.abs(kj - blk - qi) <= WINDOW, 0.0, NEG)
    kj_row = lax.broadcasted_iota(jnp.int32, (1, 3 * blk), 1)
    rowg = lax.broadcasted_iota(jnp.int32, (nrow, 1), 0) >> blk_shift
    sinks = []
    for h0 in range(0, N_Q_HEADS, ATTN_STACK):
        sk = jnp.full((nrow, 1), sink_ref[h0], F32)
        for g in range(1, ATTN_STACK):
            sk = jnp.where(rowg == g, sink_ref[h0 + g], sk)
        sinks.append(sk)
    for j in range(tq // blk):
        kpos = t0 + j * blk - blk + kj_row
        bias = band + jnp.where((kpos >= 0) & (kpos < seq), 0.0, NEG)
        for h0 in range(0, N_Q_HEADS, ATTN_STACK):
            kv = h0 // Q_PER_KV
            hs = slice(kv * HEAD_DIM, (kv + 1) * HEAD_DIM)
            kblk = kext[j * blk:j * blk + 3 * blk, hs]
            vblk = vext[j * blk:j * blk + 3 * blk, hs]
            qs = [q_ref[0, j * blk:(j + 1) * blk, (h0 + g) * HEAD_DIM:(h0 + g + 1) * HEAD_DIM]
                  for g in range(ATTN_STACK)]
            qblk = jnp.concatenate(qs, axis=0)
            s = lax.dot_general(qblk, kblk, (((1,), (1,)), ((), ())),
                                preferred_element_type=F32) + bias
            sk = sinks[h0 // ATTN_STACK]
            m = jnp.maximum(jnp.max(s, axis=-1, keepdims=True), sk)
            p = jnp.exp(s - m)
            denom = jnp.sum(p, axis=-1, keepdims=True) + jnp.exp(sk - m)
            o = jnp.dot(p.astype(BF16), vblk, preferred_element_type=F32) / denom
            for g in range(ATTN_STACK):
                c = (h0 + g) * HEAD_DIM
                o_ref[0, j * blk:(j + 1) * blk, c:c + HEAD_DIM] = (
                    o[g * blk:(g + 1) * blk].astype(BF16))


def _attention(q3, k3, v3, sink):
    b, s, aw = q3.shape
    kw = k3.shape[2]
    tq = TQ_ATTN
    blk = WINDOW
    r = tq // blk
    nblk = s // blk
    cur = lambda bi, i: (bi, i, 0)
    prev = lambda bi, i: (bi, jnp.maximum(i * r - 1, 0), 0)
    nxt = lambda bi, i: (bi, jnp.minimum((i + 1) * r, nblk - 1), 0)
    kv_specs = [pl.BlockSpec((1, blk, kw), prev), pl.BlockSpec((1, tq, kw), cur),
                pl.BlockSpec((1, blk, kw), nxt)]
    return pl.pallas_call(
        functools.partial(_attn_kernel, seq=s),
        grid=(b, s // tq),
        in_specs=[pl.BlockSpec(memory_space=pltpu.SMEM),
                  pl.BlockSpec((1, tq, aw), cur)] + kv_specs + kv_specs,
        out_specs=pl.BlockSpec((1, tq, aw), cur),
        out_shape=jax.ShapeDtypeStruct((b, s, aw), BF16),
        compiler_params=_cparams(2),
        name="attention",
    )(sink, q3, k3, k3, k3, v3, v3, v3)


def _rnn_kernel(xr_ref, yr_ref, wc_ref, bc_ref, wg_ref, bg_ref, lam_ref, out_ref,
                xpad_ref, af_ref, uf_ref, ab_ref, ub_ref, hb_ref, cf_ref, cb_ref, *, seq):
    nseg = seq // SEG
    nlane = nseg
    ngrp = nlane // SUBLANES
    hf_ref = xpad_ref
    wc = wc_ref[...]
    bc = bc_ref[0]
    wg = wg_ref[0]
    bg = 0.5 * bg_ref[0]
    lam = lam_ref[0]
    rate = (0.5 * LRU_C) * _softplus(-lam)
    a_refs = (af_ref, ab_ref)
    u_refs = (uf_ref, ub_ref)
    left = CONV_W // 2

    halo = jnp.zeros((SUBLANES, LANES), F32)
    xpad_ref[pl.ds(0, SUBLANES), :] = halo
    xpad_ref[pl.ds(seq + SUBLANES, SUBLANES), :] = halo

    def pad_copy(c, carry):
        t0 = pl.multiple_of(c * SEG, SEG)
        xpad_ref[pl.ds(t0 + SUBLANES, SEG), :] = xr_ref[0, pl.ds(t0, SEG), :]
        return carry

    lax.fori_loop(0, nseg, pad_copy, 0, unroll=4)

    tail = nlane * SCAN_LEN - seq
    for a_ref, u_ref in zip(a_refs, u_refs):
        a_ref[pl.ds(seq, tail), :] = jnp.ones((tail, LANES), F32)
        u_ref[pl.ds(seq, tail), :] = jnp.zeros((tail, LANES), F32)

    def gates(c, carry):
        t0 = pl.multiple_of(c * SEG, SEG)
        xc = bc
        for tap in range(CONV_W):
            xc = xc + xpad_ref[pl.ds(t0 + SUBLANES - left + tap, SEG), :] * wc[tap:tap + 1]
        gh = jnp.dot(xc.astype(BF16), wg, preferred_element_type=F32) + bg
        xch = 0.5 * xc
        for d in range(2):
            rt = rate[:, d * LANES:(d + 1) * LANES]
            nlog_a = rt * jnp.tanh(gh[:, (2 * d) * LANES:(2 * d + 1) * LANES]) + rt
            a = jnp.exp2(nlog_a * (-1.0 / math.log(2.0)))
            z = jnp.tanh(nlog_a) * (a * a + 1.0)
            mult = jnp.where(z > 0.0, z * lax.rsqrt(z), 0.0)
            in_gate2 = jnp.tanh(gh[:, (2 * d + 1) * LANES:(2 * d + 2) * LANES]) + 1.0
            a_refs[d][pl.ds(t0, SEG), :] = a
            u_refs[d][pl.ds(t0, SEG), :] = (xch * mult) * in_gate2
        return carry

    lax.fori_loop(0, nseg, gates, 0, unroll=8)

    def lane_rows(g, j):
        return pl.ds(g * SUBLANES * SCAN_LEN + j, SUBLANES, stride=SCAN_LEN)

    def totals_step(j, carry):
        hf, pf, hb, pb = carry
        jb = SCAN_LEN - 1 - j
        nhf, npf, nhb, npb = [], [], [], []
        for g in range(ngrp):
            a = af_ref[lane_rows(g, j), :]
            nhf.append(a * hf[g] + uf_ref[lane_rows(g, j), :])
            npf.append(a * pf[g])
            a = ab_ref[lane_rows(g, jb), :]
            nhb.append(a * hb[g] + ub_ref[lane_rows(g, jb), :])
            npb.append(a * pb[g])
        return tuple(nhf), tuple(npf), tuple(nhb), tuple(npb)

    zero = tuple(jnp.zeros((SUBLANES, LANES), F32) for _ in range(ngrp))
    one = tuple(jnp.ones((SUBLANES, LANES), F32) for _ in range(ngrp))
    hf, pf, hb, pb = lax.fori_loop(0, SCAN_LEN, totals_step, (zero, one, zero, one),
                                   unroll=SCAN_UNROLL)

    c = jnp.zeros((1, LANES), F32)
    for s in range(nlane):
        g, r = divmod(s, SUBLANES)
        cf_ref[s:s + 1, :] = c
        c = pf[g][r:r + 1] * c + hf[g][r:r + 1]
    c = jnp.zeros((1, LANES), F32)
    for s in range(nlane - 1, -1, -1):
        g, r = divmod(s, SUBLANES)
        cb_ref[s:s + 1, :] = c
        c = pb[g][r:r + 1] * c + hb[g][r:r + 1]

    def scan_step(j, carry):
        hf, hb = carry
        jb = SCAN_LEN - 1 - j
        nhf, nhb = [], []
        for g in range(ngrp):
            h = af_ref[lane_rows(g, j), :] * hf[g] + uf_ref[lane_rows(g, j), :]
            hf_ref[lane_rows(g, j), :] = h
            nhf.append(h)
            h = ab_ref[lane_rows(g, jb), :] * hb[g] + ub_ref[lane_rows(g, jb), :]
            hb_ref[lane_rows(g, jb), :] = h
            nhb.append(h)
        return tuple(nhf), tuple(nhb)

    hf0 = tuple(cf_ref[g * SUBLANES:(g + 1) * SUBLANES, :] for g in range(ngrp))
    hb0 = tuple(cb_ref[g * SUBLANES:(g + 1) * SUBLANES, :] for g in range(ngrp))
    lax.fori_loop(0, SCAN_LEN, scan_step, (hf0, hb0), unroll=SCAN_UNROLL)

    def finish(c, carry):
        t0 = pl.multiple_of(c * SEG, SEG)
        h = hf_ref[pl.ds(t0, SEG), :] + hb_ref[pl.ds(t0, SEG), :]
        out_ref[0, pl.ds(t0, SEG), :] = (h * _gelu_tanh(yr_ref[0, pl.ds(t0, SEG), :])).astype(BF16)
        return carry

    lax.fori_loop(0, nseg, finish, 0, unroll=4)


def _rnn(xr3, yr3, w_conv, b_conv, wg_cat, bg_cat, lam_cat):
    b, s, d = xr3.shape
    nblk = d // LANES
    nseg = s // SEG
    slab = lambda bi, n: (bi, 0, n)
    per_blk = lambda bi, n: (n, 0, 0)
    assert nseg % SUBLANES == 0 and nseg * SCAN_LEN >= s + 2 * SUBLANES
    scratch = ([pltpu.VMEM((nseg * SCAN_LEN, LANES), F32)] * 6
               + [pltpu.VMEM((nseg, LANES), F32)] * 2)
    return pl.pallas_call(
        functools.partial(_rnn_kernel, seq=s),
        grid=(b, nblk),
        in_specs=[pl.BlockSpec((1, s, LANES), slab),
                  pl.BlockSpec((1, s, LANES), slab),
                  pl.BlockSpec((CONV_W, LANES), lambda bi, n: (0, n)),
                  pl.BlockSpec((1, 1, LANES), per_blk),
                  pl.BlockSpec((1, LANES, 4 * LANES), per_blk),
                  pl.BlockSpec((1, 1, 4 * LANES), per_blk),
                  pl.BlockSpec((1, 1, 2 * LANES), per_blk)],
        out_specs=pl.BlockSpec((1, s, LANES), slab),
        out_shape=jax.ShapeDtypeStruct((b, s, d), BF16),
        scratch_shapes=scratch,
        compiler_params=_cparams(2),
        name="rglru",
    )(xr3, yr3, w_conv, b_conv, wg_cat, bg_cat, lam_cat)


def _route(logits):
    lane = lax.broadcasted_iota(jnp.int32, logits.shape, 1)
    lanef = lane.astype(F32)
    big = float(4 * LANES)
    gmask = (lane >= N_EXPERTS) & (lane < N_EXPERTS + N_GROUPS)
    gl = jnp.where(gmask, logits, NEG)
    gmax = jnp.max(gl, axis=-1, keepdims=True)
    ge = jnp.exp(gl - gmax)
    gprob = ge / jnp.sum(ge, axis=-1, keepdims=True)
    gval = jnp.max(gprob, axis=-1, keepdims=True)
    gidx = jnp.min(jnp.where((gprob == gval) & gmask, lanef, big), axis=-1, keepdims=True)
    gidx = gidx.astype(jnp.int32) - N_EXPERTS
    group_shift = EXPERTS_PER_GROUP.bit_length() - 1
    emask = (lane < N_EXPERTS) & ((lane >> group_shift) == gidx)
    el = jnp.where(emask, logits, NEG)
    m1 = jnp.max(el, axis=-1, keepdims=True)
    i1 = jnp.min(jnp.where((el == m1) & emask, lanef, big), axis=-1, keepdims=True)
    emask2 = emask & (lanef != i1)
    el2 = jnp.where(emask2, logits, NEG)
    m2 = jnp.max(el2, axis=-1, keepdims=True)
    i2 = jnp.min(jnp.where((el2 == m2) & emask2, lanef, big), axis=-1, keepdims=True)
    e2 = jnp.exp(m2 - m1)
    den = 1.0 + e2
    g1 = (1.0 / den) * gval
    g2 = (e2 / den) * gval
    route = jnp.where(lane == 0, i1, jnp.where(lane == 1, i2,
                      jnp.where(lane == 2, g1, jnp.where(lane == 3, g2, 0.0))))
    onehot = (lanef == i1).astype(F32) + (lanef == i2).astype(F32)
    return route, jnp.sum(onehot, axis=0, keepdims=True)


def _mixout_kernel(x_ref, o_ref, hy_ref, wmg_ref, wao_ref, wro_ref, wout_ref,
                   lng_ref, lnb_ref, wr_ref, br_ref, x1_ref, x1b_ref, route_ref, cnt_ref,
                   logits_ref, *, alpha):
    @pl.when(pl.program_id(0) == 0)
    def _():
        logits_ref[...] = jnp.zeros_like(logits_ref)

    route, cnt = _route(logits_ref[...])
    route_ref[...] = route
    cnt_ref[0] = cnt

    d = x_ref.shape[1]
    for r0 in range(0, x_ref.shape[0], MIX_ROWS):
        rows = pl.ds(r0, MIX_ROWS)
        x = x_ref[rows, :]
        xb = x.astype(BF16)
        ga = jnp.dot(xb, wmg_ref[:, :d], preferred_element_type=F32)
        gr = jnp.dot(xb, wmg_ref[:, d:], preferred_element_type=F32)
        ya = jnp.dot(o_ref[rows, :], wao_ref[...], preferred_element_type=F32)
        yr = jnp.dot(hy_ref[rows, :], wro_ref[...], preferred_element_type=F32)
        merged = _sigmoid(ga) * ya + _sigmoid(gr) * yr
        mix = jnp.dot(merged.astype(BF16), wout_ref[...], preferred_element_type=F32)
        x1 = _layer_norm(alpha * x + mix, lng_ref[...], lnb_ref[...])
        x1_ref[rows, :] = x1
        x1b = x1.astype(BF16)
        x1b_ref[rows, :] = x1b
        logits_ref[rows, :] = (jnp.dot(x1b, wr_ref[...], preferred_element_type=F32)
                               + br_ref[...])


def _mixout(x2, o2, hy2, wmg, wao, wro, wout, lng, lnb, wr, br, alpha):
    t, d = x2.shape
    tm = TD
    nt = t // tm
    row = lambda i: (jnp.minimum(i, nt - 1), 0)
    prev_row = lambda i: (jnp.maximum(i - 1, 0), 0)
    full = lambda i: (0, 0)
    return pl.pallas_call(
        functools.partial(_mixout_kernel, alpha=alpha),
        grid=(nt + 1,),
        in_specs=[pl.BlockSpec((tm, d), row)] * 3
                 + [pl.BlockSpec((d, 2 * d), full)]
                 + [pl.BlockSpec((d, d), full)] * 3
                 + [pl.BlockSpec((1, d), full)] * 2
                 + [pl.BlockSpec((d, LANES), full), pl.BlockSpec((1, LANES), full)],
        out_specs=[pl.BlockSpec((tm, d), row), pl.BlockSpec((tm, d), row),
                   pl.BlockSpec((tm, LANES), prev_row),
                   pl.BlockSpec((1, 1, LANES), lambda i: (jnp.maximum(i - 1, 0), 0, 0))],
        out_shape=[jax.ShapeDtypeStruct((t, d), F32), jax.ShapeDtypeStruct((t, d), BF16),
                   jax.ShapeDtypeStruct((t, LANES), F32),
                   jax.ShapeDtypeStruct((nt, 1, LANES), F32)],
        scratch_shapes=[pltpu.VMEM((tm, LANES), F32)],
        compiler_params=_cparams(1),
        name="mixout",
    )(x2, o2, hy2, wmg, wao, wro, wout, lng, lnb, wr, br)


def _perm_matrix(route, soff_row, w1, w2):
    td = route.shape[0]
    lane = lax.broadcasted_iota(jnp.int32, (td, LANES), 1).astype(F32)
    e1 = lane == route[:, 0:1]
    e2 = lane == route[:, 1:2]
    cnt = (e1.astype(F32) + e2.astype(F32)).astype(BF16)
    ti = lax.broadcasted_iota(jnp.int32, (td, td), 0)
    tj = lax.broadcasted_iota(jnp.int32, (td, td), 1)
    lower = (tj < ti).astype(BF16)
    pos = jnp.dot(lower, cnt, preferred_element_type=F32) + soff_row
    r1 = jnp.sum(jnp.where(e1, pos, 0.0), axis=-1, keepdims=True).astype(jnp.int32)
    r2 = jnp.sum(jnp.where(e2, pos, 0.0), axis=-1, keepdims=True).astype(jnp.int32)
    col = lax.broadcasted_iota(jnp.int32, (td, STAGE_ROWS), 1)
    return jnp.where(col == r1, w1, 0.0) + jnp.where(col == r2, w2, 0.0)


def _chunk_rows(c):
    return pl.ds(pl.multiple_of(c * CHUNK, CHUNK), CHUNK)


def _block_rows(b):
    return pl.ds(pl.multiple_of(b * MOE_BLOCK, MOE_BLOCK), MOE_BLOCK)


def _for_each_chunk(i, cch_ref, soff_ref, dch_ref, fn):
    def per_expert(e, carry):
        k = i * N_EXPERTS + e
        so = soff_ref[k]
        do = dch_ref[k]

        def per_chunk(c, carry2):
            fn(so + c, do + c)
            return carry2

        return lax.fori_loop(0, cch_ref[k], per_chunk, carry)

    lax.fori_loop(0, N_EXPERTS, per_expert, 0)


def _repeat(n, fn):
    def body(c, carry):
        fn()
        return carry

    lax.fori_loop(0, n, body, 0)


def _dispatch_kernel(cch_ref, soff_ref, dch_ref, tot_ref, padst_ref, padn_ref, nb_ref,
                     x_ref, route_ref, soffrow_ref, xb_ref, stage_ref, zero_ref, sems, zsem):
    i = pl.program_id(0)
    nt = pl.num_programs(0)
    n_blocks = xb_ref.shape[0] // MOE_BLOCK
    slot = i % 2

    def copy(s, src_chunk, dst_chunk):
        return pltpu.make_async_copy(stage_ref.at[s, _chunk_rows(src_chunk)],
                                     xb_ref.at[_chunk_rows(dst_chunk)], sems.at[s])

    @pl.when(i >= 2)
    def _():
        _repeat(tot_ref[jnp.maximum(i - 2, 0)], lambda: copy(slot, 0, 0).wait())

    pt = _perm_matrix(route_ref[...], soffrow_ref[0], 1.0, 1.0).astype(BF16)
    stage_ref[slot] = lax.dot_general(pt, x_ref[...], (((0,), (0,)), ((), ())),
                                      preferred_element_type=F32).astype(BF16)
    _for_each_chunk(i, cch_ref, soff_ref, dch_ref,
                    lambda s, d: copy(slot, s, d).start(priority=TILE_DMA_PRIORITY))

    @pl.when(i == nt - 1)
    def _():
        @pl.when(i >= 1)
        def _():
            _repeat(tot_ref[jnp.maximum(i - 1, 0)], lambda: copy(1 - slot, 0, 0).wait())

        _repeat(tot_ref[i], lambda: copy(slot, 0, 0).wait())

        zero_ref[...] = jnp.zeros_like(zero_ref)

        def zcopy(dst_chunk):
            return pltpu.make_async_copy(zero_ref.at[pl.ds(0, CHUNK)],
                                         xb_ref.at[_chunk_rows(dst_chunk)], zsem)

        def pad_start(e, n):
            def pad_chunk(c, carry2):
                zcopy(padst_ref[e] + c).start()
                return carry2

            lax.fori_loop(0, padn_ref[e], pad_chunk, 0)
            return n + padn_ref[e]

        npad = lax.fori_loop(0, N_EXPERTS, pad_start, 0)
        _repeat(npad, lambda: zcopy(0).wait())

        def zblock(b):
            return pltpu.make_async_copy(zero_ref, xb_ref.at[_block_rows(b)], zsem)

        def start_block(b, carry):
            zblock(b).start()
            return carry

        def wait_block(b, carry):
            zblock(b).wait()
            return carry

        lax.fori_loop(nb_ref[0], n_blocks, start_block, 0)
        lax.fori_loop(nb_ref[0], n_blocks, wait_block, 0)


def _dispatch(x1, route, soffrow, tables, n_rows):
    t, d = x1.shape
    nt = t // TD
    grid_spec = pltpu.PrefetchScalarGridSpec(
        num_scalar_prefetch=7,
        grid=(nt,),
        in_specs=[pl.BlockSpec((TD, d), lambda i, *_: (i, 0)),
                  pl.BlockSpec((TD, LANES), lambda i, *_: (i, 0)),
                  pl.BlockSpec((1, 1, LANES), lambda i, *_: (i, 0, 0))],
        out_specs=pl.BlockSpec(memory_space=pl.ANY),
        scratch_shapes=[pltpu.VMEM((2, STAGE_ROWS, d), BF16), pltpu.VMEM((MOE_BLOCK, d), BF16),
                        pltpu.SemaphoreType.DMA((2,)), pltpu.SemaphoreType.DMA(())],
    )
    return pl.pallas_call(
        _dispatch_kernel,
        grid_spec=grid_spec,
        out_shape=jax.ShapeDtypeStruct((n_rows, d), BF16),
        compiler_params=_cparams(1),
        name="dispatch",
    )(*tables, x1, route, soffrow)


def _experts_kernel(pst_ref, ntile_ref, nb_ref, xb_ref, wg_ref, wu_ref, wd_ref, yb_ref,
                    wgb_ref, wub_ref, wdb_ref, xbuf_ref, ybuf_ref, in_sems, out_sems, zsem):
    e = pl.program_id(0)
    ne = pl.num_programs(0)
    n = ntile_ref[e]
    row0 = pst_ref[e]
    n_blocks = yb_ref.shape[0] // MOE_BLOCK
    ahead = EXP_BUFS - 1

    def tile_rows(r0, t):
        return pl.ds(pl.multiple_of(r0 + t * EXP_TILE, MOE_BLOCK), EXP_TILE)

    def in_copy(r0, t, s):
        return pltpu.make_async_copy(xb_ref.at[tile_rows(r0, t)], xbuf_ref.at[s], in_sems.at[s])

    def out_copy(t, s):
        return pltpu.make_async_copy(ybuf_ref.at[s], yb_ref.at[tile_rows(row0, t)],
                                     out_sems.at[s])

    def start_head(r0, count):
        for k in range(ahead):
            @pl.when(k < count)
            def _():
                in_copy(r0, k, k).start(priority=TILE_DMA_PRIORITY)

    @pl.when(e == 0)
    def _():
        start_head(row0, n)

    @pl.when(n > 0)
    def _():
        wgb_ref[...] = wg_ref[0, 0].astype(BF16)
        wub_ref[...] = wu_ref[0, 0].astype(BF16)
        wdb_ref[...] = wd_ref[0, 0].astype(BF16)

    def tile(t, carry):
        s = t % EXP_BUFS

        @pl.when(t + ahead < n)
        def _():
            in_copy(row0, t + ahead, (t + ahead) % EXP_BUFS).start(priority=TILE_DMA_PRIORITY)

        in_copy(row0, t, s).wait()

        @pl.when(t >= EXP_BUFS)
        def _():
            out_copy(t - EXP_BUFS, s).wait()

        x = xbuf_ref[s]
        hg = jnp.dot(x, wgb_ref[...], preferred_element_type=F32)
        hu = jnp.dot(x, wub_ref[...], preferred_element_type=F32)
        hid = (hg * _sigmoid(hg)) * hu
        ybuf_ref[s] = jnp.dot(hid.astype(BF16), wdb_ref[...],
                              preferred_element_type=F32).astype(BF16)
        out_copy(t, s).start(priority=TILE_DMA_PRIORITY)
        return carry

    lax.fori_loop(0, n, tile, 0)

    @pl.when(e + 1 < ne)
    def _():
        nxt = jnp.minimum(e + 1, ne - 1)
        start_head(pst_ref[nxt], ntile_ref[nxt])

    for k in range(EXP_BUFS):
        @pl.when(n > k)
        def _():
            out_copy(n - 1 - k, (n - 1 - k) % EXP_BUFS).wait()

    @pl.when(e == ne - 1)
    def _():
        ybuf_ref[0] = jnp.zeros(ybuf_ref.shape[1:], BF16)

        def zblock(b):
            return pltpu.make_async_copy(ybuf_ref.at[0, pl.ds(0, MOE_BLOCK)],
                                         yb_ref.at[_block_rows(b)], zsem)

        def start_block(b, carry):
            zblock(b).start()
            return carry

        def wait_block(b, carry):
            zblock(b).wait()
            return carry

        lax.fori_loop(nb_ref[0], n_blocks, start_block, 0)
        lax.fori_loop(nb_ref[0], n_blocks, wait_block, 0)


def _experts(xb, pst, ntile, nb_used, w_g, w_u, w_d, layer):
    n_rows, d = xb.shape
    n_exp, de = w_g.shape[1], w_g.shape[3]
    wsel = lambda e, *_: (layer, e, 0, 0)
    grid_spec = pltpu.PrefetchScalarGridSpec(
        num_scalar_prefetch=3,
        grid=(n_exp,),
        in_specs=[pl.BlockSpec(memory_space=pl.ANY),
                  pl.BlockSpec((1, 1, d, de), wsel),
                  pl.BlockSpec((1, 1, d, de), wsel),
                  pl.BlockSpec((1, 1, de, d), wsel)],
        out_specs=pl.BlockSpec(memory_space=pl.ANY),
        scratch_shapes=[pltpu.VMEM((d, de), BF16), pltpu.VMEM((d, de), BF16),
                        pltpu.VMEM((de, d), BF16),
                        pltpu.VMEM((EXP_BUFS, EXP_TILE, d), BF16),
                        pltpu.VMEM((EXP_BUFS, EXP_TILE, d), BF16),
                        pltpu.SemaphoreType.DMA((EXP_BUFS,)), pltpu.SemaphoreType.DMA((EXP_BUFS,)),
                        pltpu.SemaphoreType.DMA(())],
    )
    return pl.pallas_call(
        _experts_kernel,
        grid_spec=grid_spec,
        out_shape=jax.ShapeDtypeStruct((n_rows, d), BF16),
        compiler_params=_cparams(1),
        name="experts",
    )(pst, ntile, nb_used, xb, w_g, w_u, w_d)


def _combine_kernel(cch_ref, soff_ref, dch_ref, tot_ref,
                    x_ref, route_ref, soffrow_ref, yb_ref, lng_ref, lnb_ref, out_ref,
                    stage_ref, sems, *, alpha):
    i = pl.program_id(0)
    nt = pl.num_programs(0)
    slot = i % 2

    def copy(s, src_chunk, dst_chunk):
        return pltpu.make_async_copy(yb_ref.at[_chunk_rows(src_chunk)],
                                     stage_ref.at[s, _chunk_rows(dst_chunk)], sems.at[s])

    def fetch(tile_idx, s):
        _for_each_chunk(tile_idx, cch_ref, soff_ref, dch_ref,
                        lambda so, do: copy(s, do, so).start(priority=TILE_DMA_PRIORITY))

    @pl.when(i == 0)
    def _():
        stage_ref[...] = jnp.zeros_like(stage_ref)
        fetch(i, slot)

    @pl.when(i + 1 < nt)
    def _():
        fetch(i + 1, 1 - slot)

    _repeat(tot_ref[i], lambda: copy(slot, 0, 0).wait())

    route = route_ref[...]
    pt = _perm_matrix(route, soffrow_ref[0], route[:, 2:3], route[:, 3:4]).astype(BF16)
    ffn = jnp.dot(pt, stage_ref[slot], preferred_element_type=F32)
    out_ref[...] = _layer_norm(alpha * x_ref[...] + ffn, lng_ref[...], lnb_ref[...])


def _combine(x1, route, soffrow, tables, yb, lng, lnb, alpha):
    t, d = x1.shape
    nt = t // TD
    grid_spec = pltpu.PrefetchScalarGridSpec(
        num_scalar_prefetch=4,
        grid=(nt,),
        in_specs=[pl.BlockSpec((TD, d), lambda i, *_: (i, 0)),
                  pl.BlockSpec((TD, LANES), lambda i, *_: (i, 0)),
                  pl.BlockSpec((1, 1, LANES), lambda i, *_: (i, 0, 0)),
                  pl.BlockSpec(memory_space=pl.ANY),
                  pl.BlockSpec((1, d), lambda i, *_: (0, 0)),
                  pl.BlockSpec((1, d), lambda i, *_: (0, 0))],
        out_specs=pl.BlockSpec((TD, d), lambda i, *_: (i, 0)),
        scratch_shapes=[pltpu.VMEM((2, STAGE_ROWS, d), BF16), pltpu.SemaphoreType.DMA((2,))],
    )
    return pl.pallas_call(
        functools.partial(_combine_kernel, alpha=alpha),
        grid_spec=grid_spec,
        out_shape=jax.ShapeDtypeStruct((t, d), F32),
        compiler_params=_cparams(1),
        name="combine",
    )(*tables, x1, route, soffrow, yb, lng, lnb)


def _max_blocks(t):
    nt = t // TD
    rows = 2 * t + (CHUNK - 1) * nt * N_EXPERTS + (MOE_BLOCK - CHUNK) * N_EXPERTS
    return -(-rows // MOE_BLOCK)


def _dispatch_tables(cnt):
    n = cnt[:, 0, :N_EXPERTS].astype(jnp.int32)
    cch = (n + CHUNK - 1) // CHUNK
    soff = jnp.cumsum(cch, axis=1) - cch
    tot_tile = jnp.sum(cch, axis=1)
    tot_e = jnp.sum(cch, axis=0)
    reg = (tot_e + CHUNKS_PER_BLOCK - 1) // CHUNKS_PER_BLOCK * CHUNKS_PER_BLOCK
    pend = jnp.cumsum(reg)
    pstart = pend - reg
    dch = pstart[None, :] + jnp.cumsum(cch, axis=0) - cch
    nb_used = pend[-1] // CHUNKS_PER_BLOCK
    soffrow = jnp.zeros((n.shape[0], 1, LANES), F32).at[:, 0, :N_EXPERTS].set(
        (soff * CHUNK).astype(F32))
    i32 = lambda a: a.reshape(-1).astype(jnp.int32)
    return dict(cch=i32(cch), soff=i32(soff), dch=i32(dch), tot=i32(tot_tile),
                padst=i32(pstart + tot_e), padn=i32(reg - tot_e), nb_used=i32(nb_used),
                pst=i32(pstart * CHUNK), ntile=i32((reg * CHUNK + EXP_TILE - 1) // EXP_TILE),
                soffrow=soffrow)


def _rope_tables(seq):
    inv = ROPE_THETA ** (-jnp.arange(0, HEAD_DIM, 2, dtype=F32) / HEAD_DIM)
    ang = jnp.arange(seq, dtype=F32)[:, None] * inv[None, :]
    cos, sin = jnp.cos(ang), jnp.sin(ang)
    return jnp.concatenate([cos, cos], axis=1), jnp.concatenate([-sin, sin], axis=1)


@jax.jit
def kernel(x, w_in, w_sink, w_conv, b_conv, w_rec_gate, b_rec_gate, w_in_gate, b_in_gate,
           lru_lambda, w_attn_o, w_rnn_o, w_out, ln_g, ln_b, w_router_group, b_router_group,
           w_router_expert, b_router_expert, w_exp_gate, w_exp_up, w_exp_down):
    bsz, seq, d = x.shape
    depth = w_in.shape[0]
    t = bsz * seq
    nblk = d // LANES
    alpha = (2 * depth) ** 0.25
    cosf, sinf = _rope_tables(seq)
    n_rows = _max_blocks(t) * MOE_BLOCK + (EXP_TILE - MOE_BLOCK)
    x2 = x.reshape(t, d)
    for l in range(depth):
        n_branch = w_in.shape[2] - 2 * d
        q, k, v, xr, yr = _inproj(x2, w_in[l, :, :n_branch].astype(BF16), cosf, sinf, seq)
        o = _attention(q.reshape(bsz, seq, -1), k.reshape(bsz, seq, -1), v.reshape(bsz, seq, -1),
                       w_sink[l])
        wg_cat = (0.5 * jnp.concatenate([w_rec_gate[l, 0], w_in_gate[l, 0], w_rec_gate[l, 1],
                                         w_in_gate[l, 1]], axis=-1)).astype(BF16)
        bg_cat = jnp.concatenate([b_rec_gate[l, 0].reshape(nblk, 1, LANES),
                                  b_in_gate[l, 0].reshape(nblk, 1, LANES),
                                  b_rec_gate[l, 1].reshape(nblk, 1, LANES),
                                  b_in_gate[l, 1].reshape(nblk, 1, LANES)], axis=-1)
        lam_cat = jnp.concatenate([lru_lambda[l, 0].reshape(nblk, 1, LANES),
                                   lru_lambda[l, 1].reshape(nblk, 1, LANES)], axis=-1)
        hy = _rnn(xr.reshape(bsz, seq, d), yr.reshape(bsz, seq, d), w_conv[l],
                  b_conv[l].reshape(nblk, 1, LANES), wg_cat, bg_cat, lam_cat)
        wr = jnp.zeros((d, LANES), F32)
        wr = wr.at[:, :N_EXPERTS].set(w_router_expert[l])
        wr = wr.at[:, N_EXPERTS:N_EXPERTS + N_GROUPS].set(w_router_group[l]).astype(BF16)
        br = jnp.zeros((1, LANES), F32)
        br = br.at[0, :N_EXPERTS].set(b_router_expert[l])
        br = br.at[0, N_EXPERTS:N_EXPERTS + N_GROUPS].set(b_router_group[l])
        x1, x1b, route, cnt = _mixout(x2, o.reshape(t, -1), hy.reshape(t, d),
                                 w_in[l, :, n_branch:].astype(BF16),
                                 w_attn_o[l].astype(BF16), w_rnn_o[l].astype(BF16),
                                 w_out[l].astype(BF16), ln_g[l, 0].reshape(1, d),
                                 ln_b[l, 0].reshape(1, d), wr, br, alpha)
        tb = _dispatch_tables(cnt)
        xb = _dispatch(x1b, route, tb["soffrow"],
                       (tb["cch"], tb["soff"], tb["dch"], tb["tot"], tb["padst"], tb["padn"],
                        tb["nb_used"]), n_rows)
        yb = _experts(xb, tb["pst"], tb["ntile"], tb["nb_used"], w_exp_gate, w_exp_up,
                      w_exp_down, l)
        x2 = _combine(x1, route, tb["soffrow"], (tb["cch"], tb["soff"], tb["dch"], tb["tot"]),
                      yb, ln_g[l, 1].reshape(1, d), ln_b[l, 1].reshape(1, d), alpha)
    return x2.reshape(bsz, seq, d)
```

```python
import functools
import math

import jax
import jax.numpy as jnp
from jax import lax
from jax.experimental import pallas as pl
from jax.experimental.pallas import tpu as pltpu

F32 = jnp.float32
BF16 = jnp.bfloat16

HEAD_DIM = 128
N_Q_HEADS = 8
N_KV_HEADS = 2
Q_PER_KV = N_Q_HEADS // N_KV_HEADS
WINDOW = 128
ROPE_THETA = 10000.0
CONV_W = 4
LRU_C = 8.0
N_GROUPS = 4
EXPERTS_PER_GROUP = 8
N_EXPERTS = N_GROUPS * EXPERTS_PER_GROUP
LN_EPS = 1e-5
NEG = -1e30

SUBLANES = 8
LANES = 128

TM_PROJ = 512
TQ_ATTN = 512
ATTN_STACK = Q_PER_KV
SEG = 256
SCAN_LEN = SEG + 4
SCAN_UNROLL = 10
BF16_ROWS = 2 * SUBLANES
TD = 512
MIX_ROWS = 256
CHUNK = BF16_ROWS
STAGE_ROWS = 2 * TD + CHUNK * N_EXPERTS
MOE_BLOCK = 128
CHUNKS_PER_BLOCK = MOE_BLOCK // CHUNK
EXP_TILE = 2 * MOE_BLOCK
EXP_BUFS = 6
VMEM_LIMIT = 56 * 1024 * 1024
TILE_DMA_PRIORITY = 1


def _cparams(n_axes):
    return pltpu.CompilerParams(dimension_semantics=("arbitrary",) * n_axes,
                                vmem_limit_bytes=VMEM_LIMIT)


def _softplus(z):
    e = jnp.exp(-jnp.abs(z))
    w = 1.0 + e
    tiny = w == 1.0
    log1p = jnp.where(tiny, e, jnp.log(w) * (e / jnp.where(tiny, 1.0, w - 1.0)))
    return jnp.maximum(z, 0.0) + log1p


def _sigmoid(x):
    return 0.5 * jnp.tanh(0.5 * x) + 0.5


def _gelu_tanh(y):
    c1 = math.sqrt(2.0 / math.pi)
    half = 0.5 * y
    return half + half * jnp.tanh(y * (c1 + (c1 * 0.044715) * (y * y)))


def _layer_norm(y, g, b):
    mu = jnp.mean(y, axis=-1, keepdims=True)
    d = y - mu
    var = jnp.mean(d * d, axis=-1, keepdims=True)
    return d * lax.rsqrt(var + LN_EPS) * g + b


def _project_tile(xb, w_ref, cos, sin, q_ref, k_ref, v_ref, xr_ref, yr_ref):
    aw = N_Q_HEADS * HEAD_DIM
    kw = N_KV_HEADS * HEAD_DIM
    d_model = xr_ref.shape[1]

    def proj(c0, n):
        return jnp.dot(xb, w_ref[:, c0:c0 + n], preferred_element_type=F32)

    def rope(t):
        return t * cos + pltpu.roll(t, HEAD_DIM // 2, 1) * sin

    zq = proj(0, aw)
    scale = HEAD_DIM ** -0.5
    for h in range(N_Q_HEADS):
        sl = slice(h * HEAD_DIM, (h + 1) * HEAD_DIM)
        q_ref[:, sl] = (rope(zq[:, sl]) * scale).astype(BF16)
    zk = proj(aw, kw)
    for h in range(N_KV_HEADS):
        sl = slice(h * HEAD_DIM, (h + 1) * HEAD_DIM)
        k_ref[:, sl] = rope(zk[:, sl]).astype(BF16)
    v_ref[...] = proj(aw + kw, kw).astype(BF16)
    c0 = aw + 2 * kw
    xr_ref[...] = proj(c0, d_model)
    yr_ref[...] = proj(c0 + d_model, d_model)


def _inproj_kernel(x_ref, w_ref, cos_ref, sin_ref, q_ref, k_ref, v_ref, xr_ref, yr_ref):
    _project_tile(x_ref[...].astype(BF16), w_ref, cos_ref[...], sin_ref[...],
                  q_ref, k_ref, v_ref, xr_ref, yr_ref)


def _proj_out_specs(t, d, tm, row):
    aw = N_Q_HEADS * HEAD_DIM
    kw = N_KV_HEADS * HEAD_DIM
    shapes = [jax.ShapeDtypeStruct((t, aw), BF16), jax.ShapeDtypeStruct((t, kw), BF16),
              jax.ShapeDtypeStruct((t, kw), BF16)] + [jax.ShapeDtypeStruct((t, d), F32)] * 2
    specs = [pl.BlockSpec((tm, aw), row), pl.BlockSpec((tm, kw), row),
             pl.BlockSpec((tm, kw), row)] + [pl.BlockSpec((tm, d), row)] * 2
    return shapes, specs


def _inproj(x2, w_in_b, cosf, sinf, seq):
    t, d = x2.shape
    n_in = w_in_b.shape[1]
    tm = TM_PROJ
    tiles_per_seq = seq // tm
    row = lambda i: (i, 0)
    pos = lambda i: (i % tiles_per_seq, 0)
    shapes, specs = _proj_out_specs(t, d, tm, row)
    return pl.pallas_call(
        _inproj_kernel,
        grid=(t // tm,),
        in_specs=[pl.BlockSpec((tm, d), row),
                  pl.BlockSpec((d, n_in), lambda i: (0, 0)),
                  pl.BlockSpec((tm, HEAD_DIM), pos),
                  pl.BlockSpec((tm, HEAD_DIM), pos)],
        out_specs=specs,
        out_shape=shapes,
        compiler_params=_cparams(1),
        name="inproj",
    )(x2, w_in_b, cosf, sinf)


def _attn_kernel(sink_ref, q_ref, kp_ref, kc_ref, kn_ref, vp_ref, vc_ref, vn_ref, o_ref,
                 *, seq):
    i = pl.program_id(1)
    tq = q_ref.shape[1]
    blk = WINDOW
    t0 = i * tq
    kext = jnp.concatenate([kp_ref[0], kc_ref[0], kn_ref[0]], axis=0)
    vext = jnp.concatenate([vp_ref[0], vc_ref[0], vn_ref[0]], axis=0)
    nrow = ATTN_STACK * blk
    blk_shift = blk.bit_length() - 1
    qi = lax.broadcasted_iota(jnp.int32, (nrow, 3 * blk), 0) & (blk - 1)
    kj = lax.broadcasted_iota(jnp.int32, (nrow, 3 * blk), 1)
    band = jnp.where(jnp.abs(kj - blk - qi) <= WINDOW, 0.0, NEG)
    kj_row = lax.broadcasted_iota(jnp.int32, (1, 3 * blk), 1)
    rowg = lax.broadcasted_iota(jnp.int32, (nrow, 1), 0) >> blk_shift
    sinks = []
    for h0 in range(0, N_Q_HEADS, ATTN_STACK):
        sk = jnp.full((nrow, 1), sink_ref[h0], F32)
        for g in range(1, ATTN_STACK):
            sk = jnp.where(rowg == g, sink_ref[h0 + g], sk)
        sinks.append(sk)
    for j in range(tq // blk):
        kpos = t0 + j * blk - blk + kj_row
        bias = band + jnp.where((kpos >= 0) & (kpos < seq), 0.0, NEG)
        for h0 in range(0, N_Q_HEADS, ATTN_STACK):
            kv = h0 // Q_PER_KV
            hs = slice(kv * HEAD_DIM, (kv + 1) * HEAD_DIM)
            kblk = kext[j * blk:j * blk + 3 * blk, hs]
            vblk = vext[j * blk:j * blk + 3 * blk, hs]
            qs = [q_ref[0, j * blk:(j + 1) * blk, (h0 + g) * HEAD_DIM:(h0 + g + 1) * HEAD_DIM]
                  for g in range(ATTN_STACK)]
            qblk = jnp.concatenate(qs, axis=0)
            s = lax.dot_general(qblk, kblk, (((1,), (1,)), ((), ())),
                                preferred_element_type=F32) + bias
            sk = sinks[h0 // ATTN_STACK]
            m = jnp.maximum(jnp.max(s, axis=-1, keepdims=True), sk)
            p = jnp.exp(s - m)
            denom = jnp.sum(p, axis=-1, keepdims=True) + jnp.exp(sk - m)
            o = jnp.dot(p.astype(BF16), vblk, preferred_element_type=F32) / denom
            for g in range(ATTN_STACK):
                c = (h0 + g) * HEAD_DIM
                o_ref[0, j * blk:(j + 1) * blk, c:c + HEAD_DIM] = (
                    o[g * blk:(g + 1) * blk].astype(BF16))


def _attention(q3, k3, v3, sink):
    b, s, aw = q3.shape
    kw = k3.shape[2]
    tq = TQ_ATTN
    blk = WINDOW
    r = tq // blk
    nblk = s // blk
    cur = lambda bi, i: (bi, i, 0)
    prev = lambda bi, i: (bi, jnp.maximum(i * r - 1, 0), 0)
    nxt = lambda bi, i: (bi, jnp.minimum((i + 1) * r, nblk - 1), 0)
    kv_specs = [pl.BlockSpec((1, blk, kw), prev), pl.BlockSpec((1, tq, kw), cur),
                pl.BlockSpec((1, blk, kw), nxt)]
    return pl.pallas_call(
        functools.partial(_attn_kernel, seq=s),
        grid=(b, s // tq),
        in_specs=[pl.BlockSpec(memory_space=pltpu.SMEM),
                  pl.BlockSpec((1, tq, aw), cur)] + kv_specs + kv_specs,
        out_specs=pl.BlockSpec((1, tq, aw), cur),
        out_shape=jax.ShapeDtypeStruct((b, s, aw), BF16),
        compiler_params=_cparams(2),
        name="attention",
    )(sink, q3, k3, k3, k3, v3, v3, v3)


def _rnn_kernel(xr_ref, yr_ref, wc_ref, bc_ref, wg_ref, bg_ref, lam_ref, out_ref,
                xpad_ref, af_ref, uf_ref, ab_ref, ub_ref, hb_ref, cf_ref, cb_ref, *, seq):
    nseg = seq // SEG
    nlane = nseg
    ngrp = nlane // SUBLANES
    hf_ref = xpad_ref
    wc = wc_ref[...]
    bc = bc_ref[0]
    wg = wg_ref[0]
    bg = 0.5 * bg_ref[0]
    lam = lam_ref[0]
    rate = (0.5 * LRU_C) * _softplus(-lam)
    a_refs = (af_ref, ab_ref)
    u_refs = (uf_ref, ub_ref)
    left = CONV_W // 2

    halo = jnp.zeros((SUBLANES, LANES), F32)
    xpad_ref[pl.ds(0, SUBLANES), :] = halo
    xpad_ref[pl.ds(seq + SUBLANES, SUBLANES), :] = halo

    def pad_copy(c, carry):
        t0 = pl.multiple_of(c * SEG, SEG)
        xpad_ref[pl.ds(t0 + SUBLANES, SEG), :] = xr_ref[0, pl.ds(t0, SEG), :]
        return carry

    lax.fori_loop(0, nseg, pad_copy, 0, unroll=4)

    tail = nlane * SCAN_LEN - seq
    for a_ref, u_ref in zip(a_refs, u_refs):
        a_ref[pl.ds(seq, tail), :] = jnp.ones((tail, LANES), F32)
        u_ref[pl.ds(seq, tail), :] = jnp.zeros((tail, LANES), F32)

    def gates(c, carry):
        t0 = pl.multiple_of(c * SEG, SEG)
        xc = bc
        for tap in range(CONV_W):
            xc = xc + xpad_ref[pl.ds(t0 + SUBLANES - left + tap, SEG), :] * wc[tap:tap + 1]
        gh = jnp.dot(xc.astype(BF16), wg, preferred_element_type=F32) + bg
        xch = 0.5 * xc
        for d in range(2):
            rt = rate[:, d * LANES:(d + 1) * LANES]
            nlog_a = rt * jnp.tanh(gh[:, (2 * d) * LANES:(2 * d + 1) * LANES]) + rt
            a = jnp.exp2(nlog_a * (-1.0 / math.log(2.0)))
            z = jnp.tanh(nlog_a) * (a * a + 1.0)
            mult = jnp.where(z > 0.0, z * lax.rsqrt(z), 0.0)
            in_gate2 = jnp.tanh(gh[:, (2 * d + 1) * LANES:(2 * d + 2) * LANES]) + 1.0
            a_refs[d][pl.ds(t0, SEG), :] = a
            u_refs[d][pl.ds(t0, SEG), :] = (xch * mult) * in_gate2
        return carry

    lax.fori_loop(0, nseg, gates, 0, unroll=8)

    def lane_rows(g, j):
        return pl.ds(g * SUBLANES * SCAN_LEN + j, SUBLANES, stride=SCAN_LEN)

    def totals_step(j, carry):
        hf, pf, hb, pb = carry
        jb = SCAN_LEN - 1 - j
        nhf, npf, nhb, npb = [], [], [], []
        for g in range(ngrp):
            a = af_ref[lane_rows(g, j), :]
            nhf.append(a * hf[g] + uf_ref[lane_rows(g, j), :])
            npf.append(a * pf[g])
            a = ab_ref[lane_rows(g, jb), :]
            nhb.append(a * hb[g] + ub_ref[lane_rows(g, jb), :])
            npb.append(a * pb[g])
        return tuple(nhf), tuple(npf), tuple(nhb), tuple(npb)

    zero = tuple(jnp.zeros((SUBLANES, LANES), F32) for _ in range(ngrp))
    one = tuple(jnp.ones((SUBLANES, LANES), F32) for _ in range(ngrp))
    hf, pf, hb, pb = lax.fori_loop(0, SCAN_LEN, totals_step, (zero, one, zero, one),
                                   unroll=SCAN_UNROLL)

    c = jnp.zeros((1, LANES), F32)
    for s in range(nlane):
        g, r = divmod(s, SUBLANES)
        cf_ref[s:s + 1, :] = c
        c = pf[g][r:r + 1] * c + hf[g][r:r + 1]
    c = jnp.zeros((1, LANES), F32)
    for s in range(nlane - 1, -1, -1):
        g, r = divmod(s, SUBLANES)
        cb_ref[s:s + 1, :] = c
        c = pb[g][r:r + 1] * c + hb[g][r:r + 1]

    def scan_step(j, carry):
        hf, hb = carry
        jb = SCAN_LEN - 1 - j
        nhf, nhb = [], []
        for g in range(ngrp):
            h = af_ref[lane_rows(g, j), :] * hf[g] + uf_ref[lane_rows(g, j), :]
            hf_ref[lane_rows(g, j), :] = h
            nhf.append(h)
            h = ab_ref[lane_rows(g, jb), :] * hb[g] + ub_ref[lane_rows(g, jb), :]
            hb_ref[lane_rows(g, jb), :] = h
            nhb.append(h)
        return tuple(nhf), tuple(nhb)

    hf0 = tuple(cf_ref[g * SUBLANES:(g + 1) * SUBLANES, :] for g in range(ngrp))
    hb0 = tuple(cb_ref[g * SUBLANES:(g + 1) * SUBLANES, :] for g in range(ngrp))
    lax.fori_loop(0, SCAN_LEN, scan_step, (hf0, hb0), unroll=SCAN_UNROLL)

    def finish(c, carry):
        t0 = pl.multiple_of(c * SEG, SEG)
        h = hf_ref[pl.ds(t0, SEG), :] + hb_ref[pl.ds(t0, SEG), :]
        out_ref[0, pl.ds(t0, SEG), :] = (h * _gelu_tanh(yr_ref[0, pl.ds(t0, SEG), :])).astype(BF16)
        return carry

    lax.fori_loop(0, nseg, finish, 0, unroll=4)


def _rnn(xr3, yr3, w_conv, b_conv, wg_cat, bg_cat, lam_cat):
    b, s, d = xr3.shape
    nblk = d // LANES
    nseg = s // SEG
    slab = lambda bi, n: (bi, 0, n)
    per_blk = lambda bi, n: (n, 0, 0)
    assert nseg % SUBLANES == 0 and nseg * SCAN_LEN >= s + 2 * SUBLANES
    scratch = ([pltpu.VMEM((nseg * SCAN_LEN, LANES), F32)] * 6
               + [pltpu.VMEM((nseg, LANES), F32)] * 2)
    return pl.pallas_call(
        functools.partial(_rnn_kernel, seq=s),
        grid=(b, nblk),
        in_specs=[pl.BlockSpec((1, s, LANES), slab),
                  pl.BlockSpec((1, s, LANES), slab),
                  pl.BlockSpec((CONV_W, LANES), lambda bi, n: (0, n)),
                  pl.BlockSpec((1, 1, LANES), per_blk),
                  pl.BlockSpec((1, LANES, 4 * LANES), per_blk),
                  pl.BlockSpec((1, 1, 4 * LANES), per_blk),
                  pl.BlockSpec((1, 1, 2 * LANES), per_blk)],
        out_specs=pl.BlockSpec((1, s, LANES), slab),
        out_shape=jax.ShapeDtypeStruct((b, s, d), BF16),
        scratch_shapes=scratch,
        compiler_params=_cparams(2),
        name="rglru",
    )(xr3, yr3, w_conv, b_conv, wg_cat, bg_cat, lam_cat)


def _route(logits):
    lane = lax.broadcasted_iota(jnp.int32, logits.shape, 1)
    lanef = lane.astype(F32)
    big = float(4 * LANES)
    gmask = (lane >= N_EXPERTS) & (lane < N_EXPERTS + N_GROUPS)
    gl = jnp.where(gmask, logits, NEG)
    gmax = jnp.max(gl, axis=-1, keepdims=True)
    ge = jnp.exp(gl - gmax)
    gprob = ge / jnp.sum(ge, axis=-1, keepdims=True)
    gval = jnp.max(gprob, axis=-1, keepdims=True)
    gidx = jnp.min(jnp.where((gprob == gval) & gmask, lanef, big), axis=-1, keepdims=True)
    gidx = gidx.astype(jnp.int32) - N_EXPERTS
    group_shift = EXPERTS_PER_GROUP.bit_length() - 1
    emask = (lane < N_EXPERTS) & ((lane >> group_shift) == gidx)
    el = jnp.where(emask, logits, NEG)
    m1 = jnp.max(el, axis=-1, keepdims=True)
    i1 = jnp.min(jnp.where((el == m1) & emask, lanef, big), axis=-1, keepdims=True)
    emask2 = emask & (lanef != i1)
    el2 = jnp.where(emask2, logits, NEG)
    m2 = jnp.max(el2, axis=-1, keepdims=True)
    i2 = jnp.min(jnp.where((el2 == m2) & emask2, lanef, big), axis=-1, keepdims=True)
    e2 = jnp.exp(m2 - m1)
    den = 1.0 + e2
    g1 = (1.0 / den) * gval
    g2 = (e2 / den) * gval
    route = jnp.where(lane == 0, i1, jnp.where(lane == 1, i2,
                      jnp.where(lane == 2, g1, jnp.where(lane == 3, g2, 0.0))))
    onehot = (lanef == i1).astype(F32) + (lanef == i2).astype(F32)
    return route, jnp.sum(onehot, axis=0, keepdims=True)


def _mixout_kernel(x_ref, o_ref, hy_ref, wmg_ref, wao_ref, wro_ref, wout_ref,
                   lng_ref, lnb_ref, wr_ref, br_ref, x1_ref, x1b_ref, route_ref, cnt_ref,
                   logits_ref, *, alpha):
    @pl.when(pl.program_id(0) == 0)
    def _():
        logits_ref[...] = jnp.zeros_like(logits_ref)

    route, cnt = _route(logits_ref[...])
    route_ref[...] = route
    cnt_ref[0] = cnt

    d = x_ref.shape[1]
    for r0 in range(0, x_ref.shape[0], MIX_ROWS):
        rows = pl.ds(r0, MIX_ROWS)
        x = x_ref[rows, :]
        xb = x.astype(BF16)
        ga = jnp.dot(xb, wmg_ref[:, :d], preferred_element_type=F32)
        gr = jnp.dot(xb, wmg_ref[:, d:], preferred_element_type=F32)
        ya = jnp.dot(o_ref[rows, :], wao_ref[...], preferred_element_type=F32)
        yr = jnp.dot(hy_ref[rows, :], wro_ref[...], preferred_element_type=F32)
        merged = _sigmoid(ga) * ya + _sigmoid(gr) * yr
        mix = jnp.dot(merged.astype(BF16), wout_ref[...], preferred_element_type=F32)
        x1 = _layer_norm(alpha * x + mix, lng_ref[...], lnb_ref[...])
        x1_ref[rows, :] = x1
        x1b = x1.astype(BF16)
        x1b_ref[rows, :] = x1b
        logits_ref[rows, :] = (jnp.dot(x1b, wr_ref[...], preferred_element_type=F32)
                               + br_ref[...])


def _mixout(x2, o2, hy2, wmg, wao, wro, wout, lng, lnb, wr, br, alpha):
    t, d = x2.shape
    tm = TD
    nt = t // tm
    row = lambda i: (jnp.minimum(i, nt - 1), 0)
    prev_row = lambda i: (jnp.maximum(i - 1, 0), 0)
    full = lambda i: (0, 0)
    return pl.pallas_call(
        functools.partial(_mixout_kernel, alpha=alpha),
        grid=(nt + 1,),
        in_specs=[pl.BlockSpec((tm, d), row)] * 3
                 + [pl.BlockSpec((d, 2 * d), full)]
                 + [pl.BlockSpec((d, d), full)] * 3
                 + [pl.BlockSpec((1, d), full)] * 2
                 + [pl.BlockSpec((d, LANES), full), pl.BlockSpec((1, LANES), full)],
        out_specs=[pl.BlockSpec((tm, d), row), pl.BlockSpec((tm, d), row),
                   pl.BlockSpec((tm, LANES), prev_row),
                   pl.BlockSpec((1, 1, LANES), lambda i: (jnp.maximum(i - 1, 0), 0, 0))],
        out_shape=[jax.ShapeDtypeStruct((t, d), F32), jax.ShapeDtypeStruct((t, d), BF16),
                   jax.ShapeDtypeStruct((t, LANES), F32),
                   jax.ShapeDtypeStruct((nt, 1, LANES), F32)],
        scratch_shapes=[pltpu.VMEM((tm, LANES), F32)],
        compiler_params=_cparams(1),
        name="mixout",
    )(x2, o2, hy2, wmg, wao, wro, wout, lng, lnb, wr, br)


def _perm_matrix(route, soff_row, w1, w2):
    td = route.shape[0]
    lane = lax.broadcasted_iota(jnp.int32, (td, LANES), 1).astype(F32)
    e1 = lane == route[:, 0:1]
    e2 = lane == route[:, 1:2]
    cnt = (e1.astype(F32) + e2.astype(F32)).astype(BF16)
    ti = lax.broadcasted_iota(jnp.int32, (td, td), 0)
    tj = lax.broadcasted_iota(jnp.int32, (td, td), 1)
    lower = (tj < ti).astype(BF16)
    pos = jnp.dot(lower, cnt, preferred_element_type=F32) + soff_row
    r1 = jnp.sum(jnp.where(e1, pos, 0.0), axis=-1, keepdims=True).astype(jnp.int32)
    r2 = jnp.sum(jnp.where(e2, pos, 0.0), axis=-1, keepdims=True).astype(jnp.int32)
    col = lax.broadcasted_iota(jnp.int32, (td, STAGE_ROWS), 1)
    return jnp.where(col == r1, w1, 0.0) + jnp.where(col == r2, w2, 0.0)


def _chunk_rows(c):
    return pl.ds(pl.multiple_of(c * CHUNK, CHUNK), CHUNK)


def _block_rows(b):
    return pl.ds(pl.multiple_of(b * MOE_BLOCK, MOE_BLOCK), MOE_BLOCK)


def _for_each_chunk(i, cch_ref, soff_ref, dch_ref, fn):
    def per_expert(e, carry):
        k = i * N_EXPERTS + e
        so = soff_ref[k]
        do = dch_ref[k]

        def per_chunk(c, carry2):
            fn(so + c, do + c)
            return carry2

        return lax.fori_loop(0, cch_ref[k], per_chunk, carry)

    lax.fori_loop(0, N_EXPERTS, per_expert, 0)


def _repeat(n, fn):
    def body(c, carry):
        fn()
        return carry

    lax.fori_loop(0, n, body, 0)


def _dispatch_kernel(cch_ref, soff_ref, dch_ref, tot_ref, padst_ref, padn_ref, nb_ref,
                     x_ref, route_ref, soffrow_ref, xb_ref, stage_ref, zero_ref, sems, zsem):
    i = pl.program_id(0)
    nt = pl.num_programs(0)
    n_blocks = xb_ref.shape[0] // MOE_BLOCK
    slot = i % 2

    def copy(s, src_chunk, dst_chunk):
        return pltpu.make_async_copy(stage_ref.at[s, _chunk_rows(src_chunk)],
                                     xb_ref.at[_chunk_rows(dst_chunk)], sems.at[s])

    @pl.when(i >= 2)
    def _():
        _repeat(tot_ref[jnp.maximum(i - 2, 0)], lambda: copy(slot, 0, 0).wait())

    pt = _perm_matrix(route_ref[...], soffrow_ref[0], 1.0, 1.0).astype(BF16)
    stage_ref[slot] = lax.dot_general(pt, x_ref[...], (((0,), (0,)), ((), ())),
                                      preferred_element_type=F32).astype(BF16)
    _for_each_chunk(i, cch_ref, soff_ref, dch_ref,
                    lambda s, d: copy(slot, s, d).start(priority=TILE_DMA_PRIORITY))

    @pl.when(i == nt - 1)
    def _():
        @pl.when(i >= 1)
        def _():
            _repeat(tot_ref[jnp.maximum(i - 1, 0)], lambda: copy(1 - slot, 0, 0).wait())

        _repeat(tot_ref[i], lambda: copy(slot, 0, 0).wait())

        zero_ref[...] = jnp.zeros_like(zero_ref)

        def zcopy(dst_chunk):
            return pltpu.make_async_copy(zero_ref.at[pl.ds(0, CHUNK)],
                                         xb_ref.at[_chunk_rows(dst_chunk)], zsem)

        def pad_start(e, n):
            def pad_chunk(c, carry2):
                zcopy(padst_ref[e] + c).start()
                return carry2

            lax.fori_loop(0, padn_ref[e], pad_chunk, 0)
            return n + padn_ref[e]

        npad = lax.fori_loop(0, N_EXPERTS, pad_start, 0)
        _repeat(npad, lambda: zcopy(0).wait())

        def zblock(b):
            return pltpu.make_async_copy(zero_ref, xb_ref.at[_block_rows(b)], zsem)

        def start_block(b, carry):
            zblock(b).start()
            return carry

        def wait_block(b, carry):
            zblock(b).wait()
            return carry

        lax.fori_loop(nb_ref[0], n_blocks, start_block, 0)
        lax.fori_loop(nb_ref[0], n_blocks, wait_block, 0)


def _dispatch(x1, route, soffrow, tables, n_rows):
    t, d = x1.shape
    nt = t // TD
    grid_spec = pltpu.PrefetchScalarGridSpec(
        num_scalar_prefetch=7,
        grid=(nt,),
        in_specs=[pl.BlockSpec((TD, d), lambda i, *_: (i, 0)),
                  pl.BlockSpec((TD, LANES), lambda i, *_: (i, 0)),
                  pl.BlockSpec((1, 1, LANES), lambda i, *_: (i, 0, 0))],
        out_specs=pl.BlockSpec(memory_space=pl.ANY),
        scratch_shapes=[pltpu.VMEM((2, STAGE_ROWS, d), BF16), pltpu.VMEM((MOE_BLOCK, d), BF16),
                        pltpu.SemaphoreType.DMA((2,)), pltpu.SemaphoreType.DMA(())],
    )
    return pl.pallas_call(
        _dispatch_kernel,
        grid_spec=grid_spec,
        out_shape=jax.ShapeDtypeStruct((n_rows, d), BF16),
        compiler_params=_cparams(1),
        name="dispatch",
    )(*tables, x1, route, soffrow)


def _experts_kernel(pst_ref, ntile_ref, nb_ref, xb_ref, wg_ref, wu_ref, wd_ref, yb_ref,
                    wgb_ref, wub_ref, wdb_ref, xbuf_ref, ybuf_ref, in_sems, out_sems, zsem):
    e = pl.program_id(0)
    ne = pl.num_programs(0)
    n = ntile_ref[e]
    row0 = pst_ref[e]
    n_blocks = yb_ref.shape[0] // MOE_BLOCK
    ahead = EXP_BUFS - 1

    def tile_rows(r0, t):
        return pl.ds(pl.multiple_of(r0 + t * EXP_TILE, MOE_BLOCK), EXP_TILE)

    def in_copy(r0, t, s):
        return pltpu.make_async_copy(xb_ref.at[tile_rows(r0, t)], xbuf_ref.at[s], in_sems.at[s])

    def out_copy(t, s):
        return pltpu.make_async_copy(ybuf_ref.at[s], yb_ref.at[tile_rows(row0, t)],
                                     out_sems.at[s])

    def start_head(r0, count):
        for k in range(ahead):
            @pl.when(k < count)
            def _():
                in_copy(r0, k, k).start(priority=TILE_DMA_PRIORITY)

    @pl.when(e == 0)
    def _():
        start_head(row0, n)

    @pl.when(n > 0)
    def _():
        wgb_ref[...] = wg_ref[0, 0].astype(BF16)
        wub_ref[...] = wu_ref[0, 0].astype(BF16)
        wdb_ref[...] = wd_ref[0, 0].astype(BF16)

    def tile(t, carry):
        s = t % EXP_BUFS

        @pl.when(t + ahead < n)
        def _():
            in_copy(row0, t + ahead, (t + ahead) % EXP_BUFS).start(priority=TILE_DMA_PRIORITY)

        in_copy(row0, t, s).wait()

        @pl.when(t >= EXP_BUFS)
        def _():
            out_copy(t - EXP_BUFS, s).wait()

        x = xbuf_ref[s]
        hg = jnp.dot(x, wgb_ref[...], preferred_element_type=F32)
        hu = jnp.dot(x, wub_ref[...], preferred_element_type=F32)
        hid = (hg * _sigmoid(hg)) * hu
        ybuf_ref[s] = jnp.dot(hid.astype(BF16), wdb_ref[...],
                              preferred_element_type=F32).astype(BF16)
        out_copy(t, s).start(priority=TILE_DMA_PRIORITY)
        return carry

    lax.fori_loop(0, n, tile, 0)

    @pl.when(e + 1 < ne)
    def _():
        nxt = jnp.minimum(e + 1, ne - 1)
        start_head(pst_ref[nxt], ntile_ref[nxt])

    for k in range(EXP_BUFS):
        @pl.when(n > k)
        def _():
            out_copy(n - 1 - k, (n - 1 - k) % EXP_BUFS).wait()

    @pl.when(e == ne - 1)
    def _():
        ybuf_ref[0] = jnp.zeros(ybuf_ref.shape[1:], BF16)

        def zblock(b):
            return pltpu.make_async_copy(ybuf_ref.at[0, pl.ds(0, MOE_BLOCK)],
                                         yb_ref.at[_block_rows(b)], zsem)

        def start_block(b, carry):
            zblock(b).start()
            return carry

        def wait_block(b, carry):
            zblock(b).wait()
            return carry

        lax.fori_loop(nb_ref[0], n_blocks, start_block, 0)
        lax.fori_loop(nb_ref[0], n_blocks, wait_block, 0)


def _experts(xb, pst, ntile, nb_used, w_g, w_u, w_d, layer):
    n_rows, d = xb.shape
    n_exp, de = w_g.shape[1], w_g.shape[3]
    wsel = lambda e, *_: (layer, e, 0, 0)
    grid_spec = pltpu.PrefetchScalarGridSpec(
        num_scalar_prefetch=3,
        grid=(n_exp,),
        in_specs=[pl.BlockSpec(memory_space=pl.ANY),
                  pl.BlockSpec((1, 1, d, de), wsel),
                  pl.BlockSpec((1, 1, d, de), wsel),
                  pl.BlockSpec((1, 1, de, d), wsel)],
        out_specs=pl.BlockSpec(memory_space=pl.ANY),
        scratch_shapes=[pltpu.VMEM((d, de), BF16), pltpu.VMEM((d, de), BF16),
                        pltpu.VMEM((de, d), BF16),
                        pltpu.VMEM((EXP_BUFS, EXP_TILE, d), BF16),
                        pltpu.VMEM((EXP_BUFS, EXP_TILE, d), BF16),
                        pltpu.SemaphoreType.DMA((EXP_BUFS,)), pltpu.SemaphoreType.DMA((EXP_BUFS,)),
                        pltpu.SemaphoreType.DMA(())],
    )
    return pl.pallas_call(
        _experts_kernel,
        grid_spec=grid_spec,
        out_shape=jax.ShapeDtypeStruct((n_rows, d), BF16),
        compiler_params=_cparams(1),
        name="experts",
    )(pst, ntile, nb_used, xb, w_g, w_u, w_d)


def _combine_kernel(cch_ref, soff_ref, dch_ref, tot_ref,
                    x_ref, route_ref, soffrow_ref, yb_ref, lng_ref, lnb_ref, *rest,
                    alpha, project):
    if project:
        (w_ref, cos_ref, sin_ref, out_ref, q_ref, k_ref, v_ref, xr_ref, yr_ref,
         stage_ref, sems) = rest
    else:
        out_ref, stage_ref, sems = rest
    i = pl.program_id(0)
    nt = pl.num_programs(0)
    slot = i % 2

    def copy(s, src_chunk, dst_chunk):
        return pltpu.make_async_copy(yb_ref.at[_chunk_rows(src_chunk)],
                                     stage_ref.at[s, _chunk_rows(dst_chunk)], sems.at[s])

    def fetch(tile_idx, s):
        _for_each_chunk(tile_idx, cch_ref, soff_ref, dch_ref,
                        lambda so, do: copy(s, do, so).start(priority=TILE_DMA_PRIORITY))

    @pl.when(i == 0)
    def _():
        stage_ref[...] = jnp.zeros_like(stage_ref)
        fetch(i, slot)

    @pl.when(i + 1 < nt)
    def _():
        fetch(i + 1, 1 - slot)

    _repeat(tot_ref[i], lambda: copy(slot, 0, 0).wait())

    route = route_ref[...]
    pt = _perm_matrix(route, soffrow_ref[0], route[:, 2:3], route[:, 3:4]).astype(BF16)
    ffn = jnp.dot(pt, stage_ref[slot], preferred_element_type=F32)
    out = _layer_norm(alpha * x_ref[...] + ffn, lng_ref[...], lnb_ref[...])
    out_ref[...] = out
    if project:
        _project_tile(out.astype(BF16), w_ref, cos_ref[...], sin_ref[...],
                      q_ref, k_ref, v_ref, xr_ref, yr_ref)


def _combine(x1, route, soffrow, tables, yb, lng, lnb, alpha, next_proj=None):
    t, d = x1.shape
    nt = t // TD
    row = lambda i, *_: (i, 0)
    full = lambda i, *_: (0, 0)
    in_specs = [pl.BlockSpec((TD, d), row),
                pl.BlockSpec((TD, LANES), row),
                pl.BlockSpec((1, 1, LANES), lambda i, *_: (i, 0, 0)),
                pl.BlockSpec(memory_space=pl.ANY),
                pl.BlockSpec((1, d), full),
                pl.BlockSpec((1, d), full)]
    out_shape = [jax.ShapeDtypeStruct((t, d), F32)]
    out_specs = [pl.BlockSpec((TD, d), row)]
    args = [x1, route, soffrow, yb, lng, lnb]
    if next_proj is not None:
        w_in_b, cosf, sinf, seq = next_proj
        tiles_per_seq = seq // TD
        pos = lambda i, *_: (i % tiles_per_seq, 0)
        in_specs += [pl.BlockSpec(w_in_b.shape, full), pl.BlockSpec((TD, HEAD_DIM), pos),
                     pl.BlockSpec((TD, HEAD_DIM), pos)]
        shapes, specs = _proj_out_specs(t, d, TD, row)
        out_shape += shapes
        out_specs += specs
        args += [w_in_b, cosf, sinf]
    grid_spec = pltpu.PrefetchScalarGridSpec(
        num_scalar_prefetch=4,
        grid=(nt,),
        in_specs=in_specs,
        out_specs=out_specs,
        scratch_shapes=[pltpu.VMEM((2, STAGE_ROWS, d), BF16), pltpu.SemaphoreType.DMA((2,))],
    )
    return pl.pallas_call(
        functools.partial(_combine_kernel, alpha=alpha, project=next_proj is not None),
        grid_spec=grid_spec,
        out_shape=out_shape,
        compiler_params=_cparams(1),
        name="combine",
    )(*tables, *args)


def _max_blocks(t):
    nt = t // TD
    rows = 2 * t + (CHUNK - 1) * nt * N_EXPERTS + (MOE_BLOCK - CHUNK) * N_EXPERTS
    return -(-rows // MOE_BLOCK)


def _dispatch_tables(cnt):
    n = cnt[:, 0, :N_EXPERTS].astype(jnp.int32)
    cch = (n + CHUNK - 1) // CHUNK
    soff = jnp.cumsum(cch, axis=1) - cch
    tot_tile = jnp.sum(cch, axis=1)
    tot_e = jnp.sum(cch, axis=0)
    reg = (tot_e + CHUNKS_PER_BLOCK - 1) // CHUNKS_PER_BLOCK * CHUNKS_PER_BLOCK
    pend = jnp.cumsum(reg)
    pstart = pend - reg
    dch = pstart[None, :] + jnp.cumsum(cch, axis=0) - cch
    nb_used = pend[-1] // CHUNKS_PER_BLOCK
    soffrow = jnp.zeros((n.shape[0], 1, LANES), F32).at[:, 0, :N_EXPERTS].set(
        (soff * CHUNK).astype(F32))
    i32 = lambda a: a.reshape(-1).astype(jnp.int32)
    return dict(cch=i32(cch), soff=i32(soff), dch=i32(dch), tot=i32(tot_tile),
                padst=i32(pstart + tot_e), padn=i32(reg - tot_e), nb_used=i32(nb_used),
                pst=i32(pstart * CHUNK), ntile=i32((reg * CHUNK + EXP_TILE - 1) // EXP_TILE),
                soffrow=soffrow)


def _rope_tables(seq):
    inv = ROPE_THETA ** (-jnp.arange(0, HEAD_DIM, 2, dtype=F32) / HEAD_DIM)
    ang = jnp.arange(seq, dtype=F32)[:, None] * inv[None, :]
    cos, sin = jnp.cos(ang), jnp.sin(ang)
    return jnp.concatenate([cos, cos], axis=1), jnp.concatenate([-sin, sin], axis=1)


@jax.jit
def kernel(x, w_in, w_sink, w_conv, b_conv, w_rec_gate, b_rec_gate, w_in_gate, b_in_gate,
           lru_lambda, w_attn_o, w_rnn_o, w_out, ln_g, ln_b, w_router_group, b_router_group,
           w_router_expert, b_router_expert, w_exp_gate, w_exp_up, w_exp_down):
    bsz, seq, d = x.shape
    depth = w_in.shape[0]
    t = bsz * seq
    nblk = d // LANES
    alpha = (2 * depth) ** 0.25
    cosf, sinf = _rope_tables(seq)
    n_rows = _max_blocks(t) * MOE_BLOCK + (EXP_TILE - MOE_BLOCK)
    x2 = x.reshape(t, d)
    n_branch = w_in.shape[2] - 2 * d
    w_branch = [w_in[l, :, :n_branch].astype(BF16) for l in range(depth)]
    q, k, v, xr, yr = _inproj(x2, w_branch[0], cosf, sinf, seq)
    for l in range(depth):
        o = _attention(q.reshape(bsz, seq, -1), k.reshape(bsz, seq, -1), v.reshape(bsz, seq, -1),
                       w_sink[l])
        wg_cat = (0.5 * jnp.concatenate([w_rec_gate[l, 0], w_in_gate[l, 0], w_rec_gate[l, 1],
                                         w_in_gate[l, 1]], axis=-1)).astype(BF16)
        bg_cat = jnp.concatenate([b_rec_gate[l, 0].reshape(nblk, 1, LANES),
                                  b_in_gate[l, 0].reshape(nblk, 1, LANES),
                                  b_rec_gate[l, 1].reshape(nblk, 1, LANES),
                                  b_in_gate[l, 1].reshape(nblk, 1, LANES)], axis=-1)
        lam_cat = jnp.concatenate([lru_lambda[l, 0].reshape(nblk, 1, LANES),
                                   lru_lambda[l, 1].reshape(nblk, 1, LANES)], axis=-1)
        hy = _rnn(xr.reshape(bsz, seq, d), yr.reshape(bsz, seq, d), w_conv[l],
                  b_conv[l].reshape(nblk, 1, LANES), wg_cat, bg_cat, lam_cat)
        wr = jnp.zeros((d, LANES), F32)
        wr = wr.at[:, :N_EXPERTS].set(w_router_expert[l])
        wr = wr.at[:, N_EXPERTS:N_EXPERTS + N_GROUPS].set(w_router_group[l]).astype(BF16)
        br = jnp.zeros((1, LANES), F32)
        br = br.at[0, :N_EXPERTS].set(b_router_expert[l])
        br = br.at[0, N_EXPERTS:N_EXPERTS + N_GROUPS].set(b_router_group[l])
        x1, x1b, route, cnt = _mixout(x2, o.reshape(t, -1), hy.reshape(t, d),
                                 w_in[l, :, n_branch:].astype(BF16),
                                 w_attn_o[l].astype(BF16), w_rnn_o[l].astype(BF16),
                                 w_out[l].astype(BF16), ln_g[l, 0].reshape(1, d),
                                 ln_b[l, 0].reshape(1, d), wr, br, alpha)
        tb = _dispatch_tables(cnt)
        xb = _dispatch(x1b, route, tb["soffrow"],
                       (tb["cch"], tb["soff"], tb["dch"], tb["tot"], tb["padst"], tb["padn"],
                        tb["nb_used"]), n_rows)
        yb = _experts(xb, tb["pst"], tb["ntile"], tb["nb_used"], w_exp_gate, w_exp_up,
                      w_exp_down, l)
        next_proj = (w_branch[l + 1], cosf, sinf, seq) if l + 1 < depth else None
        x2, *nxt = _combine(x1, route, tb["soffrow"],
                            (tb["cch"], tb["soff"], tb["dch"], tb["tot"]), yb,
                            ln_g[l, 1].reshape(1, d), ln_b[l, 1].reshape(1, d), alpha, next_proj)
        if nxt:
            q, k, v, xr, yr = nxt
    return x2.reshape(bsz, seq, d)
```

```python
import functools
import math

import jax
import jax.numpy as jnp
from jax import lax
from jax.experimental import pallas as pl
from jax.experimental.pallas import tpu as pltpu

F32 = jnp.float32
BF16 = jnp.bfloat16

HEAD_DIM = 128
N_Q_HEADS = 8
N_KV_HEADS = 2
Q_PER_KV = N_Q_HEADS // N_KV_HEADS
WINDOW = 128
ROPE_THETA = 10000.0
CONV_W = 4
LRU_C = 8.0
N_GROUPS = 4
EXPERTS_PER_GROUP = 8
N_EXPERTS = N_GROUPS * EXPERTS_PER_GROUP
LN_EPS = 1e-5
NEG = -1e30
LOG2E = math.log2(math.e)

SUBLANES = 8
LANES = 128

TM_PROJ = 512
TQ_ATTN = 1024
ATTN_STACK = Q_PER_KV
SEG = 256
SCAN_LEN = SEG + 4
SCAN_UNROLL = 10
BF16_ROWS = 2 * SUBLANES
TD = 512
MIX_ROWS = 256
CHUNK = BF16_ROWS
STAGE_ROWS = 2 * TD + CHUNK * N_EXPERTS
MOE_BLOCK = 128
CHUNKS_PER_BLOCK = MOE_BLOCK // CHUNK
EXP_TILE = 2 * MOE_BLOCK
EXP_BUFS = 6
VMEM_LIMIT = 56 * 1024 * 1024


def _cparams(n_axes):
    return pltpu.CompilerParams(dimension_semantics=("arbitrary",) * n_axes,
                                vmem_limit_bytes=VMEM_LIMIT)


def _softplus(z):
    e = jnp.exp(-jnp.abs(z))
    w = 1.0 + e
    tiny = w == 1.0
    log1p = jnp.where(tiny, e, jnp.log(w) * (e / jnp.where(tiny, 1.0, w - 1.0)))
    return jnp.maximum(z, 0.0) + log1p


def _sigmoid(x):
    return 0.5 * jnp.tanh(0.5 * x) + 0.5


def _gelu_tanh(y):
    c1 = math.sqrt(2.0 / math.pi)
    half = 0.5 * y
    return half + half * jnp.tanh(y * (c1 + (c1 * 0.044715) * (y * y)))


def _layer_norm(y, g, b):
    mu = jnp.mean(y, axis=-1, keepdims=True)
    d = y - mu
    var = jnp.mean(d * d, axis=-1, keepdims=True)
    return d * lax.rsqrt(var + LN_EPS) * g + b


def _project_tile(xb, w_ref, cos, sin, q_ref, k_ref, v_ref, xr_ref, yr_ref):
    aw = N_Q_HEADS * HEAD_DIM
    kw = N_KV_HEADS * HEAD_DIM
    d_model = xr_ref.shape[1]

    def proj(c0, n):
        return jnp.dot(xb, w_ref[:, c0:c0 + n], preferred_element_type=F32)

    def rope(t):
        return t * cos + pltpu.roll(t, HEAD_DIM // 2, 1) * sin

    zq = proj(0, aw)
    scale = HEAD_DIM ** -0.5 * LOG2E
    for h in range(N_Q_HEADS):
        sl = slice(h * HEAD_DIM, (h + 1) * HEAD_DIM)
        q_ref[:, sl] = (rope(zq[:, sl]) * scale).astype(BF16)
    zk = proj(aw, kw)
    for h in range(N_KV_HEADS):
        sl = slice(h * HEAD_DIM, (h + 1) * HEAD_DIM)
        k_ref[:, sl] = rope(zk[:, sl]).astype(BF16)
    v_ref[...] = proj(aw + kw, kw).astype(BF16)
    c0 = aw + 2 * kw
    xr_ref[...] = proj(c0, d_model)
    yr_ref[...] = proj(c0 + d_model, d_model)


def _inproj_kernel(x_ref, w_ref, cos_ref, sin_ref, q_ref, k_ref, v_ref, xr_ref, yr_ref):
    _project_tile(x_ref[...].astype(BF16), w_ref, cos_ref[...], sin_ref[...],
                  q_ref, k_ref, v_ref, xr_ref, yr_ref)


def _proj_out_specs(t, d, tm, row):
    aw = N_Q_HEADS * HEAD_DIM
    kw = N_KV_HEADS * HEAD_DIM
    shapes = [jax.ShapeDtypeStruct((t, aw), BF16), jax.ShapeDtypeStruct((t, kw), BF16),
              jax.ShapeDtypeStruct((t, kw), BF16)] + [jax.ShapeDtypeStruct((t, d), F32)] * 2
    specs = [pl.BlockSpec((tm, aw), row), pl.BlockSpec((tm, kw), row),
             pl.BlockSpec((tm, kw), row)] + [pl.BlockSpec((tm, d), row)] * 2
    return shapes, specs


def _inproj(x2, w_in_b, cosf, sinf, seq):
    t, d = x2.shape
    n_in = w_in_b.shape[1]
    tm = TM_PROJ
    tiles_per_seq = seq // tm
    row = lambda i: (i, 0)
    pos = lambda i: (i % tiles_per_seq, 0)
    shapes, specs = _proj_out_specs(t, d, tm, row)
    return pl.pallas_call(
        _inproj_kernel,
        grid=(t // tm,),
        in_specs=[pl.BlockSpec((tm, d), row),
                  pl.BlockSpec((d, n_in), lambda i: (0, 0)),
                  pl.BlockSpec((tm, HEAD_DIM), pos),
                  pl.BlockSpec((tm, HEAD_DIM), pos)],
        out_specs=specs,
        out_shape=shapes,
        compiler_params=_cparams(1),
        name="inproj",
    )(x2, w_in_b, cosf, sinf)


def _attn_kernel(sink_ref, q_ref, kp_ref, kc_ref, kn_ref, vp_ref, vc_ref, vn_ref, o_ref,
                 *, seq):
    i = pl.program_id(1)
    tq = q_ref.shape[1]
    blk = WINDOW
    t0 = i * tq
    kext = jnp.concatenate([kp_ref[0], kc_ref[0], kn_ref[0]], axis=0)
    vext = jnp.concatenate([vp_ref[0], vc_ref[0], vn_ref[0]], axis=0)
    nrow = ATTN_STACK * blk
    blk_shift = blk.bit_length() - 1
    qi = lax.broadcasted_iota(jnp.int32, (nrow, 3 * blk), 0) & (blk - 1)
    kj = lax.broadcasted_iota(jnp.int32, (nrow, 3 * blk), 1)
    band = jnp.where(jnp.abs(kj - blk - qi) <= WINDOW, 0.0, NEG)
    kj_row = lax.broadcasted_iota(jnp.int32, (1, 3 * blk), 1)
    rowg = lax.broadcasted_iota(jnp.int32, (nrow, 1), 0) >> blk_shift
    sinks = []
    for h0 in range(0, N_Q_HEADS, ATTN_STACK):
        sk = jnp.full((nrow, 1), sink_ref[h0], F32)
        for g in range(1, ATTN_STACK):
            sk = jnp.where(rowg == g, sink_ref[h0 + g], sk)
        sinks.append(sk * LOG2E)
    for j in range(tq // blk):
        kpos = t0 + j * blk - blk + kj_row
        bias = band + jnp.where((kpos >= 0) & (kpos < seq), 0.0, NEG)
        for h0 in range(0, N_Q_HEADS, ATTN_STACK):
            kv = h0 // Q_PER_KV
            hs = slice(kv * HEAD_DIM, (kv + 1) * HEAD_DIM)
            kblk = kext[j * blk:j * blk + 3 * blk, hs]
            vblk = vext[j * blk:j * blk + 3 * blk, hs]
            qs = [q_ref[0, j * blk:(j + 1) * blk, (h0 + g) * HEAD_DIM:(h0 + g + 1) * HEAD_DIM]
                  for g in range(ATTN_STACK)]
            qblk = jnp.concatenate(qs, axis=0)
            s = lax.dot_general(qblk, kblk, (((1,), (1,)), ((), ())),
                                preferred_element_type=F32) + bias
            sk = sinks[h0 // ATTN_STACK]
            m = jnp.maximum(jnp.max(s, axis=-1, keepdims=True), sk)
            p = jnp.exp2(s - m)
            denom = jnp.sum(p, axis=-1, keepdims=True) + jnp.exp2(sk - m)
            o = jnp.dot(p.astype(BF16), vblk, preferred_element_type=F32) / denom
            for g in range(ATTN_STACK):
                c = (h0 + g) * HEAD_DIM
                o_ref[0, j * blk:(j + 1) * blk, c:c + HEAD_DIM] = (
                    o[g * blk:(g + 1) * blk].astype(BF16))


def _attention(q3, k3, v3, sink):
    b, s, aw = q3.shape
    kw = k3.shape[2]
    tq = TQ_ATTN
    blk = WINDOW
    r = tq // blk
    nblk = s // blk
    cur = lambda bi, i: (bi, i, 0)
    prev = lambda bi, i: (bi, jnp.maximum(i * r - 1, 0), 0)
    nxt = lambda bi, i: (bi, jnp.minimum((i + 1) * r, nblk - 1), 0)
    kv_specs = [pl.BlockSpec((1, blk, kw), prev), pl.BlockSpec((1, tq, kw), cur),
                pl.BlockSpec((1, blk, kw), nxt)]
    return pl.pallas_call(
        functools.partial(_attn_kernel, seq=s),
        grid=(b, s // tq),
        in_specs=[pl.BlockSpec(memory_space=pltpu.SMEM),
                  pl.BlockSpec((1, tq, aw), cur)] + kv_specs + kv_specs,
        out_specs=pl.BlockSpec((1, tq, aw), cur),
        out_shape=jax.ShapeDtypeStruct((b, s, aw), BF16),
        compiler_params=_cparams(2),
        name="attention",
    )(sink, q3, k3, k3, k3, v3, v3, v3)


def _rnn_kernel(xr_ref, yr_ref, wc_ref, bc_ref, wg_ref, bg_ref, lam_ref, out_ref,
                xpad_ref, af_ref, uf_ref, ab_ref, ub_ref, hb_ref, cf_ref, cb_ref, *, seq):
    nseg = seq // SEG
    nlane = nseg
    ngrp = nlane // SUBLANES
    hf_ref = xpad_ref
    wc = wc_ref[...]
    bc = bc_ref[0]
    wg = wg_ref[0]
    bg = 0.5 * bg_ref[0]
    lam = lam_ref[0]
    rate = (0.5 * LRU_C) * _softplus(-lam)
    a_refs = (af_ref, ab_ref)
    u_refs = (uf_ref, ub_ref)
    left = CONV_W // 2

    halo = jnp.zeros((SUBLANES, LANES), F32)
    xpad_ref[pl.ds(0, SUBLANES), :] = halo
    xpad_ref[pl.ds(seq + SUBLANES, SUBLANES), :] = halo

    def pad_copy(c, carry):
        t0 = pl.multiple_of(c * SEG, SEG)
        xpad_ref[pl.ds(t0 + SUBLANES, SEG), :] = xr_ref[0, pl.ds(t0, SEG), :]
        return carry

    lax.fori_loop(0, nseg, pad_copy, 0, unroll=4)

    tail = nlane * SCAN_LEN - seq
    for a_ref, u_ref in zip(a_refs, u_refs):
        a_ref[pl.ds(seq, tail), :] = jnp.ones((tail, LANES), F32)
        u_ref[pl.ds(seq, tail), :] = jnp.zeros((tail, LANES), F32)

    def gates(c, carry):
        t0 = pl.multiple_of(c * SEG, SEG)
        xc = bc
        for tap in range(CONV_W):
            xc = xc + xpad_ref[pl.ds(t0 + SUBLANES - left + tap, SEG), :] * wc[tap:tap + 1]
        gh = jnp.dot(xc.astype(BF16), wg, preferred_element_type=F32) + bg
        xch = 0.5 * xc
        for d in range(2):
            rt = rate[:, d * LANES:(d + 1) * LANES]
            nlog_a = rt * jnp.tanh(gh[:, (2 * d) * LANES:(2 * d + 1) * LANES]) + rt
            a = jnp.exp2(nlog_a * (-1.0 / math.log(2.0)))
            z = jnp.tanh(nlog_a) * (a * a + 1.0)
            mult = jnp.where(z > 0.0, z * lax.rsqrt(z), 0.0)
            in_gate2 = jnp.tanh(gh[:, (2 * d + 1) * LANES:(2 * d + 2) * LANES]) + 1.0
            a_refs[d][pl.ds(t0, SEG), :] = a
            u_refs[d][pl.ds(t0, SEG), :] = (xch * mult) * in_gate2
        return carry

    lax.fori_loop(0, nseg, gates, 0, unroll=8)

    def lane_rows(g, j):
        return pl.ds(g * SUBLANES * SCAN_LEN + j, SUBLANES, stride=SCAN_LEN)

    def totals_step(j, carry):
        hf, pf, hb, pb = carry
        jb = SCAN_LEN - 1 - j
        nhf, npf, nhb, npb = [], [], [], []
        for g in range(ngrp):
            a = af_ref[lane_rows(g, j), :]
            nhf.append(a * hf[g] + uf_ref[lane_rows(g, j), :])
            npf.append(a * pf[g])
            a = ab_ref[lane_rows(g, jb), :]
            nhb.append(a * hb[g] + ub_ref[lane_rows(g, jb), :])
            npb.append(a * pb[g])
        return tuple(nhf), tuple(npf), tuple(nhb), tuple(npb)

    zero = tuple(jnp.zeros((SUBLANES, LANES), F32) for _ in range(ngrp))
    one = tuple(jnp.ones((SUBLANES, LANES), F32) for _ in range(ngrp))
    hf, pf, hb, pb = lax.fori_loop(0, SCAN_LEN, totals_step, (zero, one, zero, one),
                                   unroll=SCAN_UNROLL)

    c = jnp.zeros((1, LANES), F32)
    for s in range(nlane):
        g, r = divmod(s, SUBLANES)
        cf_ref[s:s + 1, :] = c
        c = pf[g][r:r + 1] * c + hf[g][r:r + 1]
    c = jnp.zeros((1, LANES), F32)
    for s in range(nlane - 1, -1, -1):
        g, r = divmod(s, SUBLANES)
        cb_ref[s:s + 1, :] = c
        c = pb[g][r:r + 1] * c + hb[g][r:r + 1]

    def scan_step(j, carry):
        hf, hb = carry
        jb = SCAN_LEN - 1 - j
        nhf, nhb = [], []
        for g in range(ngrp):
            h = af_ref[lane_rows(g, j), :] * hf[g] + uf_ref[lane_rows(g, j), :]
            hf_ref[lane_rows(g, j), :] = h
            nhf.append(h)
            h = ab_ref[lane_rows(g, jb), :] * hb[g] + ub_ref[lane_rows(g, jb), :]
            hb_ref[lane_rows(g, jb), :] = h
            nhb.append(h)
        return tuple(nhf), tuple(nhb)

    hf0 = tuple(cf_ref[g * SUBLANES:(g + 1) * SUBLANES, :] for g in range(ngrp))
    hb0 = tuple(cb_ref[g * SUBLANES:(g + 1) * SUBLANES, :] for g in range(ngrp))
    lax.fori_loop(0, SCAN_LEN, scan_step, (hf0, hb0), unroll=SCAN_UNROLL)

    def finish(c, carry):
        t0 = pl.multiple_of(c * SEG, SEG)
        h = hf_ref[pl.ds(t0, SEG), :] + hb_ref[pl.ds(t0, SEG), :]
        out_ref[0, pl.ds(t0, SEG), :] = (h * _gelu_tanh(yr_ref[0, pl.ds(t0, SEG), :])).astype(BF16)
        return carry

    lax.fori_loop(0, nseg, finish, 0, unroll=4)


def _rnn(xr3, yr3, w_conv, b_conv, wg_cat, bg_cat, lam_cat):
    b, s, d = xr3.shape
    nblk = d // LANES
    nseg = s // SEG
    slab = lambda bi, n: (bi, 0, n)
    per_blk = lambda bi, n: (n, 0, 0)
    assert nseg % SUBLANES == 0 and nseg * SCAN_LEN >= s + 2 * SUBLANES
    scratch = ([pltpu.VMEM((nseg * SCAN_LEN, LANES), F32)] * 6
               + [pltpu.VMEM((nseg, LANES), F32)] * 2)
    return pl.pallas_call(
        functools.partial(_rnn_kernel, seq=s),
        grid=(b, nblk),
        in_specs=[pl.BlockSpec((1, s, LANES), slab),
                  pl.BlockSpec((1, s, LANES), slab),
                  pl.BlockSpec((CONV_W, LANES), lambda bi, n: (0, n)),
                  pl.BlockSpec((1, 1, LANES), per_blk),
                  pl.BlockSpec((1, LANES, 4 * LANES), per_blk),
                  pl.BlockSpec((1, 1, 4 * LANES), per_blk),
                  pl.BlockSpec((1, 1, 2 * LANES), per_blk)],
        out_specs=pl.BlockSpec((1, s, LANES), slab),
        out_shape=jax.ShapeDtypeStruct((b, s, d), BF16),
        scratch_shapes=scratch,
        compiler_params=_cparams(2),
        name="rglru",
    )(xr3, yr3, w_conv, b_conv, wg_cat, bg_cat, lam_cat)


def _route(logits):
    lane = lax.broadcasted_iota(jnp.int32, logits.shape, 1)
    lanef = lane.astype(F32)
    big = float(4 * LANES)
    gmask = (lane >= N_EXPERTS) & (lane < N_EXPERTS + N_GROUPS)
    gl = jnp.where(gmask, logits, NEG)
    gmax = jnp.max(gl, axis=-1, keepdims=True)
    ge = jnp.exp(gl - gmax)
    gprob = ge / jnp.sum(ge, axis=-1, keepdims=True)
    gval = jnp.max(gprob, axis=-1, keepdims=True)
    gidx = jnp.min(jnp.where((gprob == gval) & gmask, lanef, big), axis=-1, keepdims=True)
    gidx = gidx.astype(jnp.int32) - N_EXPERTS
    group_shift = EXPERTS_PER_GROUP.bit_length() - 1
    emask = (lane < N_EXPERTS) & ((lane >> group_shift) == gidx)
    el = jnp.where(emask, logits, NEG)
    m1 = jnp.max(el, axis=-1, keepdims=True)
    i1 = jnp.min(jnp.where((el == m1) & emask, lanef, big), axis=-1, keepdims=True)
    emask2 = emask & (lanef != i1)
    el2 = jnp.where(emask2, logits, NEG)
    m2 = jnp.max(el2, axis=-1, keepdims=True)
    i2 = jnp.min(jnp.where((el2 == m2) & emask2, lanef, big), axis=-1, keepdims=True)
    e2 = jnp.exp(m2 - m1)
    den = 1.0 + e2
    g1 = (1.0 / den) * gval
    g2 = (e2 / den) * gval
    route = jnp.where(lane == 0, i1, jnp.where(lane == 1, i2,
                      jnp.where(lane == 2, g1, jnp.where(lane == 3, g2, 0.0))))
    onehot = (lanef == i1).astype(F32) + (lanef == i2).astype(F32)
    return route, jnp.sum(onehot, axis=0, keepdims=True)


def _mixout_kernel(x_ref, o_ref, hy_ref, wmg_ref, wao_ref, wro_ref, wout_ref,
                   lng_ref, lnb_ref, wr_ref, br_ref, x1_ref, x1b_ref, route_ref, cnt_ref,
                   logits_ref, *, alpha):
    @pl.when(pl.program_id(0) == 0)
    def _():
        logits_ref[...] = jnp.zeros_like(logits_ref)

    route, cnt = _route(logits_ref[...])
    route_ref[...] = route
    cnt_ref[0] = cnt

    d = x_ref.shape[1]
    for r0 in range(0, x_ref.shape[0], MIX_ROWS):
        rows = pl.ds(r0, MIX_ROWS)
        x = x_ref[rows, :]
        xb = x.astype(BF16)
        ga = jnp.dot(xb, wmg_ref[:, :d], preferred_element_type=F32)
        gr = jnp.dot(xb, wmg_ref[:, d:], preferred_element_type=F32)
        ya = jnp.dot(o_ref[rows, :], wao_ref[...], preferred_element_type=F32)
        yr = jnp.dot(hy_ref[rows, :], wro_ref[...], preferred_element_type=F32)
        merged = _sigmoid(ga) * ya + _sigmoid(gr) * yr
        mix = jnp.dot(merged.astype(BF16), wout_ref[...], preferred_element_type=F32)
        x1 = _layer_norm(alpha * x + mix, lng_ref[...], lnb_ref[...])
        x1_ref[rows, :] = x1
        x1b = x1.astype(BF16)
        x1b_ref[rows, :] = x1b
        logits_ref[rows, :] = (jnp.dot(x1b, wr_ref[...], preferred_element_type=F32)
                               + br_ref[...])


def _mixout(x2, o2, hy2, wmg, wao, wro, wout, lng, lnb, wr, br, alpha):
    t, d = x2.shape
    tm = TD
    nt = t // tm
    row = lambda i: (jnp.minimum(i, nt - 1), 0)
    prev_row = lambda i: (jnp.maximum(i - 1, 0), 0)
    full = lambda i: (0, 0)
    return pl.pallas_call(
        functools.partial(_mixout_kernel, alpha=alpha),
        grid=(nt + 1,),
        in_specs=[pl.BlockSpec((tm, d), row)] * 3
                 + [pl.BlockSpec((d, 2 * d), full)]
                 + [pl.BlockSpec((d, d), full)] * 3
                 + [pl.BlockSpec((1, d), full)] * 2
                 + [pl.BlockSpec((d, LANES), full), pl.BlockSpec((1, LANES), full)],
        out_specs=[pl.BlockSpec((tm, d), row), pl.BlockSpec((tm, d), row),
                   pl.BlockSpec((tm, LANES), prev_row),
                   pl.BlockSpec((1, 1, LANES), lambda i: (jnp.maximum(i - 1, 0), 0, 0))],
        out_shape=[jax.ShapeDtypeStruct((t, d), F32), jax.ShapeDtypeStruct((t, d), BF16),
                   jax.ShapeDtypeStruct((t, LANES), F32),
                   jax.ShapeDtypeStruct((nt, 1, LANES), F32)],
        scratch_shapes=[pltpu.VMEM((tm, LANES), F32)],
        compiler_params=_cparams(1),
        name="mixout",
    )(x2, o2, hy2, wmg, wao, wro, wout, lng, lnb, wr, br)


def _perm_matrix(route, soff_row, w1, w2):
    td = route.shape[0]
    lane = lax.broadcasted_iota(jnp.int32, (td, LANES), 1).astype(F32)
    e1 = lane == route[:, 0:1]
    e2 = lane == route[:, 1:2]
    cnt = (e1.astype(F32) + e2.astype(F32)).astype(BF16)
    ti = lax.broadcasted_iota(jnp.int32, (td, td), 0)
    tj = lax.broadcasted_iota(jnp.int32, (td, td), 1)
    lower = (tj < ti).astype(BF16)
    pos = jnp.dot(lower, cnt, preferred_element_type=F32) + soff_row
    r1 = jnp.sum(jnp.where(e1, pos, 0.0), axis=-1, keepdims=True).astype(jnp.int32)
    r2 = jnp.sum(jnp.where(e2, pos, 0.0), axis=-1, keepdims=True).astype(jnp.int32)
    col = lax.broadcasted_iota(jnp.int32, (td, STAGE_ROWS), 1)
    return jnp.where(col == r1, w1, 0.0) + jnp.where(col == r2, w2, 0.0)


def _chunk_rows(c):
    return pl.ds(pl.multiple_of(c * CHUNK, CHUNK), CHUNK)


def _block_rows(b):
    return pl.ds(pl.multiple_of(b * MOE_BLOCK, MOE_BLOCK), MOE_BLOCK)


def _for_each_chunk(i, cch_ref, soff_ref, dch_ref, fn):
    def per_expert(e, carry):
        k = i * N_EXPERTS + e
        so = soff_ref[k]
        do = dch_ref[k]

        def per_chunk(c, carry2):
            fn(so + c, do + c)
            return carry2

        return lax.fori_loop(0, cch_ref[k], per_chunk, carry)

    lax.fori_loop(0, N_EXPERTS, per_expert, 0)


def _repeat(n, fn):
    def body(c, carry):
        fn()
        return carry

    lax.fori_loop(0, n, body, 0)


def _dispatch_kernel(cch_ref, soff_ref, dch_ref, tot_ref, padst_ref, padn_ref, nb_ref,
                     x_ref, route_ref, soffrow_ref, xb_ref, stage_ref, zero_ref, sems, zsem):
    i = pl.program_id(0)
    nt = pl.num_programs(0)
    n_blocks = xb_ref.shape[0] // MOE_BLOCK
    slot = i % 2

    def copy(s, src_chunk, dst_chunk):
        return pltpu.make_async_copy(stage_ref.at[s, _chunk_rows(src_chunk)],
                                     xb_ref.at[_chunk_rows(dst_chunk)], sems.at[s])

    @pl.when(i >= 2)
    def _():
        _repeat(tot_ref[jnp.maximum(i - 2, 0)], lambda: copy(slot, 0, 0).wait())

    pt = _perm_matrix(route_ref[...], soffrow_ref[0], 1.0, 1.0).astype(BF16)
    stage_ref[slot] = lax.dot_general(pt, x_ref[...], (((0,), (0,)), ((), ())),
                                      preferred_element_type=F32).astype(BF16)
    _for_each_chunk(i, cch_ref, soff_ref, dch_ref,
                    lambda s, d: copy(slot, s, d).start())

    @pl.when(i == nt - 1)
    def _():
        @pl.when(i >= 1)
        def _():
            _repeat(tot_ref[jnp.maximum(i - 1, 0)], lambda: copy(1 - slot, 0, 0).wait())

        _repeat(tot_ref[i], lambda: copy(slot, 0, 0).wait())

        zero_ref[...] = jnp.zeros_like(zero_ref)

        def zcopy(dst_chunk):
            return pltpu.make_async_copy(zero_ref.at[pl.ds(0, CHUNK)],
                                         xb_ref.at[_chunk_rows(dst_chunk)], zsem)

        def pad_start(e, n):
            def pad_chunk(c, carry2):
                zcopy(padst_ref[e] + c).start()
                return carry2

            lax.fori_loop(0, padn_ref[e], pad_chunk, 0)
            return n + padn_ref[e]

        npad = lax.fori_loop(0, N_EXPERTS, pad_start, 0)
        _repeat(npad, lambda: zcopy(0).wait())

        def zblock(b):
            return pltpu.make_async_copy(zero_ref, xb_ref.at[_block_rows(b)], zsem)

        def start_block(b, carry):
            zblock(b).start()
            return carry

        def wait_block(b, carry):
            zblock(b).wait()
            return carry

        lax.fori_loop(nb_ref[0], n_blocks, start_block, 0)
        lax.fori_loop(nb_ref[0], n_blocks, wait_block, 0)


def _dispatch(x1, route, soffrow, tables, n_rows):
    t, d = x1.shape
    nt = t // TD
    grid_spec = pltpu.PrefetchScalarGridSpec(
        num_scalar_prefetch=7,
        grid=(nt,),
        in_specs=[pl.BlockSpec((TD, d), lambda i, *_: (i, 0)),
                  pl.BlockSpec((TD, LANES), lambda i, *_: (i, 0)),
                  pl.BlockSpec((1, 1, LANES), lambda i, *_: (i, 0, 0))],
        out_specs=pl.BlockSpec(memory_space=pl.ANY),
        scratch_shapes=[pltpu.VMEM((2, STAGE_ROWS, d), BF16), pltpu.VMEM((MOE_BLOCK, d), BF16),
                        pltpu.SemaphoreType.DMA((2,)), pltpu.SemaphoreType.DMA(())],
    )
    return pl.pallas_call(
        _dispatch_kernel,
        grid_spec=grid_spec,
        out_shape=jax.ShapeDtypeStruct((n_rows, d), BF16),
        compiler_params=_cparams(1),
        name="dispatch",
    )(*tables, x1, route, soffrow)


def _experts_kernel(pst_ref, ntile_ref, nb_ref, xb_ref, wg_ref, wu_ref, wd_ref, yb_ref,
                    wgb_ref, wub_ref, wdb_ref, xbuf_ref, ybuf_ref, in_sems, out_sems, zsem):
    e = pl.program_id(0)
    ne = pl.num_programs(0)
    n = ntile_ref[e]
    row0 = pst_ref[e]
    n_blocks = yb_ref.shape[0] // MOE_BLOCK
    ahead = EXP_BUFS - 1

    def tile_rows(r0, t):
        return pl.ds(pl.multiple_of(r0 + t * EXP_TILE, MOE_BLOCK), EXP_TILE)

    def in_copy(r0, t, s):
        return pltpu.make_async_copy(xb_ref.at[tile_rows(r0, t)], xbuf_ref.at[s], in_sems.at[s])

    def out_copy(t, s):
        return pltpu.make_async_copy(ybuf_ref.at[s], yb_ref.at[tile_rows(row0, t)],
                                     out_sems.at[s])

    def start_head(r0, count):
        for k in range(ahead):
            @pl.when(k < count)
            def _():
                in_copy(r0, k, k).start()

    @pl.when(e == 0)
    def _():
        start_head(row0, n)

    @pl.when(n > 0)
    def _():
        wgb_ref[...] = wg_ref[0, 0].astype(BF16)
        wub_ref[...] = wu_ref[0, 0].astype(BF16)
        wdb_ref[...] = wd_ref[0, 0].astype(BF16)

    def tile(t, carry):
        s = t % EXP_BUFS

        @pl.when(t + ahead < n)
        def _():
            in_copy(row0, t + ahead, (t + ahead) % EXP_BUFS).start()

        in_copy(row0, t, s).wait()

        @pl.when(t >= EXP_BUFS)
        def _():
            out_copy(t - EXP_BUFS, s).wait()

        x = xbuf_ref[s]
        hg = jnp.dot(x, wgb_ref[...], preferred_element_type=F32)
        hu = jnp.dot(x, wub_ref[...], preferred_element_type=F32)
        hid = (hg * _sigmoid(hg)) * hu
        ybuf_ref[s] = jnp.dot(hid.astype(BF16), wdb_ref[...],
                              preferred_element_type=F32).astype(BF16)
        out_copy(t, s).start()
        return carry

    lax.fori_loop(0, n, tile, 0)

    @pl.when(e + 1 < ne)
    def _():
        nxt = jnp.minimum(e + 1, ne - 1)
        start_head(pst_ref[nxt], ntile_ref[nxt])

    for k in range(EXP_BUFS):
        @pl.when(n > k)
        def _():
            out_copy(n - 1 - k, (n - 1 - k) % EXP_BUFS).wait()

    @pl.when(e == ne - 1)
    def _():
        ybuf_ref[0] = jnp.zeros(ybuf_ref.shape[1:], BF16)

        def zblock(b):
            return pltpu.make_async_copy(ybuf_ref.at[0, pl.ds(0, MOE_BLOCK)],
                                         yb_ref.at[_block_rows(b)], zsem)

        def start_block(b, carry):
            zblock(b).start()
            return carry

        def wait_block(b, carry):
            zblock(b).wait()
            return carry

        lax.fori_loop(nb_ref[0], n_blocks, start_block, 0)
        lax.fori_loop(nb_ref[0], n_blocks, wait_block, 0)


def _experts(xb, pst, ntile, nb_used, w_g, w_u, w_d, layer):
    n_rows, d = xb.shape
    n_exp, de = w_g.shape[1], w_g.shape[3]
    wsel = lambda e, *_: (layer, e, 0, 0)
    grid_spec = pltpu.PrefetchScalarGridSpec(
        num_scalar_prefetch=3,
        grid=(n_exp,),
        in_specs=[pl.BlockSpec(memory_space=pl.ANY),
                  pl.BlockSpec((1, 1, d, de), wsel),
                  pl.BlockSpec((1, 1, d, de), wsel),
                  pl.BlockSpec((1, 1, de, d), wsel)],
        out_specs=pl.BlockSpec(memory_space=pl.ANY),
        scratch_shapes=[pltpu.VMEM((d, de), BF16), pltpu.VMEM((d, de), BF16),
                        pltpu.VMEM((de, d), BF16),
                        pltpu.VMEM((EXP_BUFS, EXP_TILE, d), BF16),
                        pltpu.VMEM((EXP_BUFS, EXP_TILE, d), BF16),
                        pltpu.SemaphoreType.DMA((EXP_BUFS,)), pltpu.SemaphoreType.DMA((EXP_BUFS,)),
                        pltpu.SemaphoreType.DMA(())],
    )
    return pl.pallas_call(
        _experts_kernel,
        grid_spec=grid_spec,
        out_shape=jax.ShapeDtypeStruct((n_rows, d), BF16),
        compiler_params=_cparams(1),
        name="experts",
    )(pst, ntile, nb_used, xb, w_g, w_u, w_d)


def _combine_kernel(cch_ref, soff_ref, dch_ref, tot_ref,
                    x_ref, route_ref, soffrow_ref, yb_ref, lng_ref, lnb_ref, *rest,
                    alpha, project):
    if project:
        (w_ref, cos_ref, sin_ref, out_ref, q_ref, k_ref, v_ref, xr_ref, yr_ref,
         stage_ref, sems) = rest
    else:
        out_ref, stage_ref, sems = rest
    i = pl.program_id(0)
    nt = pl.num_programs(0)
    slot = i % 2

    def copy(s, src_chunk, dst_chunk):
        return pltpu.make_async_copy(yb_ref.at[_chunk_rows(src_chunk)],
                                     stage_ref.at[s, _chunk_rows(dst_chunk)], sems.at[s])

    def fetch(tile_idx, s):
        _for_each_chunk(tile_idx, cch_ref, soff_ref, dch_ref,
                        lambda so, do: copy(s, do, so).start())

    @pl.when(i == 0)
    def _():
        stage_ref[...] = jnp.zeros_like(stage_ref)
        fetch(i, slot)

    @pl.when(i + 1 < nt)
    def _():
        fetch(i + 1, 1 - slot)

    _repeat(tot_ref[i], lambda: copy(slot, 0, 0).wait())

    route = route_ref[...]
    pt = _perm_matrix(route, soffrow_ref[0], route[:, 2:3], route[:, 3:4]).astype(BF16)
    ffn = jnp.dot(pt, stage_ref[slot], preferred_element_type=F32)
    out = _layer_norm(alpha * x_ref[...] + ffn, lng_ref[...], lnb_ref[...])
    out_ref[...] = out
    if project:
        _project_tile(out.astype(BF16), w_ref, cos_ref[...], sin_ref[...],
                      q_ref, k_ref, v_ref, xr_ref, yr_ref)


def _combine(x1, route, soffrow, tables, yb, lng, lnb, alpha, next_proj=None):
    t, d = x1.shape
    nt = t // TD
    row = lambda i, *_: (i, 0)
    full = lambda i, *_: (0, 0)
    in_specs = [pl.BlockSpec((TD, d), row),
                pl.BlockSpec((TD, LANES), row),
                pl.BlockSpec((1, 1, LANES), lambda i, *_: (i, 0, 0)),
                pl.BlockSpec(memory_space=pl.ANY),
                pl.BlockSpec((1, d), full),
                pl.BlockSpec((1, d), full)]
    out_shape = [jax.ShapeDtypeStruct((t, d), F32)]
    out_specs = [pl.BlockSpec((TD, d), row)]
    args = [x1, route, soffrow, yb, lng, lnb]
    if next_proj is not None:
        w_in_b, cosf, sinf, seq = next_proj
        tiles_per_seq = seq // TD
        pos = lambda i, *_: (i % tiles_per_seq, 0)
        in_specs += [pl.BlockSpec(w_in_b.shape, full), pl.BlockSpec((TD, HEAD_DIM), pos),
                     pl.BlockSpec((TD, HEAD_DIM), pos)]
        shapes, specs = _proj_out_specs(t, d, TD, row)
        out_shape += shapes
        out_specs += specs
        args += [w_in_b, cosf, sinf]
    grid_spec = pltpu.PrefetchScalarGridSpec(
        num_scalar_prefetch=4,
        grid=(nt,),
        in_specs=in_specs,
        out_specs=out_specs,
        scratch_shapes=[pltpu.VMEM((2, STAGE_ROWS, d), BF16), pltpu.SemaphoreType.DMA((2,))],
    )
    return pl.pallas_call(
        functools.partial(_combine_kernel, alpha=alpha, project=next_proj is not None),
        grid_spec=grid_spec,
        out_shape=out_shape,
        compiler_params=_cparams(1),
        name="combine",
    )(*tables, *args)


def _max_blocks(t):
    nt = t // TD
    rows = 2 * t + (CHUNK - 1) * nt * N_EXPERTS + (MOE_BLOCK - CHUNK) * N_EXPERTS
    return -(-rows // MOE_BLOCK)


def _dispatch_tables(cnt):
    n = cnt[:, 0, :N_EXPERTS].astype(jnp.int32)
    nt = n.shape[0]
    cch = (n + CHUNK - 1) // CHUNK
    excl = lambda k: (jnp.arange(k)[:, None] < jnp.arange(k)[None, :]).astype(F32)
    dot = functools.partial(jnp.dot, precision=lax.Precision.HIGHEST)
    cchf = cch.astype(F32)
    soff = dot(cchf, excl(N_EXPERTS)).astype(jnp.int32)
    tot_tile = jnp.sum(cch, axis=1)
    tot_e = jnp.sum(cch, axis=0)
    reg = (tot_e + CHUNKS_PER_BLOCK - 1) // CHUNKS_PER_BLOCK * CHUNKS_PER_BLOCK
    pstart = dot(reg.astype(F32)[None, :], excl(N_EXPERTS))[0].astype(jnp.int32)
    before = dot(excl(nt).T, cchf).astype(jnp.int32)
    dch = pstart[None, :] + before
    nb_used = jnp.sum(reg) // CHUNKS_PER_BLOCK
    soffrow = jnp.pad((soff * CHUNK).astype(F32), ((0, 0), (0, LANES - N_EXPERTS)))[:, None, :]
    i32 = lambda a: a.reshape(-1).astype(jnp.int32)
    return dict(cch=i32(cch), soff=i32(soff), dch=i32(dch), tot=i32(tot_tile),
                padst=i32(pstart + tot_e), padn=i32(reg - tot_e), nb_used=i32(nb_used),
                pst=i32(pstart * CHUNK), ntile=i32((reg * CHUNK + EXP_TILE - 1) // EXP_TILE),
                soffrow=soffrow)


def _rope_tables(seq):
    inv = ROPE_THETA ** (-jnp.arange(0, HEAD_DIM, 2, dtype=F32) / HEAD_DIM)
    ang = jnp.arange(seq, dtype=F32)[:, None] * inv[None, :]
    cos, sin = jnp.cos(ang), jnp.sin(ang)
    return jnp.concatenate([cos, cos], axis=1), jnp.concatenate([-sin, sin], axis=1)


@jax.jit
def kernel(x, w_in, w_sink, w_conv, b_conv, w_rec_gate, b_rec_gate, w_in_gate, b_in_gate,
           lru_lambda, w_attn_o, w_rnn_o, w_out, ln_g, ln_b, w_router_group, b_router_group,
           w_router_expert, b_router_expert, w_exp_gate, w_exp_up, w_exp_down):
    bsz, seq, d = x.shape
    depth = w_in.shape[0]
    t = bsz * seq
    nblk = d // LANES
    alpha = (2 * depth) ** 0.25
    cosf, sinf = _rope_tables(seq)
    n_rows = _max_blocks(t) * MOE_BLOCK + (EXP_TILE - MOE_BLOCK)
    x2 = x.reshape(t, d)
    n_branch = w_in.shape[2] - 2 * d
    w_branch = [w_in[l, :, :n_branch].astype(BF16) for l in range(depth)]
    q, k, v, xr, yr = _inproj(x2, w_branch[0], cosf, sinf, seq)
    for l in range(depth):
        o = _attention(q.reshape(bsz, seq, -1), k.reshape(bsz, seq, -1), v.reshape(bsz, seq, -1),
                       w_sink[l])
        wg_cat = (0.5 * jnp.concatenate([w_rec_gate[l, 0], w_in_gate[l, 0], w_rec_gate[l, 1],
                                         w_in_gate[l, 1]], axis=-1)).astype(BF16)
        bg_cat = jnp.concatenate([b_rec_gate[l, 0].reshape(nblk, 1, LANES),
                                  b_in_gate[l, 0].reshape(nblk, 1, LANES),
                                  b_rec_gate[l, 1].reshape(nblk, 1, LANES),
                                  b_in_gate[l, 1].reshape(nblk, 1, LANES)], axis=-1)
        lam_cat = jnp.concatenate([lru_lambda[l, 0].reshape(nblk, 1, LANES),
                                   lru_lambda[l, 1].reshape(nblk, 1, LANES)], axis=-1)
        hy = _rnn(xr.reshape(bsz, seq, d), yr.reshape(bsz, seq, d), w_conv[l],
                  b_conv[l].reshape(nblk, 1, LANES), wg_cat, bg_cat, lam_cat)
        lane_pad = LANES - N_EXPERTS - N_GROUPS
        wr = jnp.concatenate([w_router_expert[l], w_router_group[l],
                              jnp.zeros((d, lane_pad), F32)], axis=1).astype(BF16)
        br = jnp.concatenate([b_router_expert[l], b_router_group[l],
                              jnp.zeros((lane_pad,), F32)])[None, :]
        x1, x1b, route, cnt = _mixout(x2, o.reshape(t, -1), hy.reshape(t, d),
                                 w_in[l, :, n_branch:].astype(BF16),
                                 w_attn_o[l].astype(BF16), w_rnn_o[l].astype(BF16),
                                 w_out[l].astype(BF16), ln_g[l, 0].reshape(1, d),
                                 ln_b[l, 0].reshape(1, d), wr, br, alpha)
        tb = _dispatch_tables(cnt)
        xb = _dispatch(x1b, route, tb["soffrow"],
                       (tb["cch"], tb["soff"], tb["dch"], tb["tot"], tb["padst"], tb["padn"],
                        tb["nb_used"]), n_rows)
        yb = _experts(xb, tb["pst"], tb["ntile"], tb["nb_used"], w_exp_gate, w_exp_up,
                      w_exp_down, l)
        next_proj = (w_branch[l + 1], cosf, sinf, seq) if l + 1 < depth else None
        x2, *nxt = _combine(x1, route, tb["soffrow"],
                            (tb["cch"], tb["soff"], tb["dch"], tb["tot"]), yb,
                            ln_g[l, 1].reshape(1, d), ln_b[l, 1].reshape(1, d), alpha, next_proj)
        if nxt:
            q, k, v, xr, yr = nxt
    return x2.reshape(bsz, seq, d)
```

```python
import functools
import math

import jax
import jax.numpy as jnp
from jax import lax
from jax.experimental import pallas as pl
from jax.experimental.pallas import tpu as pltpu

F32 = jnp.float32
BF16 = jnp.bfloat16

HEAD_DIM = 128
N_Q_HEADS = 8
N_KV_HEADS = 2
Q_PER_KV = N_Q_HEADS // N_KV_HEADS
WINDOW = 128
ROPE_THETA = 10000.0
CONV_W = 4
LRU_C = 8.0
N_GROUPS = 4
EXPERTS_PER_GROUP = 8
N_EXPERTS = N_GROUPS * EXPERTS_PER_GROUP
LN_EPS = 1e-5
NEG = -1e30
LOG2E = math.log2(math.e)

SUBLANES = 8
LANES = 128

TM_PROJ = 512
TQ_ATTN = 1024
ATTN_STACK = Q_PER_KV
SEG = 256
SCAN_LEN = SEG + 4
SCAN_UNROLL = 10
BF16_ROWS = 2 * SUBLANES
TD = 512
MIX_ROWS = 256
CHUNK = BF16_ROWS
WAIT_BATCH = 8
STAGE_ROWS = 2 * TD + CHUNK * N_EXPERTS
MOE_BLOCK = 128
CHUNKS_PER_BLOCK = MOE_BLOCK // CHUNK
EXP_TILE = 2 * MOE_BLOCK
EXP_BUFS = 6
VMEM_LIMIT = 56 * 1024 * 1024


def _cparams(n_axes):
    return pltpu.CompilerParams(dimension_semantics=("arbitrary",) * n_axes,
                                vmem_limit_bytes=VMEM_LIMIT)


def _softplus(z):
    e = jnp.exp(-jnp.abs(z))
    w = 1.0 + e
    tiny = w == 1.0
    log1p = jnp.where(tiny, e, jnp.log(w) * (e / jnp.where(tiny, 1.0, w - 1.0)))
    return jnp.maximum(z, 0.0) + log1p


def _sigmoid(x):
    return 0.5 * jnp.tanh(0.5 * x) + 0.5


def _gelu_tanh(y):
    c1 = math.sqrt(2.0 / math.pi)
    half = 0.5 * y
    return half + half * jnp.tanh(y * (c1 + (c1 * 0.044715) * (y * y)))


def _layer_norm(y, g, b):
    mu = jnp.mean(y, axis=-1, keepdims=True)
    d = y - mu
    var = jnp.mean(d * d, axis=-1, keepdims=True)
    return d * lax.rsqrt(var + LN_EPS) * g + b


def _project_tile(xb, w_ref, cos, sin, q_ref, k_ref, v_ref, xr_ref, yr_ref):
    aw = N_Q_HEADS * HEAD_DIM
    kw = N_KV_HEADS * HEAD_DIM
    d_model = xr_ref.shape[1]

    def proj(c0, n):
        return jnp.dot(xb, w_ref[:, c0:c0 + n], preferred_element_type=F32)

    def rope(t):
        return t * cos + pltpu.roll(t, HEAD_DIM // 2, 1) * sin

    zq = proj(0, aw)
    scale = HEAD_DIM ** -0.5 * LOG2E
    for h in range(N_Q_HEADS):
        sl = slice(h * HEAD_DIM, (h + 1) * HEAD_DIM)
        q_ref[:, sl] = (rope(zq[:, sl]) * scale).astype(BF16)
    zk = proj(aw, kw)
    for h in range(N_KV_HEADS):
        sl = slice(h * HEAD_DIM, (h + 1) * HEAD_DIM)
        k_ref[:, sl] = rope(zk[:, sl]).astype(BF16)
    v_ref[...] = proj(aw + kw, kw).astype(BF16)
    c0 = aw + 2 * kw
    xr_ref[...] = proj(c0, d_model)
    yr_ref[...] = proj(c0 + d_model, d_model)


def _inproj_kernel(x_ref, w_ref, cos_ref, sin_ref, q_ref, k_ref, v_ref, xr_ref, yr_ref):
    _project_tile(x_ref[...].astype(BF16), w_ref, cos_ref[...], sin_ref[...],
                  q_ref, k_ref, v_ref, xr_ref, yr_ref)


def _proj_out_specs(t, d, tm, row):
    aw = N_Q_HEADS * HEAD_DIM
    kw = N_KV_HEADS * HEAD_DIM
    shapes = [jax.ShapeDtypeStruct((t, aw), BF16), jax.ShapeDtypeStruct((t, kw), BF16),
              jax.ShapeDtypeStruct((t, kw), BF16)] + [jax.ShapeDtypeStruct((t, d), F32)] * 2
    specs = [pl.BlockSpec((tm, aw), row), pl.BlockSpec((tm, kw), row),
             pl.BlockSpec((tm, kw), row)] + [pl.BlockSpec((tm, d), row)] * 2
    return shapes, specs


def _inproj(x2, w_in_b, cosf, sinf, seq):
    t, d = x2.shape
    n_in = w_in_b.shape[1]
    tm = TM_PROJ
    tiles_per_seq = seq // tm
    row = lambda i: (i, 0)
    pos = lambda i: (i % tiles_per_seq, 0)
    shapes, specs = _proj_out_specs(t, d, tm, row)
    return pl.pallas_call(
        _inproj_kernel,
        grid=(t // tm,),
        in_specs=[pl.BlockSpec((tm, d), row),
                  pl.BlockSpec((d, n_in), lambda i: (0, 0)),
                  pl.BlockSpec((tm, HEAD_DIM), pos),
                  pl.BlockSpec((tm, HEAD_DIM), pos)],
        out_specs=specs,
        out_shape=shapes,
        compiler_params=_cparams(1),
        name="inproj",
    )(x2, w_in_b, cosf, sinf)


def _attn_kernel(sink_ref, q_ref, kp_ref, kc_ref, kn_ref, vp_ref, vc_ref, vn_ref, o_ref,
                 *, seq):
    i = pl.program_id(1)
    tq = q_ref.shape[1]
    blk = WINDOW
    t0 = i * tq
    kext = jnp.concatenate([kp_ref[0], kc_ref[0], kn_ref[0]], axis=0)
    vext = jnp.concatenate([vp_ref[0], vc_ref[0], vn_ref[0]], axis=0)
    nrow = ATTN_STACK * blk
    blk_shift = blk.bit_length() - 1
    qi = lax.broadcasted_iota(jnp.int32, (nrow, 3 * blk), 0) & (blk - 1)
    kj = lax.broadcasted_iota(jnp.int32, (nrow, 3 * blk), 1)
    band = jnp.where(jnp.abs(kj - blk - qi) <= WINDOW, 0.0, NEG)
    kj_row = lax.broadcasted_iota(jnp.int32, (1, 3 * blk), 1)
    rowg = lax.broadcasted_iota(jnp.int32, (nrow, 1), 0) >> blk_shift
    sinks = []
    for h0 in range(0, N_Q_HEADS, ATTN_STACK):
        sk = jnp.full((nrow, 1), sink_ref[h0], F32)
        for g in range(1, ATTN_STACK):
            sk = jnp.where(rowg == g, sink_ref[h0 + g], sk)
        sinks.append(sk * LOG2E)
    for j in range(tq // blk):
        kpos = t0 + j * blk - blk + kj_row
        bias = band + jnp.where((kpos >= 0) & (kpos < seq), 0.0, NEG)
        for h0 in range(0, N_Q_HEADS, ATTN_STACK):
            kv = h0 // Q_PER_KV
            hs = slice(kv * HEAD_DIM, (kv + 1) * HEAD_DIM)
            kblk = kext[j * blk:j * blk + 3 * blk, hs]
            vblk = vext[j * blk:j * blk + 3 * blk, hs]
            qs = [q_ref[0, j * blk:(j + 1) * blk, (h0 + g) * HEAD_DIM:(h0 + g + 1) * HEAD_DIM]
                  for g in range(ATTN_STACK)]
            qblk = jnp.concatenate(qs, axis=0)
            s = lax.dot_general(qblk, kblk, (((1,), (1,)), ((), ())),
                                preferred_element_type=F32) + bias
            sk = sinks[h0 // ATTN_STACK]
            m = jnp.maximum(jnp.max(s, axis=-1, keepdims=True), sk)
            p = jnp.exp2(s - m)
            denom = jnp.sum(p, axis=-1, keepdims=True) + jnp.exp2(sk - m)
            o = jnp.dot(p.astype(BF16), vblk, preferred_element_type=F32) / denom
            for g in range(ATTN_STACK):
                c = (h0 + g) * HEAD_DIM
                o_ref[0, j * blk:(j + 1) * blk, c:c + HEAD_DIM] = (
                    o[g * blk:(g + 1) * blk].astype(BF16))


def _attention(q3, k3, v3, sink):
    b, s, aw = q3.shape
    kw = k3.shape[2]
    tq = TQ_ATTN
    blk = WINDOW
    r = tq // blk
    nblk = s // blk
    cur = lambda bi, i: (bi, i, 0)
    prev = lambda bi, i: (bi, jnp.maximum(i * r - 1, 0), 0)
    nxt = lambda bi, i: (bi, jnp.minimum((i + 1) * r, nblk - 1), 0)
    kv_specs = [pl.BlockSpec((1, blk, kw), prev), pl.BlockSpec((1, tq, kw), cur),
                pl.BlockSpec((1, blk, kw), nxt)]
    return pl.pallas_call(
        functools.partial(_attn_kernel, seq=s),
        grid=(b, s // tq),
        in_specs=[pl.BlockSpec(memory_space=pltpu.SMEM),
                  pl.BlockSpec((1, tq, aw), cur)] + kv_specs + kv_specs,
        out_specs=pl.BlockSpec((1, tq, aw), cur),
        out_shape=jax.ShapeDtypeStruct((b, s, aw), BF16),
        compiler_params=_cparams(2),
        name="attention",
    )(sink, q3, k3, k3, k3, v3, v3, v3)


def _rnn_kernel(xr_ref, yr_ref, wc_ref, bc_ref, wg_ref, bg_ref, lam_ref, out_ref,
                xpad_ref, af_ref, uf_ref, ab_ref, ub_ref, hb_ref, cf_ref, cb_ref, *, seq):
    nseg = seq // SEG
    nlane = nseg
    ngrp = nlane // SUBLANES
    hf_ref = xpad_ref
    wc = wc_ref[...]
    bc = bc_ref[0]
    wg = wg_ref[0]
    bg = 0.5 * bg_ref[0]
    lam = lam_ref[0]
    rate = (0.5 * LRU_C) * _softplus(-lam)
    a_refs = (af_ref, ab_ref)
    u_refs = (uf_ref, ub_ref)
    left = CONV_W // 2

    halo = jnp.zeros((SUBLANES, LANES), F32)
    xpad_ref[pl.ds(0, SUBLANES), :] = halo
    xpad_ref[pl.ds(seq + SUBLANES, SUBLANES), :] = halo

    def pad_copy(c, carry):
        t0 = pl.multiple_of(c * SEG, SEG)
        xpad_ref[pl.ds(t0 + SUBLANES, SEG), :] = xr_ref[0, pl.ds(t0, SEG), :]
        return carry

    lax.fori_loop(0, nseg, pad_copy, 0, unroll=4)

    tail = nlane * SCAN_LEN - seq
    for a_ref, u_ref in zip(a_refs, u_refs):
        a_ref[pl.ds(seq, tail), :] = jnp.ones((tail, LANES), F32)
        u_ref[pl.ds(seq, tail), :] = jnp.zeros((tail, LANES), F32)

    def gates(c, carry):
        t0 = pl.multiple_of(c * SEG, SEG)
        xc = bc
        for tap in range(CONV_W):
            xc = xc + xpad_ref[pl.ds(t0 + SUBLANES - left + tap, SEG), :] * wc[tap:tap + 1]
        gh = jnp.dot(xc.astype(BF16), wg, preferred_element_type=F32) + bg
        xch = 0.5 * xc
        for d in range(2):
            rt = rate[:, d * LANES:(d + 1) * LANES]
            nlog_a = rt * jnp.tanh(gh[:, (2 * d) * LANES:(2 * d + 1) * LANES]) + rt
            a = jnp.exp2(nlog_a * (-1.0 / math.log(2.0)))
            z = jnp.tanh(nlog_a) * (a * a + 1.0)
            mult = jnp.where(z > 0.0, z * lax.rsqrt(z), 0.0)
            in_gate2 = jnp.tanh(gh[:, (2 * d + 1) * LANES:(2 * d + 2) * LANES]) + 1.0
            a_refs[d][pl.ds(t0, SEG), :] = a
            u_refs[d][pl.ds(t0, SEG), :] = (xch * mult) * in_gate2
        return carry

    lax.fori_loop(0, nseg, gates, 0, unroll=8)

    def lane_rows(g, j):
        return pl.ds(g * SUBLANES * SCAN_LEN + j, SUBLANES, stride=SCAN_LEN)

    def totals_step(j, carry):
        hf, pf, hb, pb = carry
        jb = SCAN_LEN - 1 - j
        nhf, npf, nhb, npb = [], [], [], []
        for g in range(ngrp):
            a = af_ref[lane_rows(g, j), :]
            nhf.append(a * hf[g] + uf_ref[lane_rows(g, j), :])
            npf.append(a * pf[g])
            a = ab_ref[lane_rows(g, jb), :]
            nhb.append(a * hb[g] + ub_ref[lane_rows(g, jb), :])
            npb.append(a * pb[g])
        return tuple(nhf), tuple(npf), tuple(nhb), tuple(npb)

    zero = tuple(jnp.zeros((SUBLANES, LANES), F32) for _ in range(ngrp))
    one = tuple(jnp.ones((SUBLANES, LANES), F32) for _ in range(ngrp))
    hf, pf, hb, pb = lax.fori_loop(0, SCAN_LEN, totals_step, (zero, one, zero, one),
                                   unroll=SCAN_UNROLL)

    c = jnp.zeros((1, LANES), F32)
    for s in range(nlane):
        g, r = divmod(s, SUBLANES)
        cf_ref[s:s + 1, :] = c
        c = pf[g][r:r + 1] * c + hf[g][r:r + 1]
    c = jnp.zeros((1, LANES), F32)
    for s in range(nlane - 1, -1, -1):
        g, r = divmod(s, SUBLANES)
        cb_ref[s:s + 1, :] = c
        c = pb[g][r:r + 1] * c + hb[g][r:r + 1]

    def scan_step(j, carry):
        hf, hb = carry
        jb = SCAN_LEN - 1 - j
        nhf, nhb = [], []
        for g in range(ngrp):
            h = af_ref[lane_rows(g, j), :] * hf[g] + uf_ref[lane_rows(g, j), :]
            hf_ref[lane_rows(g, j), :] = h
            nhf.append(h)
            h = ab_ref[lane_rows(g, jb), :] * hb[g] + ub_ref[lane_rows(g, jb), :]
            hb_ref[lane_rows(g, jb), :] = h
            nhb.append(h)
        return tuple(nhf), tuple(nhb)

    hf0 = tuple(cf_ref[g * SUBLANES:(g + 1) * SUBLANES, :] for g in range(ngrp))
    hb0 = tuple(cb_ref[g * SUBLANES:(g + 1) * SUBLANES, :] for g in range(ngrp))
    lax.fori_loop(0, SCAN_LEN, scan_step, (hf0, hb0), unroll=SCAN_UNROLL)

    def finish(c, carry):
        t0 = pl.multiple_of(c * SEG, SEG)
        h = hf_ref[pl.ds(t0, SEG), :] + hb_ref[pl.ds(t0, SEG), :]
        out_ref[0, pl.ds(t0, SEG), :] = (h * _gelu_tanh(yr_ref[0, pl.ds(t0, SEG), :])).astype(BF16)
        return carry

    lax.fori_loop(0, nseg, finish, 0, unroll=4)


def _rnn(xr3, yr3, w_conv, b_conv, wg_cat, bg_cat, lam_cat):
    b, s, d = xr3.shape
    nblk = d // LANES
    nseg = s // SEG
    slab = lambda bi, n: (bi, 0, n)
    per_blk = lambda bi, n: (n, 0, 0)
    assert nseg % SUBLANES == 0 and nseg * SCAN_LEN >= s + 2 * SUBLANES
    scratch = ([pltpu.VMEM((nseg * SCAN_LEN, LANES), F32)] * 6
               + [pltpu.VMEM((nseg, LANES), F32)] * 2)
    return pl.pallas_call(
        functools.partial(_rnn_kernel, seq=s),
        grid=(b, nblk),
        in_specs=[pl.BlockSpec((1, s, LANES), slab),
                  pl.BlockSpec((1, s, LANES), slab),
                  pl.BlockSpec((CONV_W, LANES), lambda bi, n: (0, n)),
                  pl.BlockSpec((1, 1, LANES), per_blk),
                  pl.BlockSpec((1, LANES, 4 * LANES), per_blk),
                  pl.BlockSpec((1, 1, 4 * LANES), per_blk),
                  pl.BlockSpec((1, 1, 2 * LANES), per_blk)],
        out_specs=pl.BlockSpec((1, s, LANES), slab),
        out_shape=jax.ShapeDtypeStruct((b, s, d), BF16),
        scratch_shapes=scratch,
        compiler_params=_cparams(2),
        name="rglru",
    )(xr3, yr3, w_conv, b_conv, wg_cat, bg_cat, lam_cat)


def _route(logits):
    lane = lax.broadcasted_iota(jnp.int32, logits.shape, 1)
    lanef = lane.astype(F32)
    big = float(4 * LANES)
    gmask = (lane >= N_EXPERTS) & (lane < N_EXPERTS + N_GROUPS)
    gl = jnp.where(gmask, logits, NEG)
    gmax = jnp.max(gl, axis=-1, keepdims=True)
    ge = jnp.exp(gl - gmax)
    gprob = ge / jnp.sum(ge, axis=-1, keepdims=True)
    gval = jnp.max(gprob, axis=-1, keepdims=True)
    gidx = jnp.min(jnp.where((gprob == gval) & gmask, lanef, big), axis=-1, keepdims=True)
    gidx = gidx.astype(jnp.int32) - N_EXPERTS
    group_shift = EXPERTS_PER_GROUP.bit_length() - 1
    emask = (lane < N_EXPERTS) & ((lane >> group_shift) == gidx)
    el = jnp.where(emask, logits, NEG)
    m1 = jnp.max(el, axis=-1, keepdims=True)
    i1 = jnp.min(jnp.where((el == m1) & emask, lanef, big), axis=-1, keepdims=True)
    emask2 = emask & (lanef != i1)
    el2 = jnp.where(emask2, logits, NEG)
    m2 = jnp.max(el2, axis=-1, keepdims=True)
    i2 = jnp.min(jnp.where((el2 == m2) & emask2, lanef, big), axis=-1, keepdims=True)
    e2 = jnp.exp(m2 - m1)
    den = 1.0 + e2
    g1 = (1.0 / den) * gval
    g2 = (e2 / den) * gval
    route = jnp.where(lane == 0, i1, jnp.where(lane == 1, i2,
                      jnp.where(lane == 2, g1, jnp.where(lane == 3, g2, 0.0))))
    onehot = (lanef == i1).astype(F32) + (lanef == i2).astype(F32)
    return route, jnp.sum(onehot, axis=0, keepdims=True)


def _mixout_kernel(x_ref, o_ref, hy_ref, wmg_ref, wao_ref, wro_ref, wout_ref,
                   lng_ref, lnb_ref, wr_ref, br_ref, x1_ref, x1b_ref, route_ref, cnt_ref,
                   logits_ref, *, alpha):
    @pl.when(pl.program_id(0) == 0)
    def _():
        logits_ref[...] = jnp.zeros_like(logits_ref)

    route, cnt = _route(logits_ref[...])
    route_ref[...] = route
    cnt_ref[0] = cnt

    d = x_ref.shape[1]
    for r0 in range(0, x_ref.shape[0], MIX_ROWS):
        rows = pl.ds(r0, MIX_ROWS)
        x = x_ref[rows, :]
        xb = x.astype(BF16)
        ga = jnp.dot(xb, wmg_ref[:, :d], preferred_element_type=F32)
        gr = jnp.dot(xb, wmg_ref[:, d:], preferred_element_type=F32)
        ya = jnp.dot(o_ref[rows, :], wao_ref[...], preferred_element_type=F32)
        yr = jnp.dot(hy_ref[rows, :], wro_ref[...], preferred_element_type=F32)
        merged = _sigmoid(ga) * ya + _sigmoid(gr) * yr
        mix = jnp.dot(merged.astype(BF16), wout_ref[...], preferred_element_type=F32)
        x1 = _layer_norm(alpha * x + mix, lng_ref[...], lnb_ref[...])
        x1_ref[rows, :] = x1
        x1b = x1.astype(BF16)
        x1b_ref[rows, :] = x1b
        logits_ref[rows, :] = (jnp.dot(x1b, wr_ref[...], preferred_element_type=F32)
                               + br_ref[...])


def _mixout(x2, o2, hy2, wmg, wao, wro, wout, lng, lnb, wr, br, alpha):
    t, d = x2.shape
    tm = TD
    nt = t // tm
    row = lambda i: (jnp.minimum(i, nt - 1), 0)
    prev_row = lambda i: (jnp.maximum(i - 1, 0), 0)
    full = lambda i: (0, 0)
    return pl.pallas_call(
        functools.partial(_mixout_kernel, alpha=alpha),
        grid=(nt + 1,),
        in_specs=[pl.BlockSpec((tm, d), row)] * 3
                 + [pl.BlockSpec((d, 2 * d), full)]
                 + [pl.BlockSpec((d, d), full)] * 3
                 + [pl.BlockSpec((1, d), full)] * 2
                 + [pl.BlockSpec((d, LANES), full), pl.BlockSpec((1, LANES), full)],
        out_specs=[pl.BlockSpec((tm, d), row), pl.BlockSpec((tm, d), row),
                   pl.BlockSpec((tm, LANES), prev_row),
                   pl.BlockSpec((1, 1, LANES), lambda i: (jnp.maximum(i - 1, 0), 0, 0))],
        out_shape=[jax.ShapeDtypeStruct((t, d), F32), jax.ShapeDtypeStruct((t, d), BF16),
                   jax.ShapeDtypeStruct((t, LANES), F32),
                   jax.ShapeDtypeStruct((nt, 1, LANES), F32)],
        scratch_shapes=[pltpu.VMEM((tm, LANES), F32)],
        compiler_params=_cparams(1),
        name="mixout",
    )(x2, o2, hy2, wmg, wao, wro, wout, lng, lnb, wr, br)


def _perm_matrix(route, soff_row, w1, w2):
    td = route.shape[0]
    lane = lax.broadcasted_iota(jnp.int32, (td, LANES), 1).astype(F32)
    e1 = lane == route[:, 0:1]
    e2 = lane == route[:, 1:2]
    cnt = (e1.astype(F32) + e2.astype(F32)).astype(BF16)
    ti = lax.broadcasted_iota(jnp.int32, (td, td), 0)
    tj = lax.broadcasted_iota(jnp.int32, (td, td), 1)
    lower = (tj < ti).astype(BF16)
    pos = jnp.dot(lower, cnt, preferred_element_type=F32) + soff_row
    r1 = jnp.sum(jnp.where(e1, pos, 0.0), axis=-1, keepdims=True).astype(jnp.int32)
    r2 = jnp.sum(jnp.where(e2, pos, 0.0), axis=-1, keepdims=True).astype(jnp.int32)
    col = lax.broadcasted_iota(jnp.int32, (td, STAGE_ROWS), 1)
    return jnp.where(col == r1, w1, 0.0) + jnp.where(col == r2, w2, 0.0)


def _chunk_rows(c):
    return pl.ds(pl.multiple_of(c * CHUNK, CHUNK), CHUNK)


def _block_rows(b):
    return pl.ds(pl.multiple_of(b * MOE_BLOCK, MOE_BLOCK), MOE_BLOCK)


def _for_each_chunk(i, cch_ref, soff_ref, dch_ref, fn):
    def per_expert(e, carry):
        k = i * N_EXPERTS + e
        so = soff_ref[k]
        do = dch_ref[k]

        def per_chunk(c, carry2):
            fn(so + c, do + c)
            return carry2

        return lax.fori_loop(0, cch_ref[k], per_chunk, carry)

    lax.fori_loop(0, N_EXPERTS, per_expert, 0)


def _repeat(n, fn):
    def body(c, carry):
        fn()
        return carry

    lax.fori_loop(0, n, body, 0)


def _drain_chunks(n, make_copy):
    _repeat(n // WAIT_BATCH, lambda: make_copy(WAIT_BATCH).wait())
    _repeat(n % WAIT_BATCH, lambda: make_copy(1).wait())


def _dispatch_kernel(cch_ref, soff_ref, dch_ref, tot_ref, padst_ref, padn_ref, nb_ref,
                     x_ref, route_ref, soffrow_ref, xb_ref, stage_ref, zero_ref, sems, zsem):
    i = pl.program_id(0)
    nt = pl.num_programs(0)
    n_blocks = xb_ref.shape[0] // MOE_BLOCK
    slot = i % 2

    def copy(s, src_chunk, dst_chunk):
        return pltpu.make_async_copy(stage_ref.at[s, _chunk_rows(src_chunk)],
                                     xb_ref.at[_chunk_rows(dst_chunk)], sems.at[s])

    def many(s):
        return lambda k: pltpu.make_async_copy(stage_ref.at[s, pl.ds(0, k * CHUNK)],
                                               xb_ref.at[pl.ds(0, k * CHUNK)], sems.at[s])

    @pl.when(i >= 2)
    def _():
        _drain_chunks(tot_ref[jnp.maximum(i - 2, 0)], many(slot))

    pt = _perm_matrix(route_ref[...], soffrow_ref[0], 1.0, 1.0).astype(BF16)
    stage_ref[slot] = lax.dot_general(pt, x_ref[...], (((0,), (0,)), ((), ())),
                                      preferred_element_type=F32).astype(BF16)
    _for_each_chunk(i, cch_ref, soff_ref, dch_ref,
                    lambda s, d: copy(slot, s, d).start())

    @pl.when(i == nt - 1)
    def _():
        @pl.when(i >= 1)
        def _():
            _drain_chunks(tot_ref[jnp.maximum(i - 1, 0)], many(1 - slot))

        _drain_chunks(tot_ref[i], many(slot))

        zero_ref[...] = jnp.zeros_like(zero_ref)

        def zcopy(dst_chunk):
            return pltpu.make_async_copy(zero_ref.at[pl.ds(0, CHUNK)],
                                         xb_ref.at[_chunk_rows(dst_chunk)], zsem)

        def pad_start(e, n):
            def pad_chunk(c, carry2):
                zcopy(padst_ref[e] + c).start()
                return carry2

            lax.fori_loop(0, padn_ref[e], pad_chunk, 0)
            return n + padn_ref[e]

        npad = lax.fori_loop(0, N_EXPERTS, pad_start, 0)
        _repeat(npad, lambda: zcopy(0).wait())

        def zblock(b):
            return pltpu.make_async_copy(zero_ref, xb_ref.at[_block_rows(b)], zsem)

        def start_block(b, carry):
            zblock(b).start()
            return carry

        def wait_block(b, carry):
            zblock(b).wait()
            return carry

        lax.fori_loop(nb_ref[0], n_blocks, start_block, 0)
        lax.fori_loop(nb_ref[0], n_blocks, wait_block, 0)


def _dispatch(x1, route, soffrow, tables, n_rows):
    t, d = x1.shape
    nt = t // TD
    grid_spec = pltpu.PrefetchScalarGridSpec(
        num_scalar_prefetch=7,
        grid=(nt,),
        in_specs=[pl.BlockSpec((TD, d), lambda i, *_: (i, 0)),
                  pl.BlockSpec((TD, LANES), lambda i, *_: (i, 0)),
                  pl.BlockSpec((1, 1, LANES), lambda i, *_: (i, 0, 0))],
        out_specs=pl.BlockSpec(memory_space=pl.ANY),
        scratch_shapes=[pltpu.VMEM((2, STAGE_ROWS, d), BF16), pltpu.VMEM((MOE_BLOCK, d), BF16),
                        pltpu.SemaphoreType.DMA((2,)), pltpu.SemaphoreType.DMA(())],
    )
    return pl.pallas_call(
        _dispatch_kernel,
        grid_spec=grid_spec,
        out_shape=jax.ShapeDtypeStruct((n_rows, d), BF16),
        compiler_params=_cparams(1),
        name="dispatch",
    )(*tables, x1, route, soffrow)


def _experts_kernel(pst_ref, ntile_ref, nb_ref, xb_ref, wg_ref, wu_ref, wd_ref, yb_ref,
                    wgb_ref, wub_ref, wdb_ref, xbuf_ref, ybuf_ref, in_sems, out_sems, zsem):
    e = pl.program_id(0)
    ne = pl.num_programs(0)
    n = ntile_ref[e]
    row0 = pst_ref[e]
    n_blocks = yb_ref.shape[0] // MOE_BLOCK
    ahead = EXP_BUFS - 1

    def tile_rows(r0, t):
        return pl.ds(pl.multiple_of(r0 + t * EXP_TILE, MOE_BLOCK), EXP_TILE)

    def in_copy(r0, t, s):
        return pltpu.make_async_copy(xb_ref.at[tile_rows(r0, t)], xbuf_ref.at[s], in_sems.at[s])

    def out_copy(t, s):
        return pltpu.make_async_copy(ybuf_ref.at[s], yb_ref.at[tile_rows(row0, t)],
                                     out_sems.at[s])

    def start_head(r0, count):
        for k in range(ahead):
            @pl.when(k < count)
            def _():
                in_copy(r0, k, k).start()

    @pl.when(e == 0)
    def _():
        start_head(row0, n)

    @pl.when(n > 0)
    def _():
        wgb_ref[...] = wg_ref[0, 0].astype(BF16)
        wub_ref[...] = wu_ref[0, 0].astype(BF16)
        wdb_ref[...] = wd_ref[0, 0].astype(BF16)

    def tile(t, carry):
        s = t % EXP_BUFS

        @pl.when(t + ahead < n)
        def _():
            in_copy(row0, t + ahead, (t + ahead) % EXP_BUFS).start()

        in_copy(row0, t, s).wait()

        @pl.when(t >= EXP_BUFS)
        def _():
            out_copy(t - EXP_BUFS, s).wait()

        x = xbuf_ref[s]
        hg = jnp.dot(x, wgb_ref[...], preferred_element_type=F32)
        hu = jnp.dot(x, wub_ref[...], preferred_element_type=F32)
        hid = (hg * _sigmoid(hg)) * hu
        ybuf_ref[s] = jnp.dot(hid.astype(BF16), wdb_ref[...],
                              preferred_element_type=F32).astype(BF16)
        out_copy(t, s).start()
        return carry

    lax.fori_loop(0, n, tile, 0)

    @pl.when(e + 1 < ne)
    def _():
        nxt = jnp.minimum(e + 1, ne - 1)
        start_head(pst_ref[nxt], ntile_ref[nxt])

    for k in range(EXP_BUFS):
        @pl.when(n > k)
        def _():
            out_copy(n - 1 - k, (n - 1 - k) % EXP_BUFS).wait()

    @pl.when(e == ne - 1)
    def _():
        ybuf_ref[0] = jnp.zeros(ybuf_ref.shape[1:], BF16)

        def zblock(b):
            return pltpu.make_async_copy(ybuf_ref.at[0, pl.ds(0, MOE_BLOCK)],
                                         yb_ref.at[_block_rows(b)], zsem)

        def start_block(b, carry):
            zblock(b).start()
            return carry

        def wait_block(b, carry):
            zblock(b).wait()
            return carry

        lax.fori_loop(nb_ref[0], n_blocks, start_block, 0)
        lax.fori_loop(nb_ref[0], n_blocks, wait_block, 0)


def _experts(xb, pst, ntile, nb_used, w_g, w_u, w_d, layer):
    n_rows, d = xb.shape
    n_exp, de = w_g.shape[1], w_g.shape[3]
    wsel = lambda e, *_: (layer, e, 0, 0)
    grid_spec = pltpu.PrefetchScalarGridSpec(
        num_scalar_prefetch=3,
        grid=(n_exp,),
        in_specs=[pl.BlockSpec(memory_space=pl.ANY),
                  pl.BlockSpec((1, 1, d, de), wsel),
                  pl.BlockSpec((1, 1, d, de), wsel),
                  pl.BlockSpec((1, 1, de, d), wsel)],
        out_specs=pl.BlockSpec(memory_space=pl.ANY),
        scratch_shapes=[pltpu.VMEM((d, de), BF16), pltpu.VMEM((d, de), BF16),
                        pltpu.VMEM((de, d), BF16),
                        pltpu.VMEM((EXP_BUFS, EXP_TILE, d), BF16),
                        pltpu.VMEM((EXP_BUFS, EXP_TILE, d), BF16),
                        pltpu.SemaphoreType.DMA((EXP_BUFS,)), pltpu.SemaphoreType.DMA((EXP_BUFS,)),
                        pltpu.SemaphoreType.DMA(())],
    )
    return pl.pallas_call(
        _experts_kernel,
        grid_spec=grid_spec,
        out_shape=jax.ShapeDtypeStruct((n_rows, d), BF16),
        compiler_params=_cparams(1),
        name="experts",
    )(pst, ntile, nb_used, xb, w_g, w_u, w_d)


def _combine_kernel(cch_ref, soff_ref, dch_ref, tot_ref,
                    x_ref, route_ref, soffrow_ref, yb_ref, lng_ref, lnb_ref, *rest,
                    alpha, project):
    if project:
        (w_ref, cos_ref, sin_ref, out_ref, q_ref, k_ref, v_ref, xr_ref, yr_ref,
         stage_ref, sems) = rest
    else:
        out_ref, stage_ref, sems = rest
    i = pl.program_id(0)
    nt = pl.num_programs(0)
    slot = i % 2

    def copy(s, src_chunk, dst_chunk):
        return pltpu.make_async_copy(yb_ref.at[_chunk_rows(src_chunk)],
                                     stage_ref.at[s, _chunk_rows(dst_chunk)], sems.at[s])

    def fetch(tile_idx, s):
        _for_each_chunk(tile_idx, cch_ref, soff_ref, dch_ref,
                        lambda so, do: copy(s, do, so).start())

    @pl.when(i == 0)
    def _():
        stage_ref[...] = jnp.zeros_like(stage_ref)
        fetch(i, slot)

    @pl.when(i + 1 < nt)
    def _():
        fetch(i + 1, 1 - slot)

    _drain_chunks(tot_ref[i], lambda k: pltpu.make_async_copy(
        yb_ref.at[pl.ds(0, k * CHUNK)], stage_ref.at[slot, pl.ds(0, k * CHUNK)], sems.at[slot]))

    route = route_ref[...]
    pt = _perm_matrix(route, soffrow_ref[0], route[:, 2:3], route[:, 3:4]).astype(BF16)
    ffn = jnp.dot(pt, stage_ref[slot], preferred_element_type=F32)
    out = _layer_norm(alpha * x_ref[...] + ffn, lng_ref[...], lnb_ref[...])
    out_ref[...] = out
    if project:
        _project_tile(out.astype(BF16), w_ref, cos_ref[...], sin_ref[...],
                      q_ref, k_ref, v_ref, xr_ref, yr_ref)


def _combine(x1, route, soffrow, tables, yb, lng, lnb, alpha, next_proj=None):
    t, d = x1.shape
    nt = t // TD
    row = lambda i, *_: (i, 0)
    full = lambda i, *_: (0, 0)
    in_specs = [pl.BlockSpec((TD, d), row),
                pl.BlockSpec((TD, LANES), row),
                pl.BlockSpec((1, 1, LANES), lambda i, *_: (i, 0, 0)),
                pl.BlockSpec(memory_space=pl.ANY),
                pl.BlockSpec((1, d), full),
                pl.BlockSpec((1, d), full)]
    out_shape = [jax.ShapeDtypeStruct((t, d), F32)]
    out_specs = [pl.BlockSpec((TD, d), row)]
    args = [x1, route, soffrow, yb, lng, lnb]
    if next_proj is not None:
        w_in_b, cosf, sinf, seq = next_proj
        tiles_per_seq = seq // TD
        pos = lambda i, *_: (i % tiles_per_seq, 0)
        in_specs += [pl.BlockSpec(w_in_b.shape, full), pl.BlockSpec((TD, HEAD_DIM), pos),
                     pl.BlockSpec((TD, HEAD_DIM), pos)]
        shapes, specs = _proj_out_specs(t, d, TD, row)
        out_shape += shapes
        out_specs += specs
        args += [w_in_b, cosf, sinf]
    grid_spec = pltpu.PrefetchScalarGridSpec(
        num_scalar_prefetch=4,
        grid=(nt,),
        in_specs=in_specs,
        out_specs=out_specs,
        scratch_shapes=[pltpu.VMEM((2, STAGE_ROWS, d), BF16), pltpu.SemaphoreType.DMA((2,))],
    )
    return pl.pallas_call(
        functools.partial(_combine_kernel, alpha=alpha, project=next_proj is not None),
        grid_spec=grid_spec,
        out_shape=out_shape,
        compiler_params=_cparams(1),
        name="combine",
    )(*tables, *args)


def _max_blocks(t):
    nt = t // TD
    rows = 2 * t + (CHUNK - 1) * nt * N_EXPERTS + (MOE_BLOCK - CHUNK) * N_EXPERTS
    return -(-rows // MOE_BLOCK)


def _dispatch_tables(cnt):
    n = cnt[:, 0, :N_EXPERTS].astype(jnp.int32)
    nt = n.shape[0]
    cch = (n + CHUNK - 1) // CHUNK
    excl = lambda k: (jnp.arange(k)[:, None] < jnp.arange(k)[None, :]).astype(F32)
    dot = functools.partial(jnp.dot, precision=lax.Precision.HIGHEST)
    cchf = cch.astype(F32)
    soff = dot(cchf, excl(N_EXPERTS)).astype(jnp.int32)
    tot_tile = jnp.sum(cch, axis=1)
    tot_e = jnp.sum(cch, axis=0)
    reg = (tot_e + CHUNKS_PER_BLOCK - 1) // CHUNKS_PER_BLOCK * CHUNKS_PER_BLOCK
    pstart = dot(reg.astype(F32)[None, :], excl(N_EXPERTS))[0].astype(jnp.int32)
    before = dot(excl(nt).T, cchf).astype(jnp.int32)
    dch = pstart[None, :] + before
    nb_used = jnp.sum(reg) // CHUNKS_PER_BLOCK
    soffrow = jnp.pad((soff * CHUNK).astype(F32), ((0, 0), (0, LANES - N_EXPERTS)))[:, None, :]
    i32 = lambda a: a.reshape(-1).astype(jnp.int32)
    return dict(cch=i32(cch), soff=i32(soff), dch=i32(dch), tot=i32(tot_tile),
                padst=i32(pstart + tot_e), padn=i32(reg - tot_e), nb_used=i32(nb_used),
                pst=i32(pstart * CHUNK), ntile=i32((reg * CHUNK + EXP_TILE - 1) // EXP_TILE),
                soffrow=soffrow)


def _rope_tables(seq):
    inv = ROPE_THETA ** (-jnp.arange(0, HEAD_DIM, 2, dtype=F32) / HEAD_DIM)
    ang = jnp.arange(seq, dtype=F32)[:, None] * inv[None, :]
    cos, sin = jnp.cos(ang), jnp.sin(ang)
    return jnp.concatenate([cos, cos], axis=1), jnp.concatenate([-sin, sin], axis=1)


@jax.jit
def kernel(x, w_in, w_sink, w_conv, b_conv, w_rec_gate, b_rec_gate, w_in_gate, b_in_gate,
           lru_lambda, w_attn_o, w_rnn_o, w_out, ln_g, ln_b, w_router_group, b_router_group,
           w_router_expert, b_router_expert, w_exp_gate, w_exp_up, w_exp_down):
    bsz, seq, d = x.shape
    depth = w_in.shape[0]
    t = bsz * seq
    nblk = d // LANES
    alpha = (2 * depth) ** 0.25
    cosf, sinf = _rope_tables(seq)
    n_rows = _max_blocks(t) * MOE_BLOCK + (EXP_TILE - MOE_BLOCK)
    x2 = x.reshape(t, d)
    n_branch = w_in.shape[2] - 2 * d
    w_branch = [w_in[l, :, :n_branch].astype(BF16) for l in range(depth)]
    q, k, v, xr, yr = _inproj(x2, w_branch[0], cosf, sinf, seq)
    for l in range(depth):
        o = _attention(q.reshape(bsz, seq, -1), k.reshape(bsz, seq, -1), v.reshape(bsz, seq, -1),
                       w_sink[l])
        wg_cat = (0.5 * jnp.concatenate([w_rec_gate[l, 0], w_in_gate[l, 0], w_rec_gate[l, 1],
                                         w_in_gate[l, 1]], axis=-1)).astype(BF16)
        bg_cat = jnp.concatenate([b_rec_gate[l, 0].reshape(nblk, 1, LANES),
                                  b_in_gate[l, 0].reshape(nblk, 1, LANES),
                                  b_rec_gate[l, 1].reshape(nblk, 1, LANES),
                                  b_in_gate[l, 1].reshape(nblk, 1, LANES)], axis=-1)
        lam_cat = jnp.concatenate([lru_lambda[l, 0].reshape(nblk, 1, LANES),
                                   lru_lambda[l, 1].reshape(nblk, 1, LANES)], axis=-1)
        hy = _rnn(xr.reshape(bsz, seq, d), yr.reshape(bsz, seq, d), w_conv[l],
                  b_conv[l].reshape(nblk, 1, LANES), wg_cat, bg_cat, lam_cat)
        lane_pad = LANES - N_EXPERTS - N_GROUPS
        wr = jnp.concatenate([w_router_expert[l], w_router_group[l],
                              jnp.zeros((d, lane_pad), F32)], axis=1).astype(BF16)
        br = jnp.concatenate([b_router_expert[l], b_router_group[l],
                              jnp.zeros((lane_pad,), F32)])[None, :]
        x1, x1b, route, cnt = _mixout(x2, o.reshape(t, -1), hy.reshape(t, d),
                                 w_in[l, :, n_branch:].astype(BF16),
                                 w_attn_o[l].astype(BF16), w_rnn_o[l].astype(BF16),
                                 w_out[l].astype(BF16), ln_g[l, 0].reshape(1, d),
                                 ln_b[l, 0].reshape(1, d), wr, br, alpha)
        tb = _dispatch_tables(cnt)
        xb = _dispatch(x1b, route, tb["soffrow"],
                       (tb["cch"], tb["soff"], tb["dch"], tb["tot"], tb["padst"], tb["padn"],
                        tb["nb_used"]), n_rows)
        yb = _experts(xb, tb["pst"], tb["ntile"], tb["nb_used"], w_exp_gate, w_exp_up,
                      w_exp_down, l)
        next_proj = (w_branch[l + 1], cosf, sinf, seq) if l + 1 < depth else None
        x2, *nxt = _combine(x1, route, tb["soffrow"],
                            (tb["cch"], tb["soff"], tb["dch"], tb["tot"]), yb,
                            ln_g[l, 1].reshape(1, d), ln_b[l, 1].reshape(1, d), alpha, next_proj)
        if nxt:
            q, k, v, xr, yr = nxt
    return x2.reshape(bsz, seq, d)
```

```python
import functools
import math

import jax
import jax.numpy as jnp
from jax import lax
from jax.experimental import pallas as pl
from jax.experimental.pallas import tpu as pltpu

F32 = jnp.float32
BF16 = jnp.bfloat16

HEAD_DIM = 128
N_Q_HEADS = 8
N_KV_HEADS = 2
Q_PER_KV = N_Q_HEADS // N_KV_HEADS
WINDOW = 128
ROPE_THETA = 10000.0
CONV_W = 4
LRU_C = 8.0
N_GROUPS = 4
EXPERTS_PER_GROUP = 8
N_EXPERTS = N_GROUPS * EXPERTS_PER_GROUP
LN_EPS = 1e-5
NEG = -1e30
LOG2E = math.log2(math.e)

SUBLANES = 8
LANES = 128

TM_PROJ = 512
TQ_ATTN = 1024
ATTN_STACK = Q_PER_KV
SEG = 256
SCAN_LEN = SEG + 4
SCAN_UNROLL = 10
BF16_ROWS = 2 * SUBLANES
TD = 512
MIX_ROWS = 256
CHUNK = BF16_ROWS
WAIT_BATCH = 8
COPY_RUN = 4
STAGE_ROWS = 2 * TD + CHUNK * N_EXPERTS
MOE_BLOCK = 128
CHUNKS_PER_BLOCK = MOE_BLOCK // CHUNK
EXP_TILE = 2 * MOE_BLOCK
EXP_BUFS = 6
VMEM_LIMIT = 56 * 1024 * 1024


def _cparams(n_axes):
    return pltpu.CompilerParams(dimension_semantics=("arbitrary",) * n_axes,
                                vmem_limit_bytes=VMEM_LIMIT)


def _softplus(z):
    e = jnp.exp(-jnp.abs(z))
    w = 1.0 + e
    tiny = w == 1.0
    log1p = jnp.where(tiny, e, jnp.log(w) * (e / jnp.where(tiny, 1.0, w - 1.0)))
    return jnp.maximum(z, 0.0) + log1p


def _sigmoid(x):
    return 0.5 * jnp.tanh(0.5 * x) + 0.5


def _gelu_tanh(y):
    c1 = math.sqrt(2.0 / math.pi)
    half = 0.5 * y
    return half + half * jnp.tanh(y * (c1 + (c1 * 0.044715) * (y * y)))


def _layer_norm(y, g, b):
    mu = jnp.mean(y, axis=-1, keepdims=True)
    d = y - mu
    var = jnp.mean(d * d, axis=-1, keepdims=True)
    return d * lax.rsqrt(var + LN_EPS) * g + b


def _project_tile(xb, w_ref, cos, sin, q_ref, k_ref, v_ref, xr_ref, yr_ref):
    aw = N_Q_HEADS * HEAD_DIM
    kw = N_KV_HEADS * HEAD_DIM
    d_model = xr_ref.shape[1]

    def proj(c0, n):
        return jnp.dot(xb, w_ref[:, c0:c0 + n], preferred_element_type=F32)

    def rope(t):
        return t * cos + pltpu.roll(t, HEAD_DIM // 2, 1) * sin

    zq = proj(0, aw)
    scale = HEAD_DIM ** -0.5 * LOG2E
    for h in range(N_Q_HEADS):
        sl = slice(h * HEAD_DIM, (h + 1) * HEAD_DIM)
        q_ref[:, sl] = (rope(zq[:, sl]) * scale).astype(BF16)
    zk = proj(aw, kw)
    for h in range(N_KV_HEADS):
        sl = slice(h * HEAD_DIM, (h + 1) * HEAD_DIM)
        k_ref[:, sl] = rope(zk[:, sl]).astype(BF16)
    v_ref[...] = proj(aw + kw, kw).astype(BF16)
    c0 = aw + 2 * kw
    xr_ref[...] = proj(c0, d_model)
    yr_ref[...] = proj(c0 + d_model, d_model)


def _inproj_kernel(x_ref, w_ref, cos_ref, sin_ref, q_ref, k_ref, v_ref, xr_ref, yr_ref):
    _project_tile(x_ref[...].astype(BF16), w_ref, cos_ref[...], sin_ref[...],
                  q_ref, k_ref, v_ref, xr_ref, yr_ref)


def _proj_out_specs(t, d, tm, row):
    aw = N_Q_HEADS * HEAD_DIM
    kw = N_KV_HEADS * HEAD_DIM
    shapes = [jax.ShapeDtypeStruct((t, aw), BF16), jax.ShapeDtypeStruct((t, kw), BF16),
              jax.ShapeDtypeStruct((t, kw), BF16)] + [jax.ShapeDtypeStruct((t, d), F32)] * 2
    specs = [pl.BlockSpec((tm, aw), row), pl.BlockSpec((tm, kw), row),
             pl.BlockSpec((tm, kw), row)] + [pl.BlockSpec((tm, d), row)] * 2
    return shapes, specs


def _inproj(x2, w_in_b, cosf, sinf, seq):
    t, d = x2.shape
    n_in = w_in_b.shape[1]
    tm = TM_PROJ
    tiles_per_seq = seq // tm
    row = lambda i: (i, 0)
    pos = lambda i: (i % tiles_per_seq, 0)
    shapes, specs = _proj_out_specs(t, d, tm, row)
    return pl.pallas_call(
        _inproj_kernel,
        grid=(t // tm,),
        in_specs=[pl.BlockSpec((tm, d), row),
                  pl.BlockSpec((d, n_in), lambda i: (0, 0)),
                  pl.BlockSpec((tm, HEAD_DIM), pos),
                  pl.BlockSpec((tm, HEAD_DIM), pos)],
        out_specs=specs,
        out_shape=shapes,
        compiler_params=_cparams(1),
        name="inproj",
    )(x2, w_in_b, cosf, sinf)


def _attn_kernel(sink_ref, q_ref, kp_ref, kc_ref, kn_ref, vp_ref, vc_ref, vn_ref, o_ref,
                 *, seq):
    i = pl.program_id(1)
    tq = q_ref.shape[1]
    blk = WINDOW
    t0 = i * tq
    kext = jnp.concatenate([kp_ref[0], kc_ref[0], kn_ref[0]], axis=0)
    vext = jnp.concatenate([vp_ref[0], vc_ref[0], vn_ref[0]], axis=0)
    nrow = ATTN_STACK * blk
    blk_shift = blk.bit_length() - 1
    qi = lax.broadcasted_iota(jnp.int32, (nrow, 3 * blk), 0) & (blk - 1)
    kj = lax.broadcasted_iota(jnp.int32, (nrow, 3 * blk), 1)
    band = jnp.where(jnp.abs(kj - blk - qi) <= WINDOW, 0.0, NEG)
    kj_row = lax.broadcasted_iota(jnp.int32, (1, 3 * blk), 1)
    rowg = lax.broadcasted_iota(jnp.int32, (nrow, 1), 0) >> blk_shift
    sinks = []
    for h0 in range(0, N_Q_HEADS, ATTN_STACK):
        sk = jnp.full((nrow, 1), sink_ref[h0], F32)
        for g in range(1, ATTN_STACK):
            sk = jnp.where(rowg == g, sink_ref[h0 + g], sk)
        sinks.append(sk * LOG2E)
    for j in range(tq // blk):
        kpos = t0 + j * blk - blk + kj_row
        bias = band + jnp.where((kpos >= 0) & (kpos < seq), 0.0, NEG)
        for h0 in range(0, N_Q_HEADS, ATTN_STACK):
            kv = h0 // Q_PER_KV
            hs = slice(kv * HEAD_DIM, (kv + 1) * HEAD_DIM)
            kblk = kext[j * blk:j * blk + 3 * blk, hs]
            vblk = vext[j * blk:j * blk + 3 * blk, hs]
            qs = [q_ref[0, j * blk:(j + 1) * blk, (h0 + g) * HEAD_DIM:(h0 + g + 1) * HEAD_DIM]
                  for g in range(ATTN_STACK)]
            qblk = jnp.concatenate(qs, axis=0)
            s = lax.dot_general(qblk, kblk, (((1,), (1,)), ((), ())),
                                preferred_element_type=F32) + bias
            sk = sinks[h0 // ATTN_STACK]
            m = jnp.maximum(jnp.max(s, axis=-1, keepdims=True), sk)
            p = jnp.exp2(s - m)
            denom = jnp.sum(p, axis=-1, keepdims=True) + jnp.exp2(sk - m)
            o = jnp.dot(p.astype(BF16), vblk, preferred_element_type=F32) / denom
            for g in range(ATTN_STACK):
                c = (h0 + g) * HEAD_DIM
                o_ref[0, j * blk:(j + 1) * blk, c:c + HEAD_DIM] = (
                    o[g * blk:(g + 1) * blk].astype(BF16))


def _attention(q3, k3, v3, sink):
    b, s, aw = q3.shape
    kw = k3.shape[2]
    tq = TQ_ATTN
    blk = WINDOW
    r = tq // blk
    nblk = s // blk
    cur = lambda bi, i: (bi, i, 0)
    prev = lambda bi, i: (bi, jnp.maximum(i * r - 1, 0), 0)
    nxt = lambda bi, i: (bi, jnp.minimum((i + 1) * r, nblk - 1), 0)
    kv_specs = [pl.BlockSpec((1, blk, kw), prev), pl.BlockSpec((1, tq, kw), cur),
                pl.BlockSpec((1, blk, kw), nxt)]
    return pl.pallas_call(
        functools.partial(_attn_kernel, seq=s),
        grid=(b, s // tq),
        in_specs=[pl.BlockSpec(memory_space=pltpu.SMEM),
                  pl.BlockSpec((1, tq, aw), cur)] + kv_specs + kv_specs,
        out_specs=pl.BlockSpec((1, tq, aw), cur),
        out_shape=jax.ShapeDtypeStruct((b, s, aw), BF16),
        compiler_params=_cparams(2),
        name="attention",
    )(sink, q3, k3, k3, k3, v3, v3, v3)


def _rnn_kernel(xr_ref, yr_ref, wc_ref, bc_ref, wg_ref, bg_ref, lam_ref, out_ref,
                xpad_ref, af_ref, uf_ref, ab_ref, ub_ref, hb_ref, cf_ref, cb_ref, *, seq):
    nseg = seq // SEG
    nlane = nseg
    ngrp = nlane // SUBLANES
    hf_ref = xpad_ref
    wc = wc_ref[...]
    bc = bc_ref[0]
    wg = wg_ref[0]
    bg = 0.5 * bg_ref[0]
    lam = lam_ref[0]
    rate = (0.5 * LRU_C) * _softplus(-lam)
    a_refs = (af_ref, ab_ref)
    u_refs = (uf_ref, ub_ref)
    left = CONV_W // 2

    halo = jnp.zeros((SUBLANES, LANES), F32)
    xpad_ref[pl.ds(0, SUBLANES), :] = halo
    xpad_ref[pl.ds(seq + SUBLANES, SUBLANES), :] = halo

    def pad_copy(c, carry):
        t0 = pl.multiple_of(c * SEG, SEG)
        xpad_ref[pl.ds(t0 + SUBLANES, SEG), :] = xr_ref[0, pl.ds(t0, SEG), :]
        return carry

    lax.fori_loop(0, nseg, pad_copy, 0, unroll=4)

    tail = nlane * SCAN_LEN - seq
    for a_ref, u_ref in zip(a_refs, u_refs):
        a_ref[pl.ds(seq, tail), :] = jnp.ones((tail, LANES), F32)
        u_ref[pl.ds(seq, tail), :] = jnp.zeros((tail, LANES), F32)

    def gates(c, carry):
        t0 = pl.multiple_of(c * SEG, SEG)
        xc = bc
        for tap in range(CONV_W):
            xc = xc + xpad_ref[pl.ds(t0 + SUBLANES - left + tap, SEG), :] * wc[tap:tap + 1]
        gh = jnp.dot(xc.astype(BF16), wg, preferred_element_type=F32) + bg
        xch = 0.5 * xc
        for d in range(2):
            rt = rate[:, d * LANES:(d + 1) * LANES]
            nlog_a = rt * jnp.tanh(gh[:, (2 * d) * LANES:(2 * d + 1) * LANES]) + rt
            a = jnp.exp2(nlog_a * (-1.0 / math.log(2.0)))
            z = jnp.tanh(nlog_a) * (a * a + 1.0)
            mult = jnp.where(z > 0.0, z * lax.rsqrt(z), 0.0)
            in_gate2 = jnp.tanh(gh[:, (2 * d + 1) * LANES:(2 * d + 2) * LANES]) + 1.0
            a_refs[d][pl.ds(t0, SEG), :] = a
            u_refs[d][pl.ds(t0, SEG), :] = (xch * mult) * in_gate2
        return carry

    lax.fori_loop(0, nseg, gates, 0, unroll=8)

    def lane_rows(g, j):
        return pl.ds(g * SUBLANES * SCAN_LEN + j, SUBLANES, stride=SCAN_LEN)

    def totals_step(j, carry):
        hf, pf, hb, pb = carry
        jb = SCAN_LEN - 1 - j
        nhf, npf, nhb, npb = [], [], [], []
        for g in range(ngrp):
            a = af_ref[lane_rows(g, j), :]
            nhf.append(a * hf[g] + uf_ref[lane_rows(g, j), :])
            npf.append(a * pf[g])
            a = ab_ref[lane_rows(g, jb), :]
            nhb.append(a * hb[g] + ub_ref[lane_rows(g, jb), :])
            npb.append(a * pb[g])
        return tuple(nhf), tuple(npf), tuple(nhb), tuple(npb)

    zero = tuple(jnp.zeros((SUBLANES, LANES), F32) for _ in range(ngrp))
    one = tuple(jnp.ones((SUBLANES, LANES), F32) for _ in range(ngrp))
    hf, pf, hb, pb = lax.fori_loop(0, SCAN_LEN, totals_step, (zero, one, zero, one),
                                   unroll=SCAN_UNROLL)

    c = jnp.zeros((1, LANES), F32)
    for s in range(nlane):
        g, r = divmod(s, SUBLANES)
        cf_ref[s:s + 1, :] = c
        c = pf[g][r:r + 1] * c + hf[g][r:r + 1]
    c = jnp.zeros((1, LANES), F32)
    for s in range(nlane - 1, -1, -1):
        g, r = divmod(s, SUBLANES)
        cb_ref[s:s + 1, :] = c
        c = pb[g][r:r + 1] * c + hb[g][r:r + 1]

    def scan_step(j, carry):
        hf, hb = carry
        jb = SCAN_LEN - 1 - j
        nhf, nhb = [], []
        for g in range(ngrp):
            h = af_ref[lane_rows(g, j), :] * hf[g] + uf_ref[lane_rows(g, j), :]
            hf_ref[lane_rows(g, j), :] = h
            nhf.append(h)
            h = ab_ref[lane_rows(g, jb), :] * hb[g] + ub_ref[lane_rows(g, jb), :]
            hb_ref[lane_rows(g, jb), :] = h
            nhb.append(h)
        return tuple(nhf), tuple(nhb)

    hf0 = tuple(cf_ref[g * SUBLANES:(g + 1) * SUBLANES, :] for g in range(ngrp))
    hb0 = tuple(cb_ref[g * SUBLANES:(g + 1) * SUBLANES, :] for g in range(ngrp))
    lax.fori_loop(0, SCAN_LEN, scan_step, (hf0, hb0), unroll=SCAN_UNROLL)

    def finish(c, carry):
        t0 = pl.multiple_of(c * SEG, SEG)
        h = hf_ref[pl.ds(t0, SEG), :] + hb_ref[pl.ds(t0, SEG), :]
        out_ref[0, pl.ds(t0, SEG), :] = (h * _gelu_tanh(yr_ref[0, pl.ds(t0, SEG), :])).astype(BF16)
        return carry

    lax.fori_loop(0, nseg, finish, 0, unroll=4)


def _rnn(xr3, yr3, w_conv, b_conv, wg_cat, bg_cat, lam_cat):
    b, s, d = xr3.shape
    nblk = d // LANES
    nseg = s // SEG
    slab = lambda bi, n: (bi, 0, n)
    per_blk = lambda bi, n: (n, 0, 0)
    assert nseg % SUBLANES == 0 and nseg * SCAN_LEN >= s + 2 * SUBLANES
    scratch = ([pltpu.VMEM((nseg * SCAN_LEN, LANES), F32)] * 6
               + [pltpu.VMEM((nseg, LANES), F32)] * 2)
    return pl.pallas_call(
        functools.partial(_rnn_kernel, seq=s),
        grid=(b, nblk),
        in_specs=[pl.BlockSpec((1, s, LANES), slab),
                  pl.BlockSpec((1, s, LANES), slab),
                  pl.BlockSpec((CONV_W, LANES), lambda bi, n: (0, n)),
                  pl.BlockSpec((1, 1, LANES), per_blk),
                  pl.BlockSpec((1, LANES, 4 * LANES), per_blk),
                  pl.BlockSpec((1, 1, 4 * LANES), per_blk),
                  pl.BlockSpec((1, 1, 2 * LANES), per_blk)],
        out_specs=pl.BlockSpec((1, s, LANES), slab),
        out_shape=jax.ShapeDtypeStruct((b, s, d), BF16),
        scratch_shapes=scratch,
        compiler_params=_cparams(2),
        name="rglru",
    )(xr3, yr3, w_conv, b_conv, wg_cat, bg_cat, lam_cat)


def _route(logits):
    lane = lax.broadcasted_iota(jnp.int32, logits.shape, 1)
    lanef = lane.astype(F32)
    big = float(4 * LANES)
    gmask = (lane >= N_EXPERTS) & (lane < N_EXPERTS + N_GROUPS)
    gl = jnp.where(gmask, logits, NEG)
    gmax = jnp.max(gl, axis=-1, keepdims=True)
    ge = jnp.exp(gl - gmax)
    gprob = ge / jnp.sum(ge, axis=-1, keepdims=True)
    gval = jnp.max(gprob, axis=-1, keepdims=True)
    gidx = jnp.min(jnp.where((gprob == gval) & gmask, lanef, big), axis=-1, keepdims=True)
    gidx = gidx.astype(jnp.int32) - N_EXPERTS
    group_shift = EXPERTS_PER_GROUP.bit_length() - 1
    emask = (lane < N_EXPERTS) & ((lane >> group_shift) == gidx)
    el = jnp.where(emask, logits, NEG)
    m1 = jnp.max(el, axis=-1, keepdims=True)
    i1 = jnp.min(jnp.where((el == m1) & emask, lanef, big), axis=-1, keepdims=True)
    emask2 = emask & (lanef != i1)
    el2 = jnp.where(emask2, logits, NEG)
    m2 = jnp.max(el2, axis=-1, keepdims=True)
    i2 = jnp.min(jnp.where((el2 == m2) & emask2, lanef, big), axis=-1, keepdims=True)
    e2 = jnp.exp(m2 - m1)
    den = 1.0 + e2
    g1 = (1.0 / den) * gval
    g2 = (e2 / den) * gval
    route = jnp.where(lane == 0, i1, jnp.where(lane == 1, i2,
                      jnp.where(lane == 2, g1, jnp.where(lane == 3, g2, 0.0))))
    onehot = (lanef == i1).astype(F32) + (lanef == i2).astype(F32)
    return route, jnp.sum(onehot, axis=0, keepdims=True)


def _mixout_kernel(x_ref, o_ref, hy_ref, wmg_ref, wao_ref, wro_ref, wout_ref,
                   lng_ref, lnb_ref, wr_ref, br_ref, x1_ref, x1b_ref, route_ref, cnt_ref,
                   logits_ref, *, alpha):
    @pl.when(pl.program_id(0) == 0)
    def _():
        logits_ref[...] = jnp.zeros_like(logits_ref)

    route, cnt = _route(logits_ref[...])
    route_ref[...] = route
    cnt_ref[0] = cnt

    d = x_ref.shape[1]
    for r0 in range(0, x_ref.shape[0], MIX_ROWS):
        rows = pl.ds(r0, MIX_ROWS)
        x = x_ref[rows, :]
        xb = x.astype(BF16)
        ga = jnp.dot(xb, wmg_ref[:, :d], preferred_element_type=F32)
        gr = jnp.dot(xb, wmg_ref[:, d:], preferred_element_type=F32)
        ya = jnp.dot(o_ref[rows, :], wao_ref[...], preferred_element_type=F32)
        yr = jnp.dot(hy_ref[rows, :], wro_ref[...], preferred_element_type=F32)
        merged = _sigmoid(ga) * ya + _sigmoid(gr) * yr
        mix = jnp.dot(merged.astype(BF16), wout_ref[...], preferred_element_type=F32)
        x1 = _layer_norm(alpha * x + mix, lng_ref[...], lnb_ref[...])
        x1_ref[rows, :] = x1
        x1b = x1.astype(BF16)
        x1b_ref[rows, :] = x1b
        logits_ref[rows, :] = (jnp.dot(x1b, wr_ref[...], preferred_element_type=F32)
                               + br_ref[...])


def _mixout(x2, o2, hy2, wmg, wao, wro, wout, lng, lnb, wr, br, alpha):
    t, d = x2.shape
    tm = TD
    nt = t // tm
    row = lambda i: (jnp.minimum(i, nt - 1), 0)
    prev_row = lambda i: (jnp.maximum(i - 1, 0), 0)
    full = lambda i: (0, 0)
    return pl.pallas_call(
        functools.partial(_mixout_kernel, alpha=alpha),
        grid=(nt + 1,),
        in_specs=[pl.BlockSpec((tm, d), row)] * 3
                 + [pl.BlockSpec((d, 2 * d), full)]
                 + [pl.BlockSpec((d, d), full)] * 3
                 + [pl.BlockSpec((1, d), full)] * 2
                 + [pl.BlockSpec((d, LANES), full), pl.BlockSpec((1, LANES), full)],
        out_specs=[pl.BlockSpec((tm, d), row), pl.BlockSpec((tm, d), row),
                   pl.BlockSpec((tm, LANES), prev_row),
                   pl.BlockSpec((1, 1, LANES), lambda i: (jnp.maximum(i - 1, 0), 0, 0))],
        out_shape=[jax.ShapeDtypeStruct((t, d), F32), jax.ShapeDtypeStruct((t, d), BF16),
                   jax.ShapeDtypeStruct((t, LANES), F32),
                   jax.ShapeDtypeStruct((nt, 1, LANES), F32)],
        scratch_shapes=[pltpu.VMEM((tm, LANES), F32)],
        compiler_params=_cparams(1),
        name="mixout",
    )(x2, o2, hy2, wmg, wao, wro, wout, lng, lnb, wr, br)


def _perm_matrix(route, soff_row, w1, w2):
    td = route.shape[0]
    lane = lax.broadcasted_iota(jnp.int32, (td, LANES), 1).astype(F32)
    e1 = lane == route[:, 0:1]
    e2 = lane == route[:, 1:2]
    cnt = (e1.astype(F32) + e2.astype(F32)).astype(BF16)
    ti = lax.broadcasted_iota(jnp.int32, (td, td), 0)
    tj = lax.broadcasted_iota(jnp.int32, (td, td), 1)
    lower = (tj < ti).astype(BF16)
    pos = jnp.dot(lower, cnt, preferred_element_type=F32) + soff_row
    r1 = jnp.sum(jnp.where(e1, pos, 0.0), axis=-1, keepdims=True).astype(jnp.int32)
    r2 = jnp.sum(jnp.where(e2, pos, 0.0), axis=-1, keepdims=True).astype(jnp.int32)
    col = lax.broadcasted_iota(jnp.int32, (td, STAGE_ROWS), 1)
    return jnp.where(col == r1, w1, 0.0) + jnp.where(col == r2, w2, 0.0)


def _chunk_rows(c, k=1):
    return pl.ds(pl.multiple_of(c * CHUNK, CHUNK), k * CHUNK)


def _block_rows(b):
    return pl.ds(pl.multiple_of(b * MOE_BLOCK, MOE_BLOCK), MOE_BLOCK)


def _for_each_run(i, cch_ref, soff_ref, dch_ref, fn):
    assert COPY_RUN == 4

    def per_expert(e, carry):
        idx = i * N_EXPERTS + e
        so = soff_ref[idx]
        do = dch_ref[idx]
        n = cch_ref[idx]
        full = n // COPY_RUN

        def per_run(c, carry2):
            fn(so + c * COPY_RUN, do + c * COPY_RUN, COPY_RUN)
            return carry2

        lax.fori_loop(0, full, per_run, 0)
        done = full * COPY_RUN
        two = n & 2

        @pl.when(two != 0)
        def _():
            fn(so + done, do + done, 2)

        @pl.when((n & 1) != 0)
        def _():
            fn(so + done + two, do + done + two, 1)

        return carry

    lax.fori_loop(0, N_EXPERTS, per_expert, 0)


def _repeat(n, fn):
    def body(c, carry):
        fn()
        return carry

    lax.fori_loop(0, n, body, 0)


def _drain_chunks(n, make_copy):
    _repeat(n // WAIT_BATCH, lambda: make_copy(WAIT_BATCH).wait())
    _repeat(n % WAIT_BATCH, lambda: make_copy(1).wait())


def _dispatch_kernel(cch_ref, soff_ref, dch_ref, tot_ref, padst_ref, padn_ref, nb_ref,
                     x_ref, route_ref, soffrow_ref, xb_ref, stage_ref, zero_ref, sems, zsem):
    i = pl.program_id(0)
    nt = pl.num_programs(0)
    n_blocks = xb_ref.shape[0] // MOE_BLOCK
    slot = i % 2

    def copy(s, src_chunk, dst_chunk, k):
        return pltpu.make_async_copy(stage_ref.at[s, _chunk_rows(src_chunk, k)],
                                     xb_ref.at[_chunk_rows(dst_chunk, k)], sems.at[s])

    def many(s):
        return lambda k: pltpu.make_async_copy(stage_ref.at[s, pl.ds(0, k * CHUNK)],
                                               xb_ref.at[pl.ds(0, k * CHUNK)], sems.at[s])

    @pl.when(i >= 2)
    def _():
        _drain_chunks(tot_ref[jnp.maximum(i - 2, 0)], many(slot))

    pt = _perm_matrix(route_ref[...], soffrow_ref[0], 1.0, 1.0).astype(BF16)
    stage_ref[slot] = lax.dot_general(pt, x_ref[...], (((0,), (0,)), ((), ())),
                                      preferred_element_type=F32).astype(BF16)
    _for_each_run(i, cch_ref, soff_ref, dch_ref,
                  lambda s, d, k: copy(slot, s, d, k).start())

    @pl.when(i == nt - 1)
    def _():
        @pl.when(i >= 1)
        def _():
            _drain_chunks(tot_ref[jnp.maximum(i - 1, 0)], many(1 - slot))

        _drain_chunks(tot_ref[i], many(slot))

        zero_ref[...] = jnp.zeros_like(zero_ref)

        def zcopy(dst_chunk):
            return pltpu.make_async_copy(zero_ref.at[pl.ds(0, CHUNK)],
                                         xb_ref.at[_chunk_rows(dst_chunk)], zsem)

        def pad_start(e, n):
            def pad_chunk(c, carry2):
                zcopy(padst_ref[e] + c).start()
                return carry2

            lax.fori_loop(0, padn_ref[e], pad_chunk, 0)
            return n + padn_ref[e]

        npad = lax.fori_loop(0, N_EXPERTS, pad_start, 0)
        _repeat(npad, lambda: zcopy(0).wait())

        def zblock(b):
            return pltpu.make_async_copy(zero_ref, xb_ref.at[_block_rows(b)], zsem)

        def start_block(b, carry):
            zblock(b).start()
            return carry

        def wait_block(b, carry):
            zblock(b).wait()
            return carry

        lax.fori_loop(nb_ref[0], n_blocks, start_block, 0)
        lax.fori_loop(nb_ref[0], n_blocks, wait_block, 0)


def _dispatch(x1, route, soffrow, tables, n_rows):
    t, d = x1.shape
    nt = t // TD
    grid_spec = pltpu.PrefetchScalarGridSpec(
        num_scalar_prefetch=7,
        grid=(nt,),
        in_specs=[pl.BlockSpec((TD, d), lambda i, *_: (i, 0)),
                  pl.BlockSpec((TD, LANES), lambda i, *_: (i, 0)),
                  pl.BlockSpec((1, 1, LANES), lambda i, *_: (i, 0, 0))],
        out_specs=pl.BlockSpec(memory_space=pl.ANY),
        scratch_shapes=[pltpu.VMEM((2, STAGE_ROWS, d), BF16), pltpu.VMEM((MOE_BLOCK, d), BF16),
                        pltpu.SemaphoreType.DMA((2,)), pltpu.SemaphoreType.DMA(())],
    )
    return pl.pallas_call(
        _dispatch_kernel,
        grid_spec=grid_spec,
        out_shape=jax.ShapeDtypeStruct((n_rows, d), BF16),
        compiler_params=_cparams(1),
        name="dispatch",
    )(*tables, x1, route, soffrow)


def _experts_kernel(pst_ref, ntile_ref, nb_ref, xb_ref, wg_ref, wu_ref, wd_ref, yb_ref,
                    wgb_ref, wub_ref, wdb_ref, xbuf_ref, ybuf_ref, in_sems, out_sems, zsem):
    e = pl.program_id(0)
    ne = pl.num_programs(0)
    n = ntile_ref[e]
    row0 = pst_ref[e]
    n_blocks = yb_ref.shape[0] // MOE_BLOCK
    ahead = EXP_BUFS - 1

    def tile_rows(r0, t):
        return pl.ds(pl.multiple_of(r0 + t * EXP_TILE, MOE_BLOCK), EXP_TILE)

    def in_copy(r0, t, s):
        return pltpu.make_async_copy(xb_ref.at[tile_rows(r0, t)], xbuf_ref.at[s], in_sems.at[s])

    def out_copy(t, s):
        return pltpu.make_async_copy(ybuf_ref.at[s], yb_ref.at[tile_rows(row0, t)],
                                     out_sems.at[s])

    def start_head(r0, count):
        for k in range(ahead):
            @pl.when(k < count)
            def _():
                in_copy(r0, k, k).start()

    @pl.when(e == 0)
    def _():
        start_head(row0, n)

    @pl.when(n > 0)
    def _():
        wgb_ref[...] = wg_ref[0, 0].astype(BF16)
        wub_ref[...] = wu_ref[0, 0].astype(BF16)
        wdb_ref[...] = wd_ref[0, 0].astype(BF16)

    def tile(t, carry):
        s = t % EXP_BUFS

        @pl.when(t + ahead < n)
        def _():
            in_copy(row0, t + ahead, (t + ahead) % EXP_BUFS).start()

        in_copy(row0, t, s).wait()

        @pl.when(t >= EXP_BUFS)
        def _():
            out_copy(t - EXP_BUFS, s).wait()

        x = xbuf_ref[s]
        hg = jnp.dot(x, wgb_ref[...], preferred_element_type=F32)
        hu = jnp.dot(x, wub_ref[...], preferred_element_type=F32)
        hid = (hg * _sigmoid(hg)) * hu
        ybuf_ref[s] = jnp.dot(hid.astype(BF16), wdb_ref[...],
                              preferred_element_type=F32).astype(BF16)
        out_copy(t, s).start()
        return carry

    lax.fori_loop(0, n, tile, 0)

    @pl.when(e + 1 < ne)
    def _():
        nxt = jnp.minimum(e + 1, ne - 1)
        start_head(pst_ref[nxt], ntile_ref[nxt])

    for k in range(EXP_BUFS):
        @pl.when(n > k)
        def _():
            out_copy(n - 1 - k, (n - 1 - k) % EXP_BUFS).wait()

    @pl.when(e == ne - 1)
    def _():
        ybuf_ref[0] = jnp.zeros(ybuf_ref.shape[1:], BF16)

        def zblock(b):
            return pltpu.make_async_copy(ybuf_ref.at[0, pl.ds(0, MOE_BLOCK)],
                                         yb_ref.at[_block_rows(b)], zsem)

        def start_block(b, carry):
            zblock(b).start()
            return carry

        def wait_block(b, carry):
            zblock(b).wait()
            return carry

        lax.fori_loop(nb_ref[0], n_blocks, start_block, 0)
        lax.fori_loop(nb_ref[0], n_blocks, wait_block, 0)


def _experts(xb, pst, ntile, nb_used, w_g, w_u, w_d, layer):
    n_rows, d = xb.shape
    n_exp, de = w_g.shape[1], w_g.shape[3]
    wsel = lambda e, *_: (layer, e, 0, 0)
    grid_spec = pltpu.PrefetchScalarGridSpec(
        num_scalar_prefetch=3,
        grid=(n_exp,),
        in_specs=[pl.BlockSpec(memory_space=pl.ANY),
                  pl.BlockSpec((1, 1, d, de), wsel),
                  pl.BlockSpec((1, 1, d, de), wsel),
                  pl.BlockSpec((1, 1, de, d), wsel)],
        out_specs=pl.BlockSpec(memory_space=pl.ANY),
        scratch_shapes=[pltpu.VMEM((d, de), BF16), pltpu.VMEM((d, de), BF16),
                        pltpu.VMEM((de, d), BF16),
                        pltpu.VMEM((EXP_BUFS, EXP_TILE, d), BF16),
                        pltpu.VMEM((EXP_BUFS, EXP_TILE, d), BF16),
                        pltpu.SemaphoreType.DMA((EXP_BUFS,)), pltpu.SemaphoreType.DMA((EXP_BUFS,)),
                        pltpu.SemaphoreType.DMA(())],
    )
    return pl.pallas_call(
        _experts_kernel,
        grid_spec=grid_spec,
        out_shape=jax.ShapeDtypeStruct((n_rows, d), BF16),
        compiler_params=_cparams(1),
        name="experts",
    )(pst, ntile, nb_used, xb, w_g, w_u, w_d)


def _combine_kernel(cch_ref, soff_ref, dch_ref, tot_ref,
                    x_ref, route_ref, soffrow_ref, yb_ref, lng_ref, lnb_ref, *rest,
                    alpha, project):
    if project:
        (w_ref, cos_ref, sin_ref, out_ref, q_ref, k_ref, v_ref, xr_ref, yr_ref,
         stage_ref, sems) = rest
    else:
        out_ref, stage_ref, sems = rest
    i = pl.program_id(0)
    nt = pl.num_programs(0)
    slot = i % 2

    def copy(s, src_chunk, dst_chunk, k):
        return pltpu.make_async_copy(yb_ref.at[_chunk_rows(src_chunk, k)],
                                     stage_ref.at[s, _chunk_rows(dst_chunk, k)], sems.at[s])

    def fetch(tile_idx, s):
        _for_each_run(tile_idx, cch_ref, soff_ref, dch_ref,
                      lambda so, do, k: copy(s, do, so, k).start())

    @pl.when(i == 0)
    def _():
        stage_ref[...] = jnp.zeros_like(stage_ref)
        fetch(i, slot)

    @pl.when(i + 1 < nt)
    def _():
        fetch(i + 1, 1 - slot)

    _drain_chunks(tot_ref[i], lambda k: pltpu.make_async_copy(
        yb_ref.at[pl.ds(0, k * CHUNK)], stage_ref.at[slot, pl.ds(0, k * CHUNK)], sems.at[slot]))

    route = route_ref[...]
    pt = _perm_matrix(route, soffrow_ref[0], route[:, 2:3], route[:, 3:4]).astype(BF16)
    ffn = jnp.dot(pt, stage_ref[slot], preferred_element_type=F32)
    out = _layer_norm(alpha * x_ref[...] + ffn, lng_ref[...], lnb_ref[...])
    out_ref[...] = out
    if project:
        _project_tile(out.astype(BF16), w_ref, cos_ref[...], sin_ref[...],
                      q_ref, k_ref, v_ref, xr_ref, yr_ref)


def _combine(x1, route, soffrow, tables, yb, lng, lnb, alpha, next_proj=None):
    t, d = x1.shape
    nt = t // TD
    row = lambda i, *_: (i, 0)
    full = lambda i, *_: (0, 0)
    in_specs = [pl.BlockSpec((TD, d), row),
                pl.BlockSpec((TD, LANES), row),
                pl.BlockSpec((1, 1, LANES), lambda i, *_: (i, 0, 0)),
                pl.BlockSpec(memory_space=pl.ANY),
                pl.BlockSpec((1, d), full),
                pl.BlockSpec((1, d), full)]
    out_shape = [jax.ShapeDtypeStruct((t, d), F32)]
    out_specs = [pl.BlockSpec((TD, d), row)]
    args = [x1, route, soffrow, yb, lng, lnb]
    if next_proj is not None:
        w_in_b, cosf, sinf, seq = next_proj
        tiles_per_seq = seq // TD
        pos = lambda i, *_: (i % tiles_per_seq, 0)
        in_specs += [pl.BlockSpec(w_in_b.shape, full), pl.BlockSpec((TD, HEAD_DIM), pos),
                     pl.BlockSpec((TD, HEAD_DIM), pos)]
        shapes, specs = _proj_out_specs(t, d, TD, row)
        out_shape += shapes
        out_specs += specs
        args += [w_in_b, cosf, sinf]
    grid_spec = pltpu.PrefetchScalarGridSpec(
        num_scalar_prefetch=4,
        grid=(nt,),
        in_specs=in_specs,
        out_specs=out_specs,
        scratch_shapes=[pltpu.VMEM((2, STAGE_ROWS, d), BF16), pltpu.SemaphoreType.DMA((2,))],
    )
    return pl.pallas_call(
        functools.partial(_combine_kernel, alpha=alpha, project=next_proj is not None),
        grid_spec=grid_spec,
        out_shape=out_shape,
        compiler_params=_cparams(1),
        name="combine",
    )(*tables, *args)


def _max_blocks(t):
    nt = t // TD
    rows = 2 * t + (CHUNK - 1) * nt * N_EXPERTS + (MOE_BLOCK - CHUNK) * N_EXPERTS
    return -(-rows // MOE_BLOCK)


def _dispatch_tables(cnt):
    n = cnt[:, 0, :N_EXPERTS].astype(jnp.int32)
    nt = n.shape[0]
    cch = (n + CHUNK - 1) // CHUNK
    excl = lambda k: (jnp.arange(k)[:, None] < jnp.arange(k)[None, :]).astype(F32)
    dot = functools.partial(jnp.dot, precision=lax.Precision.HIGHEST)
    cchf = cch.astype(F32)
    soff = dot(cchf, excl(N_EXPERTS)).astype(jnp.int32)
    tot_tile = jnp.sum(cch, axis=1)
    tot_e = jnp.sum(cch, axis=0)
    reg = (tot_e + CHUNKS_PER_BLOCK - 1) // CHUNKS_PER_BLOCK * CHUNKS_PER_BLOCK
    pstart = dot(reg.astype(F32)[None, :], excl(N_EXPERTS))[0].astype(jnp.int32)
    before = dot(excl(nt).T, cchf).astype(jnp.int32)
    dch = pstart[None, :] + before
    nb_used = jnp.sum(reg) // CHUNKS_PER_BLOCK
    soffrow = jnp.pad((soff * CHUNK).astype(F32), ((0, 0), (0, LANES - N_EXPERTS)))[:, None, :]
    i32 = lambda a: a.reshape(-1).astype(jnp.int32)
    return dict(cch=i32(cch), soff=i32(soff), dch=i32(dch), tot=i32(tot_tile),
                padst=i32(pstart + tot_e), padn=i32(reg - tot_e), nb_used=i32(nb_used),
                pst=i32(pstart * CHUNK), ntile=i32((reg * CHUNK + EXP_TILE - 1) // EXP_TILE),
                soffrow=soffrow)


def _rope_tables(seq):
    inv = ROPE_THETA ** (-jnp.arange(0, HEAD_DIM, 2, dtype=F32) / HEAD_DIM)
    ang = jnp.arange(seq, dtype=F32)[:, None] * inv[None, :]
    cos, sin = jnp.cos(ang), jnp.sin(ang)
    return jnp.concatenate([cos, cos], axis=1), jnp.concatenate([-sin, sin], axis=1)


@jax.jit
def kernel(x, w_in, w_sink, w_conv, b_conv, w_rec_gate, b_rec_gate, w_in_gate, b_in_gate,
           lru_lambda, w_attn_o, w_rnn_o, w_out, ln_g, ln_b, w_router_group, b_router_group,
           w_router_expert, b_router_expert, w_exp_gate, w_exp_up, w_exp_down):
    bsz, seq, d = x.shape
    depth = w_in.shape[0]
    t = bsz * seq
    nblk = d // LANES
    alpha = (2 * depth) ** 0.25
    cosf, sinf = _rope_tables(seq)
    n_rows = _max_blocks(t) * MOE_BLOCK + (EXP_TILE - MOE_BLOCK)
    x2 = x.reshape(t, d)
    n_branch = w_in.shape[2] - 2 * d
    w_branch = [w_in[l, :, :n_branch].astype(BF16) for l in range(depth)]
    q, k, v, xr, yr = _inproj(x2, w_branch[0], cosf, sinf, seq)
    for l in range(depth):
        o = _attention(q.reshape(bsz, seq, -1), k.reshape(bsz, seq, -1), v.reshape(bsz, seq, -1),
                       w_sink[l])
        wg_cat = (0.5 * jnp.concatenate([w_rec_gate[l, 0], w_in_gate[l, 0], w_rec_gate[l, 1],
                                         w_in_gate[l, 1]], axis=-1)).astype(BF16)
        bg_cat = jnp.concatenate([b_rec_gate[l, 0].reshape(nblk, 1, LANES),
                                  b_in_gate[l, 0].reshape(nblk, 1, LANES),
                                  b_rec_gate[l, 1].reshape(nblk, 1, LANES),
                                  b_in_gate[l, 1].reshape(nblk, 1, LANES)], axis=-1)
        lam_cat = jnp.concatenate([lru_lambda[l, 0].reshape(nblk, 1, LANES),
                                   lru_lambda[l, 1].reshape(nblk, 1, LANES)], axis=-1)
        hy = _rnn(xr.reshape(bsz, seq, d), yr.reshape(bsz, seq, d), w_conv[l],
                  b_conv[l].reshape(nblk, 1, LANES), wg_cat, bg_cat, lam_cat)
        lane_pad = LANES - N_EXPERTS - N_GROUPS
        wr = jnp.concatenate([w_router_expert[l], w_router_group[l],
                              jnp.zeros((d, lane_pad), F32)], axis=1).astype(BF16)
        br = jnp.concatenate([b_router_expert[l], b_router_group[l],
                              jnp.zeros((lane_pad,), F32)])[None, :]
        x1, x1b, route, cnt = _mixout(x2, o.reshape(t, -1), hy.reshape(t, d),
                                 w_in[l, :, n_branch:].astype(BF16),
                                 w_attn_o[l].astype(BF16), w_rnn_o[l].astype(BF16),
                                 w_out[l].astype(BF16), ln_g[l, 0].reshape(1, d),
                                 ln_b[l, 0].reshape(1, d), wr, br, alpha)
        tb = _dispatch_tables(cnt)
        xb = _dispatch(x1b, route, tb["soffrow"],
                       (tb["cch"], tb["soff"], tb["dch"], tb["tot"], tb["padst"], tb["padn"],
                        tb["nb_used"]), n_rows)
        yb = _experts(xb, tb["pst"], tb["ntile"], tb["nb_used"], w_exp_gate, w_exp_up,
                      w_exp_down, l)
        next_proj = (w_branch[l + 1], cosf, sinf, seq) if l + 1 < depth else None
        x2, *nxt = _combine(x1, route, tb["soffrow"],
                            (tb["cch"], tb["soff"], tb["dch"], tb["tot"]), yb,
                            ln_g[l, 1].reshape(1, d), ln_b[l, 1].reshape(1, d), alpha, next_proj)
        if nxt:
            q, k, v, xr, yr = nxt
    return x2.reshape(bsz, seq, d)
```

```python
import functools
import math

import jax
import jax.numpy as jnp
from jax import lax
from jax.experimental import pallas as pl
from jax.experimental.pallas import tpu as pltpu

F32 = jnp.float32
BF16 = jnp.bfloat16

HEAD_DIM = 128
N_Q_HEADS = 8
N_KV_HEADS = 2
Q_PER_KV = N_Q_HEADS // N_KV_HEADS
WINDOW = 128
ROPE_THETA = 10000.0
CONV_W = 4
LRU_C = 8.0
N_GROUPS = 4
EXPERTS_PER_GROUP = 8
N_EXPERTS = N_GROUPS * EXPERTS_PER_GROUP
LN_EPS = 1e-5
NEG = -1e30
LOG2E = math.log2(math.e)

SUBLANES = 8
LANES = 128
V7X_VMEM_BYTES = 64 * 1024 * 1024

TM_PROJ = 512
TQ_ATTN = 1024
ATTN_STACK = Q_PER_KV
SEG = 256
SCAN_LEN = SEG + 4
SCAN_UNROLL = 10
BF16_ROWS = 2 * SUBLANES
TD = 512
MIX_ROWS = 256
CHUNK = BF16_ROWS
WAIT_BATCH = 8
COPY_RUN = 4
STAGE_ROWS = 2 * TD + CHUNK * N_EXPERTS
MOE_BLOCK = 128
CHUNKS_PER_BLOCK = MOE_BLOCK // CHUNK
EXP_TILE = 2 * MOE_BLOCK
EXP_BUFS = 6
VMEM_LIMIT = V7X_VMEM_BYTES * 7 // 8


def _cparams(n_axes):
    return pltpu.CompilerParams(dimension_semantics=("arbitrary",) * n_axes,
                                vmem_limit_bytes=VMEM_LIMIT)


def _softplus(z):
    e = jnp.exp(-jnp.abs(z))
    w = 1.0 + e
    tiny = w == 1.0
    log1p = jnp.where(tiny, e, jnp.log(w) * (e / jnp.where(tiny, 1.0, w - 1.0)))
    return jnp.maximum(z, 0.0) + log1p


def _sigmoid(x):
    return 0.5 * jnp.tanh(0.5 * x) + 0.5


def _gelu_tanh(y):
    c1 = math.sqrt(2.0 / math.pi)
    half = 0.5 * y
    return half + half * jnp.tanh(y * (c1 + (c1 * 0.044715) * (y * y)))


def _layer_norm(y, g, b):
    mu = jnp.mean(y, axis=-1, keepdims=True)
    d = y - mu
    var = jnp.mean(d * d, axis=-1, keepdims=True)
    return d * lax.rsqrt(var + LN_EPS) * g + b


def _project_tile(xb, w_ref, cos, sin, q_ref, k_ref, v_ref, xr_ref, yr_ref):
    aw = N_Q_HEADS * HEAD_DIM
    kw = N_KV_HEADS * HEAD_DIM
    d_model = xr_ref.shape[1]

    def proj(c0, n):
        return jnp.dot(xb, w_ref[:, c0:c0 + n], preferred_element_type=F32)

    def rope(t):
        return t * cos + pltpu.roll(t, HEAD_DIM // 2, 1) * sin

    zq = proj(0, aw)
    scale = HEAD_DIM ** -0.5 * LOG2E
    for h in range(N_Q_HEADS):
        sl = slice(h * HEAD_DIM, (h + 1) * HEAD_DIM)
        q_ref[:, sl] = (rope(zq[:, sl]) * scale).astype(BF16)
    zk = proj(aw, kw)
    for h in range(N_KV_HEADS):
        sl = slice(h * HEAD_DIM, (h + 1) * HEAD_DIM)
        k_ref[:, sl] = rope(zk[:, sl]).astype(BF16)
    v_ref[...] = proj(aw + kw, kw).astype(BF16)
    c0 = aw + 2 * kw
    xr_ref[...] = proj(c0, d_model)
    yr_ref[...] = proj(c0 + d_model, d_model)


def _inproj_kernel(x_ref, w_ref, cos_ref, sin_ref, q_ref, k_ref, v_ref, xr_ref, yr_ref):
    _project_tile(x_ref[...].astype(BF16), w_ref, cos_ref[...], sin_ref[...],
                  q_ref, k_ref, v_ref, xr_ref, yr_ref)


def _proj_out_specs(t, d, tm, row):
    aw = N_Q_HEADS * HEAD_DIM
    kw = N_KV_HEADS * HEAD_DIM
    shapes = [jax.ShapeDtypeStruct((t, aw), BF16), jax.ShapeDtypeStruct((t, kw), BF16),
              jax.ShapeDtypeStruct((t, kw), BF16)] + [jax.ShapeDtypeStruct((t, d), F32)] * 2
    specs = [pl.BlockSpec((tm, aw), row), pl.BlockSpec((tm, kw), row),
             pl.BlockSpec((tm, kw), row)] + [pl.BlockSpec((tm, d), row)] * 2
    return shapes, specs


def _inproj(x2, w_in_b, cosf, sinf, seq):
    t, d = x2.shape
    n_in = w_in_b.shape[1]
    tm = TM_PROJ
    tiles_per_seq = seq // tm
    row = lambda i: (i, 0)
    pos = lambda i: (i % tiles_per_seq, 0)
    shapes, specs = _proj_out_specs(t, d, tm, row)
    return pl.pallas_call(
        _inproj_kernel,
        grid=(t // tm,),
        in_specs=[pl.BlockSpec((tm, d), row),
                  pl.BlockSpec((d, n_in), lambda i: (0, 0)),
                  pl.BlockSpec((tm, HEAD_DIM), pos),
                  pl.BlockSpec((tm, HEAD_DIM), pos)],
        out_specs=specs,
        out_shape=shapes,
        compiler_params=_cparams(1),
        name="inproj",
    )(x2, w_in_b, cosf, sinf)


def _attn_kernel(sink_ref, q_ref, kp_ref, kc_ref, kn_ref, vp_ref, vc_ref, vn_ref, o_ref,
                 *, seq):
    i = pl.program_id(1)
    tq = q_ref.shape[1]
    blk = WINDOW
    t0 = i * tq
    kext = jnp.concatenate([kp_ref[0], kc_ref[0], kn_ref[0]], axis=0)
    vext = jnp.concatenate([vp_ref[0], vc_ref[0], vn_ref[0]], axis=0)
    nrow = ATTN_STACK * blk
    blk_shift = blk.bit_length() - 1
    qi = lax.broadcasted_iota(jnp.int32, (nrow, 3 * blk), 0) & (blk - 1)
    kj = lax.broadcasted_iota(jnp.int32, (nrow, 3 * blk), 1)
    band = jnp.where(jnp.abs(kj - blk - qi) <= WINDOW, 0.0, NEG)
    kj_row = lax.broadcasted_iota(jnp.int32, (1, 3 * blk), 1)
    rowg = lax.broadcasted_iota(jnp.int32, (nrow, 1), 0) >> blk_shift
    sinks = []
    for h0 in range(0, N_Q_HEADS, ATTN_STACK):
        sk = jnp.full((nrow, 1), sink_ref[h0], F32)
        for g in range(1, ATTN_STACK):
            sk = jnp.where(rowg == g, sink_ref[h0 + g], sk)
        sinks.append(sk * LOG2E)
    for j in range(tq // blk):
        kpos = t0 + j * blk - blk + kj_row
        bias = band + jnp.where((kpos >= 0) & (kpos < seq), 0.0, NEG)
        for h0 in range(0, N_Q_HEADS, ATTN_STACK):
            kv = h0 // Q_PER_KV
            hs = slice(kv * HEAD_DIM, (kv + 1) * HEAD_DIM)
            kblk = kext[j * blk:j * blk + 3 * blk, hs]
            vblk = vext[j * blk:j * blk + 3 * blk, hs]
            qs = [q_ref[0, j * blk:(j + 1) * blk, (h0 + g) * HEAD_DIM:(h0 + g + 1) * HEAD_DIM]
                  for g in range(ATTN_STACK)]
            qblk = jnp.concatenate(qs, axis=0)
            s = lax.dot_general(qblk, kblk, (((1,), (1,)), ((), ())),
                                preferred_element_type=F32) + bias
            sk = sinks[h0 // ATTN_STACK]
            m = jnp.maximum(jnp.max(s, axis=-1, keepdims=True), sk)
            p = jnp.exp2(s - m)
            denom = jnp.sum(p, axis=-1, keepdims=True) + jnp.exp2(sk - m)
            o = jnp.dot(p.astype(BF16), vblk, preferred_element_type=F32) / denom
            for g in range(ATTN_STACK):
                c = (h0 + g) * HEAD_DIM
                o_ref[0, j * blk:(j + 1) * blk, c:c + HEAD_DIM] = (
                    o[g * blk:(g + 1) * blk].astype(BF16))


def _attention(q3, k3, v3, sink):
    b, s, aw = q3.shape
    kw = k3.shape[2]
    tq = TQ_ATTN
    blk = WINDOW
    r = tq // blk
    nblk = s // blk
    cur = lambda bi, i: (bi, i, 0)
    prev = lambda bi, i: (bi, jnp.maximum(i * r - 1, 0), 0)
    nxt = lambda bi, i: (bi, jnp.minimum((i + 1) * r, nblk - 1), 0)
    kv_specs = [pl.BlockSpec((1, blk, kw), prev), pl.BlockSpec((1, tq, kw), cur),
                pl.BlockSpec((1, blk, kw), nxt)]
    return pl.pallas_call(
        functools.partial(_attn_kernel, seq=s),
        grid=(b, s // tq),
        in_specs=[pl.BlockSpec(memory_space=pltpu.SMEM),
                  pl.BlockSpec((1, tq, aw), cur)] + kv_specs + kv_specs,
        out_specs=pl.BlockSpec((1, tq, aw), cur),
        out_shape=jax.ShapeDtypeStruct((b, s, aw), BF16),
        compiler_params=_cparams(2),
        name="attention",
    )(sink, q3, k3, k3, k3, v3, v3, v3)


def _rnn_kernel(xr_ref, yr_ref, wc_ref, bc_ref, wg_ref, bg_ref, lam_ref, out_ref,
                xpad_ref, af_ref, uf_ref, ab_ref, ub_ref, hb_ref, cf_ref, cb_ref, *, seq):
    nseg = seq // SEG
    nlane = nseg
    ngrp = nlane // SUBLANES
    hf_ref = xpad_ref
    wc = wc_ref[...]
    bc = bc_ref[0]
    wg = wg_ref[0]
    bg = 0.5 * bg_ref[0]
    lam = lam_ref[0]
    rate = (0.5 * LRU_C) * _softplus(-lam)
    a_refs = (af_ref, ab_ref)
    u_refs = (uf_ref, ub_ref)
    left = CONV_W // 2

    halo = jnp.zeros((SUBLANES, LANES), F32)
    xpad_ref[pl.ds(0, SUBLANES), :] = halo
    xpad_ref[pl.ds(seq + SUBLANES, SUBLANES), :] = halo

    def pad_copy(c, carry):
        t0 = pl.multiple_of(c * SEG, SEG)
        xpad_ref[pl.ds(t0 + SUBLANES, SEG), :] = xr_ref[0, pl.ds(t0, SEG), :]
        return carry

    lax.fori_loop(0, nseg, pad_copy, 0, unroll=4)

    tail = nlane * SCAN_LEN - seq
    for a_ref, u_ref in zip(a_refs, u_refs):
        a_ref[pl.ds(seq, tail), :] = jnp.ones((tail, LANES), F32)
        u_ref[pl.ds(seq, tail), :] = jnp.zeros((tail, LANES), F32)

    def gates(c, carry):
        t0 = pl.multiple_of(c * SEG, SEG)
        xc = bc
        for tap in range(CONV_W):
            xc = xc + xpad_ref[pl.ds(t0 + SUBLANES - left + tap, SEG), :] * wc[tap:tap + 1]
        gh = jnp.dot(xc.astype(BF16), wg, preferred_element_type=F32) + bg
        xch = 0.5 * xc
        for d in range(2):
            rt = rate[:, d * LANES:(d + 1) * LANES]
            nlog_a = rt * jnp.tanh(gh[:, (2 * d) * LANES:(2 * d + 1) * LANES]) + rt
            a = jnp.exp2(nlog_a * (-1.0 / math.log(2.0)))
            z = jnp.tanh(nlog_a) * (a * a + 1.0)
            mult = jnp.where(z > 0.0, z * lax.rsqrt(z), 0.0)
            in_gate2 = jnp.tanh(gh[:, (2 * d + 1) * LANES:(2 * d + 2) * LANES]) + 1.0
            a_refs[d][pl.ds(t0, SEG), :] = a
            u_refs[d][pl.ds(t0, SEG), :] = (xch * mult) * in_gate2
        return carry

    lax.fori_loop(0, nseg, gates, 0, unroll=16)

    def lane_rows(g, j):
        return pl.ds(g * SUBLANES * SCAN_LEN + j, SUBLANES, stride=SCAN_LEN)

    def totals_step(j, carry):
        hf, pf, hb, pb = carry
        jb = SCAN_LEN - 1 - j
        nhf, npf, nhb, npb = [], [], [], []
        for g in range(ngrp):
            a = af_ref[lane_rows(g, j), :]
            nhf.append(a * hf[g] + uf_ref[lane_rows(g, j), :])
            npf.append(a * pf[g])
            a = ab_ref[lane_rows(g, jb), :]
            nhb.append(a * hb[g] + ub_ref[lane_rows(g, jb), :])
            npb.append(a * pb[g])
        return tuple(nhf), tuple(npf), tuple(nhb), tuple(npb)

    zero = tuple(jnp.zeros((SUBLANES, LANES), F32) for _ in range(ngrp))
    one = tuple(jnp.ones((SUBLANES, LANES), F32) for _ in range(ngrp))
    hf, pf, hb, pb = lax.fori_loop(0, SCAN_LEN, totals_step, (zero, one, zero, one),
                                   unroll=SCAN_UNROLL)

    c = jnp.zeros((1, LANES), F32)
    for s in range(nlane):
        g, r = divmod(s, SUBLANES)
        cf_ref[s:s + 1, :] = c
        c = pf[g][r:r + 1] * c + hf[g][r:r + 1]
    c = jnp.zeros((1, LANES), F32)
    for s in range(nlane - 1, -1, -1):
        g, r = divmod(s, SUBLANES)
        cb_ref[s:s + 1, :] = c
        c = pb[g][r:r + 1] * c + hb[g][r:r + 1]

    def scan_step(j, carry):
        hf, hb = carry
        jb = SCAN_LEN - 1 - j
        nhf, nhb = [], []
        for g in range(ngrp):
            h = af_ref[lane_rows(g, j), :] * hf[g] + uf_ref[lane_rows(g, j), :]
            hf_ref[lane_rows(g, j), :] = h
            nhf.append(h)
            h = ab_ref[lane_rows(g, jb), :] * hb[g] + ub_ref[lane_rows(g, jb), :]
            hb_ref[lane_rows(g, jb), :] = h
            nhb.append(h)
        return tuple(nhf), tuple(nhb)

    hf0 = tuple(cf_ref[g * SUBLANES:(g + 1) * SUBLANES, :] for g in range(ngrp))
    hb0 = tuple(cb_ref[g * SUBLANES:(g + 1) * SUBLANES, :] for g in range(ngrp))
    lax.fori_loop(0, SCAN_LEN, scan_step, (hf0, hb0), unroll=SCAN_UNROLL)

    def finish(c, carry):
        t0 = pl.multiple_of(c * SEG, SEG)
        h = hf_ref[pl.ds(t0, SEG), :] + hb_ref[pl.ds(t0, SEG), :]
        out_ref[0, pl.ds(t0, SEG), :] = (h * _gelu_tanh(yr_ref[0, pl.ds(t0, SEG), :])).astype(BF16)
        return carry

    lax.fori_loop(0, nseg, finish, 0, unroll=4)


def _rnn(xr3, yr3, w_conv, b_conv, wg_cat, bg_cat, lam_cat):
    b, s, d = xr3.shape
    nblk = d // LANES
    nseg = s // SEG
    slab = lambda bi, n: (bi, 0, n)
    per_blk = lambda bi, n: (n, 0, 0)
    assert nseg % SUBLANES == 0 and nseg * SCAN_LEN >= s + 2 * SUBLANES
    scratch = ([pltpu.VMEM((nseg * SCAN_LEN, LANES), F32)] * 6
               + [pltpu.VMEM((nseg, LANES), F32)] * 2)
    return pl.pallas_call(
        functools.partial(_rnn_kernel, seq=s),
        grid=(b, nblk),
        in_specs=[pl.BlockSpec((1, s, LANES), slab),
                  pl.BlockSpec((1, s, LANES), slab),
                  pl.BlockSpec((CONV_W, LANES), lambda bi, n: (0, n)),
                  pl.BlockSpec((1, 1, LANES), per_blk),
                  pl.BlockSpec((1, LANES, 4 * LANES), per_blk),
                  pl.BlockSpec((1, 1, 4 * LANES), per_blk),
                  pl.BlockSpec((1, 1, 2 * LANES), per_blk)],
        out_specs=pl.BlockSpec((1, s, LANES), slab),
        out_shape=jax.ShapeDtypeStruct((b, s, d), BF16),
        scratch_shapes=scratch,
        compiler_params=_cparams(2),
        name="rglru",
    )(xr3, yr3, w_conv, b_conv, wg_cat, bg_cat, lam_cat)


def _route(logits):
    lane = lax.broadcasted_iota(jnp.int32, logits.shape, 1)
    lanef = lane.astype(F32)
    big = float(4 * LANES)
    gmask = (lane >= N_EXPERTS) & (lane < N_EXPERTS + N_GROUPS)
    gl = jnp.where(gmask, logits, NEG)
    gmax = jnp.max(gl, axis=-1, keepdims=True)
    ge = jnp.exp(gl - gmax)
    gprob = ge / jnp.sum(ge, axis=-1, keepdims=True)
    gval = jnp.max(gprob, axis=-1, keepdims=True)
    gidx = jnp.min(jnp.where((gprob == gval) & gmask, lanef, big), axis=-1, keepdims=True)
    gidx = gidx.astype(jnp.int32) - N_EXPERTS
    group_shift = EXPERTS_PER_GROUP.bit_length() - 1
    emask = (lane < N_EXPERTS) & ((lane >> group_shift) == gidx)
    el = jnp.where(emask, logits, NEG)
    m1 = jnp.max(el, axis=-1, keepdims=True)
    i1 = jnp.min(jnp.where((el == m1) & emask, lanef, big), axis=-1, keepdims=True)
    emask2 = emask & (lanef != i1)
    el2 = jnp.where(emask2, logits, NEG)
    m2 = jnp.max(el2, axis=-1, keepdims=True)
    i2 = jnp.min(jnp.where((el2 == m2) & emask2, lanef, big), axis=-1, keepdims=True)
    e2 = jnp.exp(m2 - m1)
    den = 1.0 + e2
    g1 = (1.0 / den) * gval
    g2 = (e2 / den) * gval
    route = jnp.where(lane == 0, i1, jnp.where(lane == 1, i2,
                      jnp.where(lane == 2, g1, jnp.where(lane == 3, g2, 0.0))))
    onehot = (lanef == i1).astype(F32) + (lanef == i2).astype(F32)
    return route, jnp.sum(onehot, axis=0, keepdims=True)


def _mixout_kernel(x_ref, o_ref, hy_ref, wmg_ref, wao_ref, wro_ref, wout_ref,
                   lng_ref, lnb_ref, wr_ref, br_ref, x1_ref, x1b_ref, route_ref, cnt_ref,
                   logits_ref, *, alpha):
    @pl.when(pl.program_id(0) == 0)
    def _():
        logits_ref[...] = jnp.zeros_like(logits_ref)

    route, cnt = _route(logits_ref[...])
    route_ref[...] = route
    cnt_ref[0] = cnt

    d = x_ref.shape[1]
    for r0 in range(0, x_ref.shape[0], MIX_ROWS):
        rows = pl.ds(r0, MIX_ROWS)
        x = x_ref[rows, :]
        xb = x.astype(BF16)
        ga = jnp.dot(xb, wmg_ref[:, :d], preferred_element_type=F32)
        gr = jnp.dot(xb, wmg_ref[:, d:], preferred_element_type=F32)
        ya = jnp.dot(o_ref[rows, :], wao_ref[...], preferred_element_type=F32)
        yr = jnp.dot(hy_ref[rows, :], wro_ref[...], preferred_element_type=F32)
        merged = _sigmoid(ga) * ya + _sigmoid(gr) * yr
        mix = jnp.dot(merged.astype(BF16), wout_ref[...], preferred_element_type=F32)
        x1 = _layer_norm(alpha * x + mix, lng_ref[...], lnb_ref[...])
        x1_ref[rows, :] = x1
        x1b = x1.astype(BF16)
        x1b_ref[rows, :] = x1b
        logits_ref[rows, :] = (jnp.dot(x1b, wr_ref[...], preferred_element_type=F32)
                               + br_ref[...])


def _mixout(x2, o2, hy2, wmg, wao, wro, wout, lng, lnb, wr, br, alpha):
    t, d = x2.shape
    tm = TD
    nt = t // tm
    row = lambda i: (jnp.minimum(i, nt - 1), 0)
    prev_row = lambda i: (jnp.maximum(i - 1, 0), 0)
    full = lambda i: (0, 0)
    return pl.pallas_call(
        functools.partial(_mixout_kernel, alpha=alpha),
        grid=(nt + 1,),
        in_specs=[pl.BlockSpec((tm, d), row)] * 3
                 + [pl.BlockSpec((d, 2 * d), full)]
                 + [pl.BlockSpec((d, d), full)] * 3
                 + [pl.BlockSpec((1, d), full)] * 2
                 + [pl.BlockSpec((d, LANES), full), pl.BlockSpec((1, LANES), full)],
        out_specs=[pl.BlockSpec((tm, d), row), pl.BlockSpec((tm, d), row),
                   pl.BlockSpec((tm, LANES), prev_row),
                   pl.BlockSpec((1, 1, LANES), lambda i: (jnp.maximum(i - 1, 0), 0, 0))],
        out_shape=[jax.ShapeDtypeStruct((t, d), F32), jax.ShapeDtypeStruct((t, d), BF16),
                   jax.ShapeDtypeStruct((t, LANES), F32),
                   jax.ShapeDtypeStruct((nt, 1, LANES), F32)],
        scratch_shapes=[pltpu.VMEM((tm, LANES), F32)],
        compiler_params=_cparams(1),
        name="mixout",
    )(x2, o2, hy2, wmg, wao, wro, wout, lng, lnb, wr, br)


def _perm_matrix(route, soff_row, w1, w2):
    td = route.shape[0]
    lane = lax.broadcasted_iota(jnp.int32, (td, LANES), 1).astype(F32)
    e1 = lane == route[:, 0:1]
    e2 = lane == route[:, 1:2]
    cnt = (e1.astype(F32) + e2.astype(F32)).astype(BF16)
    ti = lax.broadcasted_iota(jnp.int32, (td, td), 0)
    tj = lax.broadcasted_iota(jnp.int32, (td, td), 1)
    lower = (tj < ti).astype(BF16)
    pos = jnp.dot(lower, cnt, preferred_element_type=F32) + soff_row
    r1 = jnp.sum(jnp.where(e1, pos, 0.0), axis=-1, keepdims=True).astype(jnp.int32)
    r2 = jnp.sum(jnp.where(e2, pos, 0.0), axis=-1, keepdims=True).astype(jnp.int32)
    col = lax.broadcasted_iota(jnp.int32, (td, STAGE_ROWS), 1)
    return jnp.where(col == r1, w1, 0.0) + jnp.where(col == r2, w2, 0.0)


def _chunk_rows(c, k=1):
    return pl.ds(pl.multiple_of(c * CHUNK, CHUNK), k * CHUNK)


def _block_rows(b):
    return pl.ds(pl.multiple_of(b * MOE_BLOCK, MOE_BLOCK), MOE_BLOCK)


def _for_each_run(i, cch_ref, soff_ref, dch_ref, fn):
    assert COPY_RUN == 4

    def per_expert(e, carry):
        idx = i * N_EXPERTS + e
        so = soff_ref[idx]
        do = dch_ref[idx]
        n = cch_ref[idx]
        full = n // COPY_RUN

        def per_run(c, carry2):
            fn(so + c * COPY_RUN, do + c * COPY_RUN, COPY_RUN)
            return carry2

        lax.fori_loop(0, full, per_run, 0)
        done = full * COPY_RUN
        two = n & 2

        @pl.when(two != 0)
        def _():
            fn(so + done, do + done, 2)

        @pl.when((n & 1) != 0)
        def _():
            fn(so + done + two, do + done + two, 1)

        return carry

    lax.fori_loop(0, N_EXPERTS, per_expert, 0)


def _repeat(n, fn):
    def body(c, carry):
        fn()
        return carry

    lax.fori_loop(0, n, body, 0)


def _drain_chunks(n, make_copy):
    _repeat(n // WAIT_BATCH, lambda: make_copy(WAIT_BATCH).wait())
    _repeat(n % WAIT_BATCH, lambda: make_copy(1).wait())


def _dispatch_kernel(cch_ref, soff_ref, dch_ref, tot_ref, padst_ref, padn_ref, nb_ref,
                     x_ref, route_ref, soffrow_ref, xb_ref, stage_ref, zero_ref, sems, zsem):
    i = pl.program_id(0)
    nt = pl.num_programs(0)
    n_blocks = xb_ref.shape[0] // MOE_BLOCK
    slot = i % 2

    def copy(s, src_chunk, dst_chunk, k):
        return pltpu.make_async_copy(stage_ref.at[s, _chunk_rows(src_chunk, k)],
                                     xb_ref.at[_chunk_rows(dst_chunk, k)], sems.at[s])

    def many(s):
        return lambda k: pltpu.make_async_copy(stage_ref.at[s, pl.ds(0, k * CHUNK)],
                                               xb_ref.at[pl.ds(0, k * CHUNK)], sems.at[s])

    @pl.when(i >= 2)
    def _():
        _drain_chunks(tot_ref[jnp.maximum(i - 2, 0)], many(slot))

    pt = _perm_matrix(route_ref[...], soffrow_ref[0], 1.0, 1.0).astype(BF16)
    stage_ref[slot] = lax.dot_general(pt, x_ref[...], (((0,), (0,)), ((), ())),
                                      preferred_element_type=F32).astype(BF16)
    _for_each_run(i, cch_ref, soff_ref, dch_ref,
                  lambda s, d, k: copy(slot, s, d, k).start())

    @pl.when(i == nt - 1)
    def _():
        @pl.when(i >= 1)
        def _():
            _drain_chunks(tot_ref[jnp.maximum(i - 1, 0)], many(1 - slot))

        _drain_chunks(tot_ref[i], many(slot))

        zero_ref[...] = jnp.zeros_like(zero_ref)

        def zcopy(dst_chunk):
            return pltpu.make_async_copy(zero_ref.at[pl.ds(0, CHUNK)],
                                         xb_ref.at[_chunk_rows(dst_chunk)], zsem)

        def pad_start(e, n):
            def pad_chunk(c, carry2):
                zcopy(padst_ref[e] + c).start()
                return carry2

            lax.fori_loop(0, padn_ref[e], pad_chunk, 0)
            return n + padn_ref[e]

        npad = lax.fori_loop(0, N_EXPERTS, pad_start, 0)
        _repeat(npad, lambda: zcopy(0).wait())

        def zblock(b):
            return pltpu.make_async_copy(zero_ref, xb_ref.at[_block_rows(b)], zsem)

        def start_block(b, carry):
            zblock(b).start()
            return carry

        def wait_block(b, carry):
            zblock(b).wait()
            return carry

        lax.fori_loop(nb_ref[0], n_blocks, start_block, 0)
        lax.fori_loop(nb_ref[0], n_blocks, wait_block, 0)


def _dispatch(x1, route, soffrow, tables, n_rows):
    t, d = x1.shape
    nt = t // TD
    grid_spec = pltpu.PrefetchScalarGridSpec(
        num_scalar_prefetch=7,
        grid=(nt,),
        in_specs=[pl.BlockSpec((TD, d), lambda i, *_: (i, 0)),
                  pl.BlockSpec((TD, LANES), lambda i, *_: (i, 0)),
                  pl.BlockSpec((1, 1, LANES), lambda i, *_: (i, 0, 0))],
        out_specs=pl.BlockSpec(memory_space=pl.ANY),
        scratch_shapes=[pltpu.VMEM((2, STAGE_ROWS, d), BF16), pltpu.VMEM((MOE_BLOCK, d), BF16),
                        pltpu.SemaphoreType.DMA((2,)), pltpu.SemaphoreType.DMA(())],
    )
    return pl.pallas_call(
        _dispatch_kernel,
        grid_spec=grid_spec,
        out_shape=jax.ShapeDtypeStruct((n_rows, d), BF16),
        compiler_params=_cparams(1),
        name="dispatch",
    )(*tables, x1, route, soffrow)


def _experts_kernel(pst_ref, ntile_ref, nb_ref, xb_ref, wg_ref, wu_ref, wd_ref, yb_ref,
                    wgb_ref, wub_ref, wdb_ref, xbuf_ref, ybuf_ref, in_sems, out_sems, zsem):
    e = pl.program_id(0)
    ne = pl.num_programs(0)
    n = ntile_ref[e]
    row0 = pst_ref[e]
    n_blocks = yb_ref.shape[0] // MOE_BLOCK
    ahead = EXP_BUFS - 1

    def tile_rows(r0, t):
        return pl.ds(pl.multiple_of(r0 + t * EXP_TILE, MOE_BLOCK), EXP_TILE)

    def in_copy(r0, t, s):
        return pltpu.make_async_copy(xb_ref.at[tile_rows(r0, t)], xbuf_ref.at[s], in_sems.at[s])

    def out_copy(t, s):
        return pltpu.make_async_copy(ybuf_ref.at[s], yb_ref.at[tile_rows(row0, t)],
                                     out_sems.at[s])

    def start_head(r0, count):
        for k in range(ahead):
            @pl.when(k < count)
            def _():
                in_copy(r0, k, k).start()

    @pl.when(e == 0)
    def _():
        start_head(row0, n)

    @pl.when(n > 0)
    def _():
        wgb_ref[...] = wg_ref[0, 0].astype(BF16)
        wub_ref[...] = wu_ref[0, 0].astype(BF16)
        wdb_ref[...] = wd_ref[0, 0].astype(BF16)

    def tile(t, carry):
        s = t % EXP_BUFS

        @pl.when(t + ahead < n)
        def _():
            in_copy(row0, t + ahead, (t + ahead) % EXP_BUFS).start()

        in_copy(row0, t, s).wait()

        @pl.when(t >= EXP_BUFS)
        def _():
            out_copy(t - EXP_BUFS, s).wait()

        x = xbuf_ref[s]
        hg = jnp.dot(x, wgb_ref[...], preferred_element_type=F32)
        hu = jnp.dot(x, wub_ref[...], preferred_element_type=F32)
        hid = (hg * _sigmoid(hg)) * hu
        ybuf_ref[s] = jnp.dot(hid.astype(BF16), wdb_ref[...],
                              preferred_element_type=F32).astype(BF16)
        out_copy(t, s).start()
        return carry

    lax.fori_loop(0, n, tile, 0)

    @pl.when(e + 1 < ne)
    def _():
        nxt = jnp.minimum(e + 1, ne - 1)
        start_head(pst_ref[nxt], ntile_ref[nxt])

    for k in range(EXP_BUFS):
        @pl.when(n > k)
        def _():
            out_copy(n - 1 - k, (n - 1 - k) % EXP_BUFS).wait()

    @pl.when(e == ne - 1)
    def _():
        ybuf_ref[0] = jnp.zeros(ybuf_ref.shape[1:], BF16)

        def zblock(b):
            return pltpu.make_async_copy(ybuf_ref.at[0, pl.ds(0, MOE_BLOCK)],
                                         yb_ref.at[_block_rows(b)], zsem)

        def start_block(b, carry):
            zblock(b).start()
            return carry

        def wait_block(b, carry):
            zblock(b).wait()
            return carry

        lax.fori_loop(nb_ref[0], n_blocks, start_block, 0)
        lax.fori_loop(nb_ref[0], n_blocks, wait_block, 0)


def _experts(xb, pst, ntile, nb_used, w_g, w_u, w_d, layer):
    n_rows, d = xb.shape
    n_exp, de = w_g.shape[1], w_g.shape[3]
    wsel = lambda e, *_: (layer, e, 0, 0)
    grid_spec = pltpu.PrefetchScalarGridSpec(
        num_scalar_prefetch=3,
        grid=(n_exp,),
        in_specs=[pl.BlockSpec(memory_space=pl.ANY),
                  pl.BlockSpec((1, 1, d, de), wsel),
                  pl.BlockSpec((1, 1, d, de), wsel),
                  pl.BlockSpec((1, 1, de, d), wsel)],
        out_specs=pl.BlockSpec(memory_space=pl.ANY),
        scratch_shapes=[pltpu.VMEM((d, de), BF16), pltpu.VMEM((d, de), BF16),
                        pltpu.VMEM((de, d), BF16),
                        pltpu.VMEM((EXP_BUFS, EXP_TILE, d), BF16),
                        pltpu.VMEM((EXP_BUFS, EXP_TILE, d), BF16),
                        pltpu.SemaphoreType.DMA((EXP_BUFS,)), pltpu.SemaphoreType.DMA((EXP_BUFS,)),
                        pltpu.SemaphoreType.DMA(())],
    )
    return pl.pallas_call(
        _experts_kernel,
        grid_spec=grid_spec,
        out_shape=jax.ShapeDtypeStruct((n_rows, d), BF16),
        compiler_params=_cparams(1),
        name="experts",
    )(pst, ntile, nb_used, xb, w_g, w_u, w_d)


def _combine_kernel(cch_ref, soff_ref, dch_ref, tot_ref,
                    x_ref, route_ref, soffrow_ref, yb_ref, lng_ref, lnb_ref, *rest,
                    alpha, project):
    if project:
        (w_ref, cos_ref, sin_ref, out_ref, q_ref, k_ref, v_ref, xr_ref, yr_ref,
         stage_ref, sems) = rest
    else:
        out_ref, stage_ref, sems = rest
    i = pl.program_id(0)
    nt = pl.num_programs(0)
    slot = i % 2

    def copy(s, src_chunk, dst_chunk, k):
        return pltpu.make_async_copy(yb_ref.at[_chunk_rows(src_chunk, k)],
                                     stage_ref.at[s, _chunk_rows(dst_chunk, k)], sems.at[s])

    def fetch(tile_idx, s):
        _for_each_run(tile_idx, cch_ref, soff_ref, dch_ref,
                      lambda so, do, k: copy(s, do, so, k).start())

    @pl.when(i == 0)
    def _():
        stage_ref[...] = jnp.zeros_like(stage_ref)
        fetch(i, slot)

    @pl.when(i + 1 < nt)
    def _():
        fetch(i + 1, 1 - slot)

    _drain_chunks(tot_ref[i], lambda k: pltpu.make_async_copy(
        yb_ref.at[pl.ds(0, k * CHUNK)], stage_ref.at[slot, pl.ds(0, k * CHUNK)], sems.at[slot]))

    route = route_ref[...]
    pt = _perm_matrix(route, soffrow_ref[0], route[:, 2:3], route[:, 3:4]).astype(BF16)
    ffn = jnp.dot(pt, stage_ref[slot], preferred_element_type=F32)
    out = _layer_norm(alpha * x_ref[...] + ffn, lng_ref[...], lnb_ref[...])
    out_ref[...] = out
    if project:
        _project_tile(out.astype(BF16), w_ref, cos_ref[...], sin_ref[...],
                      q_ref, k_ref, v_ref, xr_ref, yr_ref)


def _combine(x1, route, soffrow, tables, yb, lng, lnb, alpha, next_proj=None):
    t, d = x1.shape
    nt = t // TD
    row = lambda i, *_: (i, 0)
    full = lambda i, *_: (0, 0)
    in_specs = [pl.BlockSpec((TD, d), row),
                pl.BlockSpec((TD, LANES), row),
                pl.BlockSpec((1, 1, LANES), lambda i, *_: (i, 0, 0)),
                pl.BlockSpec(memory_space=pl.ANY),
                pl.BlockSpec((1, d), full),
                pl.BlockSpec((1, d), full)]
    out_shape = [jax.ShapeDtypeStruct((t, d), F32)]
    out_specs = [pl.BlockSpec((TD, d), row)]
    args = [x1, route, soffrow, yb, lng, lnb]
    if next_proj is not None:
        w_in_b, cosf, sinf, seq = next_proj
        tiles_per_seq = seq // TD
        pos = lambda i, *_: (i % tiles_per_seq, 0)
        in_specs += [pl.BlockSpec(w_in_b.shape, full), pl.BlockSpec((TD, HEAD_DIM), pos),
                     pl.BlockSpec((TD, HEAD_DIM), pos)]
        shapes, specs = _proj_out_specs(t, d, TD, row)
        out_shape += shapes
        out_specs += specs
        args += [w_in_b, cosf, sinf]
    grid_spec = pltpu.PrefetchScalarGridSpec(
        num_scalar_prefetch=4,
        grid=(nt,),
        in_specs=in_specs,
        out_specs=out_specs,
        scratch_shapes=[pltpu.VMEM((2, STAGE_ROWS, d), BF16), pltpu.SemaphoreType.DMA((2,))],
    )
    return pl.pallas_call(
        functools.partial(_combine_kernel, alpha=alpha, project=next_proj is not None),
        grid_spec=grid_spec,
        out_shape=out_shape,
        compiler_params=_cparams(1),
        name="combine",
    )(*tables, *args)


def _max_blocks(t):
    nt = t // TD
    rows = 2 * t + (CHUNK - 1) * nt * N_EXPERTS + (MOE_BLOCK - CHUNK) * N_EXPERTS
    return -(-rows // MOE_BLOCK)


def _dispatch_tables(cnt):
    n = cnt[:, 0, :N_EXPERTS].astype(jnp.int32)
    nt = n.shape[0]
    cch = (n + CHUNK - 1) // CHUNK
    excl = lambda k: (jnp.arange(k)[:, None] < jnp.arange(k)[None, :]).astype(F32)
    dot = functools.partial(jnp.dot, precision=lax.Precision.HIGHEST)
    cchf = cch.astype(F32)
    soff = dot(cchf, excl(N_EXPERTS)).astype(jnp.int32)
    tot_tile = jnp.sum(cch, axis=1)
    tot_e = jnp.sum(cch, axis=0)
    reg = (tot_e + CHUNKS_PER_BLOCK - 1) // CHUNKS_PER_BLOCK * CHUNKS_PER_BLOCK
    pstart = dot(reg.astype(F32)[None, :], excl(N_EXPERTS))[0].astype(jnp.int32)
    before = dot(excl(nt).T, cchf).astype(jnp.int32)
    dch = pstart[None, :] + before
    nb_used = jnp.sum(reg) // CHUNKS_PER_BLOCK
    soffrow = jnp.pad((soff * CHUNK).astype(F32), ((0, 0), (0, LANES - N_EXPERTS)))[:, None, :]
    i32 = lambda a: a.reshape(-1).astype(jnp.int32)
    return dict(cch=i32(cch), soff=i32(soff), dch=i32(dch), tot=i32(tot_tile),
                padst=i32(pstart + tot_e), padn=i32(reg - tot_e), nb_used=i32(nb_used),
                pst=i32(pstart * CHUNK), ntile=i32((reg * CHUNK + EXP_TILE - 1) // EXP_TILE),
                soffrow=soffrow)


def _rope_tables(seq):
    inv = ROPE_THETA ** (-jnp.arange(0, HEAD_DIM, 2, dtype=F32) / HEAD_DIM)
    ang = jnp.arange(seq, dtype=F32)[:, None] * inv[None, :]
    cos, sin = jnp.cos(ang), jnp.sin(ang)
    return jnp.concatenate([cos, cos], axis=1), jnp.concatenate([-sin, sin], axis=1)


@jax.jit
def kernel(x, w_in, w_sink, w_conv, b_conv, w_rec_gate, b_rec_gate, w_in_gate, b_in_gate,
           lru_lambda, w_attn_o, w_rnn_o, w_out, ln_g, ln_b, w_router_group, b_router_group,
           w_router_expert, b_router_expert, w_exp_gate, w_exp_up, w_exp_down):
    bsz, seq, d = x.shape
    depth = w_in.shape[0]
    t = bsz * seq
    nblk = d // LANES
    alpha = (2 * depth) ** 0.25
    cosf, sinf = _rope_tables(seq)
    n_rows = _max_blocks(t) * MOE_BLOCK + (EXP_TILE - MOE_BLOCK)
    x2 = x.reshape(t, d)
    n_branch = w_in.shape[2] - 2 * d
    w_branch = [w_in[l, :, :n_branch].astype(BF16) for l in range(depth)]
    q, k, v, xr, yr = _inproj(x2, w_branch[0], cosf, sinf, seq)
    for l in range(depth):
        o = _attention(q.reshape(bsz, seq, -1), k.reshape(bsz, seq, -1), v.reshape(bsz, seq, -1),
                       w_sink[l])
        wg_cat = (0.5 * jnp.concatenate([w_rec_gate[l, 0], w_in_gate[l, 0], w_rec_gate[l, 1],
                                         w_in_gate[l, 1]], axis=-1)).astype(BF16)
        bg_cat = jnp.concatenate([b_rec_gate[l, 0].reshape(nblk, 1, LANES),
                                  b_in_gate[l, 0].reshape(nblk, 1, LANES),
                                  b_rec_gate[l, 1].reshape(nblk, 1, LANES),
                                  b_in_gate[l, 1].reshape(nblk, 1, LANES)], axis=-1)
        lam_cat = jnp.concatenate([lru_lambda[l, 0].reshape(nblk, 1, LANES),
                                   lru_lambda[l, 1].reshape(nblk, 1, LANES)], axis=-1)
        hy = _rnn(xr.reshape(bsz, seq, d), yr.reshape(bsz, seq, d), w_conv[l],
                  b_conv[l].reshape(nblk, 1, LANES), wg_cat, bg_cat, lam_cat)
        lane_pad = LANES - N_EXPERTS - N_GROUPS
        wr = jnp.concatenate([w_router_expert[l], w_router_group[l],
                              jnp.zeros((d, lane_pad), F32)], axis=1).astype(BF16)
        br = jnp.concatenate([b_router_expert[l], b_router_group[l],
                              jnp.zeros((lane_pad,), F32)])[None, :]
        x1, x1b, route, cnt = _mixout(x2, o.reshape(t, -1), hy.reshape(t, d),
                                 w_in[l, :, n_branch:].astype(BF16),
                                 w_attn_o[l].astype(BF16), w_rnn_o[l].astype(BF16),
                                 w_out[l].astype(BF16), ln_g[l, 0].reshape(1, d),
                                 ln_b[l, 0].reshape(1, d), wr, br, alpha)
        tb = _dispatch_tables(cnt)
        xb = _dispatch(x1b, route, tb["soffrow"],
                       (tb["cch"], tb["soff"], tb["dch"], tb["tot"], tb["padst"], tb["padn"],
                        tb["nb_used"]), n_rows)
        yb = _experts(xb, tb["pst"], tb["ntile"], tb["nb_used"], w_exp_gate, w_exp_up,
                      w_exp_down, l)
        next_proj = (w_branch[l + 1], cosf, sinf, seq) if l + 1 < depth else None
        x2, *nxt = _combine(x1, route, tb["soffrow"],
                            (tb["cch"], tb["soff"], tb["dch"], tb["tot"]), yb,
                            ln_g[l, 1].reshape(1, d), ln_b[l, 1].reshape(1, d), alpha, next_proj)
        if nxt:
            q, k, v, xr, yr = nxt
    return x2.reshape(bsz, seq, d)
```

```python
import functools
import math

import jax
import jax.numpy as jnp
from jax import lax
from jax.experimental import pallas as pl
from jax.experimental.pallas import tpu as pltpu

F32 = jnp.float32
BF16 = jnp.bfloat16

HEAD_DIM = 128
N_Q_HEADS = 8
N_KV_HEADS = 2
Q_PER_KV = N_Q_HEADS // N_KV_HEADS
WINDOW = 128
ROPE_THETA = 10000.0
CONV_W = 4
LRU_C = 8.0
N_GROUPS = 4
EXPERTS_PER_GROUP = 8
N_EXPERTS = N_GROUPS * EXPERTS_PER_GROUP
LN_EPS = 1e-5
NEG = -1e30
LOG2E = math.log2(math.e)

SUBLANES = 8
LANES = 128
V7X_VMEM_BYTES = 64 * 1024 * 1024

TM_PROJ = 512
TQ_ATTN = 1024
ATTN_STACK = Q_PER_KV
SEG = 256
SCAN_LEN = SEG + 4
SCAN_UNROLL = 10
BF16_ROWS = 2 * SUBLANES
TD = 512
MIX_ROWS = 256
CHUNK = BF16_ROWS
WAIT_BATCH = 8
COPY_RUN = 4
STAGE_ROWS = 2 * TD + CHUNK * N_EXPERTS
MOE_BLOCK = 128
CHUNKS_PER_BLOCK = MOE_BLOCK // CHUNK
EXP_TILE = 4 * MOE_BLOCK
EXP_BUFS = 6
VMEM_LIMIT = V7X_VMEM_BYTES * 7 // 8


def _cparams(n_axes):
    return pltpu.CompilerParams(dimension_semantics=("arbitrary",) * n_axes,
                                vmem_limit_bytes=VMEM_LIMIT)


def _softplus(z):
    e = jnp.exp(-jnp.abs(z))
    w = 1.0 + e
    tiny = w == 1.0
    log1p = jnp.where(tiny, e, jnp.log(w) * (e / jnp.where(tiny, 1.0, w - 1.0)))
    return jnp.maximum(z, 0.0) + log1p


def _sigmoid(x):
    return 0.5 * jnp.tanh(0.5 * x) + 0.5


def _gelu_tanh(y):
    c1 = math.sqrt(2.0 / math.pi)
    half = 0.5 * y
    return half + half * jnp.tanh(y * (c1 + (c1 * 0.044715) * (y * y)))


def _layer_norm(y, g, b):
    mu = jnp.mean(y, axis=-1, keepdims=True)
    d = y - mu
    var = jnp.mean(d * d, axis=-1, keepdims=True)
    return d * lax.rsqrt(var + LN_EPS) * g + b


def _project_tile(xb, w_ref, cos, sin, q_ref, k_ref, v_ref, xr_ref, yr_ref):
    aw = N_Q_HEADS * HEAD_DIM
    kw = N_KV_HEADS * HEAD_DIM
    d_model = xr_ref.shape[1]

    def proj(c0, n):
        return jnp.dot(xb, w_ref[:, c0:c0 + n], preferred_element_type=F32)

    def rope(t):
        return t * cos + pltpu.roll(t, HEAD_DIM // 2, 1) * sin

    zq = proj(0, aw)
    scale = HEAD_DIM ** -0.5 * LOG2E
    for h in range(N_Q_HEADS):
        sl = slice(h * HEAD_DIM, (h + 1) * HEAD_DIM)
        q_ref[:, sl] = (rope(zq[:, sl]) * scale).astype(BF16)
    zk = proj(aw, kw)
    for h in range(N_KV_HEADS):
        sl = slice(h * HEAD_DIM, (h + 1) * HEAD_DIM)
        k_ref[:, sl] = rope(zk[:, sl]).astype(BF16)
    v_ref[...] = proj(aw + kw, kw).astype(BF16)
    c0 = aw + 2 * kw
    xr_ref[...] = proj(c0, d_model)
    yr_ref[...] = proj(c0 + d_model, d_model)


def _inproj_kernel(x_ref, w_ref, cos_ref, sin_ref, q_ref, k_ref, v_ref, xr_ref, yr_ref):
    _project_tile(x_ref[...].astype(BF16), w_ref, cos_ref[...], sin_ref[...],
                  q_ref, k_ref, v_ref, xr_ref, yr_ref)


def _proj_out_specs(t, d, tm, row):
    aw = N_Q_HEADS * HEAD_DIM
    kw = N_KV_HEADS * HEAD_DIM
    shapes = [jax.ShapeDtypeStruct((t, aw), BF16), jax.ShapeDtypeStruct((t, kw), BF16),
              jax.ShapeDtypeStruct((t, kw), BF16)] + [jax.ShapeDtypeStruct((t, d), F32)] * 2
    specs = [pl.BlockSpec((tm, aw), row), pl.BlockSpec((tm, kw), row),
             pl.BlockSpec((tm, kw), row)] + [pl.BlockSpec((tm, d), row)] * 2
    return shapes, specs


def _inproj(x2, w_in_b, cosf, sinf, seq):
    t, d = x2.shape
    n_in = w_in_b.shape[1]
    tm = TM_PROJ
    tiles_per_seq = seq // tm
    row = lambda i: (i, 0)
    pos = lambda i: (i % tiles_per_seq, 0)
    shapes, specs = _proj_out_specs(t, d, tm, row)
    return pl.pallas_call(
        _inproj_kernel,
        grid=(t // tm,),
        in_specs=[pl.BlockSpec((tm, d), row),
                  pl.BlockSpec((d, n_in), lambda i: (0, 0)),
                  pl.BlockSpec((tm, HEAD_DIM), pos),
                  pl.BlockSpec((tm, HEAD_DIM), pos)],
        out_specs=specs,
        out_shape=shapes,
        compiler_params=_cparams(1),
        name="inproj",
    )(x2, w_in_b, cosf, sinf)


def _attn_kernel(sink_ref, q_ref, kp_ref, kc_ref, kn_ref, vp_ref, vc_ref, vn_ref, o_ref,
                 *, seq):
    i = pl.program_id(1)
    tq = q_ref.shape[1]
    blk = WINDOW
    t0 = i * tq
    kext = jnp.concatenate([kp_ref[0], kc_ref[0], kn_ref[0]], axis=0)
    vext = jnp.concatenate([vp_ref[0], vc_ref[0], vn_ref[0]], axis=0)
    nrow = ATTN_STACK * blk
    blk_shift = blk.bit_length() - 1
    qi = lax.broadcasted_iota(jnp.int32, (nrow, 3 * blk), 0) & (blk - 1)
    kj = lax.broadcasted_iota(jnp.int32, (nrow, 3 * blk), 1)
    band = jnp.where(jnp.abs(kj - blk - qi) <= WINDOW, 0.0, NEG)
    kj_row = lax.broadcasted_iota(jnp.int32, (1, 3 * blk), 1)
    rowg = lax.broadcasted_iota(jnp.int32, (nrow, 1), 0) >> blk_shift
    sinks = []
    for h0 in range(0, N_Q_HEADS, ATTN_STACK):
        sk = jnp.full((nrow, 1), sink_ref[h0], F32)
        for g in range(1, ATTN_STACK):
            sk = jnp.where(rowg == g, sink_ref[h0 + g], sk)
        sinks.append(sk * LOG2E)
    for j in range(tq // blk):
        kpos = t0 + j * blk - blk + kj_row
        bias = band + jnp.where((kpos >= 0) & (kpos < seq), 0.0, NEG)
        for h0 in range(0, N_Q_HEADS, ATTN_STACK):
            kv = h0 // Q_PER_KV
            hs = slice(kv * HEAD_DIM, (kv + 1) * HEAD_DIM)
            kblk = kext[j * blk:j * blk + 3 * blk, hs]
            vblk = vext[j * blk:j * blk + 3 * blk, hs]
            qs = [q_ref[0, j * blk:(j + 1) * blk, (h0 + g) * HEAD_DIM:(h0 + g + 1) * HEAD_DIM]
                  for g in range(ATTN_STACK)]
            qblk = jnp.concatenate(qs, axis=0)
            s = lax.dot_general(qblk, kblk, (((1,), (1,)), ((), ())),
                                preferred_element_type=F32) + bias
            sk = sinks[h0 // ATTN_STACK]
            m = jnp.maximum(jnp.max(s, axis=-1, keepdims=True), sk)
            p = jnp.exp2(s - m)
            denom = jnp.sum(p, axis=-1, keepdims=True) + jnp.exp2(sk - m)
            o = jnp.dot(p.astype(BF16), vblk, preferred_element_type=F32) / denom
            for g in range(ATTN_STACK):
                c = (h0 + g) * HEAD_DIM
                o_ref[0, j * blk:(j + 1) * blk, c:c + HEAD_DIM] = (
                    o[g * blk:(g + 1) * blk].astype(BF16))


def _attention(q3, k3, v3, sink):
    b, s, aw = q3.shape
    kw = k3.shape[2]
    tq = TQ_ATTN
    blk = WINDOW
    r = tq // blk
    nblk = s // blk
    cur = lambda bi, i: (bi, i, 0)
    prev = lambda bi, i: (bi, jnp.maximum(i * r - 1, 0), 0)
    nxt = lambda bi, i: (bi, jnp.minimum((i + 1) * r, nblk - 1), 0)
    kv_specs = [pl.BlockSpec((1, blk, kw), prev), pl.BlockSpec((1, tq, kw), cur),
                pl.BlockSpec((1, blk, kw), nxt)]
    return pl.pallas_call(
        functools.partial(_attn_kernel, seq=s),
        grid=(b, s // tq),
        in_specs=[pl.BlockSpec(memory_space=pltpu.SMEM),
                  pl.BlockSpec((1, tq, aw), cur)] + kv_specs + kv_specs,
        out_specs=pl.BlockSpec((1, tq, aw), cur),
        out_shape=jax.ShapeDtypeStruct((b, s, aw), BF16),
        compiler_params=_cparams(2),
        name="attention",
    )(sink, q3, k3, k3, k3, v3, v3, v3)


def _rnn_kernel(xr_ref, yr_ref, wc_ref, bc_ref, wg_ref, bg_ref, lam_ref, out_ref,
                xpad_ref, af_ref, uf_ref, ab_ref, ub_ref, hb_ref, cf_ref, cb_ref, *, seq):
    nseg = seq // SEG
    nlane = nseg
    ngrp = nlane // SUBLANES
    hf_ref = xpad_ref
    wc = wc_ref[...]
    bc = bc_ref[0]
    wg = wg_ref[0]
    bg = 0.5 * bg_ref[0]
    lam = lam_ref[0]
    rate = (0.5 * LRU_C) * _softplus(-lam)
    a_refs = (af_ref, ab_ref)
    u_refs = (uf_ref, ub_ref)
    left = CONV_W // 2

    halo = jnp.zeros((SUBLANES, LANES), F32)
    xpad_ref[pl.ds(0, SUBLANES), :] = halo
    xpad_ref[pl.ds(seq + SUBLANES, SUBLANES), :] = halo

    def pad_copy(c, carry):
        t0 = pl.multiple_of(c * SEG, SEG)
        xpad_ref[pl.ds(t0 + SUBLANES, SEG), :] = xr_ref[0, pl.ds(t0, SEG), :]
        return carry

    lax.fori_loop(0, nseg, pad_copy, 0, unroll=4)

    tail = nlane * SCAN_LEN - seq
    for a_ref, u_ref in zip(a_refs, u_refs):
        a_ref[pl.ds(seq, tail), :] = jnp.ones((tail, LANES), F32)
        u_ref[pl.ds(seq, tail), :] = jnp.zeros((tail, LANES), F32)

    def gates(c, carry):
        t0 = pl.multiple_of(c * SEG, SEG)
        xc = bc
        for tap in range(CONV_W):
            xc = xc + xpad_ref[pl.ds(t0 + SUBLANES - left + tap, SEG), :] * wc[tap:tap + 1]
        gh = jnp.dot(xc.astype(BF16), wg, preferred_element_type=F32) + bg
        xch = 0.5 * xc
        for d in range(2):
            rt = rate[:, d * LANES:(d + 1) * LANES]
            nlog_a = rt * jnp.tanh(gh[:, (2 * d) * LANES:(2 * d + 1) * LANES]) + rt
            a = jnp.exp2(nlog_a * (-1.0 / math.log(2.0)))
            z = jnp.tanh(nlog_a) * (a * a + 1.0)
            mult = jnp.where(z > 0.0, z * lax.rsqrt(z), 0.0)
            in_gate2 = jnp.tanh(gh[:, (2 * d + 1) * LANES:(2 * d + 2) * LANES]) + 1.0
            a_refs[d][pl.ds(t0, SEG), :] = a
            u_refs[d][pl.ds(t0, SEG), :] = (xch * mult) * in_gate2
        return carry

    lax.fori_loop(0, nseg, gates, 0, unroll=16)

    def lane_rows(g, j):
        return pl.ds(g * SUBLANES * SCAN_LEN + j, SUBLANES, stride=SCAN_LEN)

    def totals_step(j, carry):
        hf, pf, hb, pb = carry
        jb = SCAN_LEN - 1 - j
        nhf, npf, nhb, npb = [], [], [], []
        for g in range(ngrp):
            a = af_ref[lane_rows(g, j), :]
            nhf.append(a * hf[g] + uf_ref[lane_rows(g, j), :])
            npf.append(a * pf[g])
            a = ab_ref[lane_rows(g, jb), :]
            nhb.append(a * hb[g] + ub_ref[lane_rows(g, jb), :])
            npb.append(a * pb[g])
        return tuple(nhf), tuple(npf), tuple(nhb), tuple(npb)

    zero = tuple(jnp.zeros((SUBLANES, LANES), F32) for _ in range(ngrp))
    one = tuple(jnp.ones((SUBLANES, LANES), F32) for _ in range(ngrp))
    hf, pf, hb, pb = lax.fori_loop(0, SCAN_LEN, totals_step, (zero, one, zero, one),
                                   unroll=SCAN_UNROLL)

    c = jnp.zeros((1, LANES), F32)
    for s in range(nlane):
        g, r = divmod(s, SUBLANES)
        cf_ref[s:s + 1, :] = c
        c = pf[g][r:r + 1] * c + hf[g][r:r + 1]
    c = jnp.zeros((1, LANES), F32)
    for s in range(nlane - 1, -1, -1):
        g, r = divmod(s, SUBLANES)
        cb_ref[s:s + 1, :] = c
        c = pb[g][r:r + 1] * c + hb[g][r:r + 1]

    def scan_step(j, carry):
        hf, hb = carry
        jb = SCAN_LEN - 1 - j
        nhf, nhb = [], []
        for g in range(ngrp):
            h = af_ref[lane_rows(g, j), :] * hf[g] + uf_ref[lane_rows(g, j), :]
            hf_ref[lane_rows(g, j), :] = h
            nhf.append(h)
            h = ab_ref[lane_rows(g, jb), :] * hb[g] + ub_ref[lane_rows(g, jb), :]
            hb_ref[lane_rows(g, jb), :] = h
            nhb.append(h)
        return tuple(nhf), tuple(nhb)

    hf0 = tuple(cf_ref[g * SUBLANES:(g + 1) * SUBLANES, :] for g in range(ngrp))
    hb0 = tuple(cb_ref[g * SUBLANES:(g + 1) * SUBLANES, :] for g in range(ngrp))
    lax.fori_loop(0, SCAN_LEN, scan_step, (hf0, hb0), unroll=SCAN_UNROLL)

    def finish(c, carry):
        t0 = pl.multiple_of(c * SEG, SEG)
        h = hf_ref[pl.ds(t0, SEG), :] + hb_ref[pl.ds(t0, SEG), :]
        out_ref[0, pl.ds(t0, SEG), :] = (h * _gelu_tanh(yr_ref[0, pl.ds(t0, SEG), :])).astype(BF16)
        return carry

    lax.fori_loop(0, nseg, finish, 0, unroll=4)


def _rnn(xr3, yr3, w_conv, b_conv, wg_cat, bg_cat, lam_cat):
    b, s, d = xr3.shape
    nblk = d // LANES
    nseg = s // SEG
    slab = lambda bi, n: (bi, 0, n)
    per_blk = lambda bi, n: (n, 0, 0)
    assert nseg % SUBLANES == 0 and nseg * SCAN_LEN >= s + 2 * SUBLANES
    scratch = ([pltpu.VMEM((nseg * SCAN_LEN, LANES), F32)] * 6
               + [pltpu.VMEM((nseg, LANES), F32)] * 2)
    return pl.pallas_call(
        functools.partial(_rnn_kernel, seq=s),
        grid=(b, nblk),
        in_specs=[pl.BlockSpec((1, s, LANES), slab),
                  pl.BlockSpec((1, s, LANES), slab),
                  pl.BlockSpec((CONV_W, LANES), lambda bi, n: (0, n)),
                  pl.BlockSpec((1, 1, LANES), per_blk),
                  pl.BlockSpec((1, LANES, 4 * LANES), per_blk),
                  pl.BlockSpec((1, 1, 4 * LANES), per_blk),
                  pl.BlockSpec((1, 1, 2 * LANES), per_blk)],
        out_specs=pl.BlockSpec((1, s, LANES), slab),
        out_shape=jax.ShapeDtypeStruct((b, s, d), BF16),
        scratch_shapes=scratch,
        compiler_params=_cparams(2),
        name="rglru",
    )(xr3, yr3, w_conv, b_conv, wg_cat, bg_cat, lam_cat)


def _route(logits):
    lane = lax.broadcasted_iota(jnp.int32, logits.shape, 1)
    lanef = lane.astype(F32)
    big = float(4 * LANES)
    gmask = (lane >= N_EXPERTS) & (lane < N_EXPERTS + N_GROUPS)
    gl = jnp.where(gmask, logits, NEG)
    gmax = jnp.max(gl, axis=-1, keepdims=True)
    ge = jnp.exp(gl - gmax)
    gprob = ge / jnp.sum(ge, axis=-1, keepdims=True)
    gval = jnp.max(gprob, axis=-1, keepdims=True)
    gidx = jnp.min(jnp.where((gprob == gval) & gmask, lanef, big), axis=-1, keepdims=True)
    gidx = gidx.astype(jnp.int32) - N_EXPERTS
    group_shift = EXPERTS_PER_GROUP.bit_length() - 1
    emask = (lane < N_EXPERTS) & ((lane >> group_shift) == gidx)
    el = jnp.where(emask, logits, NEG)
    m1 = jnp.max(el, axis=-1, keepdims=True)
    i1 = jnp.min(jnp.where((el == m1) & emask, lanef, big), axis=-1, keepdims=True)
    emask2 = emask & (lanef != i1)
    el2 = jnp.where(emask2, logits, NEG)
    m2 = jnp.max(el2, axis=-1, keepdims=True)
    i2 = jnp.min(jnp.where((el2 == m2) & emask2, lanef, big), axis=-1, keepdims=True)
    e2 = jnp.exp(m2 - m1)
    den = 1.0 + e2
    g1 = (1.0 / den) * gval
    g2 = (e2 / den) * gval
    route = jnp.where(lane == 0, i1, jnp.where(lane == 1, i2,
                      jnp.where(lane == 2, g1, jnp.where(lane == 3, g2, 0.0))))
    onehot = (lanef == i1).astype(F32) + (lanef == i2).astype(F32)
    return route, jnp.sum(onehot, axis=0, keepdims=True)


def _mixout_kernel(x_ref, o_ref, hy_ref, wmg_ref, wao_ref, wro_ref, wout_ref,
                   lng_ref, lnb_ref, wr_ref, br_ref, x1_ref, x1b_ref, route_ref, cnt_ref,
                   logits_ref, *, alpha):
    @pl.when(pl.program_id(0) == 0)
    def _():
        logits_ref[...] = jnp.zeros_like(logits_ref)

    route, cnt = _route(logits_ref[...])
    route_ref[...] = route
    cnt_ref[0] = cnt

    d = x_ref.shape[1]
    for r0 in range(0, x_ref.shape[0], MIX_ROWS):
        rows = pl.ds(r0, MIX_ROWS)
        x = x_ref[rows, :]
        xb = x.astype(BF16)
        ga = jnp.dot(xb, wmg_ref[:, :d], preferred_element_type=F32)
        gr = jnp.dot(xb, wmg_ref[:, d:], preferred_element_type=F32)
        ya = jnp.dot(o_ref[rows, :], wao_ref[...], preferred_element_type=F32)
        yr = jnp.dot(hy_ref[rows, :], wro_ref[...], preferred_element_type=F32)
        merged = _sigmoid(ga) * ya + _sigmoid(gr) * yr
        mix = jnp.dot(merged.astype(BF16), wout_ref[...], preferred_element_type=F32)
        x1 = _layer_norm(alpha * x + mix, lng_ref[...], lnb_ref[...])
        x1_ref[rows, :] = x1
        x1b = x1.astype(BF16)
        x1b_ref[rows, :] = x1b
        logits_ref[rows, :] = (jnp.dot(x1b, wr_ref[...], preferred_element_type=F32)
                               + br_ref[...])


def _mixout(x2, o2, hy2, wmg, wao, wro, wout, lng, lnb, wr, br, alpha):
    t, d = x2.shape
    tm = TD
    nt = t // tm
    row = lambda i: (jnp.minimum(i, nt - 1), 0)
    prev_row = lambda i: (jnp.maximum(i - 1, 0), 0)
    full = lambda i: (0, 0)
    return pl.pallas_call(
        functools.partial(_mixout_kernel, alpha=alpha),
        grid=(nt + 1,),
        in_specs=[pl.BlockSpec((tm, d), row)] * 3
                 + [pl.BlockSpec((d, 2 * d), full)]
                 + [pl.BlockSpec((d, d), full)] * 3
                 + [pl.BlockSpec((1, d), full)] * 2
                 + [pl.BlockSpec((d, LANES), full), pl.BlockSpec((1, LANES), full)],
        out_specs=[pl.BlockSpec((tm, d), row), pl.BlockSpec((tm, d), row),
                   pl.BlockSpec((tm, LANES), prev_row),
                   pl.BlockSpec((1, 1, LANES), lambda i: (jnp.maximum(i - 1, 0), 0, 0))],
        out_shape=[jax.ShapeDtypeStruct((t, d), F32), jax.ShapeDtypeStruct((t, d), BF16),
                   jax.ShapeDtypeStruct((t, LANES), F32),
                   jax.ShapeDtypeStruct((nt, 1, LANES), F32)],
        scratch_shapes=[pltpu.VMEM((tm, LANES), F32)],
        compiler_params=_cparams(1),
        name="mixout",
    )(x2, o2, hy2, wmg, wao, wro, wout, lng, lnb, wr, br)


def _perm_matrix(route, soff_row, w1, w2):
    td = route.shape[0]
    lane = lax.broadcasted_iota(jnp.int32, (td, LANES), 1).astype(F32)
    e1 = lane == route[:, 0:1]
    e2 = lane == route[:, 1:2]
    cnt = (e1.astype(F32) + e2.astype(F32)).astype(BF16)
    ti = lax.broadcasted_iota(jnp.int32, (td, td), 0)
    tj = lax.broadcasted_iota(jnp.int32, (td, td), 1)
    lower = (tj < ti).astype(BF16)
    pos = jnp.dot(lower, cnt, preferred_element_type=F32) + soff_row
    r1 = jnp.sum(jnp.where(e1, pos, 0.0), axis=-1, keepdims=True).astype(jnp.int32)
    r2 = jnp.sum(jnp.where(e2, pos, 0.0), axis=-1, keepdims=True).astype(jnp.int32)
    col = lax.broadcasted_iota(jnp.int32, (td, STAGE_ROWS), 1)
    return jnp.where(col == r1, w1, jnp.where(col == r2, w2, 0.0))


def _chunk_rows(c, k=1):
    return pl.ds(pl.multiple_of(c * CHUNK, CHUNK), k * CHUNK)


def _block_rows(b):
    return pl.ds(pl.multiple_of(b * MOE_BLOCK, MOE_BLOCK), MOE_BLOCK)


def _for_each_run(i, cch_ref, soff_ref, dch_ref, fn):
    assert COPY_RUN == 4

    def per_expert(e, carry):
        idx = i * N_EXPERTS + e
        so = soff_ref[idx]
        do = dch_ref[idx]
        n = cch_ref[idx]
        full = n // COPY_RUN

        def per_run(c, carry2):
            fn(so + c * COPY_RUN, do + c * COPY_RUN, COPY_RUN)
            return carry2

        lax.fori_loop(0, full, per_run, 0)
        done = full * COPY_RUN
        two = n & 2

        @pl.when(two != 0)
        def _():
            fn(so + done, do + done, 2)

        @pl.when((n & 1) != 0)
        def _():
            fn(so + done + two, do + done + two, 1)

        return carry

    lax.fori_loop(0, N_EXPERTS, per_expert, 0)


def _repeat(n, fn):
    def body(c, carry):
        fn()
        return carry

    lax.fori_loop(0, n, body, 0)


def _drain_chunks(n, make_copy):
    _repeat(n // WAIT_BATCH, lambda: make_copy(WAIT_BATCH).wait())
    _repeat(n % WAIT_BATCH, lambda: make_copy(1).wait())


def _dispatch_kernel(cch_ref, soff_ref, dch_ref, tot_ref, padst_ref, padn_ref, nb_ref,
                     x_ref, route_ref, soffrow_ref, xb_ref, stage_ref, zero_ref, sems, zsem):
    i = pl.program_id(0)
    nt = pl.num_programs(0)
    n_blocks = xb_ref.shape[0] // MOE_BLOCK
    slot = i % 2

    def copy(s, src_chunk, dst_chunk, k):
        return pltpu.make_async_copy(stage_ref.at[s, _chunk_rows(src_chunk, k)],
                                     xb_ref.at[_chunk_rows(dst_chunk, k)], sems.at[s])

    def many(s):
        return lambda k: pltpu.make_async_copy(stage_ref.at[s, pl.ds(0, k * CHUNK)],
                                               xb_ref.at[pl.ds(0, k * CHUNK)], sems.at[s])

    @pl.when(i >= 2)
    def _():
        _drain_chunks(tot_ref[jnp.maximum(i - 2, 0)], many(slot))

    pt = _perm_matrix(route_ref[...], soffrow_ref[0], 1.0, 1.0).astype(BF16)
    stage_ref[slot] = lax.dot_general(pt, x_ref[...], (((0,), (0,)), ((), ())),
                                      preferred_element_type=F32).astype(BF16)
    _for_each_run(i, cch_ref, soff_ref, dch_ref,
                  lambda s, d, k: copy(slot, s, d, k).start())

    @pl.when(i == nt - 1)
    def _():
        @pl.when(i >= 1)
        def _():
            _drain_chunks(tot_ref[jnp.maximum(i - 1, 0)], many(1 - slot))

        _drain_chunks(tot_ref[i], many(slot))

        zero_ref[...] = jnp.zeros_like(zero_ref)

        def zcopy(dst_chunk):
            return pltpu.make_async_copy(zero_ref.at[pl.ds(0, CHUNK)],
                                         xb_ref.at[_chunk_rows(dst_chunk)], zsem)

        def pad_start(e, n):
            def pad_chunk(c, carry2):
                zcopy(padst_ref[e] + c).start()
                return carry2

            lax.fori_loop(0, padn_ref[e], pad_chunk, 0)
            return n + padn_ref[e]

        npad = lax.fori_loop(0, N_EXPERTS, pad_start, 0)
        _repeat(npad, lambda: zcopy(0).wait())

        def zblock(b):
            return pltpu.make_async_copy(zero_ref, xb_ref.at[_block_rows(b)], zsem)

        def start_block(b, carry):
            zblock(b).start()
            return carry

        def wait_block(b, carry):
            zblock(b).wait()
            return carry

        lax.fori_loop(nb_ref[0], n_blocks, start_block, 0)
        lax.fori_loop(nb_ref[0], n_blocks, wait_block, 0)


def _dispatch(x1, route, soffrow, tables, n_rows):
    t, d = x1.shape
    nt = t // TD
    grid_spec = pltpu.PrefetchScalarGridSpec(
        num_scalar_prefetch=7,
        grid=(nt,),
        in_specs=[pl.BlockSpec((TD, d), lambda i, *_: (i, 0)),
                  pl.BlockSpec((TD, LANES), lambda i, *_: (i, 0)),
                  pl.BlockSpec((1, 1, LANES), lambda i, *_: (i, 0, 0))],
        out_specs=pl.BlockSpec(memory_space=pl.ANY),
        scratch_shapes=[pltpu.VMEM((2, STAGE_ROWS, d), BF16), pltpu.VMEM((MOE_BLOCK, d), BF16),
                        pltpu.SemaphoreType.DMA((2,)), pltpu.SemaphoreType.DMA(())],
    )
    return pl.pallas_call(
        _dispatch_kernel,
        grid_spec=grid_spec,
        out_shape=jax.ShapeDtypeStruct((n_rows, d), BF16),
        compiler_params=_cparams(1),
        name="dispatch",
    )(*tables, x1, route, soffrow)


def _experts_kernel(pst_ref, ntile_ref, nb_ref, xb_ref, wg_ref, wu_ref, wd_ref, yb_ref,
                    wgb_ref, wub_ref, wdb_ref, xbuf_ref, ybuf_ref, in_sems, out_sems, zsem):
    e = pl.program_id(0)
    ne = pl.num_programs(0)
    n = ntile_ref[e]
    row0 = pst_ref[e]
    n_blocks = yb_ref.shape[0] // MOE_BLOCK
    ahead = EXP_BUFS - 1

    def tile_rows(r0, t):
        return pl.ds(pl.multiple_of(r0 + t * EXP_TILE, MOE_BLOCK), EXP_TILE)

    def in_copy(r0, t, s):
        return pltpu.make_async_copy(xb_ref.at[tile_rows(r0, t)], xbuf_ref.at[s], in_sems.at[s])

    def out_copy(t, s):
        return pltpu.make_async_copy(ybuf_ref.at[s], yb_ref.at[tile_rows(row0, t)],
                                     out_sems.at[s])

    def start_head(r0, count):
        for k in range(ahead):
            @pl.when(k < count)
            def _():
                in_copy(r0, k, k).start()

    @pl.when(e == 0)
    def _():
        start_head(row0, n)

    @pl.when(n > 0)
    def _():
        wgb_ref[...] = wg_ref[0, 0].astype(BF16)
        wub_ref[...] = wu_ref[0, 0].astype(BF16)
        wdb_ref[...] = wd_ref[0, 0].astype(BF16)

    def tile(t, carry):
        s = t % EXP_BUFS

        @pl.when(t + ahead < n)
        def _():
            in_copy(row0, t + ahead, (t + ahead) % EXP_BUFS).start()

        in_copy(row0, t, s).wait()

        @pl.when(t >= EXP_BUFS)
        def _():
            out_copy(t - EXP_BUFS, s).wait()

        x = xbuf_ref[s]
        hg = jnp.dot(x, wgb_ref[...], preferred_element_type=F32)
        hu = jnp.dot(x, wub_ref[...], preferred_element_type=F32)
        hid = (hg * _sigmoid(hg)) * hu
        ybuf_ref[s] = jnp.dot(hid.astype(BF16), wdb_ref[...],
                              preferred_element_type=F32).astype(BF16)
        out_copy(t, s).start()
        return carry

    lax.fori_loop(0, n, tile, 0)

    @pl.when(e + 1 < ne)
    def _():
        nxt = jnp.minimum(e + 1, ne - 1)
        start_head(pst_ref[nxt], ntile_ref[nxt])

    for k in range(EXP_BUFS):
        @pl.when(n > k)
        def _():
            out_copy(n - 1 - k, (n - 1 - k) % EXP_BUFS).wait()

    @pl.when(e == ne - 1)
    def _():
        ybuf_ref[0] = jnp.zeros(ybuf_ref.shape[1:], BF16)

        def zblock(b):
            return pltpu.make_async_copy(ybuf_ref.at[0, pl.ds(0, MOE_BLOCK)],
                                         yb_ref.at[_block_rows(b)], zsem)

        def start_block(b, carry):
            zblock(b).start()
            return carry

        def wait_block(b, carry):
            zblock(b).wait()
            return carry

        lax.fori_loop(nb_ref[0], n_blocks, start_block, 0)
        lax.fori_loop(nb_ref[0], n_blocks, wait_block, 0)


def _experts(xb, pst, ntile, nb_used, w_g, w_u, w_d, layer):
    n_rows, d = xb.shape
    n_exp, de = w_g.shape[1], w_g.shape[3]
    wsel = lambda e, *_: (layer, e, 0, 0)
    grid_spec = pltpu.PrefetchScalarGridSpec(
        num_scalar_prefetch=3,
        grid=(n_exp,),
        in_specs=[pl.BlockSpec(memory_space=pl.ANY),
                  pl.BlockSpec((1, 1, d, de), wsel),
                  pl.BlockSpec((1, 1, d, de), wsel),
                  pl.BlockSpec((1, 1, de, d), wsel)],
        out_specs=pl.BlockSpec(memory_space=pl.ANY),
        scratch_shapes=[pltpu.VMEM((d, de), BF16), pltpu.VMEM((d, de), BF16),
                        pltpu.VMEM((de, d), BF16),
                        pltpu.VMEM((EXP_BUFS, EXP_TILE, d), BF16),
                        pltpu.VMEM((EXP_BUFS, EXP_TILE, d), BF16),
                        pltpu.SemaphoreType.DMA((EXP_BUFS,)), pltpu.SemaphoreType.DMA((EXP_BUFS,)),
                        pltpu.SemaphoreType.DMA(())],
    )
    return pl.pallas_call(
        _experts_kernel,
        grid_spec=grid_spec,
        out_shape=jax.ShapeDtypeStruct((n_rows, d), BF16),
        compiler_params=_cparams(1),
        name="experts",
    )(pst, ntile, nb_used, xb, w_g, w_u, w_d)


def _combine_kernel(cch_ref, soff_ref, dch_ref, tot_ref,
                    x_ref, route_ref, soffrow_ref, yb_ref, lng_ref, lnb_ref, *rest,
                    alpha, project):
    if project:
        (w_ref, cos_ref, sin_ref, out_ref, q_ref, k_ref, v_ref, xr_ref, yr_ref,
         stage_ref, sems) = rest
    else:
        out_ref, stage_ref, sems = rest
    i = pl.program_id(0)
    nt = pl.num_programs(0)
    slot = i % 2

    def copy(s, src_chunk, dst_chunk, k):
        return pltpu.make_async_copy(yb_ref.at[_chunk_rows(src_chunk, k)],
                                     stage_ref.at[s, _chunk_rows(dst_chunk, k)], sems.at[s])

    def fetch(tile_idx, s):
        _for_each_run(tile_idx, cch_ref, soff_ref, dch_ref,
                      lambda so, do, k: copy(s, do, so, k).start())

    @pl.when(i == 0)
    def _():
        stage_ref[...] = jnp.zeros_like(stage_ref)
        fetch(i, slot)

    @pl.when(i + 1 < nt)
    def _():
        fetch(i + 1, 1 - slot)

    _drain_chunks(tot_ref[i], lambda k: pltpu.make_async_copy(
        yb_ref.at[pl.ds(0, k * CHUNK)], stage_ref.at[slot, pl.ds(0, k * CHUNK)], sems.at[slot]))

    route = route_ref[...]
    pt = _perm_matrix(route, soffrow_ref[0], route[:, 2:3], route[:, 3:4]).astype(BF16)
    ffn = jnp.dot(pt, stage_ref[slot], preferred_element_type=F32)
    out = _layer_norm(alpha * x_ref[...] + ffn, lng_ref[...], lnb_ref[...])
    out_ref[...] = out
    if project:
        _project_tile(out.astype(BF16), w_ref, cos_ref[...], sin_ref[...],
                      q_ref, k_ref, v_ref, xr_ref, yr_ref)


def _combine(x1, route, soffrow, tables, yb, lng, lnb, alpha, next_proj=None):
    t, d = x1.shape
    nt = t // TD
    row = lambda i, *_: (i, 0)
    full = lambda i, *_: (0, 0)
    in_specs = [pl.BlockSpec((TD, d), row),
                pl.BlockSpec((TD, LANES), row),
                pl.BlockSpec((1, 1, LANES), lambda i, *_: (i, 0, 0)),
                pl.BlockSpec(memory_space=pl.ANY),
                pl.BlockSpec((1, d), full),
                pl.BlockSpec((1, d), full)]
    out_shape = [jax.ShapeDtypeStruct((t, d), F32)]
    out_specs = [pl.BlockSpec((TD, d), row)]
    args = [x1, route, soffrow, yb, lng, lnb]
    if next_proj is not None:
        w_in_b, cosf, sinf, seq = next_proj
        tiles_per_seq = seq // TD
        pos = lambda i, *_: (i % tiles_per_seq, 0)
        in_specs += [pl.BlockSpec(w_in_b.shape, full), pl.BlockSpec((TD, HEAD_DIM), pos),
                     pl.BlockSpec((TD, HEAD_DIM), pos)]
        shapes, specs = _proj_out_specs(t, d, TD, row)
        out_shape += shapes
        out_specs += specs
        args += [w_in_b, cosf, sinf]
    grid_spec = pltpu.PrefetchScalarGridSpec(
        num_scalar_prefetch=4,
        grid=(nt,),
        in_specs=in_specs,
        out_specs=out_specs,
        scratch_shapes=[pltpu.VMEM((2, STAGE_ROWS, d), BF16), pltpu.SemaphoreType.DMA((2,))],
    )
    return pl.pallas_call(
        functools.partial(_combine_kernel, alpha=alpha, project=next_proj is not None),
        grid_spec=grid_spec,
        out_shape=out_shape,
        compiler_params=_cparams(1),
        name="combine",
    )(*tables, *args)


def _max_blocks(t):
    nt = t // TD
    rows = 2 * t + (CHUNK - 1) * nt * N_EXPERTS + (MOE_BLOCK - CHUNK) * N_EXPERTS
    return -(-rows // MOE_BLOCK)


def _dispatch_tables(cnt):
    n = cnt[:, 0, :N_EXPERTS].astype(jnp.int32)
    nt = n.shape[0]
    cch = (n + CHUNK - 1) // CHUNK
    excl = lambda k: (jnp.arange(k)[:, None] < jnp.arange(k)[None, :]).astype(F32)
    dot = functools.partial(jnp.dot, precision=lax.Precision.HIGHEST)
    cchf = cch.astype(F32)
    soff = dot(cchf, excl(N_EXPERTS)).astype(jnp.int32)
    tot_tile = jnp.sum(cch, axis=1)
    tot_e = jnp.sum(cch, axis=0)
    reg = (tot_e + CHUNKS_PER_BLOCK - 1) // CHUNKS_PER_BLOCK * CHUNKS_PER_BLOCK
    pstart = dot(reg.astype(F32)[None, :], excl(N_EXPERTS))[0].astype(jnp.int32)
    before = dot(excl(nt).T, cchf).astype(jnp.int32)
    dch = pstart[None, :] + before
    nb_used = jnp.sum(reg) // CHUNKS_PER_BLOCK
    soffrow = jnp.pad((soff * CHUNK).astype(F32), ((0, 0), (0, LANES - N_EXPERTS)))[:, None, :]
    i32 = lambda a: a.reshape(-1).astype(jnp.int32)
    return dict(cch=i32(cch), soff=i32(soff), dch=i32(dch), tot=i32(tot_tile),
                padst=i32(pstart + tot_e), padn=i32(reg - tot_e), nb_used=i32(nb_used),
                pst=i32(pstart * CHUNK), ntile=i32((reg * CHUNK + EXP_TILE - 1) // EXP_TILE),
                soffrow=soffrow)


def _rope_tables(seq):
    inv = ROPE_THETA ** (-jnp.arange(0, HEAD_DIM, 2, dtype=F32) / HEAD_DIM)
    ang = jnp.arange(seq, dtype=F32)[:, None] * inv[None, :]
    cos, sin = jnp.cos(ang), jnp.sin(ang)
    return jnp.concatenate([cos, cos], axis=1), jnp.concatenate([-sin, sin], axis=1)


@jax.jit
def kernel(x, w_in, w_sink, w_conv, b_conv, w_rec_gate, b_rec_gate, w_in_gate, b_in_gate,
           lru_lambda, w_attn_o, w_rnn_o, w_out, ln_g, ln_b, w_router_group, b_router_group,
           w_router_expert, b_router_expert, w_exp_gate, w_exp_up, w_exp_down):
    bsz, seq, d = x.shape
    depth = w_in.shape[0]
    t = bsz * seq
    nblk = d // LANES
    alpha = (2 * depth) ** 0.25
    cosf, sinf = _rope_tables(seq)
    n_rows = _max_blocks(t) * MOE_BLOCK + (EXP_TILE - MOE_BLOCK)
    x2 = x.reshape(t, d)
    n_branch = w_in.shape[2] - 2 * d
    w_branch = [w_in[l, :, :n_branch].astype(BF16) for l in range(depth)]
    q, k, v, xr, yr = _inproj(x2, w_branch[0], cosf, sinf, seq)
    for l in range(depth):
        o = _attention(q.reshape(bsz, seq, -1), k.reshape(bsz, seq, -1), v.reshape(bsz, seq, -1),
                       w_sink[l])
        wg_cat = (0.5 * jnp.concatenate([w_rec_gate[l, 0], w_in_gate[l, 0], w_rec_gate[l, 1],
                                         w_in_gate[l, 1]], axis=-1)).astype(BF16)
        bg_cat = jnp.concatenate([b_rec_gate[l, 0].reshape(nblk, 1, LANES),
                                  b_in_gate[l, 0].reshape(nblk, 1, LANES),
                                  b_rec_gate[l, 1].reshape(nblk, 1, LANES),
                                  b_in_gate[l, 1].reshape(nblk, 1, LANES)], axis=-1)
        lam_cat = jnp.concatenate([lru_lambda[l, 0].reshape(nblk, 1, LANES),
                                   lru_lambda[l, 1].reshape(nblk, 1, LANES)], axis=-1)
        hy = _rnn(xr.reshape(bsz, seq, d), yr.reshape(bsz, seq, d), w_conv[l],
                  b_conv[l].reshape(nblk, 1, LANES), wg_cat, bg_cat, lam_cat)
        lane_pad = LANES - N_EXPERTS - N_GROUPS
        wr = jnp.concatenate([w_router_expert[l], w_router_group[l],
                              jnp.zeros((d, lane_pad), F32)], axis=1).astype(BF16)
        br = jnp.concatenate([b_router_expert[l], b_router_group[l],
                              jnp.zeros((lane_pad,), F32)])[None, :]
        x1, x1b, route, cnt = _mixout(x2, o.reshape(t, -1), hy.reshape(t, d),
                                 w_in[l, :, n_branch:].astype(BF16),
                                 w_attn_o[l].astype(BF16), w_rnn_o[l].astype(BF16),
                                 w_out[l].astype(BF16), ln_g[l, 0].reshape(1, d),
                                 ln_b[l, 0].reshape(1, d), wr, br, alpha)
        tb = _dispatch_tables(cnt)
        xb = _dispatch(x1b, route, tb["soffrow"],
                       (tb["cch"], tb["soff"], tb["dch"], tb["tot"], tb["padst"], tb["padn"],
                        tb["nb_used"]), n_rows)
        yb = _experts(xb, tb["pst"], tb["ntile"], tb["nb_used"], w_exp_gate, w_exp_up,
                      w_exp_down, l)
        next_proj = (w_branch[l + 1], cosf, sinf, seq) if l + 1 < depth else None
        x2, *nxt = _combine(x1, route, tb["soffrow"],
                            (tb["cch"], tb["soff"], tb["dch"], tb["tot"]), yb,
                            ln_g[l, 1].reshape(1, d), ln_b[l, 1].reshape(1, d), alpha, next_proj)
        if nxt:
            q, k, v, xr, yr = nxt
    return x2.reshape(bsz, seq, d)
```

```python
import functools
import math

import jax
import jax.numpy as jnp
from jax import lax
from jax.experimental import pallas as pl
from jax.experimental.pallas import tpu as pltpu

F32 = jnp.float32
BF16 = jnp.bfloat16

HEAD_DIM = 128
N_Q_HEADS = 8
N_KV_HEADS = 2
Q_PER_KV = N_Q_HEADS // N_KV_HEADS
WINDOW = 128
ROPE_THETA = 10000.0
CONV_W = 4
LRU_C = 8.0
N_GROUPS = 4
EXPERTS_PER_GROUP = 8
N_EXPERTS = N_GROUPS * EXPERTS_PER_GROUP
LN_EPS = 1e-5
NEG = -1e30
LOG2E = math.log2(math.e)

SUBLANES = 8
LANES = 128
V7X_VMEM_BYTES = 64 * 1024 * 1024

TM_PROJ = 512
TQ_ATTN = 1024
ATTN_STACK = Q_PER_KV
SEG = 256
SCAN_LEN = SEG + 4
SCAN_UNROLL = 10
BF16_ROWS = 2 * SUBLANES
TD = 512
MIX_ROWS = 256
CHUNK = BF16_ROWS
WAIT_BATCH = 8
COPY_RUN = 4
STAGE_ROWS = 2 * TD + CHUNK * N_EXPERTS
MOE_BLOCK = 128
CHUNKS_PER_BLOCK = MOE_BLOCK // CHUNK
EXP_TILE = 4 * MOE_BLOCK
EXP_BUFS = 6
VMEM_LIMIT = V7X_VMEM_BYTES * 7 // 8


def _cparams(n_axes):
    return pltpu.CompilerParams(dimension_semantics=("arbitrary",) * n_axes,
                                vmem_limit_bytes=VMEM_LIMIT)


def _softplus(z):
    e = jnp.exp(-jnp.abs(z))
    w = 1.0 + e
    tiny = w == 1.0
    log1p = jnp.where(tiny, e, jnp.log(w) * (e / jnp.where(tiny, 1.0, w - 1.0)))
    return jnp.maximum(z, 0.0) + log1p


def _sigmoid(x):
    return 0.5 * jnp.tanh(0.5 * x) + 0.5


def _gelu_tanh(y):
    c1 = math.sqrt(2.0 / math.pi)
    half = 0.5 * y
    return half + half * jnp.tanh(y * (c1 + (c1 * 0.044715) * (y * y)))


def _layer_norm(y, g, b):
    mu = jnp.mean(y, axis=-1, keepdims=True)
    d = y - mu
    var = jnp.mean(d * d, axis=-1, keepdims=True)
    return d * lax.rsqrt(var + LN_EPS) * g + b


def _project_tile(xb, w_ref, cos, sin, q_ref, k_ref, v_ref, xr_ref, yr_ref):
    aw = N_Q_HEADS * HEAD_DIM
    kw = N_KV_HEADS * HEAD_DIM
    d_model = xr_ref.shape[1]

    def proj(c0, n):
        return jnp.dot(xb, w_ref[:, c0:c0 + n], preferred_element_type=F32)

    def rope(t):
        return t * cos + pltpu.roll(t, HEAD_DIM // 2, 1) * sin

    zq = proj(0, aw)
    scale = HEAD_DIM ** -0.5 * LOG2E
    for h in range(N_Q_HEADS):
        sl = slice(h * HEAD_DIM, (h + 1) * HEAD_DIM)
        q_ref[:, sl] = (rope(zq[:, sl]) * scale).astype(BF16)
    zk = proj(aw, kw)
    for h in range(N_KV_HEADS):
        sl = slice(h * HEAD_DIM, (h + 1) * HEAD_DIM)
        k_ref[:, sl] = rope(zk[:, sl]).astype(BF16)
    v_ref[...] = proj(aw + kw, kw).astype(BF16)
    c0 = aw + 2 * kw
    xr_ref[...] = proj(c0, d_model)
    yr_ref[...] = proj(c0 + d_model, d_model)


def _inproj_kernel(x_ref, w_ref, cos_ref, sin_ref, q_ref, k_ref, v_ref, xr_ref, yr_ref):
    _project_tile(x_ref[...].astype(BF16), w_ref, cos_ref[...], sin_ref[...],
                  q_ref, k_ref, v_ref, xr_ref, yr_ref)


def _proj_out_specs(t, d, tm, row):
    aw = N_Q_HEADS * HEAD_DIM
    kw = N_KV_HEADS * HEAD_DIM
    shapes = [jax.ShapeDtypeStruct((t, aw), BF16), jax.ShapeDtypeStruct((t, kw), BF16),
              jax.ShapeDtypeStruct((t, kw), BF16)] + [jax.ShapeDtypeStruct((t, d), F32)] * 2
    specs = [pl.BlockSpec((tm, aw), row), pl.BlockSpec((tm, kw), row),
             pl.BlockSpec((tm, kw), row)] + [pl.BlockSpec((tm, d), row)] * 2
    return shapes, specs


def _inproj(x2, w_in_b, cosf, sinf, seq):
    t, d = x2.shape
    n_in = w_in_b.shape[1]
    tm = TM_PROJ
    tiles_per_seq = seq // tm
    row = lambda i: (i, 0)
    pos = lambda i: (i % tiles_per_seq, 0)
    shapes, specs = _proj_out_specs(t, d, tm, row)
    return pl.pallas_call(
        _inproj_kernel,
        grid=(t // tm,),
        in_specs=[pl.BlockSpec((tm, d), row),
                  pl.BlockSpec((d, n_in), lambda i: (0, 0)),
                  pl.BlockSpec((tm, HEAD_DIM), pos),
                  pl.BlockSpec((tm, HEAD_DIM), pos)],
        out_specs=specs,
        out_shape=shapes,
        compiler_params=_cparams(1),
        name="inproj",
    )(x2, w_in_b, cosf, sinf)


def _attn_kernel(sink_ref, q_ref, kp_ref, kc_ref, kn_ref, vp_ref, vc_ref, vn_ref, o_ref,
                 *, seq):
    i = pl.program_id(1)
    tq = q_ref.shape[1]
    blk = WINDOW
    t0 = i * tq
    kext = jnp.concatenate([kp_ref[0], kc_ref[0], kn_ref[0]], axis=0)
    vext = jnp.concatenate([vp_ref[0], vc_ref[0], vn_ref[0]], axis=0)
    nrow = ATTN_STACK * blk
    blk_shift = blk.bit_length() - 1
    qi = lax.broadcasted_iota(jnp.int32, (nrow, 3 * blk), 0) & (blk - 1)
    kj = lax.broadcasted_iota(jnp.int32, (nrow, 3 * blk), 1)
    band = jnp.where(jnp.abs(kj - blk - qi) <= WINDOW, 0.0, NEG)
    kj_row = lax.broadcasted_iota(jnp.int32, (1, 3 * blk), 1)
    rowg = lax.broadcasted_iota(jnp.int32, (nrow, 1), 0) >> blk_shift
    sinks = []
    for h0 in range(0, N_Q_HEADS, ATTN_STACK):
        sk = jnp.full((nrow, 1), sink_ref[h0], F32)
        for g in range(1, ATTN_STACK):
            sk = jnp.where(rowg == g, sink_ref[h0 + g], sk)
        sinks.append(sk * LOG2E)
    for j in range(tq // blk):
        kpos = t0 + j * blk - blk + kj_row
        bias = band + jnp.where((kpos >= 0) & (kpos < seq), 0.0, NEG)
        for h0 in range(0, N_Q_HEADS, ATTN_STACK):
            kv = h0 // Q_PER_KV
            hs = slice(kv * HEAD_DIM, (kv + 1) * HEAD_DIM)
            kblk = kext[j * blk:j * blk + 3 * blk, hs]
            vblk = vext[j * blk:j * blk + 3 * blk, hs]
            qs = [q_ref[0, j * blk:(j + 1) * blk, (h0 + g) * HEAD_DIM:(h0 + g + 1) * HEAD_DIM]
                  for g in range(ATTN_STACK)]
            qblk = jnp.concatenate(qs, axis=0)
            s = lax.dot_general(qblk, kblk, (((1,), (1,)), ((), ())),
                                preferred_element_type=F32) + bias
            sk = sinks[h0 // ATTN_STACK]
            m = jnp.maximum(jnp.max(s, axis=-1, keepdims=True), sk)
            p = jnp.exp2(s - m)
            denom = jnp.sum(p, axis=-1, keepdims=True) + jnp.exp2(sk - m)
            o = jnp.dot(p.astype(BF16), vblk, preferred_element_type=F32) / denom
            for g in range(ATTN_STACK):
                c = (h0 + g) * HEAD_DIM
                o_ref[0, j * blk:(j + 1) * blk, c:c + HEAD_DIM] = (
                    o[g * blk:(g + 1) * blk].astype(BF16))


def _attention(q3, k3, v3, sink):
    b, s, aw = q3.shape
    kw = k3.shape[2]
    tq = TQ_ATTN
    blk = WINDOW
    r = tq // blk
    nblk = s // blk
    cur = lambda bi, i: (bi, i, 0)
    prev = lambda bi, i: (bi, jnp.maximum(i * r - 1, 0), 0)
    nxt = lambda bi, i: (bi, jnp.minimum((i + 1) * r, nblk - 1), 0)
    kv_specs = [pl.BlockSpec((1, blk, kw), prev), pl.BlockSpec((1, tq, kw), cur),
                pl.BlockSpec((1, blk, kw), nxt)]
    return pl.pallas_call(
        functools.partial(_attn_kernel, seq=s),
        grid=(b, s // tq),
        in_specs=[pl.BlockSpec(memory_space=pltpu.SMEM),
                  pl.BlockSpec((1, tq, aw), cur)] + kv_specs + kv_specs,
        out_specs=pl.BlockSpec((1, tq, aw), cur),
        out_shape=jax.ShapeDtypeStruct((b, s, aw), BF16),
        compiler_params=_cparams(2),
        name="attention",
    )(sink, q3, k3, k3, k3, v3, v3, v3)


def _rnn_kernel(xr_ref, yr_ref, wc_ref, bc_ref, wg_ref, bg_ref, lam_ref, out_ref,
                xpad_ref, af_ref, uf_ref, ab_ref, ub_ref, hb_ref, cf_ref, cb_ref, *, seq):
    nseg = seq // SEG
    nlane = nseg
    ngrp = nlane // SUBLANES
    hf_ref = xpad_ref
    wc = wc_ref[...]
    bc = bc_ref[0]
    wg = wg_ref[0]
    bg = 0.5 * bg_ref[0]
    lam = lam_ref[0]
    rate = (0.5 * LRU_C) * _softplus(-lam)
    a_refs = (af_ref, ab_ref)
    u_refs = (uf_ref, ub_ref)
    left = CONV_W // 2

    halo = jnp.zeros((SUBLANES, LANES), F32)
    xpad_ref[pl.ds(0, SUBLANES), :] = halo
    xpad_ref[pl.ds(seq + SUBLANES, SUBLANES), :] = halo

    def pad_copy(c, carry):
        t0 = pl.multiple_of(c * SEG, SEG)
        xpad_ref[pl.ds(t0 + SUBLANES, SEG), :] = xr_ref[0, pl.ds(t0, SEG), :]
        return carry

    lax.fori_loop(0, nseg, pad_copy, 0, unroll=4)

    tail = nlane * SCAN_LEN - seq
    for a_ref, u_ref in zip(a_refs, u_refs):
        a_ref[pl.ds(seq, tail), :] = jnp.ones((tail, LANES), F32)
        u_ref[pl.ds(seq, tail), :] = jnp.zeros((tail, LANES), F32)

    def gates(c, carry):
        t0 = pl.multiple_of(c * SEG, SEG)
        xc = bc
        for tap in range(CONV_W):
            xc = xc + xpad_ref[pl.ds(t0 + SUBLANES - left + tap, SEG), :] * wc[tap:tap + 1]
        gh = jnp.dot(xc.astype(BF16), wg, preferred_element_type=F32) + bg
        xch = 0.5 * xc
        for d in range(2):
            rt = rate[:, d * LANES:(d + 1) * LANES]
            nlog_a = rt * jnp.tanh(gh[:, (2 * d) * LANES:(2 * d + 1) * LANES]) + rt
            a = jnp.exp2(nlog_a * (-1.0 / math.log(2.0)))
            z = jnp.tanh(nlog_a) * (a * a + 1.0)
            mult = jnp.where(z > 0.0, z * lax.rsqrt(z), 0.0)
            in_gate2 = jnp.tanh(gh[:, (2 * d + 1) * LANES:(2 * d + 2) * LANES]) + 1.0
            a_refs[d][pl.ds(t0, SEG), :] = a
            u_refs[d][pl.ds(t0, SEG), :] = (xch * mult) * in_gate2
        return carry

    lax.fori_loop(0, nseg, gates, 0, unroll=16)

    def lane_rows(g, j):
        return pl.ds(g * SUBLANES * SCAN_LEN + j, SUBLANES, stride=SCAN_LEN)

    def totals_step(j, carry):
        hf, pf, hb, pb = carry
        jb = SCAN_LEN - 1 - j
        nhf, npf, nhb, npb = [], [], [], []
        for g in range(ngrp):
            a = af_ref[lane_rows(g, j), :]
            nhf.append(a * hf[g] + uf_ref[lane_rows(g, j), :])
            npf.append(a * pf[g])
            a = ab_ref[lane_rows(g, jb), :]
            nhb.append(a * hb[g] + ub_ref[lane_rows(g, jb), :])
            npb.append(a * pb[g])
        return tuple(nhf), tuple(npf), tuple(nhb), tuple(npb)

    zero = tuple(jnp.zeros((SUBLANES, LANES), F32) for _ in range(ngrp))
    one = tuple(jnp.ones((SUBLANES, LANES), F32) for _ in range(ngrp))
    hf, pf, hb, pb = lax.fori_loop(0, SCAN_LEN, totals_step, (zero, one, zero, one),
                                   unroll=SCAN_UNROLL)

    c = jnp.zeros((1, LANES), F32)
    for s in range(nlane):
        g, r = divmod(s, SUBLANES)
        cf_ref[s:s + 1, :] = c
        c = pf[g][r:r + 1] * c + hf[g][r:r + 1]
    c = jnp.zeros((1, LANES), F32)
    for s in range(nlane - 1, -1, -1):
        g, r = divmod(s, SUBLANES)
        cb_ref[s:s + 1, :] = c
        c = pb[g][r:r + 1] * c + hb[g][r:r + 1]

    def scan_step(j, carry):
        hf, hb = carry
        jb = SCAN_LEN - 1 - j
        nhf, nhb = [], []
        for g in range(ngrp):
            h = af_ref[lane_rows(g, j), :] * hf[g] + uf_ref[lane_rows(g, j), :]
            hf_ref[lane_rows(g, j), :] = h
            nhf.append(h)
            h = ab_ref[lane_rows(g, jb), :] * hb[g] + ub_ref[lane_rows(g, jb), :]
            hb_ref[lane_rows(g, jb), :] = h
            nhb.append(h)
        return tuple(nhf), tuple(nhb)

    hf0 = tuple(cf_ref[g * SUBLANES:(g + 1) * SUBLANES, :] for g in range(ngrp))
    hb0 = tuple(cb_ref[g * SUBLANES:(g + 1) * SUBLANES, :] for g in range(ngrp))
    lax.fori_loop(0, SCAN_LEN, scan_step, (hf0, hb0), unroll=SCAN_UNROLL)

    def finish(c, carry):
        t0 = pl.multiple_of(c * SEG, SEG)
        h = hf_ref[pl.ds(t0, SEG), :] + hb_ref[pl.ds(t0, SEG), :]
        out_ref[0, pl.ds(t0, SEG), :] = (h * _gelu_tanh(yr_ref[0, pl.ds(t0, SEG), :])).astype(BF16)
        return carry

    lax.fori_loop(0, nseg, finish, 0, unroll=4)


def _rnn(xr3, yr3, w_conv, b_conv, wg_cat, bg_cat, lam_cat):
    b, s, d = xr3.shape
    nblk = d // LANES
    nseg = s // SEG
    slab = lambda bi, n: (bi, 0, n)
    per_blk = lambda bi, n: (n, 0, 0)
    assert nseg % SUBLANES == 0 and nseg * SCAN_LEN >= s + 2 * SUBLANES
    scratch = ([pltpu.VMEM((nseg * SCAN_LEN, LANES), F32)] * 6
               + [pltpu.VMEM((nseg, LANES), F32)] * 2)
    return pl.pallas_call(
        functools.partial(_rnn_kernel, seq=s),
        grid=(b, nblk),
        in_specs=[pl.BlockSpec((1, s, LANES), slab),
                  pl.BlockSpec((1, s, LANES), slab),
                  pl.BlockSpec((CONV_W, LANES), lambda bi, n: (0, n)),
                  pl.BlockSpec((1, 1, LANES), per_blk),
                  pl.BlockSpec((1, LANES, 4 * LANES), per_blk),
                  pl.BlockSpec((1, 1, 4 * LANES), per_blk),
                  pl.BlockSpec((1, 1, 2 * LANES), per_blk)],
        out_specs=pl.BlockSpec((1, s, LANES), slab),
        out_shape=jax.ShapeDtypeStruct((b, s, d), BF16),
        scratch_shapes=scratch,
        compiler_params=_cparams(2),
        name="rglru",
    )(xr3, yr3, w_conv, b_conv, wg_cat, bg_cat, lam_cat)


def _route(logits):
    lane = lax.broadcasted_iota(jnp.int32, logits.shape, 1)
    lanef = lane.astype(F32)
    big = float(4 * LANES)
    gmask = (lane >= N_EXPERTS) & (lane < N_EXPERTS + N_GROUPS)
    gl = jnp.where(gmask, logits, NEG)
    gmax = jnp.max(gl, axis=-1, keepdims=True)
    ge = jnp.exp(gl - gmax)
    gprob = ge / jnp.sum(ge, axis=-1, keepdims=True)
    gval = jnp.max(gprob, axis=-1, keepdims=True)
    gidx = jnp.min(jnp.where((gprob == gval) & gmask, lanef, big), axis=-1, keepdims=True)
    gidx = gidx.astype(jnp.int32) - N_EXPERTS
    group_shift = EXPERTS_PER_GROUP.bit_length() - 1
    emask = (lane < N_EXPERTS) & ((lane >> group_shift) == gidx)
    el = jnp.where(emask, logits, NEG)
    m1 = jnp.max(el, axis=-1, keepdims=True)
    i1 = jnp.min(jnp.where((el == m1) & emask, lanef, big), axis=-1, keepdims=True)
    emask2 = emask & (lanef != i1)
    el2 = jnp.where(emask2, logits, NEG)
    m2 = jnp.max(el2, axis=-1, keepdims=True)
    i2 = jnp.min(jnp.where((el2 == m2) & emask2, lanef, big), axis=-1, keepdims=True)
    e2 = jnp.exp(m2 - m1)
    den = 1.0 + e2
    g1 = (1.0 / den) * gval
    g2 = (e2 / den) * gval
    route = jnp.where(lane == 0, i1, jnp.where(lane == 1, i2,
                      jnp.where(lane == 2, g1, jnp.where(lane == 3, g2, 0.0))))
    onehot = (lanef == i1).astype(F32) + (lanef == i2).astype(F32)
    return route, jnp.sum(onehot, axis=0, keepdims=True)


def _mixout_kernel(x_ref, o_ref, hy_ref, wmg_ref, wao_ref, wro_ref, wout_ref,
                   lng_ref, lnb_ref, wr_ref, br_ref, x1_ref, x1b_ref, route_ref, cnt_ref,
                   logits_ref, *, alpha):
    @pl.when(pl.program_id(0) == 0)
    def _():
        logits_ref[...] = jnp.zeros_like(logits_ref)

    route, cnt = _route(logits_ref[...])
    route_ref[...] = route
    cnt_ref[0] = cnt

    d = x_ref.shape[1]
    for r0 in range(0, x_ref.shape[0], MIX_ROWS):
        rows = pl.ds(r0, MIX_ROWS)
        x = x_ref[rows, :]
        xb = x.astype(BF16)
        ga = jnp.dot(xb, wmg_ref[:, :d], preferred_element_type=F32)
        gr = jnp.dot(xb, wmg_ref[:, d:], preferred_element_type=F32)
        ya = jnp.dot(o_ref[rows, :], wao_ref[...], preferred_element_type=F32)
        yr = jnp.dot(hy_ref[rows, :], wro_ref[...], preferred_element_type=F32)
        merged = _sigmoid(ga) * ya + _sigmoid(gr) * yr
        mix = jnp.dot(merged.astype(BF16), wout_ref[...], preferred_element_type=F32)
        x1 = _layer_norm(alpha * x + mix, lng_ref[...], lnb_ref[...])
        x1_ref[rows, :] = x1
        x1b = x1.astype(BF16)
        x1b_ref[rows, :] = x1b
        logits_ref[rows, :] = (jnp.dot(x1b, wr_ref[...], preferred_element_type=F32)
                               + br_ref[...])


def _mixout(x2, o2, hy2, wmg, wao, wro, wout, lng, lnb, wr, br, alpha):
    t, d = x2.shape
    tm = TD
    nt = t // tm
    row = lambda i: (jnp.minimum(i, nt - 1), 0)
    prev_row = lambda i: (jnp.maximum(i - 1, 0), 0)
    full = lambda i: (0, 0)
    return pl.pallas_call(
        functools.partial(_mixout_kernel, alpha=alpha),
        grid=(nt + 1,),
        in_specs=[pl.BlockSpec((tm, d), row)] * 3
                 + [pl.BlockSpec((d, 2 * d), full)]
                 + [pl.BlockSpec((d, d), full)] * 3
                 + [pl.BlockSpec((1, d), full)] * 2
                 + [pl.BlockSpec((d, LANES), full), pl.BlockSpec((1, LANES), full)],
        out_specs=[pl.BlockSpec((tm, d), row), pl.BlockSpec((tm, d), row),
                   pl.BlockSpec((tm, LANES), prev_row),
                   pl.BlockSpec((1, 1, LANES), lambda i: (jnp.maximum(i - 1, 0), 0, 0))],
        out_shape=[jax.ShapeDtypeStruct((t, d), F32), jax.ShapeDtypeStruct((t, d), BF16),
                   jax.ShapeDtypeStruct((t, LANES), F32),
                   jax.ShapeDtypeStruct((nt, 1, LANES), F32)],
        scratch_shapes=[pltpu.VMEM((tm, LANES), F32)],
        compiler_params=_cparams(1),
        name="mixout",
    )(x2, o2, hy2, wmg, wao, wro, wout, lng, lnb, wr, br)


def _perm_matrix(route, soff_row, w1, w2):
    td = route.shape[0]
    lane = lax.broadcasted_iota(jnp.int32, (td, LANES), 1).astype(F32)
    e1 = lane == route[:, 0:1]
    e2 = lane == route[:, 1:2]
    cnt = (e1.astype(F32) + e2.astype(F32)).astype(BF16)
    ti = lax.broadcasted_iota(jnp.int32, (td, td), 0)
    tj = lax.broadcasted_iota(jnp.int32, (td, td), 1)
    lower = (tj < ti).astype(BF16)
    pos = jnp.dot(lower, cnt, preferred_element_type=F32) + soff_row
    r1 = jnp.sum(jnp.where(e1, pos, 0.0), axis=-1, keepdims=True).astype(jnp.int32)
    r2 = jnp.sum(jnp.where(e2, pos, 0.0), axis=-1, keepdims=True).astype(jnp.int32)
    col = lax.broadcasted_iota(jnp.int32, (td, STAGE_ROWS), 1)
    return jnp.where(col == r1, w1, jnp.where(col == r2, w2, 0.0))


def _chunk_rows(c, k=1):
    return pl.ds(pl.multiple_of(c * CHUNK, CHUNK), k * CHUNK)


def _block_rows(b):
    return pl.ds(pl.multiple_of(b * MOE_BLOCK, MOE_BLOCK), MOE_BLOCK)


def _for_each_run(i, cch_ref, soff_ref, dch_ref, fn):
    assert COPY_RUN == 4

    def per_expert(e, carry):
        idx = i * N_EXPERTS + e
        so = soff_ref[idx]
        do = dch_ref[idx]
        n = cch_ref[idx]
        full = n // COPY_RUN

        def per_run(c, carry2):
            fn(so + c * COPY_RUN, do + c * COPY_RUN, COPY_RUN)
            return carry2

        lax.fori_loop(0, full, per_run, 0)
        done = full * COPY_RUN
        two = n & 2

        @pl.when(two != 0)
        def _():
            fn(so + done, do + done, 2)

        @pl.when((n & 1) != 0)
        def _():
            fn(so + done + two, do + done + two, 1)

        return carry

    lax.fori_loop(0, N_EXPERTS, per_expert, 0)


def _repeat(n, fn):
    def body(c, carry):
        fn()
        return carry

    lax.fori_loop(0, n, body, 0)


def _drain_chunks(n, make_copy):
    _repeat(n // WAIT_BATCH, lambda: make_copy(WAIT_BATCH).wait())
    _repeat(n % WAIT_BATCH, lambda: make_copy(1).wait())


def _dispatch_kernel(cch_ref, soff_ref, dch_ref, tot_ref, padst_ref, padn_ref, nb_ref,
                     x_ref, route_ref, soffrow_ref, xb_ref, stage_ref, zero_ref, sems, zsem):
    i = pl.program_id(0)
    nt = pl.num_programs(0)
    n_blocks = xb_ref.shape[0] // MOE_BLOCK
    slot = i % 2

    def copy(s, src_chunk, dst_chunk, k):
        return pltpu.make_async_copy(stage_ref.at[s, _chunk_rows(src_chunk, k)],
                                     xb_ref.at[_chunk_rows(dst_chunk, k)], sems.at[s])

    def many(s):
        return lambda k: pltpu.make_async_copy(stage_ref.at[s, pl.ds(0, k * CHUNK)],
                                               xb_ref.at[pl.ds(0, k * CHUNK)], sems.at[s])

    @pl.when(i >= 2)
    def _():
        _drain_chunks(tot_ref[jnp.maximum(i - 2, 0)], many(slot))

    pt = _perm_matrix(route_ref[...], soffrow_ref[0], 1.0, 1.0).astype(BF16)
    stage_ref[slot] = lax.dot_general(pt, x_ref[...], (((0,), (0,)), ((), ())),
                                      preferred_element_type=F32).astype(BF16)
    _for_each_run(i, cch_ref, soff_ref, dch_ref,
                  lambda s, d, k: copy(slot, s, d, k).start())

    @pl.when(i == nt - 1)
    def _():
        @pl.when(i >= 1)
        def _():
            _drain_chunks(tot_ref[jnp.maximum(i - 1, 0)], many(1 - slot))

        _drain_chunks(tot_ref[i], many(slot))

        zero_ref[...] = jnp.zeros_like(zero_ref)

        def zcopy(dst_chunk):
            return pltpu.make_async_copy(zero_ref.at[pl.ds(0, CHUNK)],
                                         xb_ref.at[_chunk_rows(dst_chunk)], zsem)

        def pad_start(e, n):
            def pad_chunk(c, carry2):
                zcopy(padst_ref[e] + c).start()
                return carry2

            lax.fori_loop(0, padn_ref[e], pad_chunk, 0)
            return n + padn_ref[e]

        npad = lax.fori_loop(0, N_EXPERTS, pad_start, 0)
        _repeat(npad, lambda: zcopy(0).wait())

        def zblock(b):
            return pltpu.make_async_copy(zero_ref, xb_ref.at[_block_rows(b)], zsem)

        def start_block(b, carry):
            zblock(b).start()
            return carry

        def wait_block(b, carry):
            zblock(b).wait()
            return carry

        lax.fori_loop(nb_ref[0], n_blocks, start_block, 0)
        lax.fori_loop(nb_ref[0], n_blocks, wait_block, 0)


def _dispatch(x1, route, soffrow, tables, n_rows):
    t, d = x1.shape
    nt = t // TD
    grid_spec = pltpu.PrefetchScalarGridSpec(
        num_scalar_prefetch=7,
        grid=(nt,),
        in_specs=[pl.BlockSpec((TD, d), lambda i, *_: (i, 0)),
                  pl.BlockSpec((TD, LANES), lambda i, *_: (i, 0)),
                  pl.BlockSpec((1, 1, LANES), lambda i, *_: (i, 0, 0))],
        out_specs=pl.BlockSpec(memory_space=pl.ANY),
        scratch_shapes=[pltpu.VMEM((2, STAGE_ROWS, d), BF16), pltpu.VMEM((MOE_BLOCK, d), BF16),
                        pltpu.SemaphoreType.DMA((2,)), pltpu.SemaphoreType.DMA(())],
    )
    return pl.pallas_call(
        _dispatch_kernel,
        grid_spec=grid_spec,
        out_shape=jax.ShapeDtypeStruct((n_rows, d), BF16),
        compiler_params=_cparams(1),
        name="dispatch",
    )(*tables, x1, route, soffrow)


def _experts_kernel(pst_ref, rows_ref, nb_ref, xb_ref, wg_ref, wu_ref, wd_ref, yb_ref,
                    wgb_ref, wub_ref, wdb_ref, xbuf_ref, ybuf_ref, in_sems, out_sems, zsem):
    e = pl.program_id(0)
    ne = pl.num_programs(0)
    tiles = lambda rows: (rows + EXP_TILE - 1) // EXP_TILE
    rows_e = rows_ref[e]
    n = tiles(rows_e)
    row0 = pst_ref[e]
    n_blocks = yb_ref.shape[0] // MOE_BLOCK
    ahead = EXP_BUFS - 1

    def tile_rows(r0, t):
        return pl.ds(pl.multiple_of(r0 + t * EXP_TILE, MOE_BLOCK), EXP_TILE)

    def in_copy(r0, t, s):
        return pltpu.make_async_copy(xb_ref.at[tile_rows(r0, t)], xbuf_ref.at[s], in_sems.at[s])

    def out_copy(t, s):
        return pltpu.make_async_copy(ybuf_ref.at[s], yb_ref.at[tile_rows(row0, t)],
                                     out_sems.at[s])

    def start_head(r0, count):
        for k in range(ahead):
            @pl.when(k < count)
            def _():
                in_copy(r0, k, k).start()

    @pl.when(e == 0)
    def _():
        start_head(row0, n)
        ybuf_ref[...] = jnp.zeros_like(ybuf_ref)

    @pl.when(n > 0)
    def _():
        wgb_ref[...] = wg_ref[0, 0].astype(BF16)
        wub_ref[...] = wu_ref[0, 0].astype(BF16)
        wdb_ref[...] = wd_ref[0, 0].astype(BF16)

    def tile(t, carry):
        s = t % EXP_BUFS

        @pl.when(t + ahead < n)
        def _():
            in_copy(row0, t + ahead, (t + ahead) % EXP_BUFS).start()

        in_copy(row0, t, s).wait()

        @pl.when(t >= EXP_BUFS)
        def _():
            out_copy(t - EXP_BUFS, s).wait()

        def mlp(m):
            x = xbuf_ref[s, pl.ds(0, m), :]
            hg = jnp.dot(x, wgb_ref[...], preferred_element_type=F32)
            hu = jnp.dot(x, wub_ref[...], preferred_element_type=F32)
            hid = (hg * _sigmoid(hg)) * hu
            ybuf_ref[s, pl.ds(0, m), :] = jnp.dot(hid.astype(BF16), wdb_ref[...],
                                                  preferred_element_type=F32).astype(BF16)

        owned = rows_e - t * EXP_TILE

        @pl.when(owned <= EXP_TILE // 2)
        def _():
            mlp(EXP_TILE // 2)

        @pl.when(owned > EXP_TILE // 2)
        def _():
            mlp(EXP_TILE)

        out_copy(t, s).start()
        return carry

    lax.fori_loop(0, n, tile, 0)

    @pl.when(e + 1 < ne)
    def _():
        nxt = jnp.minimum(e + 1, ne - 1)
        start_head(pst_ref[nxt], tiles(rows_ref[nxt]))

    for k in range(EXP_BUFS):
        @pl.when(n > k)
        def _():
            out_copy(n - 1 - k, (n - 1 - k) % EXP_BUFS).wait()

    @pl.when(e == ne - 1)
    def _():
        ybuf_ref[0] = jnp.zeros(ybuf_ref.shape[1:], BF16)

        def zblock(b):
            return pltpu.make_async_copy(ybuf_ref.at[0, pl.ds(0, MOE_BLOCK)],
                                         yb_ref.at[_block_rows(b)], zsem)

        def start_block(b, carry):
            zblock(b).start()
            return carry

        def wait_block(b, carry):
            zblock(b).wait()
            return carry

        lax.fori_loop(nb_ref[0], n_blocks, start_block, 0)
        lax.fori_loop(nb_ref[0], n_blocks, wait_block, 0)


def _experts(xb, pst, rows, nb_used, w_g, w_u, w_d, layer):
    n_rows, d = xb.shape
    n_exp, de = w_g.shape[1], w_g.shape[3]
    wsel = lambda e, *_: (layer, e, 0, 0)
    grid_spec = pltpu.PrefetchScalarGridSpec(
        num_scalar_prefetch=3,
        grid=(n_exp,),
        in_specs=[pl.BlockSpec(memory_space=pl.ANY),
                  pl.BlockSpec((1, 1, d, de), wsel),
                  pl.BlockSpec((1, 1, d, de), wsel),
                  pl.BlockSpec((1, 1, de, d), wsel)],
        out_specs=pl.BlockSpec(memory_space=pl.ANY),
        scratch_shapes=[pltpu.VMEM((d, de), BF16), pltpu.VMEM((d, de), BF16),
                        pltpu.VMEM((de, d), BF16),
                        pltpu.VMEM((EXP_BUFS, EXP_TILE, d), BF16),
                        pltpu.VMEM((EXP_BUFS, EXP_TILE, d), BF16),
                        pltpu.SemaphoreType.DMA((EXP_BUFS,)), pltpu.SemaphoreType.DMA((EXP_BUFS,)),
                        pltpu.SemaphoreType.DMA(())],
    )
    return pl.pallas_call(
        _experts_kernel,
        grid_spec=grid_spec,
        out_shape=jax.ShapeDtypeStruct((n_rows, d), BF16),
        compiler_params=_cparams(1),
        name="experts",
    )(pst, rows, nb_used, xb, w_g, w_u, w_d)


def _combine_kernel(cch_ref, soff_ref, dch_ref, tot_ref,
                    x_ref, route_ref, soffrow_ref, yb_ref, lng_ref, lnb_ref, *rest,
                    alpha, project):
    if project:
        (w_ref, cos_ref, sin_ref, out_ref, q_ref, k_ref, v_ref, xr_ref, yr_ref,
         stage_ref, sems) = rest
    else:
        out_ref, stage_ref, sems = rest
    i = pl.program_id(0)
    nt = pl.num_programs(0)
    slot = i % 2

    def copy(s, src_chunk, dst_chunk, k):
        return pltpu.make_async_copy(yb_ref.at[_chunk_rows(src_chunk, k)],
                                     stage_ref.at[s, _chunk_rows(dst_chunk, k)], sems.at[s])

    def fetch(tile_idx, s):
        _for_each_run(tile_idx, cch_ref, soff_ref, dch_ref,
                      lambda so, do, k: copy(s, do, so, k).start())

    @pl.when(i == 0)
    def _():
        stage_ref[...] = jnp.zeros_like(stage_ref)
        fetch(i, slot)

    @pl.when(i + 1 < nt)
    def _():
        fetch(i + 1, 1 - slot)

    _drain_chunks(tot_ref[i], lambda k: pltpu.make_async_copy(
        yb_ref.at[pl.ds(0, k * CHUNK)], stage_ref.at[slot, pl.ds(0, k * CHUNK)], sems.at[slot]))

    route = route_ref[...]
    pt = _perm_matrix(route, soffrow_ref[0], route[:, 2:3], route[:, 3:4]).astype(BF16)
    ffn = jnp.dot(pt, stage_ref[slot], preferred_element_type=F32)
    out = _layer_norm(alpha * x_ref[...] + ffn, lng_ref[...], lnb_ref[...])
    out_ref[...] = out
    if project:
        _project_tile(out.astype(BF16), w_ref, cos_ref[...], sin_ref[...],
                      q_ref, k_ref, v_ref, xr_ref, yr_ref)


def _combine(x1, route, soffrow, tables, yb, lng, lnb, alpha, next_proj=None):
    t, d = x1.shape
    nt = t // TD
    row = lambda i, *_: (i, 0)
    full = lambda i, *_: (0, 0)
    in_specs = [pl.BlockSpec((TD, d), row),
                pl.BlockSpec((TD, LANES), row),
                pl.BlockSpec((1, 1, LANES), lambda i, *_: (i, 0, 0)),
                pl.BlockSpec(memory_space=pl.ANY),
                pl.BlockSpec((1, d), full),
                pl.BlockSpec((1, d), full)]
    out_shape = [jax.ShapeDtypeStruct((t, d), F32)]
    out_specs = [pl.BlockSpec((TD, d), row)]
    args = [x1, route, soffrow, yb, lng, lnb]
    if next_proj is not None:
        w_in_b, cosf, sinf, seq = next_proj
        tiles_per_seq = seq // TD
        pos = lambda i, *_: (i % tiles_per_seq, 0)
        in_specs += [pl.BlockSpec(w_in_b.shape, full), pl.BlockSpec((TD, HEAD_DIM), pos),
                     pl.BlockSpec((TD, HEAD_DIM), pos)]
        shapes, specs = _proj_out_specs(t, d, TD, row)
        out_shape += shapes
        out_specs += specs
        args += [w_in_b, cosf, sinf]
    grid_spec = pltpu.PrefetchScalarGridSpec(
        num_scalar_prefetch=4,
        grid=(nt,),
        in_specs=in_specs,
        out_specs=out_specs,
        scratch_shapes=[pltpu.VMEM((2, STAGE_ROWS, d), BF16), pltpu.SemaphoreType.DMA((2,))],
    )
    return pl.pallas_call(
        functools.partial(_combine_kernel, alpha=alpha, project=next_proj is not None),
        grid_spec=grid_spec,
        out_shape=out_shape,
        compiler_params=_cparams(1),
        name="combine",
    )(*tables, *args)


def _max_blocks(t):
    nt = t // TD
    rows = 2 * t + (CHUNK - 1) * nt * N_EXPERTS + (MOE_BLOCK - CHUNK) * N_EXPERTS
    return -(-rows // MOE_BLOCK)


def _dispatch_tables(cnt):
    n = cnt[:, 0, :N_EXPERTS].astype(jnp.int32)
    nt = n.shape[0]
    cch = (n + CHUNK - 1) // CHUNK
    excl = lambda k: (jnp.arange(k)[:, None] < jnp.arange(k)[None, :]).astype(F32)
    dot = functools.partial(jnp.dot, precision=lax.Precision.HIGHEST)
    cchf = cch.astype(F32)
    soff = dot(cchf, excl(N_EXPERTS)).astype(jnp.int32)
    tot_tile = jnp.sum(cch, axis=1)
    tot_e = jnp.sum(cch, axis=0)
    reg = (tot_e + CHUNKS_PER_BLOCK - 1) // CHUNKS_PER_BLOCK * CHUNKS_PER_BLOCK
    pstart = dot(reg.astype(F32)[None, :], excl(N_EXPERTS))[0].astype(jnp.int32)
    before = dot(excl(nt).T, cchf).astype(jnp.int32)
    dch = pstart[None, :] + before
    nb_used = jnp.sum(reg) // CHUNKS_PER_BLOCK
    soffrow = jnp.pad((soff * CHUNK).astype(F32), ((0, 0), (0, LANES - N_EXPERTS)))[:, None, :]
    i32 = lambda a: a.reshape(-1).astype(jnp.int32)
    return dict(cch=i32(cch), soff=i32(soff), dch=i32(dch), tot=i32(tot_tile),
                padst=i32(pstart + tot_e), padn=i32(reg - tot_e), nb_used=i32(nb_used),
                pst=i32(pstart * CHUNK), rows=i32(reg * CHUNK),
                soffrow=soffrow)


def _rope_tables(seq):
    inv = ROPE_THETA ** (-jnp.arange(0, HEAD_DIM, 2, dtype=F32) / HEAD_DIM)
    ang = jnp.arange(seq, dtype=F32)[:, None] * inv[None, :]
    cos, sin = jnp.cos(ang), jnp.sin(ang)
    return jnp.concatenate([cos, cos], axis=1), jnp.concatenate([-sin, sin], axis=1)


@jax.jit
def kernel(x, w_in, w_sink, w_conv, b_conv, w_rec_gate, b_rec_gate, w_in_gate, b_in_gate,
           lru_lambda, w_attn_o, w_rnn_o, w_out, ln_g, ln_b, w_router_group, b_router_group,
           w_router_expert, b_router_expert, w_exp_gate, w_exp_up, w_exp_down):
    bsz, seq, d = x.shape
    depth = w_in.shape[0]
    t = bsz * seq
    nblk = d // LANES
    alpha = (2 * depth) ** 0.25
    cosf, sinf = _rope_tables(seq)
    n_rows = _max_blocks(t) * MOE_BLOCK + (EXP_TILE - MOE_BLOCK)
    x2 = x.reshape(t, d)
    n_branch = w_in.shape[2] - 2 * d
    w_branch = [w_in[l, :, :n_branch].astype(BF16) for l in range(depth)]
    q, k, v, xr, yr = _inproj(x2, w_branch[0], cosf, sinf, seq)
    for l in range(depth):
        o = _attention(q.reshape(bsz, seq, -1), k.reshape(bsz, seq, -1), v.reshape(bsz, seq, -1),
                       w_sink[l])
        wg_cat = (0.5 * jnp.concatenate([w_rec_gate[l, 0], w_in_gate[l, 0], w_rec_gate[l, 1],
                                         w_in_gate[l, 1]], axis=-1)).astype(BF16)
        bg_cat = jnp.concatenate([b_rec_gate[l, 0].reshape(nblk, 1, LANES),
                                  b_in_gate[l, 0].reshape(nblk, 1, LANES),
                                  b_rec_gate[l, 1].reshape(nblk, 1, LANES),
                                  b_in_gate[l, 1].reshape(nblk, 1, LANES)], axis=-1)
        lam_cat = jnp.concatenate([lru_lambda[l, 0].reshape(nblk, 1, LANES),
                                   lru_lambda[l, 1].reshape(nblk, 1, LANES)], axis=-1)
        hy = _rnn(xr.reshape(bsz, seq, d), yr.reshape(bsz, seq, d), w_conv[l],
                  b_conv[l].reshape(nblk, 1, LANES), wg_cat, bg_cat, lam_cat)
        lane_pad = LANES - N_EXPERTS - N_GROUPS
        wr = jnp.concatenate([w_router_expert[l], w_router_group[l],
                              jnp.zeros((d, lane_pad), F32)], axis=1).astype(BF16)
        br = jnp.concatenate([b_router_expert[l], b_router_group[l],
                              jnp.zeros((lane_pad,), F32)])[None, :]
        x1, x1b, route, cnt = _mixout(x2, o.reshape(t, -1), hy.reshape(t, d),
                                 w_in[l, :, n_branch:].astype(BF16),
                                 w_attn_o[l].astype(BF16), w_rnn_o[l].astype(BF16),
                                 w_out[l].astype(BF16), ln_g[l, 0].reshape(1, d),
                                 ln_b[l, 0].reshape(1, d), wr, br, alpha)
        tb = _dispatch_tables(cnt)
        xb = _dispatch(x1b, route, tb["soffrow"],
                       (tb["cch"], tb["soff"], tb["dch"], tb["tot"], tb["padst"], tb["padn"],
                        tb["nb_used"]), n_rows)
        yb = _experts(xb, tb["pst"], tb["rows"], tb["nb_used"], w_exp_gate, w_exp_up,
                      w_exp_down, l)
        next_proj = (w_branch[l + 1], cosf, sinf, seq) if l + 1 < depth else None
        x2, *nxt = _combine(x1, route, tb["soffrow"],
                            (tb["cch"], tb["soff"], tb["dch"], tb["tot"]), yb,
                            ln_g[l, 1].reshape(1, d), ln_b[l, 1].reshape(1, d), alpha, next_proj)
        if nxt:
            q, k, v, xr, yr = nxt
    return x2.reshape(bsz, seq, d)
```

```python
import functools
import math

import jax
import jax.numpy as jnp
from jax import lax
from jax.experimental import pallas as pl
from jax.experimental.pallas import tpu as pltpu

F32 = jnp.float32
BF16 = jnp.bfloat16

HEAD_DIM = 128
N_Q_HEADS = 8
N_KV_HEADS = 2
Q_PER_KV = N_Q_HEADS // N_KV_HEADS
WINDOW = 128
ROPE_THETA = 10000.0
CONV_W = 4
LRU_C = 8.0
N_GROUPS = 4
EXPERTS_PER_GROUP = 8
N_EXPERTS = N_GROUPS * EXPERTS_PER_GROUP
LN_EPS = 1e-5
NEG = -1e30
LOG2E = math.log2(math.e)

SUBLANES = 8
LANES = 128
V7X_VMEM_BYTES = 64 * 1024 * 1024

TM_PROJ = 512
TQ_ATTN = 1024
ATTN_STACK = Q_PER_KV
SEG = 256
SCAN_LEN = SEG + 4
SCAN_UNROLL = 10
BF16_ROWS = 2 * SUBLANES
TD = 512
MIX_ROWS = 256
CHUNK = BF16_ROWS
WAIT_BATCH = 8
COPY_RUN = 4
STAGE_ROWS = 2 * TD + CHUNK * N_EXPERTS
MOE_BLOCK = 128
CHUNKS_PER_BLOCK = MOE_BLOCK // CHUNK
EXP_TILE = 4 * MOE_BLOCK
EXP_BUFS = 6
VMEM_LIMIT = V7X_VMEM_BYTES * 7 // 8


def _cparams(n_axes):
    return pltpu.CompilerParams(dimension_semantics=("arbitrary",) * n_axes,
                                vmem_limit_bytes=VMEM_LIMIT)


def _softplus(z):
    e = jnp.exp(-jnp.abs(z))
    w = 1.0 + e
    tiny = w == 1.0
    log1p = jnp.where(tiny, e, jnp.log(w) * (e / jnp.where(tiny, 1.0, w - 1.0)))
    return jnp.maximum(z, 0.0) + log1p


def _sigmoid(x):
    return 0.5 * jnp.tanh(0.5 * x) + 0.5


def _gelu_tanh(y):
    c1 = math.sqrt(2.0 / math.pi)
    half = 0.5 * y
    return half + half * jnp.tanh(y * (c1 + (c1 * 0.044715) * (y * y)))


def _layer_norm(y, g, b):
    mu = jnp.mean(y, axis=-1, keepdims=True)
    d = y - mu
    var = jnp.mean(d * d, axis=-1, keepdims=True)
    return d * lax.rsqrt(var + LN_EPS) * g + b


def _project_tile(xb, w_ref, cos, sin, q_ref, k_ref, v_ref, xr_ref, yr_ref):
    aw = N_Q_HEADS * HEAD_DIM
    kw = N_KV_HEADS * HEAD_DIM
    d_model = xr_ref.shape[1]

    def proj(c0, n):
        return jnp.dot(xb, w_ref[:, c0:c0 + n], preferred_element_type=F32)

    def rope(t):
        return t * cos + pltpu.roll(t, HEAD_DIM // 2, 1) * sin

    zq = proj(0, aw)
    scale = HEAD_DIM ** -0.5 * LOG2E
    for h in range(N_Q_HEADS):
        sl = slice(h * HEAD_DIM, (h + 1) * HEAD_DIM)
        q_ref[:, sl] = (rope(zq[:, sl]) * scale).astype(BF16)
    zk = proj(aw, kw)
    for h in range(N_KV_HEADS):
        sl = slice(h * HEAD_DIM, (h + 1) * HEAD_DIM)
        k_ref[:, sl] = rope(zk[:, sl]).astype(BF16)
    v_ref[...] = proj(aw + kw, kw).astype(BF16)
    c0 = aw + 2 * kw
    xr_ref[...] = proj(c0, d_model)
    yr_ref[...] = proj(c0 + d_model, d_model)


def _inproj_kernel(x_ref, w_ref, cos_ref, sin_ref, q_ref, k_ref, v_ref, xr_ref, yr_ref):
    _project_tile(x_ref[...].astype(BF16), w_ref, cos_ref[...], sin_ref[...],
                  q_ref, k_ref, v_ref, xr_ref, yr_ref)


def _proj_out_specs(t, d, tm, row):
    aw = N_Q_HEADS * HEAD_DIM
    kw = N_KV_HEADS * HEAD_DIM
    shapes = [jax.ShapeDtypeStruct((t, aw), BF16), jax.ShapeDtypeStruct((t, kw), BF16),
              jax.ShapeDtypeStruct((t, kw), BF16)] + [jax.ShapeDtypeStruct((t, d), F32)] * 2
    specs = [pl.BlockSpec((tm, aw), row), pl.BlockSpec((tm, kw), row),
             pl.BlockSpec((tm, kw), row)] + [pl.BlockSpec((tm, d), row)] * 2
    return shapes, specs


def _inproj(x2, w_in_b, cosf, sinf, seq):
    t, d = x2.shape
    n_in = w_in_b.shape[1]
    tm = TM_PROJ
    tiles_per_seq = seq // tm
    row = lambda i: (i, 0)
    pos = lambda i: (i % tiles_per_seq, 0)
    shapes, specs = _proj_out_specs(t, d, tm, row)
    return pl.pallas_call(
        _inproj_kernel,
        grid=(t // tm,),
        in_specs=[pl.BlockSpec((tm, d), row),
                  pl.BlockSpec((d, n_in), lambda i: (0, 0)),
                  pl.BlockSpec((tm, HEAD_DIM), pos),
                  pl.BlockSpec((tm, HEAD_DIM), pos)],
        out_specs=specs,
        out_shape=shapes,
        compiler_params=_cparams(1),
        name="inproj",
    )(x2, w_in_b, cosf, sinf)


def _attn_kernel(sink_ref, q_ref, kp_ref, kc_ref, kn_ref, vp_ref, vc_ref, vn_ref, o_ref,
                 *, seq):
    i = pl.program_id(1)
    tq = q_ref.shape[1]
    blk = WINDOW
    t0 = i * tq
    kext = jnp.concatenate([kp_ref[0], kc_ref[0], kn_ref[0]], axis=0)
    vext = jnp.concatenate([vp_ref[0], vc_ref[0], vn_ref[0]], axis=0)
    nrow = ATTN_STACK * blk
    blk_shift = blk.bit_length() - 1
    qi = lax.broadcasted_iota(jnp.int32, (nrow, 3 * blk), 0) & (blk - 1)
    kj = lax.broadcasted_iota(jnp.int32, (nrow, 3 * blk), 1)
    band = jnp.where(jnp.abs(kj - blk - qi) <= WINDOW, 0.0, NEG)
    kj_row = lax.broadcasted_iota(jnp.int32, (1, 3 * blk), 1)
    rowg = lax.broadcasted_iota(jnp.int32, (nrow, 1), 0) >> blk_shift
    sinks = []
    for h0 in range(0, N_Q_HEADS, ATTN_STACK):
        sk = jnp.full((nrow, 1), sink_ref[h0], F32)
        for g in range(1, ATTN_STACK):
            sk = jnp.where(rowg == g, sink_ref[h0 + g], sk)
        sinks.append(sk * LOG2E)
    for j in range(tq // blk):
        kpos = t0 + j * blk - blk + kj_row
        bias = band + jnp.where((kpos >= 0) & (kpos < seq), 0.0, NEG)
        for h0 in range(0, N_Q_HEADS, ATTN_STACK):
            kv = h0 // Q_PER_KV
            hs = slice(kv * HEAD_DIM, (kv + 1) * HEAD_DIM)
            kblk = kext[j * blk:j * blk + 3 * blk, hs]
            vblk = vext[j * blk:j * blk + 3 * blk, hs]
            qs = [q_ref[0, j * blk:(j + 1) * blk, (h0 + g) * HEAD_DIM:(h0 + g + 1) * HEAD_DIM]
                  for g in range(ATTN_STACK)]
            qblk = jnp.concatenate(qs, axis=0)
            s = lax.dot_general(qblk, kblk, (((1,), (1,)), ((), ())),
                                preferred_element_type=F32) + bias
            sk = sinks[h0 // ATTN_STACK]
            m = jnp.maximum(jnp.max(s, axis=-1, keepdims=True), sk)
            p = jnp.exp2(s - m)
            denom = jnp.sum(p, axis=-1, keepdims=True) + jnp.exp2(sk - m)
            o = jnp.dot(p.astype(BF16), vblk, preferred_element_type=F32) / denom
            for g in range(ATTN_STACK):
                c = (h0 + g) * HEAD_DIM
                o_ref[0, j * blk:(j + 1) * blk, c:c + HEAD_DIM] = (
                    o[g * blk:(g + 1) * blk].astype(BF16))


def _attention(q3, k3, v3, sink):
    b, s, aw = q3.shape
    kw = k3.shape[2]
    tq = TQ_ATTN
    blk = WINDOW
    r = tq // blk
    nblk = s // blk
    cur = lambda bi, i: (bi, i, 0)
    prev = lambda bi, i: (bi, jnp.maximum(i * r - 1, 0), 0)
    nxt = lambda bi, i: (bi, jnp.minimum((i + 1) * r, nblk - 1), 0)
    kv_specs = [pl.BlockSpec((1, blk, kw), prev), pl.BlockSpec((1, tq, kw), cur),
                pl.BlockSpec((1, blk, kw), nxt)]
    return pl.pallas_call(
        functools.partial(_attn_kernel, seq=s),
        grid=(b, s // tq),
        in_specs=[pl.BlockSpec(memory_space=pltpu.SMEM),
                  pl.BlockSpec((1, tq, aw), cur)] + kv_specs + kv_specs,
        out_specs=pl.BlockSpec((1, tq, aw), cur),
        out_shape=jax.ShapeDtypeStruct((b, s, aw), BF16),
        compiler_params=_cparams(2),
        name="attention",
    )(sink, q3, k3, k3, k3, v3, v3, v3)


def _rnn_kernel(xr_ref, yr_ref, wc_ref, bc_ref, wg_ref, bg_ref, lam_ref, out_ref,
                xpad_ref, af_ref, uf_ref, ab_ref, ub_ref, hb_ref, cf_ref, cb_ref, *, seq):
    nseg = seq // SEG
    nlane = nseg
    ngrp = nlane // SUBLANES
    hf_ref = xpad_ref
    wc = wc_ref[...]
    bc = bc_ref[0]
    wg = wg_ref[0]
    bg = 0.5 * bg_ref[0]
    lam = lam_ref[0]
    rate = (0.5 * LRU_C) * _softplus(-lam)
    a_refs = (af_ref, ab_ref)
    u_refs = (uf_ref, ub_ref)
    left = CONV_W // 2

    halo = jnp.zeros((SUBLANES, LANES), F32)
    xpad_ref[pl.ds(0, SUBLANES), :] = halo
    xpad_ref[pl.ds(seq + SUBLANES, SUBLANES), :] = halo

    def pad_copy(c, carry):
        t0 = pl.multiple_of(c * SEG, SEG)
        xpad_ref[pl.ds(t0 + SUBLANES, SEG), :] = xr_ref[0, pl.ds(t0, SEG), :]
        return carry

    lax.fori_loop(0, nseg, pad_copy, 0, unroll=4)

    tail = nlane * SCAN_LEN - seq
    for a_ref, u_ref in zip(a_refs, u_refs):
        a_ref[pl.ds(seq, tail), :] = jnp.ones((tail, LANES), F32)
        u_ref[pl.ds(seq, tail), :] = jnp.zeros((tail, LANES), F32)

    def gates(c, carry):
        t0 = pl.multiple_of(c * SEG, SEG)
        xc = bc
        for tap in range(CONV_W):
            xc = xc + xpad_ref[pl.ds(t0 + SUBLANES - left + tap, SEG), :] * wc[tap:tap + 1]
        gh = jnp.dot(xc.astype(BF16), wg, preferred_element_type=F32) + bg
        xch = 0.5 * xc
        for d in range(2):
            rt = rate[:, d * LANES:(d + 1) * LANES]
            nlog_a = rt * jnp.tanh(gh[:, (2 * d) * LANES:(2 * d + 1) * LANES]) + rt
            a = jnp.exp2(nlog_a * (-1.0 / math.log(2.0)))
            z = jnp.tanh(nlog_a) * (a * a + 1.0)
            mult = jnp.where(z > 0.0, z * lax.rsqrt(z), 0.0)
            in_gate2 = jnp.tanh(gh[:, (2 * d + 1) * LANES:(2 * d + 2) * LANES]) + 1.0
            a_refs[d][pl.ds(t0, SEG), :] = a
            u_refs[d][pl.ds(t0, SEG), :] = (xch * mult) * in_gate2
        return carry

    lax.fori_loop(0, nseg, gates, 0, unroll=16)

    def lane_rows(g, j):
        return pl.ds(g * SUBLANES * SCAN_LEN + j, SUBLANES, stride=SCAN_LEN)

    def totals_step(j, carry):
        hf, pf, hb, pb = carry
        jb = SCAN_LEN - 1 - j
        nhf, npf, nhb, npb = [], [], [], []
        for g in range(ngrp):
            a = af_ref[lane_rows(g, j), :]
            nhf.append(a * hf[g] + uf_ref[lane_rows(g, j), :])
            npf.append(a * pf[g])
            a = ab_ref[lane_rows(g, jb), :]
            nhb.append(a * hb[g] + ub_ref[lane_rows(g, jb), :])
            npb.append(a * pb[g])
        return tuple(nhf), tuple(npf), tuple(nhb), tuple(npb)

    zero = tuple(jnp.zeros((SUBLANES, LANES), F32) for _ in range(ngrp))
    one = tuple(jnp.ones((SUBLANES, LANES), F32) for _ in range(ngrp))
    hf, pf, hb, pb = lax.fori_loop(0, SCAN_LEN, totals_step, (zero, one, zero, one),
                                   unroll=SCAN_UNROLL)

    c = jnp.zeros((1, LANES), F32)
    for s in range(nlane):
        g, r = divmod(s, SUBLANES)
        cf_ref[s:s + 1, :] = c
        c = pf[g][r:r + 1] * c + hf[g][r:r + 1]
    c = jnp.zeros((1, LANES), F32)
    for s in range(nlane - 1, -1, -1):
        g, r = divmod(s, SUBLANES)
        cb_ref[s:s + 1, :] = c
        c = pb[g][r:r + 1] * c + hb[g][r:r + 1]

    def scan_step(j, carry):
        hf, hb = carry
        jb = SCAN_LEN - 1 - j
        nhf, nhb = [], []
        for g in range(ngrp):
            h = af_ref[lane_rows(g, j), :] * hf[g] + uf_ref[lane_rows(g, j), :]
            hf_ref[lane_rows(g, j), :] = h
            nhf.append(h)
            h = ab_ref[lane_rows(g, jb), :] * hb[g] + ub_ref[lane_rows(g, jb), :]
            hb_ref[lane_rows(g, jb), :] = h
            nhb.append(h)
        return tuple(nhf), tuple(nhb)

    hf0 = tuple(cf_ref[g * SUBLANES:(g + 1) * SUBLANES, :] for g in range(ngrp))
    hb0 = tuple(cb_ref[g * SUBLANES:(g + 1) * SUBLANES, :] for g in range(ngrp))
    lax.fori_loop(0, SCAN_LEN, scan_step, (hf0, hb0), unroll=SCAN_UNROLL)

    def finish(c, carry):
        t0 = pl.multiple_of(c * SEG, SEG)
        h = hf_ref[pl.ds(t0, SEG), :] + hb_ref[pl.ds(t0, SEG), :]
        out_ref[0, pl.ds(t0, SEG), :] = (h * _gelu_tanh(yr_ref[0, pl.ds(t0, SEG), :])).astype(BF16)
        return carry

    lax.fori_loop(0, nseg, finish, 0, unroll=4)


def _rnn(xr3, yr3, w_conv, b_conv, wg_cat, bg_cat, lam_cat):
    b, s, d = xr3.shape
    nblk = d // LANES
    nseg = s // SEG
    slab = lambda bi, n: (bi, 0, n)
    per_blk = lambda bi, n: (n, 0, 0)
    assert nseg % SUBLANES == 0 and nseg * SCAN_LEN >= s + 2 * SUBLANES
    scratch = ([pltpu.VMEM((nseg * SCAN_LEN, LANES), F32)] * 6
               + [pltpu.VMEM((nseg, LANES), F32)] * 2)
    return pl.pallas_call(
        functools.partial(_rnn_kernel, seq=s),
        grid=(b, nblk),
        in_specs=[pl.BlockSpec((1, s, LANES), slab),
                  pl.BlockSpec((1, s, LANES), slab),
                  pl.BlockSpec((CONV_W, LANES), lambda bi, n: (0, n)),
                  pl.BlockSpec((1, 1, LANES), per_blk),
                  pl.BlockSpec((1, LANES, 4 * LANES), per_blk),
                  pl.BlockSpec((1, 1, 4 * LANES), per_blk),
                  pl.BlockSpec((1, 1, 2 * LANES), per_blk)],
        out_specs=pl.BlockSpec((1, s, LANES), slab),
        out_shape=jax.ShapeDtypeStruct((b, s, d), BF16),
        scratch_shapes=scratch,
        compiler_params=_cparams(2),
        name="rglru",
    )(xr3, yr3, w_conv, b_conv, wg_cat, bg_cat, lam_cat)


def _route(logits):
    lane = lax.broadcasted_iota(jnp.int32, logits.shape, 1)
    lanef = lane.astype(F32)
    big = float(4 * LANES)
    gmask = (lane >= N_EXPERTS) & (lane < N_EXPERTS + N_GROUPS)
    gl = jnp.where(gmask, logits, NEG)
    gmax = jnp.max(gl, axis=-1, keepdims=True)
    ge = jnp.exp(gl - gmax)
    gprob = ge / jnp.sum(ge, axis=-1, keepdims=True)
    gval = jnp.max(gprob, axis=-1, keepdims=True)
    gidx = jnp.min(jnp.where((gprob == gval) & gmask, lanef, big), axis=-1, keepdims=True)
    gidx = gidx.astype(jnp.int32) - N_EXPERTS
    group_shift = EXPERTS_PER_GROUP.bit_length() - 1
    emask = (lane < N_EXPERTS) & ((lane >> group_shift) == gidx)
    el = jnp.where(emask, logits, NEG)
    m1 = jnp.max(el, axis=-1, keepdims=True)
    i1 = jnp.min(jnp.where((el == m1) & emask, lanef, big), axis=-1, keepdims=True)
    emask2 = emask & (lanef != i1)
    el2 = jnp.where(emask2, logits, NEG)
    m2 = jnp.max(el2, axis=-1, keepdims=True)
    i2 = jnp.min(jnp.where((el2 == m2) & emask2, lanef, big), axis=-1, keepdims=True)
    e2 = jnp.exp(m2 - m1)
    den = 1.0 + e2
    g1 = (1.0 / den) * gval
    g2 = (e2 / den) * gval
    route = jnp.where(lane == 0, i1, jnp.where(lane == 1, i2,
                      jnp.where(lane == 2, g1, jnp.where(lane == 3, g2, 0.0))))
    onehot = (lanef == i1).astype(F32) + (lanef == i2).astype(F32)
    return route, jnp.sum(onehot, axis=0, keepdims=True)


def _mixout_kernel(x_ref, o_ref, hy_ref, wmg_ref, wao_ref, wro_ref, wout_ref,
                   lng_ref, lnb_ref, wr_ref, br_ref, x1_ref, x1b_ref, route_ref, cnt_ref,
                   logits_ref, *, alpha):
    @pl.when(pl.program_id(0) == 0)
    def _():
        logits_ref[...] = jnp.zeros_like(logits_ref)

    route, cnt = _route(logits_ref[...])
    route_ref[...] = route
    cnt_ref[0] = cnt

    d = x_ref.shape[1]
    for r0 in range(0, x_ref.shape[0], MIX_ROWS):
        rows = pl.ds(r0, MIX_ROWS)
        x = x_ref[rows, :]
        xb = x.astype(BF16)
        ga = jnp.dot(xb, wmg_ref[:, :d], preferred_element_type=F32)
        gr = jnp.dot(xb, wmg_ref[:, d:], preferred_element_type=F32)
        ya = jnp.dot(o_ref[rows, :], wao_ref[...], preferred_element_type=F32)
        yr = jnp.dot(hy_ref[rows, :], wro_ref[...], preferred_element_type=F32)
        merged = _sigmoid(ga) * ya + _sigmoid(gr) * yr
        mix = jnp.dot(merged.astype(BF16), wout_ref[...], preferred_element_type=F32)
        x1 = _layer_norm(alpha * x + mix, lng_ref[...], lnb_ref[...])
        x1_ref[rows, :] = x1
        x1b = x1.astype(BF16)
        x1b_ref[rows, :] = x1b
        logits_ref[rows, :] = (jnp.dot(x1b, wr_ref[...], preferred_element_type=F32)
                               + br_ref[...])


def _mixout(x2, o2, hy2, wmg, wao, wro, wout, lng, lnb, wr, br, alpha):
    t, d = x2.shape
    tm = TD
    nt = t // tm
    row = lambda i: (jnp.minimum(i, nt - 1), 0)
    prev_row = lambda i: (jnp.maximum(i - 1, 0), 0)
    full = lambda i: (0, 0)
    return pl.pallas_call(
        functools.partial(_mixout_kernel, alpha=alpha),
        grid=(nt + 1,),
        in_specs=[pl.BlockSpec((tm, d), row)] * 3
                 + [pl.BlockSpec((d, 2 * d), full)]
                 + [pl.BlockSpec((d, d), full)] * 3
                 + [pl.BlockSpec((1, d), full)] * 2
                 + [pl.BlockSpec((d, LANES), full), pl.BlockSpec((1, LANES), full)],
        out_specs=[pl.BlockSpec((tm, d), row), pl.BlockSpec((tm, d), row),
                   pl.BlockSpec((tm, LANES), prev_row),
                   pl.BlockSpec((1, 1, LANES), lambda i: (jnp.maximum(i - 1, 0), 0, 0))],
        out_shape=[jax.ShapeDtypeStruct((t, d), F32), jax.ShapeDtypeStruct((t, d), BF16),
                   jax.ShapeDtypeStruct((t, LANES), F32),
                   jax.ShapeDtypeStruct((nt, 1, LANES), F32)],
        scratch_shapes=[pltpu.VMEM((tm, LANES), F32)],
        compiler_params=_cparams(1),
        name="mixout",
    )(x2, o2, hy2, wmg, wao, wro, wout, lng, lnb, wr, br)


def _perm_matrix(route, soff_row, w1, w2):
    td = route.shape[0]
    lane = lax.broadcasted_iota(jnp.int32, (td, LANES), 1).astype(F32)
    e1 = lane == route[:, 0:1]
    e2 = lane == route[:, 1:2]
    cnt = (e1.astype(F32) + e2.astype(F32)).astype(BF16)
    ti = lax.broadcasted_iota(jnp.int32, (td, td), 0)
    tj = lax.broadcasted_iota(jnp.int32, (td, td), 1)
    lower = (tj < ti).astype(BF16)
    pos = jnp.dot(lower, cnt, preferred_element_type=F32) + soff_row
    r1 = jnp.sum(jnp.where(e1, pos, 0.0), axis=-1, keepdims=True).astype(jnp.int32)
    r2 = jnp.sum(jnp.where(e2, pos, 0.0), axis=-1, keepdims=True).astype(jnp.int32)
    col = lax.broadcasted_iota(jnp.int32, (td, STAGE_ROWS), 1)
    return jnp.where(col == r1, w1, jnp.where(col == r2, w2, 0.0))


def _chunk_rows(c, k=1):
    return pl.ds(pl.multiple_of(c * CHUNK, CHUNK), k * CHUNK)


def _block_rows(b):
    return pl.ds(pl.multiple_of(b * MOE_BLOCK, MOE_BLOCK), MOE_BLOCK)


def _for_each_run(i, cch_ref, soff_ref, dch_ref, fn):
    assert COPY_RUN == 4

    def per_expert(e, carry):
        idx = i * N_EXPERTS + e
        so = soff_ref[idx]
        do = dch_ref[idx]
        n = cch_ref[idx]
        full = n // COPY_RUN

        def per_run(c, carry2):
            fn(so + c * COPY_RUN, do + c * COPY_RUN, COPY_RUN)
            return carry2

        lax.fori_loop(0, full, per_run, 0)
        done = full * COPY_RUN
        two = n & 2

        @pl.when(two != 0)
        def _():
            fn(so + done, do + done, 2)

        @pl.when((n & 1) != 0)
        def _():
            fn(so + done + two, do + done + two, 1)

        return carry

    lax.fori_loop(0, N_EXPERTS, per_expert, 0)


def _repeat(n, fn):
    def body(c, carry):
        fn()
        return carry

    lax.fori_loop(0, n, body, 0)


def _drain_chunks(n, make_copy):
    _repeat(n // WAIT_BATCH, lambda: make_copy(WAIT_BATCH).wait())
    _repeat(n % WAIT_BATCH, lambda: make_copy(1).wait())


def _dispatch_kernel(cch_ref, soff_ref, dch_ref, tot_ref, padst_ref, padn_ref, nb_ref,
                     x_ref, route_ref, soffrow_ref, xb_ref, stage_ref, zero_ref, sems, zsem):
    i = pl.program_id(0)
    nt = pl.num_programs(0)
    n_blocks = xb_ref.shape[0] // MOE_BLOCK
    slot = i % 2

    def copy(s, src_chunk, dst_chunk, k):
        return pltpu.make_async_copy(stage_ref.at[s, _chunk_rows(src_chunk, k)],
                                     xb_ref.at[_chunk_rows(dst_chunk, k)], sems.at[s])

    def many(s):
        return lambda k: pltpu.make_async_copy(stage_ref.at[s, pl.ds(0, k * CHUNK)],
                                               xb_ref.at[pl.ds(0, k * CHUNK)], sems.at[s])

    @pl.when(i >= 2)
    def _():
        _drain_chunks(tot_ref[jnp.maximum(i - 2, 0)], many(slot))

    pt = _perm_matrix(route_ref[...], soffrow_ref[0], 1.0, 1.0).astype(BF16)
    stage_ref[slot] = lax.dot_general(pt, x_ref[...], (((0,), (0,)), ((), ())),
                                      preferred_element_type=F32).astype(BF16)
    _for_each_run(i, cch_ref, soff_ref, dch_ref,
                  lambda s, d, k: copy(slot, s, d, k).start())

    @pl.when(i == nt - 1)
    def _():
        @pl.when(i >= 1)
        def _():
            _drain_chunks(tot_ref[jnp.maximum(i - 1, 0)], many(1 - slot))

        _drain_chunks(tot_ref[i], many(slot))

        zero_ref[...] = jnp.zeros_like(zero_ref)

        def zcopy(dst_chunk):
            return pltpu.make_async_copy(zero_ref.at[pl.ds(0, CHUNK)],
                                         xb_ref.at[_chunk_rows(dst_chunk)], zsem)

        def pad_start(e, n):
            def pad_chunk(c, carry2):
                zcopy(padst_ref[e] + c).start()
                return carry2

            lax.fori_loop(0, padn_ref[e], pad_chunk, 0)
            return n + padn_ref[e]

        npad = lax.fori_loop(0, N_EXPERTS, pad_start, 0)
        _repeat(npad, lambda: zcopy(0).wait())

        def zblock(b):
            return pltpu.make_async_copy(zero_ref, xb_ref.at[_block_rows(b)], zsem)

        def start_block(b, carry):
            zblock(b).start()
            return carry

        def wait_block(b, carry):
            zblock(b).wait()
            return carry

        lax.fori_loop(nb_ref[0], n_blocks, start_block, 0)
        lax.fori_loop(nb_ref[0], n_blocks, wait_block, 0)


def _dispatch(x1, route, soffrow, tables, n_rows):
    t, d = x1.shape
    nt = t // TD
    grid_spec = pltpu.PrefetchScalarGridSpec(
        num_scalar_prefetch=7,
        grid=(nt,),
        in_specs=[pl.BlockSpec((TD, d), lambda i, *_: (i, 0)),
                  pl.BlockSpec((TD, LANES), lambda i, *_: (i, 0)),
                  pl.BlockSpec((1, 1, LANES), lambda i, *_: (i, 0, 0))],
        out_specs=pl.BlockSpec(memory_space=pl.ANY),
        scratch_shapes=[pltpu.VMEM((2, STAGE_ROWS, d), BF16), pltpu.VMEM((MOE_BLOCK, d), BF16),
                        pltpu.SemaphoreType.DMA((2,)), pltpu.SemaphoreType.DMA(())],
    )
    return pl.pallas_call(
        _dispatch_kernel,
        grid_spec=grid_spec,
        out_shape=jax.ShapeDtypeStruct((n_rows, d), BF16),
        compiler_params=_cparams(1),
        name="dispatch",
    )(*tables, x1, route, soffrow)


def _experts_kernel(pst_ref, rows_ref, nb_ref, xb_ref, wg_ref, wu_ref, wd_ref, yb_ref,
                    wgb_ref, wub_ref, wdb_ref, xbuf_ref, ybuf_ref, in_sems, out_sems, zsem):
    e = pl.program_id(0)
    ne = pl.num_programs(0)
    tiles = lambda rows: (rows + EXP_TILE - 1) // EXP_TILE
    rows_e = rows_ref[e]
    n = tiles(rows_e)
    row0 = pst_ref[e]
    n_blocks = yb_ref.shape[0] // MOE_BLOCK
    ahead = EXP_BUFS - 1

    def tile_rows(r0, t):
        return pl.ds(pl.multiple_of(r0 + t * EXP_TILE, MOE_BLOCK), EXP_TILE)

    def in_copy(r0, t, s):
        return pltpu.make_async_copy(xb_ref.at[tile_rows(r0, t)], xbuf_ref.at[s], in_sems.at[s])

    def out_copy(t, s):
        return pltpu.make_async_copy(ybuf_ref.at[s], yb_ref.at[tile_rows(row0, t)],
                                     out_sems.at[s])

    def start_head(r0, count):
        for k in range(ahead):
            @pl.when(k < count)
            def _():
                in_copy(r0, k, k).start()

    @pl.when(e == 0)
    def _():
        start_head(row0, n)
        ybuf_ref[...] = jnp.zeros_like(ybuf_ref)

    @pl.when(n > 0)
    def _():
        wgb_ref[...] = wg_ref[0, 0].astype(BF16)
        wub_ref[...] = wu_ref[0, 0].astype(BF16)
        wdb_ref[...] = wd_ref[0, 0].astype(BF16)

    def tile(t, carry):
        s = t % EXP_BUFS

        @pl.when(t + ahead < n)
        def _():
            in_copy(row0, t + ahead, (t + ahead) % EXP_BUFS).start()

        in_copy(row0, t, s).wait()

        @pl.when(t >= EXP_BUFS)
        def _():
            out_copy(t - EXP_BUFS, s).wait()

        def mlp(m):
            x = xbuf_ref[s, pl.ds(0, m), :]
            hg = jnp.dot(x, wgb_ref[...], preferred_element_type=F32)
            hu = jnp.dot(x, wub_ref[...], preferred_element_type=F32)
            hid = (hg * _sigmoid(hg)) * hu
            ybuf_ref[s, pl.ds(0, m), :] = jnp.dot(hid.astype(BF16), wdb_ref[...],
                                                  preferred_element_type=F32).astype(BF16)

        owned = jnp.minimum(rows_e - t * EXP_TILE, EXP_TILE)
        for m in range(MOE_BLOCK, EXP_TILE + 1, MOE_BLOCK):
            @pl.when(owned == m)
            def _():
                mlp(m)

        out_copy(t, s).start()
        return carry

    lax.fori_loop(0, n, tile, 0)

    @pl.when(e + 1 < ne)
    def _():
        nxt = jnp.minimum(e + 1, ne - 1)
        start_head(pst_ref[nxt], tiles(rows_ref[nxt]))

    for k in range(EXP_BUFS):
        @pl.when(n > k)
        def _():
            out_copy(n - 1 - k, (n - 1 - k) % EXP_BUFS).wait()

    @pl.when(e == ne - 1)
    def _():
        ybuf_ref[0] = jnp.zeros(ybuf_ref.shape[1:], BF16)

        def zblock(b):
            return pltpu.make_async_copy(ybuf_ref.at[0, pl.ds(0, MOE_BLOCK)],
                                         yb_ref.at[_block_rows(b)], zsem)

        def start_block(b, carry):
            zblock(b).start()
            return carry

        def wait_block(b, carry):
            zblock(b).wait()
            return carry

        lax.fori_loop(nb_ref[0], n_blocks, start_block, 0)
        lax.fori_loop(nb_ref[0], n_blocks, wait_block, 0)


def _experts(xb, pst, rows, nb_used, w_g, w_u, w_d, layer):
    n_rows, d = xb.shape
    n_exp, de = w_g.shape[1], w_g.shape[3]
    wsel = lambda e, *_: (layer, e, 0, 0)
    grid_spec = pltpu.PrefetchScalarGridSpec(
        num_scalar_prefetch=3,
        grid=(n_exp,),
        in_specs=[pl.BlockSpec(memory_space=pl.ANY),
                  pl.BlockSpec((1, 1, d, de), wsel),
                  pl.BlockSpec((1, 1, d, de), wsel),
                  pl.BlockSpec((1, 1, de, d), wsel)],
        out_specs=pl.BlockSpec(memory_space=pl.ANY),
        scratch_shapes=[pltpu.VMEM((d, de), BF16), pltpu.VMEM((d, de), BF16),
                        pltpu.VMEM((de, d), BF16),
                        pltpu.VMEM((EXP_BUFS, EXP_TILE, d), BF16),
                        pltpu.VMEM((EXP_BUFS, EXP_TILE, d), BF16),
                        pltpu.SemaphoreType.DMA((EXP_BUFS,)), pltpu.SemaphoreType.DMA((EXP_BUFS,)),
                        pltpu.SemaphoreType.DMA(())],
    )
    return pl.pallas_call(
        _experts_kernel,
        grid_spec=grid_spec,
        out_shape=jax.ShapeDtypeStruct((n_rows, d), BF16),
        compiler_params=_cparams(1),
        name="experts",
    )(pst, rows, nb_used, xb, w_g, w_u, w_d)


def _combine_kernel(cch_ref, soff_ref, dch_ref, tot_ref,
                    x_ref, route_ref, soffrow_ref, yb_ref, lng_ref, lnb_ref, *rest,
                    alpha, project):
    if project:
        (w_ref, cos_ref, sin_ref, out_ref, q_ref, k_ref, v_ref, xr_ref, yr_ref,
         stage_ref, sems) = rest
    else:
        out_ref, stage_ref, sems = rest
    i = pl.program_id(0)
    nt = pl.num_programs(0)
    slot = i % 2

    def copy(s, src_chunk, dst_chunk, k):
        return pltpu.make_async_copy(yb_ref.at[_chunk_rows(src_chunk, k)],
                                     stage_ref.at[s, _chunk_rows(dst_chunk, k)], sems.at[s])

    def fetch(tile_idx, s):
        _for_each_run(tile_idx, cch_ref, soff_ref, dch_ref,
                      lambda so, do, k: copy(s, do, so, k).start())

    @pl.when(i == 0)
    def _():
        stage_ref[...] = jnp.zeros_like(stage_ref)
        fetch(i, slot)

    @pl.when(i + 1 < nt)
    def _():
        fetch(i + 1, 1 - slot)

    _drain_chunks(tot_ref[i], lambda k: pltpu.make_async_copy(
        yb_ref.at[pl.ds(0, k * CHUNK)], stage_ref.at[slot, pl.ds(0, k * CHUNK)], sems.at[slot]))

    route = route_ref[...]
    pt = _perm_matrix(route, soffrow_ref[0], route[:, 2:3], route[:, 3:4]).astype(BF16)
    ffn = jnp.dot(pt, stage_ref[slot], preferred_element_type=F32)
    out = _layer_norm(alpha * x_ref[...] + ffn, lng_ref[...], lnb_ref[...])
    out_ref[...] = out
    if project:
        _project_tile(out.astype(BF16), w_ref, cos_ref[...], sin_ref[...],
                      q_ref, k_ref, v_ref, xr_ref, yr_ref)


def _combine(x1, route, soffrow, tables, yb, lng, lnb, alpha, next_proj=None):
    t, d = x1.shape
    nt = t // TD
    row = lambda i, *_: (i, 0)
    full = lambda i, *_: (0, 0)
    in_specs = [pl.BlockSpec((TD, d), row),
                pl.BlockSpec((TD, LANES), row),
                pl.BlockSpec((1, 1, LANES), lambda i, *_: (i, 0, 0)),
                pl.BlockSpec(memory_space=pl.ANY),
                pl.BlockSpec((1, d), full),
                pl.BlockSpec((1, d), full)]
    out_shape = [jax.ShapeDtypeStruct((t, d), F32)]
    out_specs = [pl.BlockSpec((TD, d), row)]
    args = [x1, route, soffrow, yb, lng, lnb]
    if next_proj is not None:
        w_in_b, cosf, sinf, seq = next_proj
        tiles_per_seq = seq // TD
        pos = lambda i, *_: (i % tiles_per_seq, 0)
        in_specs += [pl.BlockSpec(w_in_b.shape, full), pl.BlockSpec((TD, HEAD_DIM), pos),
                     pl.BlockSpec((TD, HEAD_DIM), pos)]
        shapes, specs = _proj_out_specs(t, d, TD, row)
        out_shape += shapes
        out_specs += specs
        args += [w_in_b, cosf, sinf]
    grid_spec = pltpu.PrefetchScalarGridSpec(
        num_scalar_prefetch=4,
        grid=(nt,),
        in_specs=in_specs,
        out_specs=out_specs,
        scratch_shapes=[pltpu.VMEM((2, STAGE_ROWS, d), BF16), pltpu.SemaphoreType.DMA((2,))],
    )
    return pl.pallas_call(
        functools.partial(_combine_kernel, alpha=alpha, project=next_proj is not None),
        grid_spec=grid_spec,
        out_shape=out_shape,
        compiler_params=_cparams(1),
        name="combine",
    )(*tables, *args)


def _max_blocks(t):
    nt = t // TD
    rows = 2 * t + (CHUNK - 1) * nt * N_EXPERTS + (MOE_BLOCK - CHUNK) * N_EXPERTS
    return -(-rows // MOE_BLOCK)


def _dispatch_tables(cnt):
    n = cnt[:, 0, :N_EXPERTS].astype(jnp.int32)
    nt = n.shape[0]
    cch = (n + CHUNK - 1) // CHUNK
    excl = lambda k: (jnp.arange(k)[:, None] < jnp.arange(k)[None, :]).astype(F32)
    dot = functools.partial(jnp.dot, precision=lax.Precision.HIGHEST)
    cchf = cch.astype(F32)
    soff = dot(cchf, excl(N_EXPERTS)).astype(jnp.int32)
    tot_tile = jnp.sum(cch, axis=1)
    tot_e = jnp.sum(cch, axis=0)
    reg = (tot_e + CHUNKS_PER_BLOCK - 1) // CHUNKS_PER_BLOCK * CHUNKS_PER_BLOCK
    pstart = dot(reg.astype(F32)[None, :], excl(N_EXPERTS))[0].astype(jnp.int32)
    before = dot(excl(nt).T, cchf).astype(jnp.int32)
    dch = pstart[None, :] + before
    nb_used = jnp.sum(reg) // CHUNKS_PER_BLOCK
    soffrow = jnp.pad((soff * CHUNK).astype(F32), ((0, 0), (0, LANES - N_EXPERTS)))[:, None, :]
    i32 = lambda a: a.reshape(-1).astype(jnp.int32)
    return dict(cch=i32(cch), soff=i32(soff), dch=i32(dch), tot=i32(tot_tile),
                padst=i32(pstart + tot_e), padn=i32(reg - tot_e), nb_used=i32(nb_used),
                pst=i32(pstart * CHUNK), rows=i32(reg * CHUNK),
                soffrow=soffrow)


def _rope_tables(seq):
    inv = ROPE_THETA ** (-jnp.arange(0, HEAD_DIM, 2, dtype=F32) / HEAD_DIM)
    ang = jnp.arange(seq, dtype=F32)[:, None] * inv[None, :]
    cos, sin = jnp.cos(ang), jnp.sin(ang)
    return jnp.concatenate([cos, cos], axis=1), jnp.concatenate([-sin, sin], axis=1)


@jax.jit
def kernel(x, w_in, w_sink, w_conv, b_conv, w_rec_gate, b_rec_gate, w_in_gate, b_in_gate,
           lru_lambda, w_attn_o, w_rnn_o, w_out, ln_g, ln_b, w_router_group, b_router_group,
           w_router_expert, b_router_expert, w_exp_gate, w_exp_up, w_exp_down):
    bsz, seq, d = x.shape
    depth = w_in.shape[0]
    t = bsz * seq
    nblk = d // LANES
    alpha = (2 * depth) ** 0.25
    cosf, sinf = _rope_tables(seq)
    n_rows = _max_blocks(t) * MOE_BLOCK + (EXP_TILE - MOE_BLOCK)
    x2 = x.reshape(t, d)
    n_branch = w_in.shape[2] - 2 * d
    w_branch = [w_in[l, :, :n_branch].astype(BF16) for l in range(depth)]
    q, k, v, xr, yr = _inproj(x2, w_branch[0], cosf, sinf, seq)
    for l in range(depth):
        o = _attention(q.reshape(bsz, seq, -1), k.reshape(bsz, seq, -1), v.reshape(bsz, seq, -1),
                       w_sink[l])
        wg_cat = (0.5 * jnp.concatenate([w_rec_gate[l, 0], w_in_gate[l, 0], w_rec_gate[l, 1],
                                         w_in_gate[l, 1]], axis=-1)).astype(BF16)
        bg_cat = jnp.concatenate([b_rec_gate[l, 0].reshape(nblk, 1, LANES),
                                  b_in_gate[l, 0].reshape(nblk, 1, LANES),
                                  b_rec_gate[l, 1].reshape(nblk, 1, LANES),
                                  b_in_gate[l, 1].reshape(nblk, 1, LANES)], axis=-1)
        lam_cat = jnp.concatenate([lru_lambda[l, 0].reshape(nblk, 1, LANES),
                                   lru_lambda[l, 1].reshape(nblk, 1, LANES)], axis=-1)
        hy = _rnn(xr.reshape(bsz, seq, d), yr.reshape(bsz, seq, d), w_conv[l],
                  b_conv[l].reshape(nblk, 1, LANES), wg_cat, bg_cat, lam_cat)
        lane_pad = LANES - N_EXPERTS - N_GROUPS
        wr = jnp.concatenate([w_router_expert[l], w_router_group[l],
                              jnp.zeros((d, lane_pad), F32)], axis=1).astype(BF16)
        br = jnp.concatenate([b_router_expert[l], b_router_group[l],
                              jnp.zeros((lane_pad,), F32)])[None, :]
        x1, x1b, route, cnt = _mixout(x2, o.reshape(t, -1), hy.reshape(t, d),
                                 w_in[l, :, n_branch:].astype(BF16),
                                 w_attn_o[l].astype(BF16), w_rnn_o[l].astype(BF16),
                                 w_out[l].astype(BF16), ln_g[l, 0].reshape(1, d),
                                 ln_b[l, 0].reshape(1, d), wr, br, alpha)
        tb = _dispatch_tables(cnt)
        xb = _dispatch(x1b, route, tb["soffrow"],
                       (tb["cch"], tb["soff"], tb["dch"], tb["tot"], tb["padst"], tb["padn"],
                        tb["nb_used"]), n_rows)
        yb = _experts(xb, tb["pst"], tb["rows"], tb["nb_used"], w_exp_gate, w_exp_up,
                      w_exp_down, l)
        next_proj = (w_branch[l + 1], cosf, sinf, seq) if l + 1 < depth else None
        x2, *nxt = _combine(x1, route, tb["soffrow"],
                            (tb["cch"], tb["soff"], tb["dch"], tb["tot"]), yb,
                            ln_g[l, 1].reshape(1, d), ln_b[l, 1].reshape(1, d), alpha, next_proj)
        if nxt:
            q, k, v, xr, yr = nxt
    return x2.reshape(bsz, seq, d)
```
